```python
import math
import jax, jax.numpy as jnp
from jax import lax
import numpy as np

D_MODEL = 1024
BATCH = 8
SEQ = 4096
DEPTH = 2

GRID_W = 64
CTX_LEN = 256

HEAD_DIM = 64
W_GROUP = D_MODEL // 4
MIX_WIDTH = 4 * W_GROUP
N_WA_HEADS = W_GROUP // HEAD_DIM
N_WA_KV = N_WA_HEADS // 2
N_NA_HEADS = W_GROUP // HEAD_DIM
KV_WA = N_WA_KV * HEAD_DIM
OFF_POOL = 3 * W_GROUP
OFF_WA_Q = 4 * W_GROUP
OFF_NA_Q = 5 * W_GROUP
OFF_KV = 6 * W_GROUP
IN_WIDTH = OFF_KV + 2 * KV_WA + 2 * W_GROUP

HY_ORDER = 2
HY_BANDS = 16
HY_EMB = 1 + 2 * HY_BANDS
HY_HIDDEN = 64
HY_DECAY_MIN = abs(math.log(1e-2) / 1.5)
HY_DECAY_MAX = abs(math.log(1e-2) / 0.3)

WA_WINDOW = 128
WA_BLOCK = 128

POOL_WINDOWS = (2, 4, 8, 16)
POOL_GROUP = W_GROUP // len(POOL_WINDOWS)

NA_ROWS = 8
NA_COLS = 16
NA_QCOLS = 16
NA_BAND = 2 * NA_COLS

ROPE_BASE = 10000.0

N_EXPERTS = 16
EXPERT_HIDDEN = 2 * D_MODEL
EC_CAPACITY = 2

NORM_EPS = 1e-6
NEG_INF = -1e30

kernel_name = 'hybrid_parallel_mixer_diffusion_block'


def _rmsnorm(x, g):
    x32 = x.astype(jnp.float32)
    y = x32 * lax.rsqrt(jnp.mean(x32 * x32, axis=-1, keepdims=True) + NORM_EPS)
    return (y * g.astype(jnp.float32)).astype(x.dtype)


def _modulate(h, shift, scale):
    return h * (1 + scale) + shift


def _short_conv(u, w, b):
    up = jnp.pad(u, ((0, 0), (1, 1), (0, 0)))
    return up[:, :-2] * w[0] + up[:, 1:-1] * w[1] + up[:, 2:] * w[2] + b


def _hyena_filters(L, w1, b1, w2, b2, w3, freq):
    f32 = jnp.float32
    pos = jnp.arange(L, dtype=f32)
    t = pos / max(L - 1, 1)
    bands = jnp.linspace(1e-4, HY_BANDS - 1, HY_BANDS, dtype=f32)
    ang = (2 * math.pi / L) * pos[:, None] * bands[None, :]
    feats = jnp.concatenate([t[:, None], jnp.cos(ang), -jnp.sin(ang)], axis=-1)
    fr = freq.astype(f32)
    h = jnp.sin(fr * (feats @ w1.astype(f32) + b1.astype(f32)))
    h = jnp.sin(fr * (h @ w2.astype(f32) + b2.astype(f32)))
    h = (h @ w3.astype(f32)).reshape(L, HY_ORDER, 2, W_GROUP)
    deltas = jnp.linspace(HY_DECAY_MIN, HY_DECAY_MAX, W_GROUP, dtype=f32)
    window = jnp.exp(-t[:, None] * deltas[None, :])
    return h * window[:, None, None, :]


def _two_sided(h_fwd, h_bwd):
    lag0 = h_fwd[:1] + h_bwd[:1]
    return jnp.concatenate([lag0, h_fwd[1:], jnp.zeros_like(h_fwd[:1]), h_bwd[:0:-1]], axis=0)


def _fft_conv(z, filt, skip):
    L = z.shape[1]
    zf = jnp.fft.rfft(z, n=2 * L, axis=1)
    ff = jnp.fft.rfft(filt, n=2 * L, axis=0)
    y = jnp.fft.irfft(zf * ff[None], n=2 * L, axis=1)[:, :L]
    return y + z * skip


def _hyena(u, conv_w, conv_b, w1, b1, w2, b2, w3, freq, skip):
    L = u.shape[1]
    uc = _short_conv(u, conv_w, conv_b).astype(jnp.float32)
    v, x1, x2 = jnp.split(uc, 3, axis=-1)
    h = _hyena_filters(L, w1, b1, w2, b2, w3, freq)
    sk = skip.astype(jnp.float32)
    z = x1 * _fft_conv(v, _two_sided(h[:, 0, 0], h[:, 0, 1]), sk[0])
    z = x2 * _fft_conv(z, _two_sided(h[:, 1, 0], h[:, 1, 1]), sk[1])
    return z.astype(u.dtype)


def _pool_mixer(u, w_grp, scale):
    B, L, C = u.shape
    u32 = u.astype(jnp.float32)
    cs = jnp.concatenate([jnp.zeros((B, 1, C), jnp.float32), jnp.cumsum(u32, axis=1)], axis=1)
    pos = np.arange(L)
    outs = []
    for g, win in enumerate(POOL_WINDOWS):
        lo = np.clip(pos - win // 2, 0, L)
        hi = np.clip(pos + win // 2, 0, L)
        sl = slice(g * POOL_GROUP, (g + 1) * POOL_GROUP)
        mean = (cs[:, hi, sl] - cs[:, lo, sl]) / (hi - lo).astype(np.float32)[None, :, None]
        outs.append((mean - u32[:, :, sl]).astype(u.dtype) @ w_grp[g])
    return jnp.concatenate(outs, axis=-1) * scale


def _axial_rope(x, rows, cols):
    quarter = HEAD_DIM // 4
    half = HEAD_DIM // 2
    inv = ROPE_BASE ** (-jnp.arange(quarter, dtype=jnp.float32) / quarter)

    def rot(xh, p):
        ang = p.astype(jnp.float32)[:, None] * inv[None, :]
        cos = jnp.cos(ang)[None, :, None, :].astype(x.dtype)
        sin = jnp.sin(ang)[None, :, None, :].astype(x.dtype)
        a, b = xh[..., :quarter], xh[..., quarter:]
        return jnp.concatenate([a * cos - b * sin, a * sin + b * cos], axis=-1)

    return jnp.concatenate([rot(x[..., :half], rows), rot(x[..., half:], cols)], axis=-1)


def _window_attn(q, k, v, kc, vc, sink):
    f32 = jnp.float32
    B, S, HQ, hd = q.shape
    HKV = k.shape[2]
    G = HQ // HKV
    Lc = kc.shape[1]
    nb = S // WA_BLOCK
    nbw = 3 * WA_BLOCK
    qb = q.reshape(B, nb, WA_BLOCK, HKV, G, hd)

    def band(t):
        tp = jnp.pad(t, ((0, 0), (WA_BLOCK, WA_BLOCK), (0, 0), (0, 0))).reshape(B, nb + 2, WA_BLOCK, HKV, hd)
        return jnp.concatenate([tp[:, :-2], tp[:, 1:-1], tp[:, 2:]], axis=2)

    kb, vb = band(k), band(v)
    scl = hd ** -0.5
    qpos = np.arange(nb)[:, None, None] * WA_BLOCK + np.arange(WA_BLOCK)[None, :, None]
    kpos = np.arange(nb)[:, None, None] * WA_BLOCK - WA_BLOCK + np.arange(nbw)[None, None, :]
    valid = (np.abs(qpos - kpos) <= WA_WINDOW) & (kpos >= 0) & (kpos < S)
    s_loc = jnp.where(valid, jnp.einsum('bnqkgd,bnjkd->bkgnqj', qb, kb).astype(f32) * scl, NEG_INF)
    s_ctx = jnp.einsum('bnqkgd,bckd->bkgnqc', qb, kc).astype(f32) * scl
    s_snk = jnp.broadcast_to(sink.astype(f32).reshape(HKV, G)[None, :, :, None, None, None],
                             (B, HKV, G, nb, WA_BLOCK, 1))
    p = jax.nn.softmax(jnp.concatenate([s_loc, s_ctx, s_snk], axis=-1), axis=-1).astype(v.dtype)
    out = (jnp.einsum('bkgnqj,bnjkd->bnqkgd', p[..., :nbw], vb)
           + jnp.einsum('bkgnqc,bckd->bnqkgd', p[..., nbw:nbw + Lc], vc))
    return out.reshape(B, S, HQ * hd)


def _ctx_attn(q, k, v, sink):
    B, L, HQ, hd = q.shape
    HKV = k.shape[2]
    G = HQ // HKV
    qg = q.reshape(B, L, HKV, G, hd)
    s = jnp.einsum('bqkgd,bjkd->bkgqj', qg, k).astype(jnp.float32) * hd ** -0.5
    if sink is not None:
        snk = jnp.broadcast_to(sink.astype(jnp.float32).reshape(HKV, G)[None, :, :, None, None], (B, HKV, G, L, 1))
        s = jnp.concatenate([s, snk], axis=-1)
    p = jax.nn.softmax(s, axis=-1)[..., :L].astype(v.dtype)
    return jnp.einsum('bkgqj,bjkd->bqkgd', p, v).reshape(B, L, HQ * hd)


def _neighbourhood_attn(q, k, v, kc, vc, rpb):
    f32 = jnp.float32
    B, S, H, hd = q.shape
    rows = S // GRID_W
    wr = min(NA_ROWS, rows)
    ncb = GRID_W // NA_QCOLS
    nk = wr * NA_BAND
    r = np.arange(rows)
    ri = np.clip(r - wr // 2, 0, rows - wr)[:, None] + np.arange(wr)[None, :]
    ci = (np.clip(np.arange(ncb) * NA_QCOLS - NA_COLS // 2, 0, GRID_W - NA_BAND)[:, None]
          + np.arange(NA_BAND)[None, :])
    qcol = np.arange(GRID_W).reshape(ncb, NA_QCOLS)
    ws = np.clip(qcol - NA_COLS // 2, 0, GRID_W - NA_COLS)
    col_ok = (ci[:, None, :] >= ws[:, :, None]) & (ci[:, None, :] < ws[:, :, None] + NA_COLS)
    mask = np.broadcast_to(col_ok[:, :, None, :], (ncb, NA_QCOLS, wr, NA_BAND)).reshape(ncb, NA_QCOLS, nk)
    dr = ri - r[:, None] + NA_ROWS - 1
    dc = np.clip(ci[:, None, :] - qcol[:, :, None] + NA_COLS - 1, 0, 2 * NA_COLS - 2)
    bias = rpb.astype(f32)[:, dr[:, None, None, :, None], dc[None, :, :, None, :]]
    bias = bias.reshape(H, rows, ncb, NA_QCOLS, nk)
    qg = q.reshape(B, rows, ncb, NA_QCOLS, H, hd)

    def gather(t):
        tg = t.reshape(B, rows, GRID_W, H, hd)
        return tg[:, ri[:, None, :, None], ci[None, :, None, :]].reshape(B, rows, ncb, nk, H, hd)

    kg, vg = gather(k), gather(v)
    scl = hd ** -0.5
    s_loc = jnp.where(mask, jnp.einsum('brnqhd,brnjhd->bhrnqj', qg, kg).astype(f32) * scl + bias[None], NEG_INF)
    s_ctx = jnp.einsum('brnqhd,bchd->bhrnqc', qg, kc).astype(f32) * scl
    p = jax.nn.softmax(jnp.concatenate([s_loc, s_ctx], axis=-1), axis=-1).astype(v.dtype)
    out = (jnp.einsum('bhrnqj,brnjhd->brnqhd', p[..., :nk], vg)
           + jnp.einsum('bhrnqc,bchd->brnqhd', p[..., nk:], vc))
    return out.reshape(B, S, H * hd)


def _expert_choice_ffn(h, router_w, w_gate, w_up, w_down):
    B, L, D = h.shape
    cap = EC_CAPACITY * L // N_EXPERTS
    aff = jax.nn.softmax((h @ router_w).astype(jnp.float32), axis=-1)
    g, idx = lax.top_k(jnp.swapaxes(aff, 1, 2), cap)
    bidx = jnp.arange(B)[:, None, None]
    xg = h[bidx, idx]
    a = jnp.einsum('becd,edf->becf', xg, w_gate)
    u = jnp.einsum('becd,edf->becf', xg, w_up)
    o = jnp.einsum('becf,efd->becd', jax.nn.silu(a) * u, w_down) * g[..., None].astype(h.dtype)
    return jnp.zeros_like(h).at[bidx, idx].add(o)


def _split_kv(p, B, L):
    wk = p[..., :KV_WA].reshape(B, L, N_WA_KV, HEAD_DIM)
    wv = p[..., KV_WA:2 * KV_WA].reshape(B, L, N_WA_KV, HEAD_DIM)
    nk = p[..., 2 * KV_WA:2 * KV_WA + W_GROUP].reshape(B, L, N_NA_HEADS, HEAD_DIM)
    nv = p[..., 2 * KV_WA + W_GROUP:].reshape(B, L, N_NA_HEADS, HEAD_DIM)
    return wk, wv, nk, nv


def _layer(x, ctx, c, c_ctx, ada_w, ada_b, g1, g2, w_in, hy_conv_w, hy_conv_b, hy_w1, hy_b1, hy_w2, hy_b2,
           hy_w3, hy_freq, hy_skip, wa_sink, pool_w, pool_scale, na_rpb, w_out, router_w, exp_w_gate,
           exp_w_up, exp_w_down, update_ctx):
    D = D_MODEL
    B, S, _ = x.shape
    Lc = ctx.shape[1]
    hy = (hy_conv_w, hy_conv_b, hy_w1, hy_b1, hy_w2, hy_b2, hy_w3, hy_freq, hy_skip)
    mod = jax.nn.silu(c) @ ada_w + ada_b
    sh1, sc1, gt1, sh2, sc2, gt2 = jnp.split(mod[:, None, :], 6, axis=-1)
    n_cm = 6 if update_ctx else 2
    mod_c = (jax.nn.silu(c_ctx) @ ada_w[:, :n_cm * D] + ada_b[:n_cm * D]).reshape(n_cm, D)

    hx = _modulate(_rmsnorm(x, g1), sh1, sc1)
    hc = _modulate(_rmsnorm(ctx, g1), mod_c[0], mod_c[1])
    px = hx @ w_in
    if update_ctx:
        pc = hc @ w_in
        pc_kv = pc[..., OFF_KV:]
    else:
        pc_kv = hc @ w_in[:, OFF_KV:]
    kc_wa, vc_wa, kc_na, vc_na = _split_kv(pc_kv, B, Lc)
    k_wa, v_wa, k_na, v_na = _split_kv(px[..., OFF_KV:], B, S)

    pos = jnp.arange(S)
    rows, cols = pos // GRID_W, pos % GRID_W
    q_wa = _axial_rope(px[..., OFF_WA_Q:OFF_NA_Q].reshape(B, S, N_WA_HEADS, HEAD_DIM), rows, cols)
    k_wa = _axial_rope(k_wa, rows, cols)
    q_na = px[..., OFF_NA_Q:OFF_KV].reshape(B, S, N_NA_HEADS, HEAD_DIM)

    y = jnp.concatenate([
        _hyena(px[..., :OFF_POOL], *hy),
        _window_attn(q_wa, k_wa, v_wa, kc_wa, vc_wa, wa_sink),
        _pool_mixer(px[..., OFF_POOL:OFF_WA_Q], pool_w, pool_scale),
        _neighbourhood_attn(q_na, k_na, v_na, kc_na, vc_na, na_rpb),
    ], axis=-1)
    x = x + gt1 * (y @ w_out)
    x = x + gt2 * _expert_choice_ffn(_modulate(_rmsnorm(x, g2), sh2, sc2), router_w, exp_w_gate, exp_w_up, exp_w_down)

    if update_ctx:
        qc_wa = pc[..., OFF_WA_Q:OFF_NA_Q].reshape(B, Lc, N_WA_HEADS, HEAD_DIM)
        qc_na = pc[..., OFF_NA_Q:OFF_KV].reshape(B, Lc, N_NA_HEADS, HEAD_DIM)
        yc = jnp.concatenate([
            _hyena(pc[..., :OFF_POOL], *hy),
            _ctx_attn(qc_wa, kc_wa, vc_wa, wa_sink),
            _pool_mixer(pc[..., OFF_POOL:OFF_WA_Q], pool_w, pool_scale),
            _ctx_attn(qc_na, kc_na, vc_na, None),
        ], axis=-1)
        ctx = ctx + mod_c[2] * (yc @ w_out)
        ctx = ctx + mod_c[5] * _expert_choice_ffn(_modulate(_rmsnorm(ctx, g2), mod_c[3], mod_c[4]),
                                                  router_w, exp_w_gate, exp_w_up, exp_w_down)
    return x, ctx


def setup_inputs(seed: int = 0) -> dict:
    key = jax.random.key(seed)
    ks = jax.random.split(key, 28)
    D, NL, W, F, E = D_MODEL, DEPTH, W_GROUP, EXPERT_HIDDEN, N_EXPERTS

    def nrm(k, shape, s):
        return jax.random.normal(k, shape, jnp.float32) * s

    return {
        'x': nrm(ks[0], (BATCH, SEQ, D), 1.0),
        'c': nrm(ks[1], (BATCH, D), 1.0),
        'ctx': nrm(ks[2], (BATCH, CTX_LEN, D), 1.0),
        'c_ctx': nrm(ks[3], (D,), 1.0),
        'ada_w': nrm(ks[4], (NL, D, 6 * D), 0.5 * D ** -0.5),
        'ada_b': nrm(ks[5], (NL, 6 * D), 0.02),
        'norm1_g': 1.0 + nrm(ks[6], (NL, D), 0.05),
        'norm2_g': 1.0 + nrm(ks[7], (NL, D), 0.05),
        'w_in': nrm(ks[8], (NL, D, IN_WIDTH), D ** -0.5),
        'hy_conv_w': nrm(ks[9], (NL, 3, 3 * W), 3 ** -0.5),
        'hy_conv_b': nrm(ks[10], (NL, 3 * W), 0.02),
        'hy_w1': nrm(ks[11], (NL, HY_EMB, HY_HIDDEN), HY_EMB ** -0.5),
        'hy_b1': nrm(ks[12], (NL, HY_HIDDEN), 0.1),
        'hy_w2': nrm(ks[13], (NL, HY_HIDDEN, HY_HIDDEN), HY_HIDDEN ** -0.5),
        'hy_b2': nrm(ks[14], (NL, HY_HIDDEN), 0.1),
        'hy_w3': nrm(ks[15], (NL, HY_HIDDEN, HY_ORDER * 2 * W), 0.01),
        'hy_freq': 1.0 + nrm(ks[16], (NL, HY_HIDDEN), 0.1),
        'hy_skip': nrm(ks[17], (NL, HY_ORDER, W), 0.5),
        'wa_sink': nrm(ks[18], (NL, N_WA_HEADS), 0.5),
        'pool_w': nrm(ks[19], (NL, len(POOL_WINDOWS), POOL_GROUP, POOL_GROUP), POOL_GROUP ** -0.5),
        'pool_scale': 1.0 + nrm(ks[20], (NL, W), 0.1),
        'na_rpb': nrm(ks[21], (NL, N_NA_HEADS, 2 * NA_ROWS - 1, 2 * NA_COLS - 1), 0.2),
        'w_out': nrm(ks[22], (NL, MIX_WIDTH, D), MIX_WIDTH ** -0.5),
        'router_w': nrm(ks[23], (NL, D, E), D ** -0.5),
        'exp_w_gate': nrm(ks[24], (NL, E, D, F), D ** -0.5),
        'exp_w_up': nrm(ks[25], (NL, E, D, F), D ** -0.5),
        'exp_w_down': nrm(ks[26], (NL, E, F, D), F ** -0.5),
        'final_norm_g': 1.0 + nrm(ks[27], (D,), 0.05),
    }


def reference(x, c, ctx, c_ctx, ada_w, ada_b, norm1_g, norm2_g, w_in, hy_conv_w, hy_conv_b, hy_w1, hy_b1,
              hy_w2, hy_b2, hy_w3, hy_freq, hy_skip, wa_sink, pool_w, pool_scale, na_rpb, w_out, router_w,
              exp_w_gate, exp_w_up, exp_w_down, final_norm_g):
    for l in range(DEPTH):
        x, ctx = _layer(x, ctx, c, c_ctx, ada_w[l], ada_b[l], norm1_g[l], norm2_g[l], w_in[l],
                        hy_conv_w[l], hy_conv_b[l], hy_w1[l], hy_b1[l], hy_w2[l], hy_b2[l], hy_w3[l],
                        hy_freq[l], hy_skip[l], wa_sink[l], pool_w[l], pool_scale[l], na_rpb[l], w_out[l],
                        router_w[l], exp_w_gate[l], exp_w_up[l], exp_w_down[l],
                        update_ctx=(l < DEPTH - 1))
    return _rmsnorm(x, final_norm_g)
```

```python
import functools
import math

import jax
import jax.numpy as jnp
import numpy as np
from jax import lax
from jax.experimental import pallas as pl
from jax.experimental.pallas import tpu as pltpu

F32 = jnp.float32
BF16 = jnp.bfloat16

D_MODEL = 1024
DEPTH = 2
GRID_W = 64
HEAD_DIM = 64
W_GROUP = 256
N_WA_HEADS = 4
N_WA_KV = 2
N_NA_HEADS = 4
KV_WA = 128
OFF_POOL = 768
OFF_WA_Q = 1024
OFF_NA_Q = 1280
OFF_KV = 1536
IN_WIDTH = 2304
KV_WIDTH = IN_WIDTH - OFF_KV
HY_ORDER = 2
HY_BANDS = 16
HY_DECAY_MIN = abs(math.log(1e-2) / 1.5)
HY_DECAY_MAX = abs(math.log(1e-2) / 0.3)
WA_BLOCK = 128
POOL_WINDOWS = (2, 4, 8, 16)
POOL_GROUP = 64
POOL_HALO = 8
NA_ROWS = 8
NA_COLS = 16
ROPE_BASE = 10000.0
N_EXPERTS = 16
EXPERT_HIDDEN = 2048
EC_CAPACITY = 2
NORM_EPS = 1e-6
NEG_INF = -1e30
Q_SCALE = HEAD_DIM ** -0.5

MOD_ROWS = 16
VMEM_LIMIT = 48 * 1024 * 1024


def _cparams(sem, vmem=VMEM_LIMIT):
    return pltpu.CompilerParams(dimension_semantics=sem, vmem_limit_bytes=vmem)


def _nt_dot(a, b):
    return lax.dot_general(a, b, (((1,), (1,)), ((), ())), preferred_element_type=F32)


def _rmsnorm_mod(x, g, shift, scale):
    y = x * lax.rsqrt(jnp.mean(x * x, axis=-1, keepdims=True) + NORM_EPS) * g
    return y * (1.0 + scale) + shift


def _ada_kernel(c_ref, w_ref, b_ref, o_ref):
    c = c_ref[...]
    s = (c * jax.nn.sigmoid(c)).astype(BF16)
    o_ref[0] = jnp.dot(s, w_ref[0].astype(BF16), preferred_element_type=F32) + b_ref[0]


def _ada_mod(cs, ada_w, ada_b):
    nl, d, n = ada_w.shape
    tn = 1024
    return pl.pallas_call(
        _ada_kernel,
        out_shape=jax.ShapeDtypeStruct((nl, MOD_ROWS, n), F32),
        grid=(nl, n // tn),
        in_specs=[pl.BlockSpec((MOD_ROWS, d), lambda l, j: (0, 0)),
                  pl.BlockSpec((1, d, tn), lambda l, j: (l, 0, j)),
                  pl.BlockSpec((1, 1, tn), lambda l, j: (l, 0, j))],
        out_specs=pl.BlockSpec((1, MOD_ROWS, tn), lambda l, j: (l, 0, j)),
        compiler_params=_cparams(("parallel", "parallel")),
        name="ada_mod",
    )(cs, ada_w, ada_b)


def _rope(t, cos, s_up, s_dn):
    w = t.shape[-1]
    if w > 128:
        cos, s_up, s_dn = (jnp.concatenate([a] * (w // 128), axis=-1) for a in (cos, s_up, s_dn))
    up = pltpu.roll(t, w - 16, 1)
    dn = pltpu.roll(t, 16, 1)
    return t * cos + up * s_up + dn * s_dn


def _inproj_kernel(*refs, rope, kv_only):
    if rope:
        x_ref, sh_ref, sc_ref, g_ref, w_ref, cos_ref, sup_ref, sdn_ref, *outs = refs
    else:
        x_ref, sh_ref, sc_ref, g_ref, w_ref, *outs = refs
    h = _rmsnorm_mod(x_ref[0], g_ref[...], sh_ref[0], sc_ref[0])
    p = jnp.dot(h.astype(BF16), w_ref[...], preferred_element_type=F32)
    if kv_only:
        outs[0][0] = p.astype(BF16)
        return
    hy_ref, pool_ref, qwa_ref, qna_ref, kv_ref = outs
    hy_ref[0] = p[:, :OFF_POOL]
    pool_ref[0] = p[:, OFF_POOL:OFF_WA_Q]
    qwa = p[:, OFF_WA_Q:OFF_NA_Q]
    kwa = p[:, OFF_KV:OFF_KV + KV_WA]
    if rope:
        tabs = (cos_ref[...], sup_ref[...], sdn_ref[...])
        qwa = _rope(qwa, *tabs)
        kwa = _rope(kwa, *tabs)
    qwa_ref[0] = (qwa * Q_SCALE).astype(BF16)
    qna_ref[0] = (p[:, OFF_NA_Q:OFF_KV] * Q_SCALE).astype(BF16)
    kv_ref[0] = jnp.concatenate([kwa, p[:, OFF_KV + KV_WA:]], axis=-1).astype(BF16)


def _inproj(x, shift, scale, g, w_bf16, rope_tabs=None, kv_only=False):
    b, l, d = x.shape
    n = w_bf16.shape[1]
    tm = min(l, 512)
    rope = rope_tabs is not None
    bm = shift.shape[0]
    mod_map = (lambda j, i: (i, 0, 0)) if bm > 1 else (lambda j, i: (0, 0, 0))
    in_specs = [pl.BlockSpec((1, tm, d), lambda j, i: (i, j, 0)),
                pl.BlockSpec((1, 1, d), mod_map),
                pl.BlockSpec((1, 1, d), mod_map),
                pl.BlockSpec((1, d), lambda j, i: (0, 0)),
                pl.BlockSpec((d, n), lambda j, i: (0, 0))]
    args = [x, shift, scale, g, w_bf16]
    if rope:
        in_specs += [pl.BlockSpec((tm, 128), lambda j, i: (j, 0))] * 3
        args += list(rope_tabs)

    def tok(width, dtype):
        return (jax.ShapeDtypeStruct((b, l, width), dtype),
                pl.BlockSpec((1, tm, width), lambda j, i: (i, j, 0)))

    if kv_only:
        outs = [tok(n, BF16)]
    else:
        outs = [tok(OFF_POOL, F32), tok(W_GROUP, F32), tok(W_GROUP, BF16), tok(W_GROUP, BF16),
                tok(KV_WIDTH, BF16)]
    res = pl.pallas_call(
        functools.partial(_inproj_kernel, rope=rope, kv_only=kv_only),
        out_shape=[o[0] for o in outs],
        grid=(l // tm, b),
        in_specs=in_specs,
        out_specs=[o[1] for o in outs],
        compiler_params=_cparams(("parallel", "parallel")),
        name="inproj_kv" if kv_only else "inproj",
    )(*args)
    return res[0] if kv_only else res


def _rope_tables(s):
    pos = jnp.arange(s)
    p2 = jnp.stack([pos // GRID_W, pos % GRID_W], axis=-1).astype(F32)
    inv = ROPE_BASE ** (-jnp.arange(16, dtype=F32) / 16)
    lane = np.arange(HEAD_DIM)
    ang = p2[:, lane // 32] * inv[lane % 16][None, :]
    first = jnp.asarray((lane % 32) < 16)[None, :]
    cos, sin = jnp.cos(ang), jnp.sin(ang)
    s_up = jnp.where(first, -sin, 0.0)
    s_dn = jnp.where(first, 0.0, sin)
    return tuple(jnp.tile(t, (1, 2)) for t in (cos, s_up, s_dn))


def _softmax_parts(parts, extra=None):
    m = parts[0].max(axis=-1, keepdims=True)
    for s in parts[1:]:
        m = jnp.maximum(m, s.max(axis=-1, keepdims=True))
    if extra is not None:
        m = jnp.maximum(m, extra)
    ps = [jnp.exp(s - m) for s in parts]
    den = ps[0].sum(axis=-1, keepdims=True)
    for p in ps[1:]:
        den = den + p.sum(axis=-1, keepdims=True)
    if extra is not None:
        den = den + jnp.exp(extra - m)
    return ps, den


def _wattn_kernel(sink_ref, q_ref, kp_ref, ko_ref, kn_ref, vp_ref, vo_ref, vn_ref, kc_ref, vc_ref, o_ref, *, nb):
    n = pl.program_id(1)
    blk = WA_BLOCK
    q = q_ref[0]
    k3 = jnp.concatenate([kp_ref[0], ko_ref[0], kn_ref[0]], axis=0)
    v3 = jnp.concatenate([vp_ref[0], vo_ref[0], vn_ref[0]], axis=0)
    kc, vc = kc_ref[0], vc_ref[0]
    i = lax.broadcasted_iota(jnp.int32, (2 * blk, 3 * blk), 0) & (blk - 1)
    j = lax.broadcasted_iota(jnp.int32, (2 * blk, 3 * blk), 1)
    jlo = jnp.where(n == 0, blk, 0)
    jhi = jnp.where(n == nb - 1, 2 * blk, 3 * blk)
    valid = (j >= i) & (j <= i + 2 * blk) & (j >= jlo) & (j < jhi)
    row = lax.broadcasted_iota(jnp.int32, (2 * blk, 1), 0)
    outs = []
    for kh in range(N_WA_KV):
        sl = slice(kh * HEAD_DIM, (kh + 1) * HEAD_DIM)
        q2 = jnp.concatenate([q[:, (2 * kh) * HEAD_DIM:(2 * kh + 1) * HEAD_DIM],
                              q[:, (2 * kh + 1) * HEAD_DIM:(2 * kh + 2) * HEAD_DIM]], axis=0)
        s_loc = jnp.where(valid, _nt_dot(q2, k3[:, sl]), NEG_INF)
        s_ctx = _nt_dot(q2, kc[:, sl])
        snk = jnp.where(row < blk, sink_ref[2 * kh], sink_ref[2 * kh + 1])
        (p_loc, p_ctx), den = _softmax_parts([s_loc, s_ctx], snk)
        o = (jnp.dot(p_loc.astype(BF16), v3[:, sl], preferred_element_type=F32)
             + jnp.dot(p_ctx.astype(BF16), vc[:, sl], preferred_element_type=F32)) / den
        outs += [o[:blk], o[blk:]]
    o_ref[0] = jnp.concatenate(outs, axis=-1).astype(BF16)


def _window_attn(q, kv, ckv, sink):
    b, s, _ = q.shape
    lc = ckv.shape[1]
    nb = s // WA_BLOCK

    def kv_spec(col, off):
        return pl.BlockSpec((1, WA_BLOCK, KV_WA),
                            lambda i, n: (i, jnp.clip(n + off, 0, nb - 1), col))

    return pl.pallas_call(
        functools.partial(_wattn_kernel, nb=nb),
        out_shape=jax.ShapeDtypeStruct((b, s, W_GROUP), BF16),
        grid=(b, nb),
        in_specs=[pl.BlockSpec(memory_space=pltpu.SMEM),
                  pl.BlockSpec((1, WA_BLOCK, W_GROUP), lambda i, n: (i, n, 0)),
                  kv_spec(0, -1), kv_spec(0, 0), kv_spec(0, 1),
                  kv_spec(1, -1), kv_spec(1, 0), kv_spec(1, 1),
                  pl.BlockSpec((1, lc, KV_WA), lambda i, n: (i, 0, 0)),
                  pl.BlockSpec((1, lc, KV_WA), lambda i, n: (i, 0, 1))],
        out_specs=pl.BlockSpec((1, WA_BLOCK, W_GROUP), lambda i, n: (i, n, 0)),
        compiler_params=_cparams(("parallel", "parallel")),
        name="window_attn",
    )(sink, q, kv, kv, kv, kv, kv, kv, ckv, ckv)


def _nattn_kernel(q_ref, k_ref, v_ref, kc_ref, vc_ref, bias_ref, o_ref, *, rows):
    r = pl.program_id(1)
    r0 = jnp.clip(r - NA_ROWS // 2, 0, rows - NA_ROWS)
    var = r - r0
    start = pl.multiple_of(r0 * GRID_W, GRID_W)
    nk = NA_ROWS * GRID_W
    q = q_ref[0]
    kt = k_ref[0, pl.ds(start, nk), :]
    vt = v_ref[0, pl.ds(start, nk), :]
    kc, vc = kc_ref[0], vc_ref[0]
    outs = []
    for h in range(N_NA_HEADS):
        sl = slice(h * HEAD_DIM, (h + 1) * HEAD_DIM)
        qh = q[:, sl]
        s_loc = _nt_dot(qh, kt[:, sl]) + bias_ref[h, var]
        s_ctx = _nt_dot(qh, kc[:, sl])
        (p_loc, p_ctx), den = _softmax_parts([s_loc, s_ctx])
        o = (jnp.dot(p_loc.astype(BF16), vt[:, sl], preferred_element_type=F32)
             + jnp.dot(p_ctx.astype(BF16), vc[:, sl], preferred_element_type=F32)) / den
        outs.append(o)
    o_ref[0] = jnp.concatenate(outs, axis=-1).astype(BF16)


def _na_bias(rpb):
    var = np.arange(NA_ROWS)
    j = np.arange(NA_ROWS)
    qc = np.arange(GRID_W)
    kc = np.arange(GRID_W)
    dr = j[None, :] - var[:, None] + NA_ROWS - 1
    dc = np.clip(kc[None, :] - qc[:, None] + NA_COLS - 1, 0, 2 * NA_COLS - 2)
    ws = np.clip(qc - NA_COLS // 2, 0, GRID_W - NA_COLS)
    ok = (kc[None, :] >= ws[:, None]) & (kc[None, :] < ws[:, None] + NA_COLS)
    bias = rpb.astype(F32)[:, dr[:, None, :, None], dc[None, :, None, :]]
    bias = jnp.where(jnp.asarray(ok)[None, None, :, None, :], bias, NEG_INF)
    return bias.reshape(rpb.shape[0], NA_ROWS, GRID_W, NA_ROWS * GRID_W)


def _neighbourhood_attn(q, kv, ckv, rpb):
    b, s, _ = q.shape
    lc = ckv.shape[1]
    rows = s // GRID_W
    bias = _na_bias(rpb)
    return pl.pallas_call(
        functools.partial(_nattn_kernel, rows=rows),
        out_shape=jax.ShapeDtypeStruct((b, s, W_GROUP), BF16),
        grid=(b, rows),
        in_specs=[pl.BlockSpec((1, GRID_W, W_GROUP), lambda i, r: (i, r, 0)),
                  pl.BlockSpec((1, s, W_GROUP), lambda i, r: (i, 0, 1)),
                  pl.BlockSpec((1, s, W_GROUP), lambda i, r: (i, 0, 2)),
                  pl.BlockSpec((1, lc, W_GROUP), lambda i, r: (i, 0, 1)),
                  pl.BlockSpec((1, lc, W_GROUP), lambda i, r: (i, 0, 2)),
                  pl.BlockSpec(bias.shape, lambda i, r: (0, 0, 0, 0))],
        out_specs=pl.BlockSpec((1, GRID_W, W_GROUP), lambda i, r: (i, r, 0)),
        compiler_params=_cparams(("parallel", "arbitrary")),
        name="neighbourhood_attn",
    )(q, kv, kv, ckv, ckv, bias)


def _cattn_kernel(*refs, n_kv, with_sink):
    if with_sink:
        sink_ref, q_ref, k_ref, v_ref, o_ref = refs
    else:
        q_ref, k_ref, v_ref, o_ref = refs
    q, k, v = q_ref[0], k_ref[0], v_ref[0]
    group = N_WA_HEADS // n_kv
    outs = []
    for h in range(N_WA_HEADS):
        sl = slice((h // group) * HEAD_DIM, (h // group + 1) * HEAD_DIM)
        s = _nt_dot(q[:, h * HEAD_DIM:(h + 1) * HEAD_DIM], k[:, sl])
        extra = jnp.full((s.shape[0], 1), sink_ref[h], F32) if with_sink else None
        (p,), den = _softmax_parts([s], extra)
        outs.append(jnp.dot(p.astype(BF16), v[:, sl], preferred_element_type=F32) / den)
    o_ref[0] = jnp.concatenate(outs, axis=-1).astype(BF16)


def _ctx_attn(q, ckv, k_col, v_col, n_kv, sink):
    b, lc, _ = q.shape
    w = n_kv * HEAD_DIM
    with_sink = sink is not None
    in_specs = [pl.BlockSpec((1, lc, W_GROUP), lambda i: (i, 0, 0)),
                pl.BlockSpec((1, lc, w), lambda i: (i, 0, k_col)),
                pl.BlockSpec((1, lc, w), lambda i: (i, 0, v_col))]
    args = [q, ckv, ckv]
    if with_sink:
        in_specs = [pl.BlockSpec(memory_space=pltpu.SMEM)] + in_specs
        args = [sink] + args
    return pl.pallas_call(
        functools.partial(_cattn_kernel, n_kv=n_kv, with_sink=with_sink),
        out_shape=jax.ShapeDtypeStruct((b, lc, W_GROUP), BF16),
        grid=(b,),
        in_specs=in_specs,
        out_specs=pl.BlockSpec((1, lc, W_GROUP), lambda i: (i, 0, 0)),
        compiler_params=_cparams(("parallel",)),
        name="ctx_attn",
    )(*args)


def _pool_kernel(prev_ref, cur_ref, next_ref, w_ref, scale_ref, o_ref, *, seq, tl):
    j = pl.program_id(1)
    nt = seq // tl
    cur = cur_ref[0]
    prev = jnp.where(j == 0, 0.0, prev_ref[0])
    nxt = jnp.where(j == nt - 1, 0.0, next_ref[0])
    e = jnp.concatenate([prev, cur, nxt], axis=0)
    n = tl + 2 * POOL_HALO

    def sh(a, d):
        return pltpu.roll(a, d % n, 0)

    s2 = e + sh(e, 1)
    s4 = sh(s2, 1) + sh(s2, -1)
    s8 = sh(s4, 2) + sh(s4, -2)
    s16 = sh(s8, 4) + sh(s8, -4)
    lane = lax.broadcasted_iota(jnp.int32, (tl, W_GROUP), 1)
    t = lax.broadcasted_iota(jnp.int32, (tl, W_GROUP), 0) + j * tl
    g = lane // POOL_GROUP
    lo, hi = POOL_HALO, POOL_HALO + tl
    ssum = jnp.where(g == 0, s2[lo:hi], jnp.where(g == 1, s4[lo:hi], jnp.where(g == 2, s8[lo:hi], s16[lo:hi])))
    half = jnp.where(g == 0, 1, jnp.where(g == 1, 2, jnp.where(g == 2, 4, 8)))
    cnt = (jnp.minimum(t + half, seq) - jnp.maximum(t - half, 0)).astype(F32)
    d = (ssum / cnt - cur).astype(BF16)
    o_ref[0] = (jnp.dot(d, w_ref[...], preferred_element_type=F32) * scale_ref[...]).astype(BF16)


def _pool_mixer(u, w_bd, scale):
    b, l, c = u.shape
    tl = min(l, 512)
    hb = tl // POOL_HALO
    nh = l // POOL_HALO
    return pl.pallas_call(
        functools.partial(_pool_kernel, seq=l, tl=tl),
        out_shape=jax.ShapeDtypeStruct((b, l, c), BF16),
        grid=(b, l // tl),
        in_specs=[pl.BlockSpec((1, POOL_HALO, c), lambda i, j: (i, jnp.maximum(j * hb - 1, 0), 0)),
                  pl.BlockSpec((1, tl, c), lambda i, j: (i, j, 0)),
                  pl.BlockSpec((1, POOL_HALO, c), lambda i, j: (i, jnp.minimum((j + 1) * hb, nh - 1), 0)),
                  pl.BlockSpec((c, c), lambda i, j: (0, 0)),
                  pl.BlockSpec((1, c), lambda i, j: (0, 0))],
        out_specs=pl.BlockSpec((1, tl, c), lambda i, j: (i, j, 0)),
        compiler_params=_cparams(("parallel", "parallel")),
        name="pool_mixer",
    )(u, u, u, w_bd, scale)


def _pool_weight(pool_w):
    z = jnp.zeros((W_GROUP, W_GROUP), F32)
    for g in range(len(POOL_WINDOWS)):
        z = z.at[g * POOL_GROUP:(g + 1) * POOL_GROUP, g * POOL_GROUP:(g + 1) * POOL_GROUP].set(pool_w[g])
    return z.astype(BF16)


def _outproj_kernel(yh_ref, ya_ref, yp_ref, yn_ref, x_ref, gt_ref, w_ref, g2_ref, sh_ref, sc_ref, rw_ref,
                    xo_ref, h_ref, aff_ref):
    y = jnp.concatenate([yh_ref[0], ya_ref[0], yp_ref[0], yn_ref[0]], axis=-1)
    x = x_ref[0] + gt_ref[0] * jnp.dot(y, w_ref[...], preferred_element_type=F32)
    xo_ref[0] = x
    h = _rmsnorm_mod(x, g2_ref[...], sh_ref[0], sc_ref[0]).astype(BF16)
    h_ref[0] = h
    logits = _nt_dot(rw_ref[...], h)
    m = logits.max(axis=0, keepdims=True)
    p = jnp.exp(logits - m)
    aff_ref[0] = p / p.sum(axis=0, keepdims=True)


def _outproj(ys, x, gate, w_bf16, g2, shift, scale, rw_t):
    b, l, d = x.shape
    tm = min(l, 512)
    bm = gate.shape[0]
    mod_map = (lambda i, j: (i, 0, 0)) if bm > 1 else (lambda i, j: (0, 0, 0))
    tok = lambda w: pl.BlockSpec((1, tm, w), lambda i, j: (i, j, 0))
    mod = pl.BlockSpec((1, 1, d), mod_map)
    return pl.pallas_call(
        _outproj_kernel,
        out_shape=[jax.ShapeDtypeStruct((b, l, d), F32), jax.ShapeDtypeStruct((b, l, d), BF16),
                   jax.ShapeDtypeStruct((b, N_EXPERTS, l), F32)],
        grid=(b, l // tm),
        in_specs=[tok(W_GROUP)] * 4 + [tok(d), mod, pl.BlockSpec((d, d), lambda i, j: (0, 0)),
                                        pl.BlockSpec((1, d), lambda i, j: (0, 0)), mod, mod,
                                        pl.BlockSpec((N_EXPERTS, d), lambda i, j: (0, 0))],
        out_specs=[tok(d), tok(d), pl.BlockSpec((1, N_EXPERTS, tm), lambda i, j: (i, 0, j))],
        compiler_params=_cparams(("parallel", "parallel")),
        name="outproj_router",
    )(*ys, x, gate, w_bf16, g2, shift, scale, rw_t)


def _moe_kernel(x_ref, g_ref, wg_ref, wu_ref, wd_ref, o_ref):
    f = pl.program_id(2)
    x = x_ref[0]
    a = jnp.dot(x, wg_ref[0].astype(BF16), preferred_element_type=F32)
    u = jnp.dot(x, wu_ref[0].astype(BF16), preferred_element_type=F32)
    hid = (a * jax.nn.sigmoid(a) * u).astype(BF16)
    part = jnp.dot(hid, wd_ref[0].astype(BF16), preferred_element_type=F32)

    @pl.when(f == 0)
    def _():
        o_ref[0] = part

    @pl.when(f != 0)
    def _():
        o_ref[0] += part

    @pl.when(f == pl.num_programs(2) - 1)
    def _():
        o_ref[0] = o_ref[0] * g_ref[0]


def _moe_ffn(xg, gate, w_gate, w_up, w_down):
    e, m, d = xg.shape
    f = w_gate.shape[2]
    tm = min(m, 1024)
    tf = 512
    return pl.pallas_call(
        _moe_kernel,
        out_shape=jax.ShapeDtypeStruct((e, m, d), F32),
        grid=(e, m // tm, f // tf),
        in_specs=[pl.BlockSpec((1, tm, d), lambda i, j, k: (i, j, 0)),
                  pl.BlockSpec((1, tm, 1), lambda i, j, k: (i, j, 0)),
                  pl.BlockSpec((1, d, tf), lambda i, j, k: (i, 0, k)),
                  pl.BlockSpec((1, d, tf), lambda i, j, k: (i, 0, k)),
                  pl.BlockSpec((1, tf, d), lambda i, j, k: (i, k, 0))],
        out_specs=pl.BlockSpec((1, tm, d), lambda i, j, k: (i, j, 0)),
        compiler_params=_cparams(("parallel", "parallel", "arbitrary")),
        name="moe_ffn",
    )(xg, gate, w_gate, w_up, w_down)


def _expert_choice_ffn(h, aff_t, w_gate, w_up, w_down):
    b, l, d = h.shape
    cap = EC_CAPACITY * l // N_EXPERTS
    g, idx = lax.top_k(aff_t, cap)
    idx_t = jnp.swapaxes(idx, 0, 1)
    bidx = jnp.arange(b)[None, :, None]
    xg = h[bidx, idx_t].reshape(N_EXPERTS, b * cap, d)
    gate = jnp.swapaxes(g, 0, 1).reshape(N_EXPERTS, b * cap, 1)
    o = _moe_ffn(xg, gate, w_gate, w_up, w_down).reshape(N_EXPERTS, b, cap, d)
    return jnp.zeros((b, l, d), F32).at[bidx, idx_t].add(o)


def _combine_kernel(*refs, final):
    if final:
        x_ref, m_ref, gt_ref, g_ref, o_ref = refs
    else:
        x_ref, m_ref, gt_ref, o_ref = refs
    x = x_ref[0] + gt_ref[0] * m_ref[0]
    if final:
        x = x * lax.rsqrt(jnp.mean(x * x, axis=-1, keepdims=True) + NORM_EPS) * g_ref[...]
    o_ref[0] = x


def _combine(x, moe, gate, final_g=None):
    b, l, d = x.shape
    tm = min(l, 512)
    bm = gate.shape[0]
    mod_map = (lambda i, j: (i, 0, 0)) if bm > 1 else (lambda i, j: (0, 0, 0))
    tok = pl.BlockSpec((1, tm, d), lambda i, j: (i, j, 0))
    final = final_g is not None
    in_specs = [tok, tok, pl.BlockSpec((1, 1, d), mod_map)]
    args = [x, moe, gate]
    if final:
        in_specs.append(pl.BlockSpec((1, d), lambda i, j: (0, 0)))
        args.append(final_g)
    return pl.pallas_call(
        functools.partial(_combine_kernel, final=final),
        out_shape=jax.ShapeDtypeStruct((b, l, d), F32),
        grid=(b, l // tm),
        in_specs=in_specs,
        out_specs=tok,
        compiler_params=_cparams(("parallel", "parallel")),
        name="combine_final" if final else "combine",
    )(*args)


def _hyena_filters(l, w1, b1, w2, b2, w3, freq):
    hp = lax.Precision.HIGHEST
    pos = jnp.arange(l, dtype=F32)
    t = pos / max(l - 1, 1)
    bands = jnp.linspace(1e-4, HY_BANDS - 1, HY_BANDS, dtype=F32)
    ang = (2 * math.pi / l) * pos[:, None] * bands[None, :]
    feats = jnp.concatenate([t[:, None], jnp.cos(ang), -jnp.sin(ang)], axis=-1)
    h = jnp.sin(freq * (jnp.dot(feats, w1, precision=hp) + b1))
    h = jnp.sin(freq * (jnp.dot(h, w2, precision=hp) + b2))
    h = jnp.dot(h, w3, precision=hp).reshape(l, HY_ORDER, 2, W_GROUP)
    deltas = jnp.linspace(HY_DECAY_MIN, HY_DECAY_MAX, W_GROUP, dtype=F32)
    window = jnp.exp(-t[:, None] * deltas[None, :])
    return h * window[:, None, None, :]


def _two_sided(h_fwd, h_bwd):
    lag0 = h_fwd[:1] + h_bwd[:1]
    return jnp.concatenate([lag0, h_fwd[1:], jnp.zeros_like(h_fwd[:1]), h_bwd[:0:-1]], axis=0)


def _fft_conv(z, filt, skip):
    l = z.shape[1]
    zf = jnp.fft.rfft(z, n=2 * l, axis=1)
    ff = jnp.fft.rfft(filt, n=2 * l, axis=0)
    return jnp.fft.irfft(zf * ff[None], n=2 * l, axis=1)[:, :l] + z * skip


def _hyena(u, conv_w, conv_b, filt, skip):
    up = jnp.pad(u, ((0, 0), (1, 1), (0, 0)))
    uc = up[:, :-2] * conv_w[0] + up[:, 1:-1] * conv_w[1] + up[:, 2:] * conv_w[2] + conv_b
    v, x1, x2 = jnp.split(uc, 3, axis=-1)
    z = x1 * _fft_conv(v, _two_sided(filt[:, 0, 0], filt[:, 0, 1]), skip[0])
    z = x2 * _fft_conv(z, _two_sided(filt[:, 1, 0], filt[:, 1, 1]), skip[1])
    return z.astype(BF16)


def _layer(x, ctx, mod, lyr, rope_tabs, update_ctx, final_g):
    b, s, d = x.shape
    lc = ctx.shape[1]
    mx = [mod[:b, None, i * d:(i + 1) * d] for i in range(6)]
    mc = [mod[b:b + 1, None, i * d:(i + 1) * d] for i in range(6)]
    g1, g2 = lyr["g1"][None, :], lyr["g2"][None, :]
    w_in = lyr["w_in"].astype(BF16)
    w_out = lyr["w_out"].astype(BF16)
    rw_t = lyr["router_w"].T.astype(BF16)
    w_bd = _pool_weight(lyr["pool_w"])
    pscale = lyr["pool_scale"][None, :]
    hy_args = (lyr["hy_w1"], lyr["hy_b1"], lyr["hy_w2"], lyr["hy_b2"], lyr["hy_w3"], lyr["hy_freq"])

    u_hy, u_pool, q_wa, q_na, kv = _inproj(x, mx[0], mx[1], g1, w_in, rope_tabs)
    if update_ctx:
        cu_hy, cu_pool, cq_wa, cq_na, ckv = _inproj(ctx, mc[0], mc[1], g1, w_in)
    else:
        ckv = _inproj(ctx, mc[0], mc[1], g1, w_in[:, OFF_KV:], kv_only=True)

    ys = [_hyena(u_hy, lyr["hy_conv_w"], lyr["hy_conv_b"], _hyena_filters(s, *hy_args), lyr["hy_skip"]),
          _window_attn(q_wa, kv, ckv, lyr["wa_sink"]),
          _pool_mixer(u_pool, w_bd, pscale),
          _neighbourhood_attn(q_na, kv, ckv, lyr["na_rpb"])]
    x, h, aff = _outproj(ys, x, mx[2], w_out, g2, mx[3], mx[4], rw_t)
    moe = _expert_choice_ffn(h, aff, lyr["exp_w_gate"], lyr["exp_w_up"], lyr["exp_w_down"])
    x = _combine(x, moe, mx[5], final_g)

    if update_ctx:
        ycs = [_hyena(cu_hy, lyr["hy_conv_w"], lyr["hy_conv_b"], _hyena_filters(lc, *hy_args), lyr["hy_skip"]),
               _ctx_attn(cq_wa, ckv, 0, 1, N_WA_KV, lyr["wa_sink"]),
               _pool_mixer(cu_pool, w_bd, pscale),
               _ctx_attn(cq_na, ckv, 1, 2, N_NA_HEADS, None)]
        ctx, hc, affc = _outproj(ycs, ctx, mc[2], w_out, g2, mc[3], mc[4], rw_t)
        moe_c = _expert_choice_ffn(hc, affc, lyr["exp_w_gate"], lyr["exp_w_up"], lyr["exp_w_down"])
        ctx = _combine(ctx, moe_c, mc[5])
    return x, ctx


def kernel(x, c, ctx, c_ctx, ada_w, ada_b, norm1_g, norm2_g, w_in, hy_conv_w, hy_conv_b, hy_w1, hy_b1, hy_w2,
           hy_b2, hy_w3, hy_freq, hy_skip, wa_sink, pool_w, pool_scale, na_rpb, w_out, router_w, exp_w_gate,
           exp_w_up, exp_w_down, final_norm_g):
    b, s, d = x.shape
    cs = jnp.zeros((MOD_ROWS, d), F32).at[:b].set(c).at[b].set(c_ctx)
    mods = _ada_mod(cs, ada_w, ada_b[:, None, :])
    rope_tabs = _rope_tables(s)
    params = dict(g1=norm1_g, g2=norm2_g, w_in=w_in, hy_conv_w=hy_conv_w, hy_conv_b=hy_conv_b, hy_w1=hy_w1,
                  hy_b1=hy_b1, hy_w2=hy_w2, hy_b2=hy_b2, hy_w3=hy_w3, hy_freq=hy_freq, hy_skip=hy_skip,
                  wa_sink=wa_sink, pool_w=pool_w, pool_scale=pool_scale, na_rpb=na_rpb, w_out=w_out,
                  router_w=router_w, exp_w_gate=exp_w_gate, exp_w_up=exp_w_up, exp_w_down=exp_w_down)
    for l in range(DEPTH):
        lyr = {k: v[l] for k, v in params.items()}
        last = l == DEPTH - 1
        x, ctx = _layer(x, ctx, mods[l], lyr, rope_tabs, update_ctx=not last,
                        final_g=final_norm_g[None, :] if last else None)
    return x
```

```python
import functools
import math

import jax
import jax.numpy as jnp
import numpy as np
from jax import lax
from jax.experimental import pallas as pl
from jax.experimental.pallas import tpu as pltpu

F32 = jnp.float32
BF16 = jnp.bfloat16

D_MODEL = 1024
DEPTH = 2
GRID_W = 64
HEAD_DIM = 64
W_GROUP = 256
N_WA_HEADS = 4
N_WA_KV = 2
N_NA_HEADS = 4
KV_WA = 128
OFF_POOL = 768
OFF_WA_Q = 1024
OFF_NA_Q = 1280
OFF_KV = 1536
IN_WIDTH = 2304
KV_WIDTH = IN_WIDTH - OFF_KV
HY_ORDER = 2
HY_BANDS = 16
HY_DECAY_MIN = abs(math.log(1e-2) / 1.5)
HY_DECAY_MAX = abs(math.log(1e-2) / 0.3)
WA_BLOCK = 128
POOL_WINDOWS = (2, 4, 8, 16)
POOL_GROUP = 64
HALO = 8
LANES = 128
DFT_N2 = 128
X_PITCH = DFT_N2 + 8
S_PITCH = 64 + 8
Y_PITCH = 32 + 8
NA_ROWS = 8
NA_COLS = 16
ROPE_BASE = 10000.0
N_EXPERTS = 16
EXPERT_HIDDEN = 2048
EC_CAPACITY = 2
NORM_EPS = 1e-6
NEG_INF = -1e30
Q_SCALE = HEAD_DIM ** -0.5

MOD_ROWS = 16
VMEM_LIMIT = 48 * 1024 * 1024


def _cparams(sem, vmem=VMEM_LIMIT):
    return pltpu.CompilerParams(dimension_semantics=sem, vmem_limit_bytes=vmem)


def _nt_dot(a, b):
    return lax.dot_general(a, b, (((1,), (1,)), ((), ())), preferred_element_type=F32)


def _rmsnorm_mod(x, g, shift, scale):
    y = x * lax.rsqrt(jnp.mean(x * x, axis=-1, keepdims=True) + NORM_EPS) * g
    return y * (1.0 + scale) + shift


def _ada_kernel(c_ref, w_ref, b_ref, o_ref):
    c = c_ref[...]
    s = (c * jax.nn.sigmoid(c)).astype(BF16)
    o_ref[0] = jnp.dot(s, w_ref[0].astype(BF16), preferred_element_type=F32) + b_ref[0]


def _ada_mod(cs, ada_w, ada_b):
    nl, d, n = ada_w.shape
    tn = 1024
    return pl.pallas_call(
        _ada_kernel,
        out_shape=jax.ShapeDtypeStruct((nl, MOD_ROWS, n), F32),
        grid=(nl, n // tn),
        in_specs=[pl.BlockSpec((MOD_ROWS, d), lambda l, j: (0, 0)),
                  pl.BlockSpec((1, d, tn), lambda l, j: (l, 0, j)),
                  pl.BlockSpec((1, 1, tn), lambda l, j: (l, 0, j))],
        out_specs=pl.BlockSpec((1, MOD_ROWS, tn), lambda l, j: (l, 0, j)),
        compiler_params=_cparams(("parallel", "parallel")),
        name="ada_mod",
    )(cs, ada_w, ada_b)


def _rope(t, cos, s_up, s_dn):
    w = t.shape[-1]
    if w > 128:
        cos, s_up, s_dn = (jnp.concatenate([a] * (w // 128), axis=-1) for a in (cos, s_up, s_dn))
    up = pltpu.roll(t, w - 16, 1)
    dn = pltpu.roll(t, 16, 1)
    return t * cos + up * s_up + dn * s_dn


def _inproj_kernel(*refs, rope, kv_only):
    if rope:
        x_ref, sh_ref, sc_ref, g_ref, w_ref, cos_ref, sup_ref, sdn_ref, *outs = refs
    else:
        x_ref, sh_ref, sc_ref, g_ref, w_ref, *outs = refs
    h = _rmsnorm_mod(x_ref[0], g_ref[...], sh_ref[0], sc_ref[0])
    p = jnp.dot(h.astype(BF16), w_ref[...], preferred_element_type=F32)
    if kv_only:
        outs[0][0] = p.astype(BF16)
        return
    hy_ref, pool_ref, qwa_ref, qna_ref, kv_ref = outs
    hy_ref[0] = p[:, :OFF_POOL]
    pool_ref[0] = p[:, OFF_POOL:OFF_WA_Q]
    qwa = p[:, OFF_WA_Q:OFF_NA_Q]
    kwa = p[:, OFF_KV:OFF_KV + KV_WA]
    if rope:
        tabs = (cos_ref[...], sup_ref[...], sdn_ref[...])
        qwa = _rope(qwa, *tabs)
        kwa = _rope(kwa, *tabs)
    qwa_ref[0] = (qwa * Q_SCALE).astype(BF16)
    qna_ref[0] = (p[:, OFF_NA_Q:OFF_KV] * Q_SCALE).astype(BF16)
    kv_ref[0] = jnp.concatenate([kwa, p[:, OFF_KV + KV_WA:]], axis=-1).astype(BF16)


def _inproj(x, shift, scale, g, w_bf16, rope_tabs=None, kv_only=False):
    b, l, d = x.shape
    n = w_bf16.shape[1]
    tm = min(l, 512)
    rope = rope_tabs is not None
    bm = shift.shape[0]
    mod_map = (lambda j, i: (i, 0, 0)) if bm > 1 else (lambda j, i: (0, 0, 0))
    in_specs = [pl.BlockSpec((1, tm, d), lambda j, i: (i, j, 0)),
                pl.BlockSpec((1, 1, d), mod_map),
                pl.BlockSpec((1, 1, d), mod_map),
                pl.BlockSpec((1, d), lambda j, i: (0, 0)),
                pl.BlockSpec((d, n), lambda j, i: (0, 0))]
    args = [x, shift, scale, g, w_bf16]
    if rope:
        in_specs += [pl.BlockSpec((tm, 128), lambda j, i: (j, 0))] * 3
        args += list(rope_tabs)

    def tok(width, dtype):
        return (jax.ShapeDtypeStruct((b, l, width), dtype),
                pl.BlockSpec((1, tm, width), lambda j, i: (i, j, 0)))

    if kv_only:
        outs = [tok(n, BF16)]
    else:
        outs = [tok(OFF_POOL, F32), tok(W_GROUP, F32), tok(W_GROUP, BF16), tok(W_GROUP, BF16),
                tok(KV_WIDTH, BF16)]
    res = pl.pallas_call(
        functools.partial(_inproj_kernel, rope=rope, kv_only=kv_only),
        out_shape=[o[0] for o in outs],
        grid=(l // tm, b),
        in_specs=in_specs,
        out_specs=[o[1] for o in outs],
        compiler_params=_cparams(("parallel", "parallel")),
        name="inproj_kv" if kv_only else "inproj",
    )(*args)
    return res[0] if kv_only else res


def _rope_tables(s):
    pos = jnp.arange(s)
    p2 = jnp.stack([pos // GRID_W, pos % GRID_W], axis=-1).astype(F32)
    inv = ROPE_BASE ** (-jnp.arange(16, dtype=F32) / 16)
    lane = np.arange(HEAD_DIM)
    ang = p2[:, lane // 32] * inv[lane % 16][None, :]
    first = jnp.asarray((lane % 32) < 16)[None, :]
    cos, sin = jnp.cos(ang), jnp.sin(ang)
    s_up = jnp.where(first, -sin, 0.0)
    s_dn = jnp.where(first, 0.0, sin)
    return tuple(jnp.tile(t, (1, 2)) for t in (cos, s_up, s_dn))


def _softmax_parts(parts, extra=None):
    m = parts[0].max(axis=-1, keepdims=True)
    for s in parts[1:]:
        m = jnp.maximum(m, s.max(axis=-1, keepdims=True))
    if extra is not None:
        m = jnp.maximum(m, extra)
    ps = [jnp.exp(s - m) for s in parts]
    den = ps[0].sum(axis=-1, keepdims=True)
    for p in ps[1:]:
        den = den + p.sum(axis=-1, keepdims=True)
    if extra is not None:
        den = den + jnp.exp(extra - m)
    return ps, den


def _wattn_kernel(sink_ref, q_ref, kp_ref, ko_ref, kn_ref, vp_ref, vo_ref, vn_ref, kc_ref, vc_ref, o_ref, *, nb):
    n = pl.program_id(1)
    blk = WA_BLOCK
    q = q_ref[0]
    k3 = jnp.concatenate([kp_ref[0], ko_ref[0], kn_ref[0]], axis=0)
    v3 = jnp.concatenate([vp_ref[0], vo_ref[0], vn_ref[0]], axis=0)
    kc, vc = kc_ref[0], vc_ref[0]
    i = lax.broadcasted_iota(jnp.int32, (2 * blk, 3 * blk), 0) & (blk - 1)
    j = lax.broadcasted_iota(jnp.int32, (2 * blk, 3 * blk), 1)
    jlo = jnp.where(n == 0, blk, 0)
    jhi = jnp.where(n == nb - 1, 2 * blk, 3 * blk)
    valid = (j >= i) & (j <= i + 2 * blk) & (j >= jlo) & (j < jhi)
    row = lax.broadcasted_iota(jnp.int32, (2 * blk, 1), 0)
    outs = []
    for kh in range(N_WA_KV):
        sl = slice(kh * HEAD_DIM, (kh + 1) * HEAD_DIM)
        q2 = jnp.concatenate([q[:, (2 * kh) * HEAD_DIM:(2 * kh + 1) * HEAD_DIM],
                              q[:, (2 * kh + 1) * HEAD_DIM:(2 * kh + 2) * HEAD_DIM]], axis=0)
        s_loc = jnp.where(valid, _nt_dot(q2, k3[:, sl]), NEG_INF)
        s_ctx = _nt_dot(q2, kc[:, sl])
        snk = jnp.where(row < blk, sink_ref[2 * kh], sink_ref[2 * kh + 1])
        (p_loc, p_ctx), den = _softmax_parts([s_loc, s_ctx], snk)
        o = (jnp.dot(p_loc.astype(BF16), v3[:, sl], preferred_element_type=F32)
             + jnp.dot(p_ctx.astype(BF16), vc[:, sl], preferred_element_type=F32)) / den
        outs += [o[:blk], o[blk:]]
    o_ref[0] = jnp.concatenate(outs, axis=-1).astype(BF16)


def _window_attn(q, kv, ckv, sink):
    b, s, _ = q.shape
    lc = ckv.shape[1]
    nb = s // WA_BLOCK

    def kv_spec(col, off):
        return pl.BlockSpec((1, WA_BLOCK, KV_WA),
                            lambda i, n: (i, jnp.clip(n + off, 0, nb - 1), col))

    return pl.pallas_call(
        functools.partial(_wattn_kernel, nb=nb),
        out_shape=jax.ShapeDtypeStruct((b, s, W_GROUP), BF16),
        grid=(b, nb),
        in_specs=[pl.BlockSpec(memory_space=pltpu.SMEM),
                  pl.BlockSpec((1, WA_BLOCK, W_GROUP), lambda i, n: (i, n, 0)),
                  kv_spec(0, -1), kv_spec(0, 0), kv_spec(0, 1),
                  kv_spec(1, -1), kv_spec(1, 0), kv_spec(1, 1),
                  pl.BlockSpec((1, lc, KV_WA), lambda i, n: (i, 0, 0)),
                  pl.BlockSpec((1, lc, KV_WA), lambda i, n: (i, 0, 1))],
        out_specs=pl.BlockSpec((1, WA_BLOCK, W_GROUP), lambda i, n: (i, n, 0)),
        compiler_params=_cparams(("parallel", "parallel")),
        name="window_attn",
    )(sink, q, kv, kv, kv, kv, kv, kv, ckv, ckv)


def _nattn_kernel(q_ref, k_ref, v_ref, kc_ref, vc_ref, bias_ref, o_ref, *, rows):
    r = pl.program_id(1)
    r0 = jnp.clip(r - NA_ROWS // 2, 0, rows - NA_ROWS)
    var = r - r0
    start = pl.multiple_of(r0 * GRID_W, GRID_W)
    nk = NA_ROWS * GRID_W
    q = q_ref[0]
    kt = k_ref[0, pl.ds(start, nk), :]
    vt = v_ref[0, pl.ds(start, nk), :]
    kc, vc = kc_ref[0], vc_ref[0]
    outs = []
    for h in range(N_NA_HEADS):
        sl = slice(h * HEAD_DIM, (h + 1) * HEAD_DIM)
        qh = q[:, sl]
        s_loc = _nt_dot(qh, kt[:, sl]) + bias_ref[h, var]
        s_ctx = _nt_dot(qh, kc[:, sl])
        (p_loc, p_ctx), den = _softmax_parts([s_loc, s_ctx])
        o = (jnp.dot(p_loc.astype(BF16), vt[:, sl], preferred_element_type=F32)
             + jnp.dot(p_ctx.astype(BF16), vc[:, sl], preferred_element_type=F32)) / den
        outs.append(o)
    o_ref[0] = jnp.concatenate(outs, axis=-1).astype(BF16)


def _na_bias(rpb):
    var = np.arange(NA_ROWS)
    j = np.arange(NA_ROWS)
    qc = np.arange(GRID_W)
    kc = np.arange(GRID_W)
    dr = j[None, :] - var[:, None] + NA_ROWS - 1
    dc = np.clip(kc[None, :] - qc[:, None] + NA_COLS - 1, 0, 2 * NA_COLS - 2)
    ws = np.clip(qc - NA_COLS // 2, 0, GRID_W - NA_COLS)
    ok = (kc[None, :] >= ws[:, None]) & (kc[None, :] < ws[:, None] + NA_COLS)
    bias = rpb.astype(F32)[:, dr[:, None, :, None], dc[None, :, None, :]]
    bias = jnp.where(jnp.asarray(ok)[None, None, :, None, :], bias, NEG_INF)
    return bias.reshape(rpb.shape[0], NA_ROWS, GRID_W, NA_ROWS * GRID_W)


def _neighbourhood_attn(q, kv, ckv, rpb):
    b, s, _ = q.shape
    lc = ckv.shape[1]
    rows = s // GRID_W
    bias = _na_bias(rpb)
    return pl.pallas_call(
        functools.partial(_nattn_kernel, rows=rows),
        out_shape=jax.ShapeDtypeStruct((b, s, W_GROUP), BF16),
        grid=(b, rows),
        in_specs=[pl.BlockSpec((1, GRID_W, W_GROUP), lambda i, r: (i, r, 0)),
                  pl.BlockSpec((1, s, W_GROUP), lambda i, r: (i, 0, 1)),
                  pl.BlockSpec((1, s, W_GROUP), lambda i, r: (i, 0, 2)),
                  pl.BlockSpec((1, lc, W_GROUP), lambda i, r: (i, 0, 1)),
                  pl.BlockSpec((1, lc, W_GROUP), lambda i, r: (i, 0, 2)),
                  pl.BlockSpec(bias.shape, lambda i, r: (0, 0, 0, 0))],
        out_specs=pl.BlockSpec((1, GRID_W, W_GROUP), lambda i, r: (i, r, 0)),
        compiler_params=_cparams(("parallel", "arbitrary")),
        name="neighbourhood_attn",
    )(q, kv, kv, ckv, ckv, bias)


def _cattn_kernel(*refs, n_kv, with_sink):
    if with_sink:
        sink_ref, q_ref, k_ref, v_ref, o_ref = refs
    else:
        q_ref, k_ref, v_ref, o_ref = refs
    q, k, v = q_ref[0], k_ref[0], v_ref[0]
    group = N_WA_HEADS // n_kv
    outs = []
    for h in range(N_WA_HEADS):
        sl = slice((h // group) * HEAD_DIM, (h // group + 1) * HEAD_DIM)
        s = _nt_dot(q[:, h * HEAD_DIM:(h + 1) * HEAD_DIM], k[:, sl])
        extra = jnp.full((s.shape[0], 1), sink_ref[h], F32) if with_sink else None
        (p,), den = _softmax_parts([s], extra)
        outs.append(jnp.dot(p.astype(BF16), v[:, sl], preferred_element_type=F32) / den)
    o_ref[0] = jnp.concatenate(outs, axis=-1).astype(BF16)


def _ctx_attn(q, ckv, k_col, v_col, n_kv, sink):
    b, lc, _ = q.shape
    w = n_kv * HEAD_DIM
    with_sink = sink is not None
    in_specs = [pl.BlockSpec((1, lc, W_GROUP), lambda i: (i, 0, 0)),
                pl.BlockSpec((1, lc, w), lambda i: (i, 0, k_col)),
                pl.BlockSpec((1, lc, w), lambda i: (i, 0, v_col))]
    args = [q, ckv, ckv]
    if with_sink:
        in_specs = [pl.BlockSpec(memory_space=pltpu.SMEM)] + in_specs
        args = [sink] + args
    return pl.pallas_call(
        functools.partial(_cattn_kernel, n_kv=n_kv, with_sink=with_sink),
        out_shape=jax.ShapeDtypeStruct((b, lc, W_GROUP), BF16),
        grid=(b,),
        in_specs=in_specs,
        out_specs=pl.BlockSpec((1, lc, W_GROUP), lambda i: (i, 0, 0)),
        compiler_params=_cparams(("parallel",)),
        name="ctx_attn",
    )(*args)


def _pool_kernel(prev_ref, cur_ref, next_ref, w_ref, scale_ref, o_ref, *, seq, tl):
    j = pl.program_id(1)
    nt = seq // tl
    cur = cur_ref[0]
    prev = jnp.where(j == 0, 0.0, prev_ref[0])
    nxt = jnp.where(j == nt - 1, 0.0, next_ref[0])
    e = jnp.concatenate([prev, cur, nxt], axis=0)
    n = tl + 2 * HALO

    def sh(a, d):
        return pltpu.roll(a, d % n, 0)

    s2 = e + sh(e, 1)
    s4 = sh(s2, 1) + sh(s2, -1)
    s8 = sh(s4, 2) + sh(s4, -2)
    s16 = sh(s8, 4) + sh(s8, -4)
    lane = lax.broadcasted_iota(jnp.int32, (tl, W_GROUP), 1)
    t = lax.broadcasted_iota(jnp.int32, (tl, W_GROUP), 0) + j * tl
    g = lane // POOL_GROUP
    lo, hi = HALO, HALO + tl
    ssum = jnp.where(g == 0, s2[lo:hi], jnp.where(g == 1, s4[lo:hi], jnp.where(g == 2, s8[lo:hi], s16[lo:hi])))
    half = jnp.where(g == 0, 1, jnp.where(g == 1, 2, jnp.where(g == 2, 4, 8)))
    cnt = (jnp.minimum(t + half, seq) - jnp.maximum(t - half, 0)).astype(F32)
    d = (ssum / cnt - cur).astype(BF16)
    o_ref[0] = (jnp.dot(d, w_ref[...], preferred_element_type=F32) * scale_ref[...]).astype(BF16)


def _pool_mixer(u, w_bd, scale):
    b, l, c = u.shape
    tl = min(l, 512)
    hb = tl // HALO
    nh = l // HALO
    return pl.pallas_call(
        functools.partial(_pool_kernel, seq=l, tl=tl),
        out_shape=jax.ShapeDtypeStruct((b, l, c), BF16),
        grid=(b, l // tl),
        in_specs=[pl.BlockSpec((1, HALO, c), lambda i, j: (i, jnp.maximum(j * hb - 1, 0), 0)),
                  pl.BlockSpec((1, tl, c), lambda i, j: (i, j, 0)),
                  pl.BlockSpec((1, HALO, c), lambda i, j: (i, jnp.minimum((j + 1) * hb, nh - 1), 0)),
                  pl.BlockSpec((c, c), lambda i, j: (0, 0)),
                  pl.BlockSpec((1, c), lambda i, j: (0, 0))],
        out_specs=pl.BlockSpec((1, tl, c), lambda i, j: (i, j, 0)),
        compiler_params=_cparams(("parallel", "parallel")),
        name="pool_mixer",
    )(u, u, u, w_bd, scale)


def _pool_weight(pool_w):
    z = jnp.zeros((W_GROUP, W_GROUP), F32)
    for g in range(len(POOL_WINDOWS)):
        z = z.at[g * POOL_GROUP:(g + 1) * POOL_GROUP, g * POOL_GROUP:(g + 1) * POOL_GROUP].set(pool_w[g])
    return z.astype(BF16)


def _outproj_kernel(yh_ref, ya_ref, yp_ref, yn_ref, x_ref, gt_ref, w_ref, g2_ref, sh_ref, sc_ref, rw_ref,
                    xo_ref, h_ref, aff_ref):
    y = jnp.concatenate([yh_ref[0], ya_ref[0], yp_ref[0], yn_ref[0]], axis=-1)
    x = x_ref[0] + gt_ref[0] * jnp.dot(y, w_ref[...], preferred_element_type=F32)
    xo_ref[0] = x
    h = _rmsnorm_mod(x, g2_ref[...], sh_ref[0], sc_ref[0]).astype(BF16)
    h_ref[0] = h
    logits = _nt_dot(rw_ref[...], h)
    m = logits.max(axis=0, keepdims=True)
    p = jnp.exp(logits - m)
    aff_ref[0] = p / p.sum(axis=0, keepdims=True)


def _outproj(ys, x, gate, w_bf16, g2, shift, scale, rw_t):
    b, l, d = x.shape
    tm = min(l, 512)
    bm = gate.shape[0]
    mod_map = (lambda i, j: (i, 0, 0)) if bm > 1 else (lambda i, j: (0, 0, 0))
    tok = lambda w: pl.BlockSpec((1, tm, w), lambda i, j: (i, j, 0))
    mod = pl.BlockSpec((1, 1, d), mod_map)
    return pl.pallas_call(
        _outproj_kernel,
        out_shape=[jax.ShapeDtypeStruct((b, l, d), F32), jax.ShapeDtypeStruct((b, l, d), BF16),
                   jax.ShapeDtypeStruct((b, N_EXPERTS, l), F32)],
        grid=(b, l // tm),
        in_specs=[tok(W_GROUP)] * 4 + [tok(d), mod, pl.BlockSpec((d, d), lambda i, j: (0, 0)),
                                        pl.BlockSpec((1, d), lambda i, j: (0, 0)), mod, mod,
                                        pl.BlockSpec((N_EXPERTS, d), lambda i, j: (0, 0))],
        out_specs=[tok(d), tok(d), pl.BlockSpec((1, N_EXPERTS, tm), lambda i, j: (i, 0, j))],
        compiler_params=_cparams(("parallel", "parallel")),
        name="outproj_router",
    )(*ys, x, gate, w_bf16, g2, shift, scale, rw_t)


def _moe_kernel(x_ref, g_ref, wg_ref, wu_ref, wd_ref, o_ref):
    f = pl.program_id(2)
    x = x_ref[0]
    a = jnp.dot(x, wg_ref[0].astype(BF16), preferred_element_type=F32)
    u = jnp.dot(x, wu_ref[0].astype(BF16), preferred_element_type=F32)
    hid = (a * jax.nn.sigmoid(a) * u).astype(BF16)
    part = jnp.dot(hid, wd_ref[0].astype(BF16), preferred_element_type=F32)

    @pl.when(f == 0)
    def _():
        o_ref[0] = part

    @pl.when(f != 0)
    def _():
        o_ref[0] += part

    @pl.when(f == pl.num_programs(2) - 1)
    def _():
        o_ref[0] = o_ref[0] * g_ref[0]


def _moe_ffn(xg, gate, w_gate, w_up, w_down):
    e, m, d = xg.shape
    f = w_gate.shape[2]
    tm = min(m, 1024)
    tf = 512
    return pl.pallas_call(
        _moe_kernel,
        out_shape=jax.ShapeDtypeStruct((e, m, d), F32),
        grid=(e, m // tm, f // tf),
        in_specs=[pl.BlockSpec((1, tm, d), lambda i, j, k: (i, j, 0)),
                  pl.BlockSpec((1, tm, 1), lambda i, j, k: (i, j, 0)),
                  pl.BlockSpec((1, d, tf), lambda i, j, k: (i, 0, k)),
                  pl.BlockSpec((1, d, tf), lambda i, j, k: (i, 0, k)),
                  pl.BlockSpec((1, tf, d), lambda i, j, k: (i, k, 0))],
        out_specs=pl.BlockSpec((1, tm, d), lambda i, j, k: (i, j, 0)),
        compiler_params=_cparams(("parallel", "parallel", "arbitrary")),
        name="moe_ffn",
    )(xg, gate, w_gate, w_up, w_down)


def _expert_choice_ffn(h, aff_t, w_gate, w_up, w_down):
    b, l, d = h.shape
    cap = EC_CAPACITY * l // N_EXPERTS
    g, idx = lax.top_k(aff_t, cap)
    idx_t = jnp.swapaxes(idx, 0, 1)
    bidx = jnp.arange(b)[None, :, None]
    xg = h[bidx, idx_t].reshape(N_EXPERTS, b * cap, d)
    gate = jnp.swapaxes(g, 0, 1).reshape(N_EXPERTS, b * cap, 1)
    o = _moe_ffn(xg, gate, w_gate, w_up, w_down).reshape(N_EXPERTS, b, cap, d)
    return jnp.zeros((b, l, d), F32).at[bidx, idx_t].add(o)


def _combine_kernel(*refs, final):
    if final:
        x_ref, m_ref, gt_ref, g_ref, o_ref = refs
    else:
        x_ref, m_ref, gt_ref, o_ref = refs
    x = x_ref[0] + gt_ref[0] * m_ref[0]
    if final:
        x = x * lax.rsqrt(jnp.mean(x * x, axis=-1, keepdims=True) + NORM_EPS) * g_ref[...]
    o_ref[0] = x


def _combine(x, moe, gate, final_g=None):
    b, l, d = x.shape
    tm = min(l, 512)
    bm = gate.shape[0]
    mod_map = (lambda i, j: (i, 0, 0)) if bm > 1 else (lambda i, j: (0, 0, 0))
    tok = pl.BlockSpec((1, tm, d), lambda i, j: (i, j, 0))
    final = final_g is not None
    in_specs = [tok, tok, pl.BlockSpec((1, 1, d), mod_map)]
    args = [x, moe, gate]
    if final:
        in_specs.append(pl.BlockSpec((1, d), lambda i, j: (0, 0)))
        args.append(final_g)
    return pl.pallas_call(
        functools.partial(_combine_kernel, final=final),
        out_shape=jax.ShapeDtypeStruct((b, l, d), F32),
        grid=(b, l // tm),
        in_specs=in_specs,
        out_specs=tok,
        compiler_params=_cparams(("parallel", "parallel")),
        name="combine_final" if final else "combine",
    )(*args)


def _hyena_filters(l, w1, b1, w2, b2, w3, freq):
    hp = lax.Precision.HIGHEST
    pos = jnp.arange(l, dtype=F32)
    t = pos / max(l - 1, 1)
    bands = jnp.linspace(1e-4, HY_BANDS - 1, HY_BANDS, dtype=F32)
    ang = (2 * math.pi / l) * pos[:, None] * bands[None, :]
    feats = jnp.concatenate([t[:, None], jnp.cos(ang), -jnp.sin(ang)], axis=-1)
    h = jnp.sin(freq * (jnp.dot(feats, w1, precision=hp) + b1))
    h = jnp.sin(freq * (jnp.dot(h, w2, precision=hp) + b2))
    h = jnp.dot(h, w3, precision=hp).reshape(l, HY_ORDER, 2, W_GROUP)
    deltas = jnp.linspace(HY_DECAY_MIN, HY_DECAY_MAX, W_GROUP, dtype=F32)
    window = jnp.exp(-t[:, None] * deltas[None, :])
    return h * window[:, None, None, :]


def _two_sided(h_fwd, h_bwd):
    lag0 = h_fwd[:1] + h_bwd[:1]
    return jnp.concatenate([lag0, h_fwd[1:], jnp.zeros_like(h_fwd[:1]), h_bwd[:0:-1]], axis=0)


def _short_conv_kernel(prev_ref, cur_ref, next_ref, w_ref, b_ref, o_ref, *, seq, tl):
    j = pl.program_id(1)
    nt = seq // tl
    prev = jnp.where(j == 0, 0.0, prev_ref[0])
    nxt = jnp.where(j == nt - 1, 0.0, next_ref[0])
    e = jnp.concatenate([prev, cur_ref[0], nxt], axis=0)
    n = tl + 2 * HALO
    y = pltpu.roll(e, 1, 0) * w_ref[0:1, :] + e * w_ref[1:2, :] + pltpu.roll(e, n - 1, 0) * w_ref[2:3, :]
    o_ref[0] = y[HALO:HALO + tl] + b_ref[...]


def _short_conv(u, w, b):
    bsz, l, c = u.shape
    tl = min(l, 512)
    hb = tl // HALO
    nh = l // HALO
    return pl.pallas_call(
        functools.partial(_short_conv_kernel, seq=l, tl=tl),
        out_shape=jax.ShapeDtypeStruct((bsz, l, c), F32),
        grid=(bsz, l // tl),
        in_specs=[pl.BlockSpec((1, HALO, c), lambda i, j: (i, jnp.maximum(j * hb - 1, 0), 0)),
                  pl.BlockSpec((1, tl, c), lambda i, j: (i, j, 0)),
                  pl.BlockSpec((1, HALO, c), lambda i, j: (i, jnp.minimum((j + 1) * hb, nh - 1), 0)),
                  pl.BlockSpec((3, c), lambda i, j: (0, 0)),
                  pl.BlockSpec((1, c), lambda i, j: (0, 0))],
        out_specs=pl.BlockSpec((1, tl, c), lambda i, j: (i, j, 0)),
        compiler_params=_cparams(("parallel", "parallel")),
        name="hyena_short_conv",
    )(u, u, u, w, b)


def _cplx_block(e):
    return np.block([[e.real, -e.imag], [e.imag, e.real]])


@functools.lru_cache(maxsize=None)
def _dft_consts(l):
    n = 2 * l
    n1f = n // DFT_N2
    n1h = n1f // 2
    k1, n1, n2 = np.arange(n1f), np.arange(n1h), np.arange(DFT_N2)
    ph = np.outer(k1, n1)[None] / n1f + (n2[:, None, None] * k1[None, :, None]) / n
    e1 = np.exp(-2j * np.pi * ph)
    w1 = np.stack([_cplx_block(e1[i]) for i in range(DFT_N2)])
    e2 = np.exp(-2j * np.pi * np.outer(n2, n2) / DFT_N2)
    f2 = _cplx_block(e2)
    g2 = _cplx_block(np.conj(e2).T)
    g1 = np.stack([_cplx_block(np.conj(e1[i]).T / n) for i in range(DFT_N2)])
    return tuple(jnp.asarray(a, F32).astype(BF16) for a in (w1, f2, g2, g1))


def _lconv_kernel(a_ref, hf_ref, w1_ref, f2_ref, g2_ref, g1_ref, y_ref, xy_ref, s_ref, *, n1h):
    n1f = 2 * n1h

    def copy_in(i, c):
        for ri in range(2):
            xy_ref[ri, pl.ds(pl.multiple_of(i * X_PITCH, 8), DFT_N2), :] = \
                a_ref[0, ri, pl.ds(pl.multiple_of(i * DFT_N2, DFT_N2), DFT_N2), :]
        return c

    lax.fori_loop(0, n1h, copy_in, 0)

    def stage1(n2, c):
        slab = jnp.concatenate([xy_ref[0, pl.ds(n2, n1h, stride=X_PITCH), :],
                                xy_ref[1, pl.ds(n2, n1h, stride=X_PITCH), :]], axis=0).astype(BF16)
        a = jnp.dot(w1_ref[n2], slab, preferred_element_type=F32)
        base = pl.multiple_of(n2 * S_PITCH, 8)
        s_ref[0, pl.ds(base, n1f), :] = a[:n1f]
        s_ref[1, pl.ds(base, n1f), :] = a[n1f:]
        return c

    lax.fori_loop(0, DFT_N2, stage1, 0)

    def stage2(k1, c):
        slab = jnp.concatenate([s_ref[0, pl.ds(k1, DFT_N2, stride=S_PITCH), :],
                                s_ref[1, pl.ds(k1, DFT_N2, stride=S_PITCH), :]], axis=0).astype(BF16)
        x = jnp.dot(f2_ref[...], slab, preferred_element_type=F32)
        xr, xi = x[:DFT_N2], x[DFT_N2:]
        hr, hi = hf_ref[k1, 0], hf_ref[k1, 1]
        y = jnp.concatenate([xr * hr - xi * hi, xr * hi + xi * hr], axis=0).astype(BF16)
        cc = jnp.dot(g2_ref[...], y, preferred_element_type=F32)
        s_ref[0, pl.ds(k1, DFT_N2, stride=S_PITCH), :] = cc[:DFT_N2]
        s_ref[1, pl.ds(k1, DFT_N2, stride=S_PITCH), :] = cc[DFT_N2:]
        return c

    lax.fori_loop(0, n1f, stage2, 0)

    def stage3(n2, c):
        base = pl.multiple_of(n2 * S_PITCH, 8)
        d = jnp.concatenate([s_ref[0, pl.ds(base, n1f), :], s_ref[1, pl.ds(base, n1f), :]], axis=0).astype(BF16)
        yv = jnp.dot(g1_ref[n2], d, preferred_element_type=F32)
        yb = pl.multiple_of(n2 * Y_PITCH, 8)
        xy_ref[0, pl.ds(yb, n1h), :] = yv[:n1h]
        xy_ref[1, pl.ds(yb, n1h), :] = yv[n1h:]
        return c

    lax.fori_loop(0, DFT_N2, stage3, 0)

    def copy_out(i, c):
        for ri in range(2):
            y_ref[0, ri, pl.ds(pl.multiple_of(i * DFT_N2, DFT_N2), DFT_N2), :] = \
                xy_ref[ri, pl.ds(i, DFT_N2, stride=Y_PITCH), :]
        return c

    lax.fori_loop(0, n1h, copy_out, 0)


def _lconv_small_kernel(a_ref, hf_ref, f_ref, g_ref, y_ref):
    l = a_ref.shape[2]
    slab = jnp.concatenate([a_ref[0, 0], a_ref[0, 1]], axis=0).astype(BF16)
    x = jnp.dot(f_ref[...], slab, preferred_element_type=F32)
    xr, xi = x[:2 * l], x[2 * l:]
    hr, hi = hf_ref[0], hf_ref[1]
    y = jnp.concatenate([xr * hr - xi * hi, xr * hi + xi * hr], axis=0).astype(BF16)
    out = jnp.dot(g_ref[...], y, preferred_element_type=F32)
    y_ref[0, 0] = out[:l]
    y_ref[0, 1] = out[l:]


@functools.lru_cache(maxsize=None)
def _dft_consts_small(l):
    n = 2 * l
    e = np.exp(-2j * np.pi * np.outer(np.arange(n), np.arange(l)) / n)
    f = _cplx_block(e)
    g = _cplx_block(np.conj(e).T / n)
    return jnp.asarray(f, F32).astype(BF16), jnp.asarray(g, F32).astype(BF16)


def _long_conv(a, hf):
    b, l, ca = a.shape
    c = W_GROUP
    a4 = a.reshape(b // 2, 2, l, ca)
    io_spec = pl.BlockSpec((1, 2, l, LANES), lambda j, p: (p, 0, 0, j))
    once = dict(pipeline_mode=pl.Buffered(1))
    const = lambda arr: pl.BlockSpec(arr.shape, lambda j, p: (0,) * arr.ndim, **once)
    if l <= DFT_N2 * 2:
        f, g = _dft_consts_small(l)
        y = pl.pallas_call(
            _lconv_small_kernel,
            out_shape=jax.ShapeDtypeStruct((b // 2, 2, l, c), F32),
            grid=(c // LANES, b // 2),
            in_specs=[io_spec, pl.BlockSpec((2, 2 * l, LANES), lambda j, p: (0, 0, j)), const(f), const(g)],
            out_specs=io_spec,
            compiler_params=_cparams(("parallel", "arbitrary")),
            name="long_conv_small",
        )(a4, hf, f, g)
        return y.reshape(b, l, c)
    n1f = 2 * l // DFT_N2
    n1h = n1f // 2
    assert (n1f + 8, n1h + 8) == (S_PITCH, Y_PITCH), "scratch pitches are sized for this sequence length"
    w1, f2, g2, g1 = _dft_consts(l)
    rows_xy = max(n1h * X_PITCH, DFT_N2 * Y_PITCH)
    y = pl.pallas_call(
        functools.partial(_lconv_kernel, n1h=n1h),
        out_shape=jax.ShapeDtypeStruct((b // 2, 2, l, c), F32),
        grid=(c // LANES, b // 2),
        in_specs=[io_spec, pl.BlockSpec((n1f, 2, DFT_N2, LANES), lambda j, p: (0, 0, 0, j), **once),
                  const(w1), const(f2), const(g2), const(g1)],
        out_specs=io_spec,
        scratch_shapes=[pltpu.VMEM((2, rows_xy, LANES), F32), pltpu.VMEM((2, DFT_N2 * S_PITCH, LANES), F32)],
        compiler_params=_cparams(("parallel", "arbitrary"), 52 * 1024 * 1024),
        name="long_conv",
    )(a4, hf, w1, f2, g2, g1)
    return y.reshape(b, l, c)


def _filter_spectrum(h_fwd, h_bwd):
    filt = _two_sided(h_fwd, h_bwd)
    n, c = filt.shape
    ff = jnp.fft.fft(filt, axis=0)
    if n <= DFT_N2 * 4:
        return jnp.stack([ff.real, ff.imag], axis=0).astype(F32)
    n1f = n // DFT_N2
    ff = jnp.swapaxes(ff.reshape(DFT_N2, n1f, c), 0, 1)
    return jnp.stack([ff.real, ff.imag], axis=1).astype(F32)


def _hy_gate_kernel(y_ref, a_ref, m_ref, sk_ref, o_ref):
    o_ref[0] = (m_ref[0] * (y_ref[0] + a_ref[0] * sk_ref[...])).astype(o_ref.dtype)


def _hy_gate(y, a, a_col, m, m_col, sk, out_dtype):
    b, l, c = y.shape
    tl = min(l, 1024)
    spec = lambda col: pl.BlockSpec((1, tl, c), lambda i, j: (i, j, col))
    return pl.pallas_call(
        _hy_gate_kernel,
        out_shape=jax.ShapeDtypeStruct((b, l, c), out_dtype),
        grid=(b, l // tl),
        in_specs=[spec(0), spec(a_col), spec(m_col), pl.BlockSpec((1, c), lambda i, j: (0, 0))],
        out_specs=spec(0),
        compiler_params=_cparams(("parallel", "parallel")),
        name="hyena_gate",
    )(y, a, m, sk)


def _hyena(u, conv_w, conv_b, filt, skip):
    uc = _short_conv(u, conv_w, conv_b[None, :])
    y1 = _long_conv(uc, _filter_spectrum(filt[:, 0, 0], filt[:, 0, 1]))
    z1 = _hy_gate(y1, uc, 0, uc, 1, skip[0:1], F32)
    y2 = _long_conv(z1, _filter_spectrum(filt[:, 1, 0], filt[:, 1, 1]))
    return _hy_gate(y2, z1, 0, uc, 2, skip[1:2], BF16)


def _layer(x, ctx, mod, lyr, rope_tabs, update_ctx, final_g):
    b, s, d = x.shape
    lc = ctx.shape[1]
    mx = [mod[:b, None, i * d:(i + 1) * d] for i in range(6)]
    mc = [mod[b:b + 1, None, i * d:(i + 1) * d] for i in range(6)]
    g1, g2 = lyr["g1"][None, :], lyr["g2"][None, :]
    w_in = lyr["w_in"].astype(BF16)
    w_out = lyr["w_out"].astype(BF16)
    rw_t = lyr["router_w"].T.astype(BF16)
    w_bd = _pool_weight(lyr["pool_w"])
    pscale = lyr["pool_scale"][None, :]
    hy_args = (lyr["hy_w1"], lyr["hy_b1"], lyr["hy_w2"], lyr["hy_b2"], lyr["hy_w3"], lyr["hy_freq"])

    u_hy, u_pool, q_wa, q_na, kv = _inproj(x, mx[0], mx[1], g1, w_in, rope_tabs)
    if update_ctx:
        cu_hy, cu_pool, cq_wa, cq_na, ckv = _inproj(ctx, mc[0], mc[1], g1, w_in)
    else:
        ckv = _inproj(ctx, mc[0], mc[1], g1, w_in[:, OFF_KV:], kv_only=True)

    ys = [_hyena(u_hy, lyr["hy_conv_w"], lyr["hy_conv_b"], _hyena_filters(s, *hy_args), lyr["hy_skip"]),
          _window_attn(q_wa, kv, ckv, lyr["wa_sink"]),
          _pool_mixer(u_pool, w_bd, pscale),
          _neighbourhood_attn(q_na, kv, ckv, lyr["na_rpb"])]
    x, h, aff = _outproj(ys, x, mx[2], w_out, g2, mx[3], mx[4], rw_t)
    moe = _expert_choice_ffn(h, aff, lyr["exp_w_gate"], lyr["exp_w_up"], lyr["exp_w_down"])
    x = _combine(x, moe, mx[5], final_g)

    if update_ctx:
        ycs = [_hyena(cu_hy, lyr["hy_conv_w"], lyr["hy_conv_b"], _hyena_filters(lc, *hy_args), lyr["hy_skip"]),
               _ctx_attn(cq_wa, ckv, 0, 1, N_WA_KV, lyr["wa_sink"]),
               _pool_mixer(cu_pool, w_bd, pscale),
               _ctx_attn(cq_na, ckv, 1, 2, N_NA_HEADS, None)]
        ctx, hc, affc = _outproj(ycs, ctx, mc[2], w_out, g2, mc[3], mc[4], rw_t)
        moe_c = _expert_choice_ffn(hc, affc, lyr["exp_w_gate"], lyr["exp_w_up"], lyr["exp_w_down"])
        ctx = _combine(ctx, moe_c, mc[5])
    return x, ctx


def kernel(x, c, ctx, c_ctx, ada_w, ada_b, norm1_g, norm2_g, w_in, hy_conv_w, hy_conv_b, hy_w1, hy_b1, hy_w2,
           hy_b2, hy_w3, hy_freq, hy_skip, wa_sink, pool_w, pool_scale, na_rpb, w_out, router_w, exp_w_gate,
           exp_w_up, exp_w_down, final_norm_g):
    b, s, d = x.shape
    cs = jnp.zeros((MOD_ROWS, d), F32).at[:b].set(c).at[b].set(c_ctx)
    mods = _ada_mod(cs, ada_w, ada_b[:, None, :])
    rope_tabs = _rope_tables(s)
    params = dict(g1=norm1_g, g2=norm2_g, w_in=w_in, hy_conv_w=hy_conv_w, hy_conv_b=hy_conv_b, hy_w1=hy_w1,
                  hy_b1=hy_b1, hy_w2=hy_w2, hy_b2=hy_b2, hy_w3=hy_w3, hy_freq=hy_freq, hy_skip=hy_skip,
                  wa_sink=wa_sink, pool_w=pool_w, pool_scale=pool_scale, na_rpb=na_rpb, w_out=w_out,
                  router_w=router_w, exp_w_gate=exp_w_gate, exp_w_up=exp_w_up, exp_w_down=exp_w_down)
    for l in range(DEPTH):
        lyr = {k: v[l] for k, v in params.items()}
        last = l == DEPTH - 1
        x, ctx = _layer(x, ctx, mods[l], lyr, rope_tabs, update_ctx=not last,
                        final_g=final_norm_g[None, :] if last else None)
    return x
```

```python
import functools
import math

import jax
import jax.numpy as jnp
import numpy as np
from jax import lax
from jax.experimental import pallas as pl
from jax.experimental.pallas import tpu as pltpu

F32 = jnp.float32
BF16 = jnp.bfloat16

D_MODEL = 1024
DEPTH = 2
GRID_W = 64
HEAD_DIM = 64
W_GROUP = 256
N_WA_HEADS = 4
N_WA_KV = 2
N_NA_HEADS = 4
KV_WA = 128
OFF_POOL = 768
OFF_WA_Q = 1024
OFF_NA_Q = 1280
OFF_KV = 1536
IN_WIDTH = 2304
KV_WIDTH = IN_WIDTH - OFF_KV
HY_ORDER = 2
HY_BANDS = 16
HY_DECAY_MIN = abs(math.log(1e-2) / 1.5)
HY_DECAY_MAX = abs(math.log(1e-2) / 0.3)
WA_BLOCK = 128
POOL_WINDOWS = (2, 4, 8, 16)
POOL_GROUP = 64
HALO = 8
LANES = 128
DFT_N2 = 128
X_PITCH = DFT_N2 + 8
S_PITCH = 64 + 8
Y_PITCH = 32 + 8
NA_ROWS = 8
NA_COLS = 16
ROPE_BASE = 10000.0
N_EXPERTS = 16
EXPERT_HIDDEN = 2048
EC_CAPACITY = 2
NORM_EPS = 1e-6
NEG_INF = -1e30
Q_SCALE = HEAD_DIM ** -0.5

GATHER_CHUNK = 512
COMBINE_TILE = 256
COMBINE_SUB = 256
COMBINE_ROUND = 1024
MOD_ROWS = 16
VMEM_LIMIT = 48 * 1024 * 1024


def _cparams(sem, vmem=VMEM_LIMIT):
    return pltpu.CompilerParams(dimension_semantics=sem, vmem_limit_bytes=vmem)


def _nt_dot(a, b):
    return lax.dot_general(a, b, (((1,), (1,)), ((), ())), preferred_element_type=F32)


def _rmsnorm_mod(x, g, shift, scale):
    y = x * lax.rsqrt(jnp.mean(x * x, axis=-1, keepdims=True) + NORM_EPS) * g
    return y * (1.0 + scale) + shift


def _ada_kernel(c_ref, w_ref, b_ref, o_ref):
    c = c_ref[...]
    s = (c * jax.nn.sigmoid(c)).astype(BF16)
    o_ref[0] = jnp.dot(s, w_ref[0].astype(BF16), preferred_element_type=F32) + b_ref[0]


def _ada_mod(cs, ada_w, ada_b):
    nl, d, n = ada_w.shape
    tn = 1024
    return pl.pallas_call(
        _ada_kernel,
        out_shape=jax.ShapeDtypeStruct((nl, MOD_ROWS, n), F32),
        grid=(nl, n // tn),
        in_specs=[pl.BlockSpec((MOD_ROWS, d), lambda l, j: (0, 0)),
                  pl.BlockSpec((1, d, tn), lambda l, j: (l, 0, j)),
                  pl.BlockSpec((1, 1, tn), lambda l, j: (l, 0, j))],
        out_specs=pl.BlockSpec((1, MOD_ROWS, tn), lambda l, j: (l, 0, j)),
        compiler_params=_cparams(("parallel", "parallel")),
        name="ada_mod",
    )(cs, ada_w, ada_b)


def _rope(t, cos, s_up, s_dn):
    w = t.shape[-1]
    if w > 128:
        cos, s_up, s_dn = (jnp.concatenate([a] * (w // 128), axis=-1) for a in (cos, s_up, s_dn))
    up = pltpu.roll(t, w - 16, 1)
    dn = pltpu.roll(t, 16, 1)
    return t * cos + up * s_up + dn * s_dn


def _inproj_kernel(*refs, rope, kv_only):
    if rope:
        x_ref, sh_ref, sc_ref, g_ref, w_ref, cos_ref, sup_ref, sdn_ref, *outs = refs
    else:
        x_ref, sh_ref, sc_ref, g_ref, w_ref, *outs = refs
    h = _rmsnorm_mod(x_ref[0], g_ref[...], sh_ref[0], sc_ref[0])
    p = jnp.dot(h.astype(BF16), w_ref[...], preferred_element_type=F32)
    if kv_only:
        outs[0][0] = p.astype(BF16)
        return
    hy_ref, pool_ref, qwa_ref, qna_ref, kv_ref = outs
    hy_ref[0] = p[:, :OFF_POOL]
    pool_ref[0] = p[:, OFF_POOL:OFF_WA_Q]
    qwa = p[:, OFF_WA_Q:OFF_NA_Q]
    kwa = p[:, OFF_KV:OFF_KV + KV_WA]
    if rope:
        tabs = (cos_ref[...], sup_ref[...], sdn_ref[...])
        qwa = _rope(qwa, *tabs)
        kwa = _rope(kwa, *tabs)
    qwa_ref[0] = (qwa * Q_SCALE).astype(BF16)
    qna_ref[0] = (p[:, OFF_NA_Q:OFF_KV] * Q_SCALE).astype(BF16)
    kv_ref[0] = jnp.concatenate([kwa, p[:, OFF_KV + KV_WA:]], axis=-1).astype(BF16)


def _inproj(x, shift, scale, g, w_bf16, rope_tabs=None, kv_only=False):
    b, l, d = x.shape
    n = w_bf16.shape[1]
    tm = min(l, 512)
    rope = rope_tabs is not None
    bm = shift.shape[0]
    mod_map = (lambda j, i: (i, 0, 0)) if bm > 1 else (lambda j, i: (0, 0, 0))
    in_specs = [pl.BlockSpec((1, tm, d), lambda j, i: (i, j, 0)),
                pl.BlockSpec((1, 1, d), mod_map),
                pl.BlockSpec((1, 1, d), mod_map),
                pl.BlockSpec((1, d), lambda j, i: (0, 0)),
                pl.BlockSpec((d, n), lambda j, i: (0, 0))]
    args = [x, shift, scale, g, w_bf16]
    if rope:
        in_specs += [pl.BlockSpec((tm, 128), lambda j, i: (j, 0))] * 3
        args += list(rope_tabs)

    def tok(width, dtype):
        return (jax.ShapeDtypeStruct((b, l, width), dtype),
                pl.BlockSpec((1, tm, width), lambda j, i: (i, j, 0)))

    if kv_only:
        outs = [tok(n, BF16)]
    else:
        outs = [tok(OFF_POOL, F32), tok(W_GROUP, F32), tok(W_GROUP, BF16), tok(W_GROUP, BF16),
                tok(KV_WIDTH, BF16)]
    res = pl.pallas_call(
        functools.partial(_inproj_kernel, rope=rope, kv_only=kv_only),
        out_shape=[o[0] for o in outs],
        grid=(l // tm, b),
        in_specs=in_specs,
        out_specs=[o[1] for o in outs],
        compiler_params=_cparams(("parallel", "parallel")),
        name="inproj_kv" if kv_only else "inproj",
    )(*args)
    return res[0] if kv_only else res


def _rope_tables(s):
    pos = jnp.arange(s)
    p2 = jnp.stack([pos // GRID_W, pos % GRID_W], axis=-1).astype(F32)
    inv = ROPE_BASE ** (-jnp.arange(16, dtype=F32) / 16)
    lane = np.arange(HEAD_DIM)
    ang = p2[:, lane // 32] * inv[lane % 16][None, :]
    first = jnp.asarray((lane % 32) < 16)[None, :]
    cos, sin = jnp.cos(ang), jnp.sin(ang)
    s_up = jnp.where(first, -sin, 0.0)
    s_dn = jnp.where(first, 0.0, sin)
    return tuple(jnp.tile(t, (1, 2)) for t in (cos, s_up, s_dn))


def _softmax_parts(parts, extra=None):
    m = parts[0].max(axis=-1, keepdims=True)
    for s in parts[1:]:
        m = jnp.maximum(m, s.max(axis=-1, keepdims=True))
    if extra is not None:
        m = jnp.maximum(m, extra)
    ps = [jnp.exp(s - m) for s in parts]
    den = ps[0].sum(axis=-1, keepdims=True)
    for p in ps[1:]:
        den = den + p.sum(axis=-1, keepdims=True)
    if extra is not None:
        den = den + jnp.exp(extra - m)
    return ps, den


def _wattn_kernel(sink_ref, q_ref, kp_ref, ko_ref, kn_ref, vp_ref, vo_ref, vn_ref, kc_ref, vc_ref, o_ref, *, nb):
    n = pl.program_id(1)
    blk = WA_BLOCK
    q = q_ref[0]
    k3 = jnp.concatenate([kp_ref[0], ko_ref[0], kn_ref[0]], axis=0)
    v3 = jnp.concatenate([vp_ref[0], vo_ref[0], vn_ref[0]], axis=0)
    kc, vc = kc_ref[0], vc_ref[0]
    i = lax.broadcasted_iota(jnp.int32, (2 * blk, 3 * blk), 0) & (blk - 1)
    j = lax.broadcasted_iota(jnp.int32, (2 * blk, 3 * blk), 1)
    jlo = jnp.where(n == 0, blk, 0)
    jhi = jnp.where(n == nb - 1, 2 * blk, 3 * blk)
    valid = (j >= i) & (j <= i + 2 * blk) & (j >= jlo) & (j < jhi)
    row = lax.broadcasted_iota(jnp.int32, (2 * blk, 1), 0)
    outs = []
    for kh in range(N_WA_KV):
        sl = slice(kh * HEAD_DIM, (kh + 1) * HEAD_DIM)
        q2 = jnp.concatenate([q[:, (2 * kh) * HEAD_DIM:(2 * kh + 1) * HEAD_DIM],
                              q[:, (2 * kh + 1) * HEAD_DIM:(2 * kh + 2) * HEAD_DIM]], axis=0)
        s_loc = jnp.where(valid, _nt_dot(q2, k3[:, sl]), NEG_INF)
        s_ctx = _nt_dot(q2, kc[:, sl])
        snk = jnp.where(row < blk, sink_ref[2 * kh], sink_ref[2 * kh + 1])
        (p_loc, p_ctx), den = _softmax_parts([s_loc, s_ctx], snk)
        o = (jnp.dot(p_loc.astype(BF16), v3[:, sl], preferred_element_type=F32)
             + jnp.dot(p_ctx.astype(BF16), vc[:, sl], preferred_element_type=F32)) / den
        outs += [o[:blk], o[blk:]]
    o_ref[0] = jnp.concatenate(outs, axis=-1).astype(BF16)


def _window_attn(q, kv, ckv, sink):
    b, s, _ = q.shape
    lc = ckv.shape[1]
    nb = s // WA_BLOCK

    def kv_spec(col, off):
        return pl.BlockSpec((1, WA_BLOCK, KV_WA),
                            lambda i, n: (i, jnp.clip(n + off, 0, nb - 1), col))

    return pl.pallas_call(
        functools.partial(_wattn_kernel, nb=nb),
        out_shape=jax.ShapeDtypeStruct((b, s, W_GROUP), BF16),
        grid=(b, nb),
        in_specs=[pl.BlockSpec(memory_space=pltpu.SMEM),
                  pl.BlockSpec((1, WA_BLOCK, W_GROUP), lambda i, n: (i, n, 0)),
                  kv_spec(0, -1), kv_spec(0, 0), kv_spec(0, 1),
                  kv_spec(1, -1), kv_spec(1, 0), kv_spec(1, 1),
                  pl.BlockSpec((1, lc, KV_WA), lambda i, n: (i, 0, 0)),
                  pl.BlockSpec((1, lc, KV_WA), lambda i, n: (i, 0, 1))],
        out_specs=pl.BlockSpec((1, WA_BLOCK, W_GROUP), lambda i, n: (i, n, 0)),
        compiler_params=_cparams(("parallel", "parallel")),
        name="window_attn",
    )(sink, q, kv, kv, kv, kv, kv, kv, ckv, ckv)


def _nattn_kernel(q_ref, k_ref, v_ref, kc_ref, vc_ref, bias_ref, o_ref, *, rows):
    r = pl.program_id(1)
    r0 = jnp.clip(r - NA_ROWS // 2, 0, rows - NA_ROWS)
    var = r - r0
    start = pl.multiple_of(r0 * GRID_W, GRID_W)
    nk = NA_ROWS * GRID_W
    q = q_ref[0]
    kt = k_ref[0, pl.ds(start, nk), :]
    vt = v_ref[0, pl.ds(start, nk), :]
    kc, vc = kc_ref[0], vc_ref[0]
    outs = []
    for h in range(N_NA_HEADS):
        sl = slice(h * HEAD_DIM, (h + 1) * HEAD_DIM)
        qh = q[:, sl]
        s_loc = _nt_dot(qh, kt[:, sl]) + bias_ref[h, var]
        s_ctx = _nt_dot(qh, kc[:, sl])
        (p_loc, p_ctx), den = _softmax_parts([s_loc, s_ctx])
        o = (jnp.dot(p_loc.astype(BF16), vt[:, sl], preferred_element_type=F32)
             + jnp.dot(p_ctx.astype(BF16), vc[:, sl], preferred_element_type=F32)) / den
        outs.append(o)
    o_ref[0] = jnp.concatenate(outs, axis=-1).astype(BF16)


def _na_bias(rpb):
    var = np.arange(NA_ROWS)
    j = np.arange(NA_ROWS)
    qc = np.arange(GRID_W)
    kc = np.arange(GRID_W)
    dr = j[None, :] - var[:, None] + NA_ROWS - 1
    dc = np.clip(kc[None, :] - qc[:, None] + NA_COLS - 1, 0, 2 * NA_COLS - 2)
    ws = np.clip(qc - NA_COLS // 2, 0, GRID_W - NA_COLS)
    ok = (kc[None, :] >= ws[:, None]) & (kc[None, :] < ws[:, None] + NA_COLS)
    bias = rpb.astype(F32)[:, dr[:, None, :, None], dc[None, :, None, :]]
    bias = jnp.where(jnp.asarray(ok)[None, None, :, None, :], bias, NEG_INF)
    return bias.reshape(rpb.shape[0], NA_ROWS, GRID_W, NA_ROWS * GRID_W)


def _neighbourhood_attn(q, kv, ckv, rpb):
    b, s, _ = q.shape
    lc = ckv.shape[1]
    rows = s // GRID_W
    bias = _na_bias(rpb)
    return pl.pallas_call(
        functools.partial(_nattn_kernel, rows=rows),
        out_shape=jax.ShapeDtypeStruct((b, s, W_GROUP), BF16),
        grid=(b, rows),
        in_specs=[pl.BlockSpec((1, GRID_W, W_GROUP), lambda i, r: (i, r, 0)),
                  pl.BlockSpec((1, s, W_GROUP), lambda i, r: (i, 0, 1)),
                  pl.BlockSpec((1, s, W_GROUP), lambda i, r: (i, 0, 2)),
                  pl.BlockSpec((1, lc, W_GROUP), lambda i, r: (i, 0, 1)),
                  pl.BlockSpec((1, lc, W_GROUP), lambda i, r: (i, 0, 2)),
                  pl.BlockSpec(bias.shape, lambda i, r: (0, 0, 0, 0))],
        out_specs=pl.BlockSpec((1, GRID_W, W_GROUP), lambda i, r: (i, r, 0)),
        compiler_params=_cparams(("parallel", "arbitrary")),
        name="neighbourhood_attn",
    )(q, kv, kv, ckv, ckv, bias)


def _cattn_kernel(*refs, n_kv, with_sink):
    if with_sink:
        sink_ref, q_ref, k_ref, v_ref, o_ref = refs
    else:
        q_ref, k_ref, v_ref, o_ref = refs
    q, k, v = q_ref[0], k_ref[0], v_ref[0]
    group = N_WA_HEADS // n_kv
    outs = []
    for h in range(N_WA_HEADS):
        sl = slice((h // group) * HEAD_DIM, (h // group + 1) * HEAD_DIM)
        s = _nt_dot(q[:, h * HEAD_DIM:(h + 1) * HEAD_DIM], k[:, sl])
        extra = jnp.full((s.shape[0], 1), sink_ref[h], F32) if with_sink else None
        (p,), den = _softmax_parts([s], extra)
        outs.append(jnp.dot(p.astype(BF16), v[:, sl], preferred_element_type=F32) / den)
    o_ref[0] = jnp.concatenate(outs, axis=-1).astype(BF16)


def _ctx_attn(q, ckv, k_col, v_col, n_kv, sink):
    b, lc, _ = q.shape
    w = n_kv * HEAD_DIM
    with_sink = sink is not None
    in_specs = [pl.BlockSpec((1, lc, W_GROUP), lambda i: (i, 0, 0)),
                pl.BlockSpec((1, lc, w), lambda i: (i, 0, k_col)),
                pl.BlockSpec((1, lc, w), lambda i: (i, 0, v_col))]
    args = [q, ckv, ckv]
    if with_sink:
        in_specs = [pl.BlockSpec(memory_space=pltpu.SMEM)] + in_specs
        args = [sink] + args
    return pl.pallas_call(
        functools.partial(_cattn_kernel, n_kv=n_kv, with_sink=with_sink),
        out_shape=jax.ShapeDtypeStruct((b, lc, W_GROUP), BF16),
        grid=(b,),
        in_specs=in_specs,
        out_specs=pl.BlockSpec((1, lc, W_GROUP), lambda i: (i, 0, 0)),
        compiler_params=_cparams(("parallel",)),
        name="ctx_attn",
    )(*args)


def _pool_kernel(prev_ref, cur_ref, next_ref, w_ref, scale_ref, o_ref, *, seq, tl):
    j = pl.program_id(1)
    nt = seq // tl
    cur = cur_ref[0]
    prev = jnp.where(j == 0, 0.0, prev_ref[0])
    nxt = jnp.where(j == nt - 1, 0.0, next_ref[0])
    e = jnp.concatenate([prev, cur, nxt], axis=0)
    n = tl + 2 * HALO

    def sh(a, d):
        return pltpu.roll(a, d % n, 0)

    s2 = e + sh(e, 1)
    s4 = sh(s2, 1) + sh(s2, -1)
    s8 = sh(s4, 2) + sh(s4, -2)
    s16 = sh(s8, 4) + sh(s8, -4)
    lane = lax.broadcasted_iota(jnp.int32, (tl, W_GROUP), 1)
    t = lax.broadcasted_iota(jnp.int32, (tl, W_GROUP), 0) + j * tl
    g = lane // POOL_GROUP
    lo, hi = HALO, HALO + tl
    ssum = jnp.where(g == 0, s2[lo:hi], jnp.where(g == 1, s4[lo:hi], jnp.where(g == 2, s8[lo:hi], s16[lo:hi])))
    half = jnp.where(g == 0, 1, jnp.where(g == 1, 2, jnp.where(g == 2, 4, 8)))
    cnt = (jnp.minimum(t + half, seq) - jnp.maximum(t - half, 0)).astype(F32)
    d = (ssum / cnt - cur).astype(BF16)
    o_ref[0] = (jnp.dot(d, w_ref[...], preferred_element_type=F32) * scale_ref[...]).astype(BF16)


def _pool_mixer(u, w_bd, scale):
    b, l, c = u.shape
    tl = min(l, 512)
    hb = tl // HALO
    nh = l // HALO
    return pl.pallas_call(
        functools.partial(_pool_kernel, seq=l, tl=tl),
        out_shape=jax.ShapeDtypeStruct((b, l, c), BF16),
        grid=(b, l // tl),
        in_specs=[pl.BlockSpec((1, HALO, c), lambda i, j: (i, jnp.maximum(j * hb - 1, 0), 0)),
                  pl.BlockSpec((1, tl, c), lambda i, j: (i, j, 0)),
                  pl.BlockSpec((1, HALO, c), lambda i, j: (i, jnp.minimum((j + 1) * hb, nh - 1), 0)),
                  pl.BlockSpec((c, c), lambda i, j: (0, 0)),
                  pl.BlockSpec((1, c), lambda i, j: (0, 0))],
        out_specs=pl.BlockSpec((1, tl, c), lambda i, j: (i, j, 0)),
        compiler_params=_cparams(("parallel", "parallel")),
        name="pool_mixer",
    )(u, u, u, w_bd, scale)


def _pool_weight(pool_w):
    z = jnp.zeros((W_GROUP, W_GROUP), F32)
    for g in range(len(POOL_WINDOWS)):
        z = z.at[g * POOL_GROUP:(g + 1) * POOL_GROUP, g * POOL_GROUP:(g + 1) * POOL_GROUP].set(pool_w[g])
    return z.astype(BF16)


def _outproj_kernel(yh_ref, ya_ref, yp_ref, yn_ref, x_ref, gt_ref, w_ref, g2_ref, sh_ref, sc_ref, rw_ref,
                    xo_ref, h_ref, aff_ref):
    y = jnp.concatenate([yh_ref[0], ya_ref[0], yp_ref[0], yn_ref[0]], axis=-1)
    x = x_ref[0] + gt_ref[0] * jnp.dot(y, w_ref[...], preferred_element_type=F32)
    xo_ref[0] = x
    h = _rmsnorm_mod(x, g2_ref[...], sh_ref[0], sc_ref[0])
    h_ref[0] = h
    logits = _nt_dot(rw_ref[...], h.astype(BF16))
    m = logits.max(axis=0, keepdims=True)
    p = jnp.exp(logits - m)
    aff_ref[0] = p / p.sum(axis=0, keepdims=True)


def _outproj(ys, x, gate, w_bf16, g2, shift, scale, rw_t):
    b, l, d = x.shape
    tm = min(l, 512)
    bm = gate.shape[0]
    mod_map = (lambda i, j: (i, 0, 0)) if bm > 1 else (lambda i, j: (0, 0, 0))
    tok = lambda w: pl.BlockSpec((1, tm, w), lambda i, j: (i, j, 0))
    mod = pl.BlockSpec((1, 1, d), mod_map)
    return pl.pallas_call(
        _outproj_kernel,
        out_shape=[jax.ShapeDtypeStruct((b, l, d), F32), jax.ShapeDtypeStruct((b, l, d), F32),
                   jax.ShapeDtypeStruct((b, N_EXPERTS, l), F32)],
        grid=(b, l // tm),
        in_specs=[tok(W_GROUP)] * 4 + [tok(d), mod, pl.BlockSpec((d, d), lambda i, j: (0, 0)),
                                        pl.BlockSpec((1, d), lambda i, j: (0, 0)), mod, mod,
                                        pl.BlockSpec((N_EXPERTS, d), lambda i, j: (0, 0))],
        out_specs=[tok(d), tok(d), pl.BlockSpec((1, N_EXPERTS, tm), lambda i, j: (i, 0, j))],
        compiler_params=_cparams(("parallel", "parallel")),
        name="outproj_router",
    )(*ys, x, gate, w_bf16, g2, shift, scale, rw_t)


def _row_gather_kernel(idx_ref, table_ref, out_ref, sems):
    i = pl.program_id(0)
    n = pl.num_programs(0)
    slot = i % 2

    def row_copy(src, dst, s):
        return pltpu.make_async_copy(table_ref.at[pl.ds(src, 1)], out_ref.at[pl.ds(dst, 1)], sems.at[s])

    def issue(r, c):
        row_copy(idx_ref[0, 0, r], i * GATHER_CHUNK + r, slot).start()
        return c

    lax.fori_loop(0, GATHER_CHUNK, issue, 0, unroll=8)

    def drain(s):
        def wait_one(r, c):
            row_copy(0, 0, s).wait()
            return c
        lax.fori_loop(0, GATHER_CHUNK, wait_one, 0)

    @pl.when(i > 0)
    def _():
        drain(1 - slot)

    @pl.when(i == n - 1)
    def _():
        drain(slot)


def _row_gather(table, rows):
    v, d = table.shape
    n = rows.shape[0]
    nch = n // GATHER_CHUNK
    return pl.pallas_call(
        _row_gather_kernel,
        out_shape=jax.ShapeDtypeStruct((n, d), table.dtype),
        grid=(nch,),
        in_specs=[pl.BlockSpec((1, 1, GATHER_CHUNK), lambda i: (i, 0, 0), memory_space=pltpu.SMEM),
                  pl.BlockSpec(memory_space=pl.ANY)],
        out_specs=pl.BlockSpec(memory_space=pl.ANY),
        scratch_shapes=[pltpu.SemaphoreType.DMA((2,))],
        compiler_params=_cparams(("arbitrary",)),
        name="moe_row_gather",
    )(rows.reshape(nch, 1, GATHER_CHUNK), table)


def _moe_kernel(x_ref, g_ref, wg_ref, wu_ref, wd_ref, o_ref):
    f = pl.program_id(2)
    x = x_ref[0].astype(BF16)
    a = jnp.dot(x, wg_ref[0].astype(BF16), preferred_element_type=F32)
    u = jnp.dot(x, wu_ref[0].astype(BF16), preferred_element_type=F32)
    hid = (a * jax.nn.sigmoid(a) * u).astype(BF16)
    part = jnp.dot(hid, wd_ref[0].astype(BF16), preferred_element_type=F32)

    @pl.when(f == 0)
    def _():
        o_ref[0] = part

    @pl.when(f != 0)
    def _():
        o_ref[0] += part

    @pl.when(f == pl.num_programs(2) - 1)
    def _():
        o_ref[0] = o_ref[0] * g_ref[0]


def _moe_ffn(xg, gate, w_gate, w_up, w_down):
    e, m, d = xg.shape
    f = w_gate.shape[2]
    tm = min(m, 1024)
    tf = 512
    return pl.pallas_call(
        _moe_kernel,
        out_shape=jax.ShapeDtypeStruct((e, m, d), F32),
        grid=(e, m // tm, f // tf),
        in_specs=[pl.BlockSpec((1, tm, d), lambda i, j, k: (i, j, 0)),
                  pl.BlockSpec((1, tm, 1), lambda i, j, k: (i, j, 0)),
                  pl.BlockSpec((1, d, tf), lambda i, j, k: (i, 0, k)),
                  pl.BlockSpec((1, d, tf), lambda i, j, k: (i, 0, k)),
                  pl.BlockSpec((1, tf, d), lambda i, j, k: (i, k, 0))],
        out_specs=pl.BlockSpec((1, tm, d), lambda i, j, k: (i, j, 0)),
        compiler_params=_cparams(("parallel", "parallel", "arbitrary")),
        name="moe_ffn",
    )(xg, gate, w_gate, w_up, w_down)


def _expert_choice_ffn(h, aff_t, w_gate, w_up, w_down):
    b, l, d = h.shape
    cap = EC_CAPACITY * l // N_EXPERTS
    m = b * cap
    g, idx = lax.top_k(aff_t, cap)
    idx_t = jnp.swapaxes(idx, 0, 1)
    rows = (idx_t + (jnp.arange(b, dtype=jnp.int32) * l)[None, :, None]).reshape(-1)
    xg = _row_gather(h.reshape(b * l, d), rows).reshape(N_EXPERTS, m, d)
    gate = jnp.swapaxes(g, 0, 1).reshape(N_EXPERTS, m, 1)
    o = _moe_ffn(xg, gate, w_gate, w_up, w_down).reshape(N_EXPERTS * m, d)
    src = (jnp.arange(N_EXPERTS, dtype=jnp.int32)[None, :, None] * m
           + jnp.arange(b, dtype=jnp.int32)[:, None, None] * cap
           + jnp.arange(cap, dtype=jnp.int32)[None, None, :])
    tok_s, src_s = lax.sort((idx.reshape(b, -1), src.reshape(b, -1)), dimension=1, num_keys=1)
    return o, tok_s, src_s


def _combine_kernel(*refs, final, tile, nt):
    if final:
        off_ref, src_ref, tok_ref, x_ref, gt_ref, g_ref, o_hbm, out_ref, stage, sems = refs
    else:
        off_ref, src_ref, tok_ref, x_ref, gt_ref, o_hbm, out_ref, stage, sems = refs
    bi, j = pl.program_id(0), pl.program_id(1)
    step = bi * nt + j
    cur = step % 2
    cap_rows = stage.shape[1]
    sub = COMBINE_SUB

    def row_copy(src, dst, s):
        return pltpu.make_async_copy(o_hbm.at[pl.ds(src, 1)], stage.at[s, pl.ds(dst, 1)], sems.at[s])

    def issue(s, start, cnt):
        def one(r, c):
            row_copy(src_ref[0, 0, start + r], r, s).start()
            return c
        lax.fori_loop(0, cnt, one, 0)

    def wait(s, cnt):
        def one(r, c):
            row_copy(0, 0, s).wait()
            return c
        lax.fori_loop(0, cnt, one, 0)

    def span(jj):
        start = (off_ref[bi, jj] // LANES) * LANES
        return start, off_ref[bi, jj + 1] - start

    def reduce(s, start, cnt):
        t_ids = lax.broadcasted_iota(jnp.int32, (tile, sub), 0) + j * tile

        def one(k, acc):
            base = pl.multiple_of(k * sub, sub)
            rows = stage[s, pl.ds(base, sub), :]
            toks = tok_ref[0, :, pl.ds(pl.multiple_of(start + base, LANES), sub)]
            onehot = jnp.where(toks == t_ids, 1.0, 0.0).astype(BF16)
            hi = rows.astype(BF16)
            lo = (rows - hi.astype(F32)).astype(BF16)
            return (acc + jnp.dot(onehot, hi, preferred_element_type=F32)
                    + jnp.dot(onehot, lo, preferred_element_type=F32))

        return lax.fori_loop(0, (cnt + sub - 1) // sub, one, jnp.zeros((tile, out_ref.shape[2]), F32))

    @pl.when(step == 0)
    def _():
        stage[...] = jnp.zeros(stage.shape, F32)

    start, n = span(j)
    n0 = jnp.minimum(n, cap_rows)

    @pl.when(j == 0)
    def _():
        issue(cur, start, n0)

    @pl.when(j + 1 < nt)
    def _():
        nstart, nn = span(jnp.minimum(j + 1, nt - 1))
        issue(1 - cur, nstart, jnp.minimum(nn, cap_rows))

    wait(cur, n0)
    acc = reduce(cur, start, n0)

    def extra_round(r, acc):
        st = start + r * cap_rows
        cnt = jnp.minimum(n - r * cap_rows, cap_rows)
        issue(cur, st, cnt)
        wait(cur, cnt)
        return acc + reduce(cur, st, cnt)

    acc = lax.fori_loop(1, (n + cap_rows - 1) // cap_rows, extra_round, acc)
    x = x_ref[0] + gt_ref[0] * acc
    if final:
        x = x * lax.rsqrt(jnp.mean(x * x, axis=-1, keepdims=True) + NORM_EPS) * g_ref[...]
    out_ref[0] = x


def _combine(x, o, tok_s, src_s, gate, final_g=None):
    b, l, d = x.shape
    p = tok_s.shape[1]
    tile = min(l, COMBINE_TILE)
    nt = l // tile
    cap_rows = min(COMBINE_ROUND, -(-(p + LANES) // COMBINE_SUB) * COMBINE_SUB)
    pad = cap_rows + COMBINE_SUB
    bounds = jnp.arange(nt + 1, dtype=jnp.int32) * tile
    off = jnp.sum(tok_s[:, None, :] < bounds[None, :, None], axis=-1, dtype=jnp.int32)
    tok_p = jnp.pad(tok_s, ((0, 0), (0, pad)), constant_values=l)[:, None, :]
    src_p = jnp.pad(src_s, ((0, 0), (0, pad)))[:, None, :]
    bm = gate.shape[0]
    mod_map = (lambda i, j: (i, 0, 0)) if bm > 1 else (lambda i, j: (0, 0, 0))
    tok = pl.BlockSpec((1, tile, d), lambda i, j: (i, j, 0))
    final = final_g is not None
    in_specs = [pl.BlockSpec(memory_space=pltpu.SMEM),
                pl.BlockSpec((1, 1, p + pad), lambda i, j: (i, 0, 0), memory_space=pltpu.SMEM),
                pl.BlockSpec((1, 1, p + pad), lambda i, j: (i, 0, 0)),
                tok, pl.BlockSpec((1, 1, d), mod_map)]
    args = [off, src_p, tok_p, x, gate]
    if final:
        in_specs.append(pl.BlockSpec((1, d), lambda i, j: (0, 0)))
        args.append(final_g)
    in_specs.append(pl.BlockSpec(memory_space=pl.ANY))
    args.append(o)
    return pl.pallas_call(
        functools.partial(_combine_kernel, final=final, tile=tile, nt=nt),
        out_shape=jax.ShapeDtypeStruct((b, l, d), F32),
        grid=(b, nt),
        in_specs=in_specs,
        out_specs=tok,
        scratch_shapes=[pltpu.VMEM((2, cap_rows, d), F32), pltpu.SemaphoreType.DMA((2,))],
        compiler_params=_cparams(("arbitrary", "arbitrary")),
        name="moe_combine_final" if final else "moe_combine",
    )(*args)


def _hyena_filters(l, w1, b1, w2, b2, w3, freq):
    hp = lax.Precision.HIGHEST
    pos = jnp.arange(l, dtype=F32)
    t = pos / max(l - 1, 1)
    bands = jnp.linspace(1e-4, HY_BANDS - 1, HY_BANDS, dtype=F32)
    ang = (2 * math.pi / l) * pos[:, None] * bands[None, :]
    feats = jnp.concatenate([t[:, None], jnp.cos(ang), -jnp.sin(ang)], axis=-1)
    h = jnp.sin(freq * (jnp.dot(feats, w1, precision=hp) + b1))
    h = jnp.sin(freq * (jnp.dot(h, w2, precision=hp) + b2))
    h = jnp.dot(h, w3, precision=hp).reshape(l, HY_ORDER, 2, W_GROUP)
    deltas = jnp.linspace(HY_DECAY_MIN, HY_DECAY_MAX, W_GROUP, dtype=F32)
    window = jnp.exp(-t[:, None] * deltas[None, :])
    return h * window[:, None, None, :]


def _two_sided(h_fwd, h_bwd):
    lag0 = h_fwd[:1] + h_bwd[:1]
    return jnp.concatenate([lag0, h_fwd[1:], jnp.zeros_like(h_fwd[:1]), h_bwd[:0:-1]], axis=0)


def _short_conv_kernel(prev_ref, cur_ref, next_ref, w_ref, b_ref, o_ref, *, seq, tl):
    j = pl.program_id(1)
    nt = seq // tl
    prev = jnp.where(j == 0, 0.0, prev_ref[0])
    nxt = jnp.where(j == nt - 1, 0.0, next_ref[0])
    e = jnp.concatenate([prev, cur_ref[0], nxt], axis=0)
    n = tl + 2 * HALO
    y = pltpu.roll(e, 1, 0) * w_ref[0:1, :] + e * w_ref[1:2, :] + pltpu.roll(e, n - 1, 0) * w_ref[2:3, :]
    o_ref[0] = y[HALO:HALO + tl] + b_ref[...]


def _short_conv(u, w, b):
    bsz, l, c = u.shape
    tl = min(l, 512)
    hb = tl // HALO
    nh = l // HALO
    return pl.pallas_call(
        functools.partial(_short_conv_kernel, seq=l, tl=tl),
        out_shape=jax.ShapeDtypeStruct((bsz, l, c), F32),
        grid=(bsz, l // tl),
        in_specs=[pl.BlockSpec((1, HALO, c), lambda i, j: (i, jnp.maximum(j * hb - 1, 0), 0)),
                  pl.BlockSpec((1, tl, c), lambda i, j: (i, j, 0)),
                  pl.BlockSpec((1, HALO, c), lambda i, j: (i, jnp.minimum((j + 1) * hb, nh - 1), 0)),
                  pl.BlockSpec((3, c), lambda i, j: (0, 0)),
                  pl.BlockSpec((1, c), lambda i, j: (0, 0))],
        out_specs=pl.BlockSpec((1, tl, c), lambda i, j: (i, j, 0)),
        compiler_params=_cparams(("parallel", "parallel")),
        name="hyena_short_conv",
    )(u, u, u, w, b)


def _cplx_block(e):
    return np.block([[e.real, -e.imag], [e.imag, e.real]])


@functools.lru_cache(maxsize=None)
def _dft_consts(l):
    n = 2 * l
    n1f = n // DFT_N2
    n1h = n1f // 2
    k1, n1, n2 = np.arange(n1f), np.arange(n1h), np.arange(DFT_N2)
    ph = np.outer(k1, n1)[None] / n1f + (n2[:, None, None] * k1[None, :, None]) / n
    e1 = np.exp(-2j * np.pi * ph)
    w1 = np.stack([_cplx_block(e1[i]) for i in range(DFT_N2)])
    e2 = np.exp(-2j * np.pi * np.outer(n2, n2) / DFT_N2)
    f2 = _cplx_block(e2)
    g2 = _cplx_block(np.conj(e2).T)
    g1 = np.stack([_cplx_block(np.conj(e1[i]).T / n) for i in range(DFT_N2)])
    return tuple(jnp.asarray(a, F32).astype(BF16) for a in (w1, f2, g2, g1))


def _lconv_kernel(a_ref, hf_ref, w1_ref, f2_ref, g2_ref, g1_ref, y_ref, xy_ref, s_ref, *, n1h):
    n1f = 2 * n1h

    def copy_in(i, c):
        for ri in range(2):
            xy_ref[ri, pl.ds(pl.multiple_of(i * X_PITCH, 8), DFT_N2), :] = \
                a_ref[0, ri, pl.ds(pl.multiple_of(i * DFT_N2, DFT_N2), DFT_N2), :]
        return c

    lax.fori_loop(0, n1h, copy_in, 0)

    def stage1(n2, c):
        slab = jnp.concatenate([xy_ref[0, pl.ds(n2, n1h, stride=X_PITCH), :],
                                xy_ref[1, pl.ds(n2, n1h, stride=X_PITCH), :]], axis=0).astype(BF16)
        a = jnp.dot(w1_ref[n2], slab, preferred_element_type=F32)
        base = pl.multiple_of(n2 * S_PITCH, 8)
        s_ref[0, pl.ds(base, n1f), :] = a[:n1f]
        s_ref[1, pl.ds(base, n1f), :] = a[n1f:]
        return c

    lax.fori_loop(0, DFT_N2, stage1, 0)

    def stage2(k1, c):
        slab = jnp.concatenate([s_ref[0, pl.ds(k1, DFT_N2, stride=S_PITCH), :],
                                s_ref[1, pl.ds(k1, DFT_N2, stride=S_PITCH), :]], axis=0).astype(BF16)
        x = jnp.dot(f2_ref[...], slab, preferred_element_type=F32)
        xr, xi = x[:DFT_N2], x[DFT_N2:]
        hr, hi = hf_ref[k1, 0], hf_ref[k1, 1]
        y = jnp.concatenate([xr * hr - xi * hi, xr * hi + xi * hr], axis=0).astype(BF16)
        cc = jnp.dot(g2_ref[...], y, preferred_element_type=F32)
        s_ref[0, pl.ds(k1, DFT_N2, stride=S_PITCH), :] = cc[:DFT_N2]
        s_ref[1, pl.ds(k1, DFT_N2, stride=S_PITCH), :] = cc[DFT_N2:]
        return c

    lax.fori_loop(0, n1f, stage2, 0)

    def stage3(n2, c):
        base = pl.multiple_of(n2 * S_PITCH, 8)
        d = jnp.concatenate([s_ref[0, pl.ds(base, n1f), :], s_ref[1, pl.ds(base, n1f), :]], axis=0).astype(BF16)
        yv = jnp.dot(g1_ref[n2], d, preferred_element_type=F32)
        yb = pl.multiple_of(n2 * Y_PITCH, 8)
        xy_ref[0, pl.ds(yb, n1h), :] = yv[:n1h]
        xy_ref[1, pl.ds(yb, n1h), :] = yv[n1h:]
        return c

    lax.fori_loop(0, DFT_N2, stage3, 0)

    def copy_out(i, c):
        for ri in range(2):
            y_ref[0, ri, pl.ds(pl.multiple_of(i * DFT_N2, DFT_N2), DFT_N2), :] = \
                xy_ref[ri, pl.ds(i, DFT_N2, stride=Y_PITCH), :]
        return c

    lax.fori_loop(0, n1h, copy_out, 0)


def _lconv_small_kernel(a_ref, hf_ref, f_ref, g_ref, y_ref):
    l = a_ref.shape[2]
    slab = jnp.concatenate([a_ref[0, 0], a_ref[0, 1]], axis=0).astype(BF16)
    x = jnp.dot(f_ref[...], slab, preferred_element_type=F32)
    xr, xi = x[:2 * l], x[2 * l:]
    hr, hi = hf_ref[0], hf_ref[1]
    y = jnp.concatenate([xr * hr - xi * hi, xr * hi + xi * hr], axis=0).astype(BF16)
    out = jnp.dot(g_ref[...], y, preferred_element_type=F32)
    y_ref[0, 0] = out[:l]
    y_ref[0, 1] = out[l:]


@functools.lru_cache(maxsize=None)
def _dft_consts_small(l):
    n = 2 * l
    e = np.exp(-2j * np.pi * np.outer(np.arange(n), np.arange(l)) / n)
    f = _cplx_block(e)
    g = _cplx_block(np.conj(e).T / n)
    return jnp.asarray(f, F32).astype(BF16), jnp.asarray(g, F32).astype(BF16)


def _long_conv(a, hf):
    b, l, ca = a.shape
    c = W_GROUP
    a4 = a.reshape(b // 2, 2, l, ca)
    io_spec = pl.BlockSpec((1, 2, l, LANES), lambda j, p: (p, 0, 0, j))
    once = dict(pipeline_mode=pl.Buffered(1))
    const = lambda arr: pl.BlockSpec(arr.shape, lambda j, p: (0,) * arr.ndim, **once)
    if l <= DFT_N2 * 2:
        f, g = _dft_consts_small(l)
        y = pl.pallas_call(
            _lconv_small_kernel,
            out_shape=jax.ShapeDtypeStruct((b // 2, 2, l, c), F32),
            grid=(c // LANES, b // 2),
            in_specs=[io_spec, pl.BlockSpec((2, 2 * l, LANES), lambda j, p: (0, 0, j)), const(f), const(g)],
            out_specs=io_spec,
            compiler_params=_cparams(("parallel", "arbitrary")),
            name="long_conv_small",
        )(a4, hf, f, g)
        return y.reshape(b, l, c)
    n1f = 2 * l // DFT_N2
    n1h = n1f // 2
    assert (n1f + 8, n1h + 8) == (S_PITCH, Y_PITCH), "scratch pitches are sized for this sequence length"
    w1, f2, g2, g1 = _dft_consts(l)
    rows_xy = max(n1h * X_PITCH, DFT_N2 * Y_PITCH)
    y = pl.pallas_call(
        functools.partial(_lconv_kernel, n1h=n1h),
        out_shape=jax.ShapeDtypeStruct((b // 2, 2, l, c), F32),
        grid=(c // LANES, b // 2),
        in_specs=[io_spec, pl.BlockSpec((n1f, 2, DFT_N2, LANES), lambda j, p: (0, 0, 0, j), **once),
                  const(w1), const(f2), const(g2), const(g1)],
        out_specs=io_spec,
        scratch_shapes=[pltpu.VMEM((2, rows_xy, LANES), F32), pltpu.VMEM((2, DFT_N2 * S_PITCH, LANES), F32)],
        compiler_params=_cparams(("parallel", "arbitrary"), 52 * 1024 * 1024),
        name="long_conv",
    )(a4, hf, w1, f2, g2, g1)
    return y.reshape(b, l, c)


def _filter_spectrum(h_fwd, h_bwd):
    filt = _two_sided(h_fwd, h_bwd)
    n, c = filt.shape
    ff = jnp.fft.fft(filt, axis=0)
    if n <= DFT_N2 * 4:
        return jnp.stack([ff.real, ff.imag], axis=0).astype(F32)
    n1f = n // DFT_N2
    ff = jnp.swapaxes(ff.reshape(DFT_N2, n1f, c), 0, 1)
    return jnp.stack([ff.real, ff.imag], axis=1).astype(F32)


def _hy_gate_kernel(y_ref, a_ref, m_ref, sk_ref, o_ref):
    o_ref[0] = (m_ref[0] * (y_ref[0] + a_ref[0] * sk_ref[...])).astype(o_ref.dtype)


def _hy_gate(y, a, a_col, m, m_col, sk, out_dtype):
    b, l, c = y.shape
    tl = min(l, 1024)
    spec = lambda col: pl.BlockSpec((1, tl, c), lambda i, j: (i, j, col))
    return pl.pallas_call(
        _hy_gate_kernel,
        out_shape=jax.ShapeDtypeStruct((b, l, c), out_dtype),
        grid=(b, l // tl),
        in_specs=[spec(0), spec(a_col), spec(m_col), pl.BlockSpec((1, c), lambda i, j: (0, 0))],
        out_specs=spec(0),
        compiler_params=_cparams(("parallel", "parallel")),
        name="hyena_gate",
    )(y, a, m, sk)


def _hyena(u, conv_w, conv_b, filt, skip):
    uc = _short_conv(u, conv_w, conv_b[None, :])
    y1 = _long_conv(uc, _filter_spectrum(filt[:, 0, 0], filt[:, 0, 1]))
    z1 = _hy_gate(y1, uc, 0, uc, 1, skip[0:1], F32)
    y2 = _long_conv(z1, _filter_spectrum(filt[:, 1, 0], filt[:, 1, 1]))
    return _hy_gate(y2, z1, 0, uc, 2, skip[1:2], BF16)


def _layer(x, ctx, mod, lyr, rope_tabs, update_ctx, final_g):
    b, s, d = x.shape
    lc = ctx.shape[1]
    mx = [mod[:b, None, i * d:(i + 1) * d] for i in range(6)]
    mc = [mod[b:b + 1, None, i * d:(i + 1) * d] for i in range(6)]
    g1, g2 = lyr["g1"][None, :], lyr["g2"][None, :]
    w_in = lyr["w_in"].astype(BF16)
    w_out = lyr["w_out"].astype(BF16)
    rw_t = lyr["router_w"].T.astype(BF16)
    w_bd = _pool_weight(lyr["pool_w"])
    pscale = lyr["pool_scale"][None, :]
    hy_args = (lyr["hy_w1"], lyr["hy_b1"], lyr["hy_w2"], lyr["hy_b2"], lyr["hy_w3"], lyr["hy_freq"])

    u_hy, u_pool, q_wa, q_na, kv = _inproj(x, mx[0], mx[1], g1, w_in, rope_tabs)
    if update_ctx:
        cu_hy, cu_pool, cq_wa, cq_na, ckv = _inproj(ctx, mc[0], mc[1], g1, w_in)
    else:
        ckv = _inproj(ctx, mc[0], mc[1], g1, w_in[:, OFF_KV:], kv_only=True)

    ys = [_hyena(u_hy, lyr["hy_conv_w"], lyr["hy_conv_b"], _hyena_filters(s, *hy_args), lyr["hy_skip"]),
          _window_attn(q_wa, kv, ckv, lyr["wa_sink"]),
          _pool_mixer(u_pool, w_bd, pscale),
          _neighbourhood_attn(q_na, kv, ckv, lyr["na_rpb"])]
    x, h, aff = _outproj(ys, x, mx[2], w_out, g2, mx[3], mx[4], rw_t)
    moe = _expert_choice_ffn(h, aff, lyr["exp_w_gate"], lyr["exp_w_up"], lyr["exp_w_down"])
    x = _combine(x, *moe, mx[5], final_g)

    if update_ctx:
        ycs = [_hyena(cu_hy, lyr["hy_conv_w"], lyr["hy_conv_b"], _hyena_filters(lc, *hy_args), lyr["hy_skip"]),
               _ctx_attn(cq_wa, ckv, 0, 1, N_WA_KV, lyr["wa_sink"]),
               _pool_mixer(cu_pool, w_bd, pscale),
               _ctx_attn(cq_na, ckv, 1, 2, N_NA_HEADS, None)]
        ctx, hc, affc = _outproj(ycs, ctx, mc[2], w_out, g2, mc[3], mc[4], rw_t)
        moe_c = _expert_choice_ffn(hc, affc, lyr["exp_w_gate"], lyr["exp_w_up"], lyr["exp_w_down"])
        ctx = _combine(ctx, *moe_c, mc[5])
    return x, ctx


def kernel(x, c, ctx, c_ctx, ada_w, ada_b, norm1_g, norm2_g, w_in, hy_conv_w, hy_conv_b, hy_w1, hy_b1, hy_w2,
           hy_b2, hy_w3, hy_freq, hy_skip, wa_sink, pool_w, pool_scale, na_rpb, w_out, router_w, exp_w_gate,
           exp_w_up, exp_w_down, final_norm_g):
    b, s, d = x.shape
    cs = jnp.zeros((MOD_ROWS, d), F32).at[:b].set(c).at[b].set(c_ctx)
    mods = _ada_mod(cs, ada_w, ada_b[:, None, :])
    rope_tabs = _rope_tables(s)
    params = dict(g1=norm1_g, g2=norm2_g, w_in=w_in, hy_conv_w=hy_conv_w, hy_conv_b=hy_conv_b, hy_w1=hy_w1,
                  hy_b1=hy_b1, hy_w2=hy_w2, hy_b2=hy_b2, hy_w3=hy_w3, hy_freq=hy_freq, hy_skip=hy_skip,
                  wa_sink=wa_sink, pool_w=pool_w, pool_scale=pool_scale, na_rpb=na_rpb, w_out=w_out,
                  router_w=router_w, exp_w_gate=exp_w_gate, exp_w_up=exp_w_up, exp_w_down=exp_w_down)
    for l in range(DEPTH):
        lyr = {k: v[l] for k, v in params.items()}
        last = l == DEPTH - 1
        x, ctx = _layer(x, ctx, mods[l], lyr, rope_tabs, update_ctx=not last,
                        final_g=final_norm_g[None, :] if last else None)
    return x
```

```python
import functools
import math

import jax
import jax.numpy as jnp
import numpy as np
from jax import lax
from jax.experimental import pallas as pl
from jax.experimental.pallas import tpu as pltpu

F32 = jnp.float32
BF16 = jnp.bfloat16

D_MODEL = 1024
DEPTH = 2
GRID_W = 64
HEAD_DIM = 64
W_GROUP = 256
N_WA_HEADS = 4
N_WA_KV = 2
N_NA_HEADS = 4
KV_WA = 128
OFF_POOL = 768
OFF_WA_Q = 1024
OFF_NA_Q = 1280
OFF_KV = 1536
IN_WIDTH = 2304
KV_WIDTH = IN_WIDTH - OFF_KV
HY_ORDER = 2
HY_BANDS = 16
HY_DECAY_MIN = abs(math.log(1e-2) / 1.5)
HY_DECAY_MAX = abs(math.log(1e-2) / 0.3)
WA_BLOCK = 128
POOL_WINDOWS = (2, 4, 8, 16)
POOL_GROUP = 64
HALO = 8
LANES = 128
SUBLANES = 8
DFT_N2 = 128
X_PITCH = DFT_N2 + 8
S_PITCH = 64 + 8
Y_PITCH = 32 + 8
NA_ROWS = 8
NA_COLS = 16
ROPE_BASE = 10000.0
N_EXPERTS = 16
EXPERT_HIDDEN = 2048
EC_CAPACITY = 2
NORM_EPS = 1e-6
NEG_INF = -1e30
Q_SCALE = HEAD_DIM ** -0.5

GATHER_CHUNK = 512
COMBINE_TILE = 256
COMBINE_SUB = 256
COMBINE_ROUND = 1024
MOD_ROWS = 16
VMEM_LIMIT = 48 * 1024 * 1024


def _cparams(sem, vmem=VMEM_LIMIT):
    return pltpu.CompilerParams(dimension_semantics=sem, vmem_limit_bytes=vmem)


def _nt_dot(a, b):
    return lax.dot_general(a, b, (((1,), (1,)), ((), ())), preferred_element_type=F32)


def _rmsnorm_mod(x, g, shift, scale):
    y = x * lax.rsqrt(jnp.mean(x * x, axis=-1, keepdims=True) + NORM_EPS) * g
    return y * (1.0 + scale) + shift


def _ada_kernel(c_ref, w_ref, b_ref, o_ref):
    c = c_ref[...]
    s = (c * jax.nn.sigmoid(c)).astype(BF16)
    o_ref[0] = jnp.dot(s, w_ref[0].astype(BF16), preferred_element_type=F32) + b_ref[0]


def _ada_mod(cs, ada_w, ada_b):
    nl, d, n = ada_w.shape
    tn = 1024
    return pl.pallas_call(
        _ada_kernel,
        out_shape=jax.ShapeDtypeStruct((nl, MOD_ROWS, n), F32),
        grid=(nl, n // tn),
        in_specs=[pl.BlockSpec((MOD_ROWS, d), lambda l, j: (0, 0)),
                  pl.BlockSpec((1, d, tn), lambda l, j: (l, 0, j)),
                  pl.BlockSpec((1, 1, tn), lambda l, j: (l, 0, j))],
        out_specs=pl.BlockSpec((1, MOD_ROWS, tn), lambda l, j: (l, 0, j)),
        compiler_params=_cparams(("parallel", "parallel")),
        name="ada_mod",
    )(cs, ada_w, ada_b)


def _rope(t, cos, s_up, s_dn):
    w = t.shape[-1]
    if w > 128:
        cos, s_up, s_dn = (jnp.concatenate([a] * (w // 128), axis=-1) for a in (cos, s_up, s_dn))
    up = pltpu.roll(t, w - 16, 1)
    dn = pltpu.roll(t, 16, 1)
    return t * cos + up * s_up + dn * s_dn


def _inproj_kernel(*refs, rope, kv_only):
    if rope:
        x_ref, sh_ref, sc_ref, g_ref, w_ref, cos_ref, sup_ref, sdn_ref, *outs = refs
    else:
        x_ref, sh_ref, sc_ref, g_ref, w_ref, *outs = refs
    h = _rmsnorm_mod(x_ref[0], g_ref[...], sh_ref[0], sc_ref[0])
    p = jnp.dot(h.astype(BF16), w_ref[...], preferred_element_type=F32)
    if kv_only:
        outs[0][0] = p.astype(BF16)
        return
    hy_ref, pool_ref, qwa_ref, qna_ref, kv_ref = outs
    hy_ref[0] = p[:, :OFF_POOL]
    pool_ref[0] = p[:, OFF_POOL:OFF_WA_Q]
    qwa = p[:, OFF_WA_Q:OFF_NA_Q]
    kwa = p[:, OFF_KV:OFF_KV + KV_WA]
    if rope:
        tabs = (cos_ref[...], sup_ref[...], sdn_ref[...])
        qwa = _rope(qwa, *tabs)
        kwa = _rope(kwa, *tabs)
    qwa_ref[0] = (qwa * Q_SCALE).astype(BF16)
    qna_ref[0] = (p[:, OFF_NA_Q:OFF_KV] * Q_SCALE).astype(BF16)
    kv_ref[0] = jnp.concatenate([kwa, p[:, OFF_KV + KV_WA:]], axis=-1).astype(BF16)


def _inproj(x, shift, scale, g, w_bf16, rope_tabs=None, kv_only=False):
    b, l, d = x.shape
    n = w_bf16.shape[1]
    tm = min(l, 512)
    rope = rope_tabs is not None
    bm = shift.shape[0]
    mod_map = (lambda j, i: (i, 0, 0)) if bm > 1 else (lambda j, i: (0, 0, 0))
    in_specs = [pl.BlockSpec((1, tm, d), lambda j, i: (i, j, 0)),
                pl.BlockSpec((1, 1, d), mod_map),
                pl.BlockSpec((1, 1, d), mod_map),
                pl.BlockSpec((1, d), lambda j, i: (0, 0)),
                pl.BlockSpec((d, n), lambda j, i: (0, 0))]
    args = [x, shift, scale, g, w_bf16]
    if rope:
        in_specs += [pl.BlockSpec((tm, 128), lambda j, i: (j, 0))] * 3
        args += list(rope_tabs)

    def tok(width, dtype):
        return (jax.ShapeDtypeStruct((b, l, width), dtype),
                pl.BlockSpec((1, tm, width), lambda j, i: (i, j, 0)))

    if kv_only:
        outs = [tok(n, BF16)]
    else:
        outs = [tok(OFF_POOL, F32), tok(W_GROUP, F32), tok(W_GROUP, BF16), tok(W_GROUP, BF16),
                tok(KV_WIDTH, BF16)]
    res = pl.pallas_call(
        functools.partial(_inproj_kernel, rope=rope, kv_only=kv_only),
        out_shape=[o[0] for o in outs],
        grid=(l // tm, b),
        in_specs=in_specs,
        out_specs=[o[1] for o in outs],
        compiler_params=_cparams(("parallel", "parallel")),
        name="inproj_kv" if kv_only else "inproj",
    )(*args)
    return res[0] if kv_only else res


def _rope_tables(s):
    pos = jnp.arange(s)
    p2 = jnp.stack([pos // GRID_W, pos % GRID_W], axis=-1).astype(F32)
    inv = ROPE_BASE ** (-jnp.arange(16, dtype=F32) / 16)
    lane = np.arange(HEAD_DIM)
    ang = p2[:, lane // 32] * inv[lane % 16][None, :]
    first = jnp.asarray((lane % 32) < 16)[None, :]
    cos, sin = jnp.cos(ang), jnp.sin(ang)
    s_up = jnp.where(first, -sin, 0.0)
    s_dn = jnp.where(first, 0.0, sin)
    return tuple(jnp.tile(t, (1, 2)) for t in (cos, s_up, s_dn))


def _softmax_parts(parts, extra=None):
    m = parts[0].max(axis=-1, keepdims=True)
    for s in parts[1:]:
        m = jnp.maximum(m, s.max(axis=-1, keepdims=True))
    if extra is not None:
        m = jnp.maximum(m, extra)
    ps = [jnp.exp(s - m) for s in parts]
    den = ps[0].sum(axis=-1, keepdims=True)
    for p in ps[1:]:
        den = den + p.sum(axis=-1, keepdims=True)
    if extra is not None:
        den = den + jnp.exp(extra - m)
    return ps, den


def _wattn_kernel(sink_ref, q_ref, kp_ref, ko_ref, kn_ref, vp_ref, vo_ref, vn_ref, kc_ref, vc_ref, o_ref, *, nb):
    n = pl.program_id(1)
    blk = WA_BLOCK
    q = q_ref[0]
    k3 = jnp.concatenate([kp_ref[0], ko_ref[0], kn_ref[0]], axis=0)
    v3 = jnp.concatenate([vp_ref[0], vo_ref[0], vn_ref[0]], axis=0)
    kc, vc = kc_ref[0], vc_ref[0]
    i = lax.broadcasted_iota(jnp.int32, (2 * blk, 3 * blk), 0) & (blk - 1)
    j = lax.broadcasted_iota(jnp.int32, (2 * blk, 3 * blk), 1)
    jlo = jnp.where(n == 0, blk, 0)
    jhi = jnp.where(n == nb - 1, 2 * blk, 3 * blk)
    valid = (j >= i) & (j <= i + 2 * blk) & (j >= jlo) & (j < jhi)
    row = lax.broadcasted_iota(jnp.int32, (2 * blk, 1), 0)
    outs = []
    for kh in range(N_WA_KV):
        sl = slice(kh * HEAD_DIM, (kh + 1) * HEAD_DIM)
        q2 = jnp.concatenate([q[:, (2 * kh) * HEAD_DIM:(2 * kh + 1) * HEAD_DIM],
                              q[:, (2 * kh + 1) * HEAD_DIM:(2 * kh + 2) * HEAD_DIM]], axis=0)
        s_loc = jnp.where(valid, _nt_dot(q2, k3[:, sl]), NEG_INF)
        s_ctx = _nt_dot(q2, kc[:, sl])
        snk = jnp.where(row < blk, sink_ref[2 * kh], sink_ref[2 * kh + 1])
        (p_loc, p_ctx), den = _softmax_parts([s_loc, s_ctx], snk)
        o = (jnp.dot(p_loc.astype(BF16), v3[:, sl], preferred_element_type=F32)
             + jnp.dot(p_ctx.astype(BF16), vc[:, sl], preferred_element_type=F32)) / den
        outs += [o[:blk], o[blk:]]
    o_ref[0] = jnp.concatenate(outs, axis=-1).astype(BF16)


def _window_attn(q, kv, ckv, sink):
    b, s, _ = q.shape
    lc = ckv.shape[1]
    nb = s // WA_BLOCK

    def kv_spec(col, off):
        return pl.BlockSpec((1, WA_BLOCK, KV_WA),
                            lambda i, n: (i, jnp.clip(n + off, 0, nb - 1), col))

    return pl.pallas_call(
        functools.partial(_wattn_kernel, nb=nb),
        out_shape=jax.ShapeDtypeStruct((b, s, W_GROUP), BF16),
        grid=(b, nb),
        in_specs=[pl.BlockSpec(memory_space=pltpu.SMEM),
                  pl.BlockSpec((1, WA_BLOCK, W_GROUP), lambda i, n: (i, n, 0)),
                  kv_spec(0, -1), kv_spec(0, 0), kv_spec(0, 1),
                  kv_spec(1, -1), kv_spec(1, 0), kv_spec(1, 1),
                  pl.BlockSpec((1, lc, KV_WA), lambda i, n: (i, 0, 0)),
                  pl.BlockSpec((1, lc, KV_WA), lambda i, n: (i, 0, 1))],
        out_specs=pl.BlockSpec((1, WA_BLOCK, W_GROUP), lambda i, n: (i, n, 0)),
        compiler_params=_cparams(("parallel", "parallel")),
        name="window_attn",
    )(sink, q, kv, kv, kv, kv, kv, kv, ckv, ckv)


def _nattn_kernel(q_ref, k_ref, v_ref, kc_ref, vc_ref, bias_ref, o_ref, *, rows):
    r = pl.program_id(1)
    r0 = jnp.clip(r - NA_ROWS // 2, 0, rows - NA_ROWS)
    var = r - r0
    start = pl.multiple_of(r0 * GRID_W, GRID_W)
    nk = NA_ROWS * GRID_W
    q = q_ref[0]
    kt = k_ref[0, pl.ds(start, nk), :]
    vt = v_ref[0, pl.ds(start, nk), :]
    kc, vc = kc_ref[0], vc_ref[0]
    outs = []
    for h in range(N_NA_HEADS):
        sl = slice(h * HEAD_DIM, (h + 1) * HEAD_DIM)
        qh = q[:, sl]
        s_loc = _nt_dot(qh, kt[:, sl]) + bias_ref[h, var]
        s_ctx = _nt_dot(qh, kc[:, sl])
        (p_loc, p_ctx), den = _softmax_parts([s_loc, s_ctx])
        o = (jnp.dot(p_loc.astype(BF16), vt[:, sl], preferred_element_type=F32)
             + jnp.dot(p_ctx.astype(BF16), vc[:, sl], preferred_element_type=F32)) / den
        outs.append(o)
    o_ref[0] = jnp.concatenate(outs, axis=-1).astype(BF16)


def _na_bias(rpb):
    var = np.arange(NA_ROWS)
    j = np.arange(NA_ROWS)
    qc = np.arange(GRID_W)
    kc = np.arange(GRID_W)
    dr = j[None, :] - var[:, None] + NA_ROWS - 1
    dc = np.clip(kc[None, :] - qc[:, None] + NA_COLS - 1, 0, 2 * NA_COLS - 2)
    ws = np.clip(qc - NA_COLS // 2, 0, GRID_W - NA_COLS)
    ok = (kc[None, :] >= ws[:, None]) & (kc[None, :] < ws[:, None] + NA_COLS)
    onehot = (dc[None] == np.arange(2 * NA_COLS - 1)[:, None, None]).astype(np.float32)
    bias = jnp.einsum("hvjd,dqk->hvqjk", rpb.astype(F32)[:, dr], jnp.asarray(onehot),
                      precision=lax.Precision.HIGHEST)
    bias = jnp.where(jnp.asarray(ok)[None, None, :, None, :], bias, NEG_INF)
    return bias.reshape(rpb.shape[0], NA_ROWS, GRID_W, NA_ROWS * GRID_W)


def _neighbourhood_attn(q, kv, ckv, rpb):
    b, s, _ = q.shape
    lc = ckv.shape[1]
    rows = s // GRID_W
    bias = _na_bias(rpb)
    return pl.pallas_call(
        functools.partial(_nattn_kernel, rows=rows),
        out_shape=jax.ShapeDtypeStruct((b, s, W_GROUP), BF16),
        grid=(b, rows),
        in_specs=[pl.BlockSpec((1, GRID_W, W_GROUP), lambda i, r: (i, r, 0)),
                  pl.BlockSpec((1, s, W_GROUP), lambda i, r: (i, 0, 1)),
                  pl.BlockSpec((1, s, W_GROUP), lambda i, r: (i, 0, 2)),
                  pl.BlockSpec((1, lc, W_GROUP), lambda i, r: (i, 0, 1)),
                  pl.BlockSpec((1, lc, W_GROUP), lambda i, r: (i, 0, 2)),
                  pl.BlockSpec(bias.shape, lambda i, r: (0, 0, 0, 0))],
        out_specs=pl.BlockSpec((1, GRID_W, W_GROUP), lambda i, r: (i, r, 0)),
        compiler_params=_cparams(("parallel", "arbitrary")),
        name="neighbourhood_attn",
    )(q, kv, kv, ckv, ckv, bias)


def _cattn_kernel(*refs, n_kv, with_sink):
    if with_sink:
        sink_ref, q_ref, k_ref, v_ref, o_ref = refs
    else:
        q_ref, k_ref, v_ref, o_ref = refs
    q, k, v = q_ref[0], k_ref[0], v_ref[0]
    group = N_WA_HEADS // n_kv
    outs = []
    for h in range(N_WA_HEADS):
        sl = slice((h // group) * HEAD_DIM, (h // group + 1) * HEAD_DIM)
        s = _nt_dot(q[:, h * HEAD_DIM:(h + 1) * HEAD_DIM], k[:, sl])
        extra = jnp.full((s.shape[0], 1), sink_ref[h], F32) if with_sink else None
        (p,), den = _softmax_parts([s], extra)
        outs.append(jnp.dot(p.astype(BF16), v[:, sl], preferred_element_type=F32) / den)
    o_ref[0] = jnp.concatenate(outs, axis=-1).astype(BF16)


def _ctx_attn(q, ckv, k_col, v_col, n_kv, sink):
    b, lc, _ = q.shape
    w = n_kv * HEAD_DIM
    with_sink = sink is not None
    in_specs = [pl.BlockSpec((1, lc, W_GROUP), lambda i: (i, 0, 0)),
                pl.BlockSpec((1, lc, w), lambda i: (i, 0, k_col)),
                pl.BlockSpec((1, lc, w), lambda i: (i, 0, v_col))]
    args = [q, ckv, ckv]
    if with_sink:
        in_specs = [pl.BlockSpec(memory_space=pltpu.SMEM)] + in_specs
        args = [sink] + args
    return pl.pallas_call(
        functools.partial(_cattn_kernel, n_kv=n_kv, with_sink=with_sink),
        out_shape=jax.ShapeDtypeStruct((b, lc, W_GROUP), BF16),
        grid=(b,),
        in_specs=in_specs,
        out_specs=pl.BlockSpec((1, lc, W_GROUP), lambda i: (i, 0, 0)),
        compiler_params=_cparams(("parallel",)),
        name="ctx_attn",
    )(*args)


def _pool_kernel(prev_ref, cur_ref, next_ref, w_ref, scale_ref, o_ref, *, seq, tl):
    j = pl.program_id(1)
    nt = seq // tl
    cur = cur_ref[0]
    prev = jnp.where(j == 0, 0.0, prev_ref[0])
    nxt = jnp.where(j == nt - 1, 0.0, next_ref[0])
    e = jnp.concatenate([prev, cur, nxt], axis=0)
    n = tl + 2 * HALO

    def sh(a, d):
        return pltpu.roll(a, d % n, 0)

    s2 = e + sh(e, 1)
    s4 = sh(s2, 1) + sh(s2, -1)
    s8 = sh(s4, 2) + sh(s4, -2)
    s16 = sh(s8, 4) + sh(s8, -4)
    lane = lax.broadcasted_iota(jnp.int32, (tl, W_GROUP), 1)
    t = lax.broadcasted_iota(jnp.int32, (tl, W_GROUP), 0) + j * tl
    g = lane // POOL_GROUP
    lo, hi = HALO, HALO + tl
    ssum = jnp.where(g == 0, s2[lo:hi], jnp.where(g == 1, s4[lo:hi], jnp.where(g == 2, s8[lo:hi], s16[lo:hi])))
    half = jnp.where(g == 0, 1, jnp.where(g == 1, 2, jnp.where(g == 2, 4, 8)))
    cnt = (jnp.minimum(t + half, seq) - jnp.maximum(t - half, 0)).astype(F32)
    d = (ssum / cnt - cur).astype(BF16)
    o_ref[0] = (jnp.dot(d, w_ref[...], preferred_element_type=F32) * scale_ref[...]).astype(BF16)


def _pool_mixer(u, w_bd, scale):
    b, l, c = u.shape
    tl = min(l, 512)
    hb = tl // HALO
    nh = l // HALO
    return pl.pallas_call(
        functools.partial(_pool_kernel, seq=l, tl=tl),
        out_shape=jax.ShapeDtypeStruct((b, l, c), BF16),
        grid=(b, l // tl),
        in_specs=[pl.BlockSpec((1, HALO, c), lambda i, j: (i, jnp.maximum(j * hb - 1, 0), 0)),
                  pl.BlockSpec((1, tl, c), lambda i, j: (i, j, 0)),
                  pl.BlockSpec((1, HALO, c), lambda i, j: (i, jnp.minimum((j + 1) * hb, nh - 1), 0)),
                  pl.BlockSpec((c, c), lambda i, j: (0, 0)),
                  pl.BlockSpec((1, c), lambda i, j: (0, 0))],
        out_specs=pl.BlockSpec((1, tl, c), lambda i, j: (i, j, 0)),
        compiler_params=_cparams(("parallel", "parallel")),
        name="pool_mixer",
    )(u, u, u, w_bd, scale)


def _pool_weight(pool_w):
    z = jnp.zeros((W_GROUP, W_GROUP), F32)
    for g in range(len(POOL_WINDOWS)):
        z = z.at[g * POOL_GROUP:(g + 1) * POOL_GROUP, g * POOL_GROUP:(g + 1) * POOL_GROUP].set(pool_w[g])
    return z.astype(BF16)


def _outproj_kernel(yh_ref, ya_ref, yp_ref, yn_ref, x_ref, gt_ref, w_ref, g2_ref, sh_ref, sc_ref, rw_ref,
                    xo_ref, h_ref, aff_ref):
    y = jnp.concatenate([yh_ref[0], ya_ref[0], yp_ref[0], yn_ref[0]], axis=-1)
    x = x_ref[0] + gt_ref[0] * jnp.dot(y, w_ref[...], preferred_element_type=F32)
    xo_ref[0] = x
    h = _rmsnorm_mod(x, g2_ref[...], sh_ref[0], sc_ref[0])
    h_ref[0] = h
    logits = _nt_dot(rw_ref[...], h.astype(BF16))
    m = logits.max(axis=0, keepdims=True)
    p = jnp.exp(logits - m)
    aff_ref[0] = p / p.sum(axis=0, keepdims=True)


def _outproj(ys, x, gate, w_bf16, g2, shift, scale, rw_t):
    b, l, d = x.shape
    tm = min(l, 512)
    bm = gate.shape[0]
    mod_map = (lambda i, j: (i, 0, 0)) if bm > 1 else (lambda i, j: (0, 0, 0))
    tok = lambda w: pl.BlockSpec((1, tm, w), lambda i, j: (i, j, 0))
    mod = pl.BlockSpec((1, 1, d), mod_map)
    return pl.pallas_call(
        _outproj_kernel,
        out_shape=[jax.ShapeDtypeStruct((b, l, d), F32), jax.ShapeDtypeStruct((b, l, d), F32),
                   jax.ShapeDtypeStruct((b, N_EXPERTS, l), F32)],
        grid=(b, l // tm),
        in_specs=[tok(W_GROUP)] * 4 + [tok(d), mod, pl.BlockSpec((d, d), lambda i, j: (0, 0)),
                                        pl.BlockSpec((1, d), lambda i, j: (0, 0)), mod, mod,
                                        pl.BlockSpec((N_EXPERTS, d), lambda i, j: (0, 0))],
        out_specs=[tok(d), tok(d), pl.BlockSpec((1, N_EXPERTS, tm), lambda i, j: (i, 0, j))],
        compiler_params=_cparams(("parallel", "parallel")),
        name="outproj_router",
    )(*ys, x, gate, w_bf16, g2, shift, scale, rw_t)


def _row_gather_kernel(idx_ref, table_ref, out_ref, sem):
    def issue(r, c):
        pltpu.make_async_copy(table_ref.at[pl.ds(idx_ref[0, 0, r], 1)], out_ref.at[pl.ds(r, 1)], sem).start()
        return c

    lax.fori_loop(0, GATHER_CHUNK, issue, 0, unroll=8)
    pltpu.make_async_copy(table_ref.at[pl.ds(0, GATHER_CHUNK)], out_ref, sem).wait()


def _row_gather(table, rows):
    v, d = table.shape
    n = rows.shape[0]
    nch = n // GATHER_CHUNK
    return pl.pallas_call(
        _row_gather_kernel,
        out_shape=jax.ShapeDtypeStruct((n, d), table.dtype),
        grid=(nch,),
        in_specs=[pl.BlockSpec((1, 1, GATHER_CHUNK), lambda i: (i, 0, 0), memory_space=pltpu.SMEM),
                  pl.BlockSpec(memory_space=pl.ANY)],
        out_specs=pl.BlockSpec((GATHER_CHUNK, d), lambda i: (i, 0)),
        scratch_shapes=[pltpu.SemaphoreType.DMA(())],
        compiler_params=pltpu.CompilerParams(dimension_semantics=("arbitrary",), vmem_limit_bytes=VMEM_LIMIT,
                                             disable_bounds_checks=True),
        name="moe_row_gather",
    )(rows.reshape(nch, 1, GATHER_CHUNK), table)


def _moe_kernel(x_ref, g_ref, wg_ref, wu_ref, wd_ref, o_ref):
    f = pl.program_id(2)
    x = x_ref[0].astype(BF16)
    a = jnp.dot(x, wg_ref[0, 0].astype(BF16), preferred_element_type=F32)
    u = jnp.dot(x, wu_ref[0, 0].astype(BF16), preferred_element_type=F32)
    hid = (a * jax.nn.sigmoid(a) * u).astype(BF16)
    part = jnp.dot(hid, wd_ref[0, 0].astype(BF16), preferred_element_type=F32)

    @pl.when(f == 0)
    def _():
        o_ref[0] = part

    @pl.when(f != 0)
    def _():
        o_ref[0] += part

    @pl.when(f == pl.num_programs(2) - 1)
    def _():
        o_ref[0] = o_ref[0] * g_ref[0]


def _moe_ffn(xg, gate, w_gate, w_up, w_down, lyr):
    e, m, d = xg.shape
    f = w_gate.shape[3]
    tm = min(m, 1024)
    tf = 512
    return pl.pallas_call(
        _moe_kernel,
        out_shape=jax.ShapeDtypeStruct((e, m, d), F32),
        grid=(e, m // tm, f // tf),
        in_specs=[pl.BlockSpec((1, tm, d), lambda i, j, k: (i, j, 0)),
                  pl.BlockSpec((1, tm, 1), lambda i, j, k: (i, j, 0)),
                  pl.BlockSpec((1, 1, d, tf), lambda i, j, k: (lyr, i, 0, k)),
                  pl.BlockSpec((1, 1, d, tf), lambda i, j, k: (lyr, i, 0, k)),
                  pl.BlockSpec((1, 1, tf, d), lambda i, j, k: (lyr, i, k, 0))],
        out_specs=pl.BlockSpec((1, tm, d), lambda i, j, k: (i, j, 0)),
        compiler_params=_cparams(("parallel", "parallel", "arbitrary")),
        name="moe_ffn",
    )(xg, gate, w_gate, w_up, w_down)


def _expert_choice_ffn(h, aff_t, w_gate, w_up, w_down, lyr):
    b, l, d = h.shape
    cap = EC_CAPACITY * l // N_EXPERTS
    m = b * cap
    g, idx = lax.top_k(aff_t, cap)
    idx_t = jnp.swapaxes(idx, 0, 1)
    rows = (idx_t + (jnp.arange(b, dtype=jnp.int32) * l)[None, :, None]).reshape(-1)
    xg = _row_gather(h.reshape(b * l, d), rows).reshape(N_EXPERTS, m, d)
    gate = jnp.swapaxes(g, 0, 1).reshape(N_EXPERTS, m, 1)
    o = _moe_ffn(xg, gate, w_gate, w_up, w_down, lyr).reshape(N_EXPERTS * m, d)
    src = (jnp.arange(N_EXPERTS, dtype=jnp.int32)[None, :, None] * m
           + jnp.arange(b, dtype=jnp.int32)[:, None, None] * cap
           + jnp.arange(cap, dtype=jnp.int32)[None, None, :])
    tok_s, src_s = lax.sort((idx.reshape(b, -1), src.reshape(b, -1)), dimension=1, num_keys=1)
    return o, tok_s, src_s


def _combine_kernel(*refs, final, tile, nt):
    if final:
        off_ref, src_ref, tok_ref, x_ref, gt_ref, g_ref, o_hbm, out_ref, stage, sems = refs
    else:
        off_ref, src_ref, tok_ref, x_ref, gt_ref, o_hbm, out_ref, stage, sems = refs
    bi, j = pl.program_id(0), pl.program_id(1)
    step = bi * nt + j
    cur = step % 2
    cap_rows = stage.shape[1]
    sub = COMBINE_SUB

    def row_copy(src, dst, s):
        return pltpu.make_async_copy(o_hbm.at[pl.ds(src, 1)], stage.at[s, pl.ds(dst, 1)], sems.at[s])

    def issue(s, start, cnt):
        def group(q, c):
            for r in range(SUBLANES):
                row_copy(src_ref[0, 0, start + q * SUBLANES + r], q * SUBLANES + r, s).start()
            return c
        lax.fori_loop(0, cnt // SUBLANES, group, 0)

    def wait(s, cnt):
        @pl.when(cnt > 0)
        def _():
            rows = pl.ds(0, pl.multiple_of(cnt, SUBLANES))
            pltpu.make_async_copy(o_hbm.at[rows], stage.at[s, rows], sems.at[s]).wait()

    def span(jj):
        start = (off_ref[bi, jj] // LANES) * LANES
        n = off_ref[bi, jj + 1] - start
        return start, ((n + SUBLANES - 1) // SUBLANES) * SUBLANES

    def reduce(s, start, cnt):
        t_ids = lax.broadcasted_iota(jnp.int32, (tile, sub), 0) + j * tile

        def one(k, acc):
            base = pl.multiple_of(k * sub, sub)
            rows = stage[s, pl.ds(base, sub), :]
            toks = tok_ref[0, :, pl.ds(pl.multiple_of(start + base, LANES), sub)]
            onehot = jnp.where(toks == t_ids, 1.0, 0.0).astype(BF16)
            hi = rows.astype(BF16)
            lo = (rows - hi.astype(F32)).astype(BF16)
            return (acc + jnp.dot(onehot, hi, preferred_element_type=F32)
                    + jnp.dot(onehot, lo, preferred_element_type=F32))

        return lax.fori_loop(0, (cnt + sub - 1) // sub, one, jnp.zeros((tile, out_ref.shape[2]), F32))

    @pl.when(step == 0)
    def _():
        stage[...] = jnp.zeros(stage.shape, F32)

    start, n = span(j)
    n0 = jnp.minimum(n, cap_rows)

    @pl.when(j == 0)
    def _():
        issue(cur, start, n0)

    @pl.when(j + 1 < nt)
    def _():
        nstart, nn = span(jnp.minimum(j + 1, nt - 1))
        issue(1 - cur, nstart, jnp.minimum(nn, cap_rows))

    wait(cur, n0)
    acc = reduce(cur, start, n0)

    def extra_round(r, acc):
        st = start + r * cap_rows
        cnt = jnp.minimum(n - r * cap_rows, cap_rows)
        issue(cur, st, cnt)
        wait(cur, cnt)
        return acc + reduce(cur, st, cnt)

    acc = lax.fori_loop(1, (n + cap_rows - 1) // cap_rows, extra_round, acc)
    x = x_ref[0] + gt_ref[0] * acc
    if final:
        x = x * lax.rsqrt(jnp.mean(x * x, axis=-1, keepdims=True) + NORM_EPS) * g_ref[...]
    out_ref[0] = x


def _combine(x, o, tok_s, src_s, gate, final_g=None):
    b, l, d = x.shape
    p = tok_s.shape[1]
    tile = min(l, COMBINE_TILE)
    nt = l // tile
    cap_rows = min(COMBINE_ROUND, -(-(p + LANES) // COMBINE_SUB) * COMBINE_SUB)
    pad = cap_rows + COMBINE_SUB
    bounds = jnp.arange(nt + 1, dtype=jnp.int32) * tile
    off = jnp.sum(tok_s[:, None, :] < bounds[None, :, None], axis=-1, dtype=jnp.int32)
    tok_p = jnp.pad(tok_s, ((0, 0), (0, pad)), constant_values=l)[:, None, :]
    src_p = jnp.pad(src_s, ((0, 0), (0, pad)))[:, None, :]
    bm = gate.shape[0]
    mod_map = (lambda i, j: (i, 0, 0)) if bm > 1 else (lambda i, j: (0, 0, 0))
    tok = pl.BlockSpec((1, tile, d), lambda i, j: (i, j, 0))
    final = final_g is not None
    in_specs = [pl.BlockSpec(memory_space=pltpu.SMEM),
                pl.BlockSpec((1, 1, p + pad), lambda i, j: (i, 0, 0), memory_space=pltpu.SMEM),
                pl.BlockSpec((1, 1, p + pad), lambda i, j: (i, 0, 0)),
                tok, pl.BlockSpec((1, 1, d), mod_map)]
    args = [off, src_p, tok_p, x, gate]
    if final:
        in_specs.append(pl.BlockSpec((1, d), lambda i, j: (0, 0)))
        args.append(final_g)
    in_specs.append(pl.BlockSpec(memory_space=pl.ANY))
    args.append(o)
    return pl.pallas_call(
        functools.partial(_combine_kernel, final=final, tile=tile, nt=nt),
        out_shape=jax.ShapeDtypeStruct((b, l, d), F32),
        grid=(b, nt),
        in_specs=in_specs,
        out_specs=tok,
        scratch_shapes=[pltpu.VMEM((2, cap_rows, d), F32), pltpu.SemaphoreType.DMA((2,))],
        compiler_params=pltpu.CompilerParams(dimension_semantics=("arbitrary", "arbitrary"),
                                             vmem_limit_bytes=VMEM_LIMIT, disable_bounds_checks=True),
        name="moe_combine_final" if final else "moe_combine",
    )(*args)


def _hyena_filters(l, w1, b1, w2, b2, w3, freq):
    hp = lax.Precision.HIGHEST
    pos = jnp.arange(l, dtype=F32)
    t = pos / max(l - 1, 1)
    bands = jnp.linspace(1e-4, HY_BANDS - 1, HY_BANDS, dtype=F32)
    ang = (2 * math.pi / l) * pos[:, None] * bands[None, :]
    feats = jnp.concatenate([t[:, None], jnp.cos(ang), -jnp.sin(ang)], axis=-1)
    h = jnp.sin(freq * (jnp.dot(feats, w1, precision=hp) + b1))
    h = jnp.sin(freq * (jnp.dot(h, w2, precision=hp) + b2))
    h = jnp.dot(h, w3, precision=hp).reshape(l, HY_ORDER, 2, W_GROUP)
    deltas = jnp.linspace(HY_DECAY_MIN, HY_DECAY_MAX, W_GROUP, dtype=F32)
    window = jnp.exp(-t[:, None] * deltas[None, :])
    return h * window[:, None, None, :]


def _two_sided(h_fwd, h_bwd):
    lag0 = h_fwd[:1] + h_bwd[:1]
    return jnp.concatenate([lag0, h_fwd[1:], jnp.zeros_like(h_fwd[:1]), h_bwd[:0:-1]], axis=0)


def _short_conv_kernel(prev_ref, cur_ref, next_ref, w_ref, b_ref, o_ref, *, seq, tl):
    j = pl.program_id(1)
    nt = seq // tl
    prev = jnp.where(j == 0, 0.0, prev_ref[0])
    nxt = jnp.where(j == nt - 1, 0.0, next_ref[0])
    e = jnp.concatenate([prev, cur_ref[0], nxt], axis=0)
    n = tl + 2 * HALO
    y = pltpu.roll(e, 1, 0) * w_ref[0:1, :] + e * w_ref[1:2, :] + pltpu.roll(e, n - 1, 0) * w_ref[2:3, :]
    o_ref[0] = y[HALO:HALO + tl] + b_ref[...]


def _short_conv(u, w, b):
    bsz, l, c = u.shape
    tl = min(l, 512)
    hb = tl // HALO
    nh = l // HALO
    return pl.pallas_call(
        functools.partial(_short_conv_kernel, seq=l, tl=tl),
        out_shape=jax.ShapeDtypeStruct((bsz, l, c), F32),
        grid=(bsz, l // tl),
        in_specs=[pl.BlockSpec((1, HALO, c), lambda i, j: (i, jnp.maximum(j * hb - 1, 0), 0)),
                  pl.BlockSpec((1, tl, c), lambda i, j: (i, j, 0)),
                  pl.BlockSpec((1, HALO, c), lambda i, j: (i, jnp.minimum((j + 1) * hb, nh - 1), 0)),
                  pl.BlockSpec((3, c), lambda i, j: (0, 0)),
                  pl.BlockSpec((1, c), lambda i, j: (0, 0))],
        out_specs=pl.BlockSpec((1, tl, c), lambda i, j: (i, j, 0)),
        compiler_params=_cparams(("parallel", "parallel")),
        name="hyena_short_conv",
    )(u, u, u, w, b)


def _cplx_block(e):
    return np.block([[e.real, -e.imag], [e.imag, e.real]])


@functools.lru_cache(maxsize=None)
def _dft_consts(l):
    n = 2 * l
    n1f = n // DFT_N2
    n1h = n1f // 2
    k1, n1, n2 = np.arange(n1f), np.arange(n1h), np.arange(DFT_N2)
    ph = np.outer(k1, n1)[None] / n1f + (n2[:, None, None] * k1[None, :, None]) / n
    e1 = np.exp(-2j * np.pi * ph)
    w1 = np.stack([_cplx_block(e1[i]) for i in range(DFT_N2)])
    e2 = np.exp(-2j * np.pi * np.outer(n2, n2) / DFT_N2)
    f2 = _cplx_block(e2)
    g2 = _cplx_block(np.conj(e2).T)
    g1 = np.stack([_cplx_block(np.conj(e1[i]).T / n) for i in range(DFT_N2)])
    return tuple(jnp.asarray(a, F32).astype(BF16) for a in (w1, f2, g2, g1))


def _lconv_kernel(a_ref, hf_ref, w1_ref, f2_ref, g2_ref, g1_ref, y_ref, xy_ref, s_ref, *, n1h):
    n1f = 2 * n1h

    def copy_in(i, c):
        for ri in range(2):
            xy_ref[ri, pl.ds(pl.multiple_of(i * X_PITCH, 8), DFT_N2), :] = \
                a_ref[0, ri, pl.ds(pl.multiple_of(i * DFT_N2, DFT_N2), DFT_N2), :]
        return c

    lax.fori_loop(0, n1h, copy_in, 0)

    def stage1(n2, c):
        slab = jnp.concatenate([xy_ref[0, pl.ds(n2, n1h, stride=X_PITCH), :],
                                xy_ref[1, pl.ds(n2, n1h, stride=X_PITCH), :]], axis=0).astype(BF16)
        a = jnp.dot(w1_ref[n2], slab, preferred_element_type=F32)
        base = pl.multiple_of(n2 * S_PITCH, 8)
        s_ref[0, pl.ds(base, n1f), :] = a[:n1f]
        s_ref[1, pl.ds(base, n1f), :] = a[n1f:]
        return c

    lax.fori_loop(0, DFT_N2, stage1, 0)

    def stage2(k1, c):
        slab = jnp.concatenate([s_ref[0, pl.ds(k1, DFT_N2, stride=S_PITCH), :],
                                s_ref[1, pl.ds(k1, DFT_N2, stride=S_PITCH), :]], axis=0).astype(BF16)
        x = jnp.dot(f2_ref[...], slab, preferred_element_type=F32)
        xr, xi = x[:DFT_N2], x[DFT_N2:]
        hr, hi = hf_ref[k1, 0], hf_ref[k1, 1]
        y = jnp.concatenate([xr * hr - xi * hi, xr * hi + xi * hr], axis=0).astype(BF16)
        cc = jnp.dot(g2_ref[...], y, preferred_element_type=F32)
        s_ref[0, pl.ds(k1, DFT_N2, stride=S_PITCH), :] = cc[:DFT_N2]
        s_ref[1, pl.ds(k1, DFT_N2, stride=S_PITCH), :] = cc[DFT_N2:]
        return c

    lax.fori_loop(0, n1f, stage2, 0)

    def stage3(n2, c):
        base = pl.multiple_of(n2 * S_PITCH, 8)
        d = jnp.concatenate([s_ref[0, pl.ds(base, n1f), :], s_ref[1, pl.ds(base, n1f), :]], axis=0).astype(BF16)
        yv = jnp.dot(g1_ref[n2], d, preferred_element_type=F32)
        yb = pl.multiple_of(n2 * Y_PITCH, 8)
        xy_ref[0, pl.ds(yb, n1h), :] = yv[:n1h]
        xy_ref[1, pl.ds(yb, n1h), :] = yv[n1h:]
        return c

    lax.fori_loop(0, DFT_N2, stage3, 0)

    def copy_out(i, c):
        for ri in range(2):
            y_ref[0, ri, pl.ds(pl.multiple_of(i * DFT_N2, DFT_N2), DFT_N2), :] = \
                xy_ref[ri, pl.ds(i, DFT_N2, stride=Y_PITCH), :]
        return c

    lax.fori_loop(0, n1h, copy_out, 0)


def _lconv_small_kernel(a_ref, hf_ref, f_ref, g_ref, y_ref):
    l = a_ref.shape[2]
    slab = jnp.concatenate([a_ref[0, 0], a_ref[0, 1]], axis=0).astype(BF16)
    x = jnp.dot(f_ref[...], slab, preferred_element_type=F32)
    xr, xi = x[:2 * l], x[2 * l:]
    hr, hi = hf_ref[0], hf_ref[1]
    y = jnp.concatenate([xr * hr - xi * hi, xr * hi + xi * hr], axis=0).astype(BF16)
    out = jnp.dot(g_ref[...], y, preferred_element_type=F32)
    y_ref[0, 0] = out[:l]
    y_ref[0, 1] = out[l:]


@functools.lru_cache(maxsize=None)
def _dft_consts_small(l):
    n = 2 * l
    e = np.exp(-2j * np.pi * np.outer(np.arange(n), np.arange(l)) / n)
    f = _cplx_block(e)
    g = _cplx_block(np.conj(e).T / n)
    return jnp.asarray(f, F32).astype(BF16), jnp.asarray(g, F32).astype(BF16)


def _long_conv(a, hf):
    b, l, ca = a.shape
    c = W_GROUP
    a4 = a.reshape(b // 2, 2, l, ca)
    io_spec = pl.BlockSpec((1, 2, l, LANES), lambda j, p: (p, 0, 0, j))
    once = dict(pipeline_mode=pl.Buffered(1))
    const = lambda arr: pl.BlockSpec(arr.shape, lambda j, p: (0,) * arr.ndim, **once)
    if l <= DFT_N2 * 2:
        f, g = _dft_consts_small(l)
        y = pl.pallas_call(
            _lconv_small_kernel,
            out_shape=jax.ShapeDtypeStruct((b // 2, 2, l, c), F32),
            grid=(c // LANES, b // 2),
            in_specs=[io_spec, pl.BlockSpec((2, 2 * l, LANES), lambda j, p: (0, 0, j)), const(f), const(g)],
            out_specs=io_spec,
            compiler_params=_cparams(("parallel", "arbitrary")),
            name="long_conv_small",
        )(a4, hf, f, g)
        return y.reshape(b, l, c)
    n1f = 2 * l // DFT_N2
    n1h = n1f // 2
    assert (n1f + 8, n1h + 8) == (S_PITCH, Y_PITCH), "scratch pitches are sized for this sequence length"
    w1, f2, g2, g1 = _dft_consts(l)
    rows_xy = max(n1h * X_PITCH, DFT_N2 * Y_PITCH)
    y = pl.pallas_call(
        functools.partial(_lconv_kernel, n1h=n1h),
        out_shape=jax.ShapeDtypeStruct((b // 2, 2, l, c), F32),
        grid=(c // LANES, b // 2),
        in_specs=[io_spec, pl.BlockSpec((n1f, 2, DFT_N2, LANES), lambda j, p: (0, 0, 0, j), **once),
                  const(w1), const(f2), const(g2), const(g1)],
        out_specs=io_spec,
        scratch_shapes=[pltpu.VMEM((2, rows_xy, LANES), F32), pltpu.VMEM((2, DFT_N2 * S_PITCH, LANES), F32)],
        compiler_params=_cparams(("parallel", "arbitrary"), 52 * 1024 * 1024),
        name="long_conv",
    )(a4, hf, w1, f2, g2, g1)
    return y.reshape(b, l, c)


def _filter_spectrum(h_fwd, h_bwd):
    filt = _two_sided(h_fwd, h_bwd)
    n, c = filt.shape
    ff = jnp.fft.fft(filt, axis=0)
    if n <= DFT_N2 * 4:
        return jnp.stack([ff.real, ff.imag], axis=0).astype(F32)
    n1f = n // DFT_N2
    ff = jnp.swapaxes(ff.reshape(DFT_N2, n1f, c), 0, 1)
    return jnp.stack([ff.real, ff.imag], axis=1).astype(F32)


def _hy_gate_kernel(y_ref, a_ref, m_ref, sk_ref, o_ref):
    o_ref[0] = (m_ref[0] * (y_ref[0] + a_ref[0] * sk_ref[...])).astype(o_ref.dtype)


def _hy_gate(y, a, a_col, m, m_col, sk, out_dtype):
    b, l, c = y.shape
    tl = min(l, 1024)
    spec = lambda col: pl.BlockSpec((1, tl, c), lambda i, j: (i, j, col))
    return pl.pallas_call(
        _hy_gate_kernel,
        out_shape=jax.ShapeDtypeStruct((b, l, c), out_dtype),
        grid=(b, l // tl),
        in_specs=[spec(0), spec(a_col), spec(m_col), pl.BlockSpec((1, c), lambda i, j: (0, 0))],
        out_specs=spec(0),
        compiler_params=_cparams(("parallel", "parallel")),
        name="hyena_gate",
    )(y, a, m, sk)


def _hyena(u, conv_w, conv_b, filt, skip):
    uc = _short_conv(u, conv_w, conv_b[None, :])
    y1 = _long_conv(uc, _filter_spectrum(filt[:, 0, 0], filt[:, 0, 1]))
    z1 = _hy_gate(y1, uc, 0, uc, 1, skip[0:1], F32)
    y2 = _long_conv(z1, _filter_spectrum(filt[:, 1, 0], filt[:, 1, 1]))
    return _hy_gate(y2, z1, 0, uc, 2, skip[1:2], BF16)


def _layer(x, ctx, mod, lyr, rope_tabs, update_ctx, final_g):
    b, s, d = x.shape
    lc = ctx.shape[1]
    mx = [mod[:b, None, i * d:(i + 1) * d] for i in range(6)]
    mc = [mod[b:b + 1, None, i * d:(i + 1) * d] for i in range(6)]
    g1, g2 = lyr["g1"][None, :], lyr["g2"][None, :]
    w_in = lyr["w_in"].astype(BF16)
    w_out = lyr["w_out"].astype(BF16)
    rw_t = lyr["router_w"].T.astype(BF16)
    w_bd = _pool_weight(lyr["pool_w"])
    pscale = lyr["pool_scale"][None, :]
    hy_args = (lyr["hy_w1"], lyr["hy_b1"], lyr["hy_w2"], lyr["hy_b2"], lyr["hy_w3"], lyr["hy_freq"])

    u_hy, u_pool, q_wa, q_na, kv = _inproj(x, mx[0], mx[1], g1, w_in, rope_tabs)
    if update_ctx:
        cu_hy, cu_pool, cq_wa, cq_na, ckv = _inproj(ctx, mc[0], mc[1], g1, w_in)
    else:
        ckv = _inproj(ctx, mc[0], mc[1], g1, w_in[:, OFF_KV:], kv_only=True)

    ys = [_hyena(u_hy, lyr["hy_conv_w"], lyr["hy_conv_b"], _hyena_filters(s, *hy_args), lyr["hy_skip"]),
          _window_attn(q_wa, kv, ckv, lyr["wa_sink"]),
          _pool_mixer(u_pool, w_bd, pscale),
          _neighbourhood_attn(q_na, kv, ckv, lyr["na_rpb"])]
    x, h, aff = _outproj(ys, x, mx[2], w_out, g2, mx[3], mx[4], rw_t)
    moe = _expert_choice_ffn(h, aff, *lyr["experts"], lyr["index"])
    x = _combine(x, *moe, mx[5], final_g)

    if update_ctx:
        ycs = [_hyena(cu_hy, lyr["hy_conv_w"], lyr["hy_conv_b"], _hyena_filters(lc, *hy_args), lyr["hy_skip"]),
               _ctx_attn(cq_wa, ckv, 0, 1, N_WA_KV, lyr["wa_sink"]),
               _pool_mixer(cu_pool, w_bd, pscale),
               _ctx_attn(cq_na, ckv, 1, 2, N_NA_HEADS, None)]
        ctx, hc, affc = _outproj(ycs, ctx, mc[2], w_out, g2, mc[3], mc[4], rw_t)
        moe_c = _expert_choice_ffn(hc, affc, *lyr["experts"], lyr["index"])
        ctx = _combine(ctx, *moe_c, mc[5])
    return x, ctx


def kernel(x, c, ctx, c_ctx, ada_w, ada_b, norm1_g, norm2_g, w_in, hy_conv_w, hy_conv_b, hy_w1, hy_b1, hy_w2,
           hy_b2, hy_w3, hy_freq, hy_skip, wa_sink, pool_w, pool_scale, na_rpb, w_out, router_w, exp_w_gate,
           exp_w_up, exp_w_down, final_norm_g):
    b, s, d = x.shape
    cs = jnp.zeros((MOD_ROWS, d), F32).at[:b].set(c).at[b].set(c_ctx)
    mods = _ada_mod(cs, ada_w, ada_b[:, None, :])
    rope_tabs = _rope_tables(s)
    params = dict(g1=norm1_g, g2=norm2_g, w_in=w_in, hy_conv_w=hy_conv_w, hy_conv_b=hy_conv_b, hy_w1=hy_w1,
                  hy_b1=hy_b1, hy_w2=hy_w2, hy_b2=hy_b2, hy_w3=hy_w3, hy_freq=hy_freq, hy_skip=hy_skip,
                  wa_sink=wa_sink, pool_w=pool_w, pool_scale=pool_scale, na_rpb=na_rpb, w_out=w_out,
                  router_w=router_w)
    for l in range(DEPTH):
        lyr = {k: v[l] for k, v in params.items()}
        lyr.update(index=l, experts=(exp_w_gate, exp_w_up, exp_w_down))
        last = l == DEPTH - 1
        x, ctx = _layer(x, ctx, mods[l], lyr, rope_tabs, update_ctx=not last,
                        final_g=final_norm_g[None, :] if last else None)
    return x
```

```python
import functools
import math

import jax
import jax.numpy as jnp
import numpy as np
from jax import lax
from jax.experimental import pallas as pl
from jax.experimental.pallas import tpu as pltpu

F32 = jnp.float32
BF16 = jnp.bfloat16

D_MODEL = 1024
DEPTH = 2
GRID_W = 64
HEAD_DIM = 64
W_GROUP = 256
N_WA_HEADS = 4
N_WA_KV = 2
N_NA_HEADS = 4
KV_WA = 128
OFF_POOL = 768
OFF_WA_Q = 1024
OFF_NA_Q = 1280
OFF_KV = 1536
IN_WIDTH = 2304
KV_WIDTH = IN_WIDTH - OFF_KV
HY_ORDER = 2
HY_BANDS = 16
HY_DECAY_MIN = abs(math.log(1e-2) / 1.5)
HY_DECAY_MAX = abs(math.log(1e-2) / 0.3)
WA_BLOCK = 128
WA_STEP_BLOCKS = 4
POOL_WINDOWS = (2, 4, 8, 16)
POOL_GROUP = 64
HALO = 8
LANES = 128
SUBLANES = 8
DFT_N2 = 128
X_PITCH = DFT_N2 + 8
S_PITCH = 64 + 8
Y_PITCH = 32 + 8
NA_ROWS = 8
NA_COLS = 16
NA_ROW_BLOCK = 8
ROPE_BASE = 10000.0
N_EXPERTS = 16
EXPERT_HIDDEN = 2048
EC_CAPACITY = 2
NORM_EPS = 1e-6
NEG_INF = -1e30
Q_SCALE = HEAD_DIM ** -0.5

GATHER_CHUNK = 512
COMBINE_TILE = 256
COMBINE_SUB = 256
COMBINE_ROUND = 1024
MOD_ROWS = 16
VMEM_LIMIT = 48 * 1024 * 1024


def _cparams(sem, vmem=VMEM_LIMIT):
    return pltpu.CompilerParams(dimension_semantics=sem, vmem_limit_bytes=vmem)


def _nt_dot(a, b):
    return lax.dot_general(a, b, (((1,), (1,)), ((), ())), preferred_element_type=F32)


def _rmsnorm_mod(x, g, shift, scale):
    y = x * lax.rsqrt(jnp.mean(x * x, axis=-1, keepdims=True) + NORM_EPS) * g
    return y * (1.0 + scale) + shift


def _ada_kernel(c_ref, w_ref, b_ref, o_ref):
    c = c_ref[...]
    s = (c * jax.nn.sigmoid(c)).astype(BF16)
    o_ref[0] = jnp.dot(s, w_ref[0].astype(BF16), preferred_element_type=F32) + b_ref[0]


def _ada_mod(cs, ada_w, ada_b):
    nl, d, n = ada_w.shape
    tn = 1024
    return pl.pallas_call(
        _ada_kernel,
        out_shape=jax.ShapeDtypeStruct((nl, MOD_ROWS, n), F32),
        grid=(nl, n // tn),
        in_specs=[pl.BlockSpec((MOD_ROWS, d), lambda l, j: (0, 0)),
                  pl.BlockSpec((1, d, tn), lambda l, j: (l, 0, j)),
                  pl.BlockSpec((1, 1, tn), lambda l, j: (l, 0, j))],
        out_specs=pl.BlockSpec((1, MOD_ROWS, tn), lambda l, j: (l, 0, j)),
        compiler_params=_cparams(("parallel", "parallel")),
        name="ada_mod",
    )(cs, ada_w, ada_b)


def _rope(t, cos, s_up, s_dn):
    w = t.shape[-1]
    if w > 128:
        cos, s_up, s_dn = (jnp.concatenate([a] * (w // 128), axis=-1) for a in (cos, s_up, s_dn))
    up = pltpu.roll(t, w - 16, 1)
    dn = pltpu.roll(t, 16, 1)
    return t * cos + up * s_up + dn * s_dn


def _inproj_kernel(*refs, rope, kv_only):
    if rope:
        x_ref, sh_ref, sc_ref, g_ref, w_ref, cos_ref, sup_ref, sdn_ref, *outs = refs
    else:
        x_ref, sh_ref, sc_ref, g_ref, w_ref, *outs = refs
    h = _rmsnorm_mod(x_ref[0], g_ref[...], sh_ref[0], sc_ref[0])
    p = jnp.dot(h.astype(BF16), w_ref[...], preferred_element_type=F32)
    if kv_only:
        outs[0][0] = p.astype(BF16)
        return
    hy_ref, pool_ref, qwa_ref, qna_ref, kv_ref = outs
    hy_ref[0] = p[:, :OFF_POOL]
    pool_ref[0] = p[:, OFF_POOL:OFF_WA_Q]
    qwa = p[:, OFF_WA_Q:OFF_NA_Q]
    kwa = p[:, OFF_KV:OFF_KV + KV_WA]
    if rope:
        tabs = (cos_ref[...], sup_ref[...], sdn_ref[...])
        qwa = _rope(qwa, *tabs)
        kwa = _rope(kwa, *tabs)
    qwa_ref[0] = (qwa * Q_SCALE).astype(BF16)
    qna_ref[0] = (p[:, OFF_NA_Q:OFF_KV] * Q_SCALE).astype(BF16)
    kv_ref[0] = jnp.concatenate([kwa, p[:, OFF_KV + KV_WA:]], axis=-1).astype(BF16)


def _inproj(x, shift, scale, g, w_bf16, rope_tabs=None, kv_only=False):
    b, l, d = x.shape
    n = w_bf16.shape[1]
    tm = min(l, 512)
    rope = rope_tabs is not None
    bm = shift.shape[0]
    mod_map = (lambda j, i: (i, 0, 0)) if bm > 1 else (lambda j, i: (0, 0, 0))
    in_specs = [pl.BlockSpec((1, tm, d), lambda j, i: (i, j, 0)),
                pl.BlockSpec((1, 1, d), mod_map),
                pl.BlockSpec((1, 1, d), mod_map),
                pl.BlockSpec((1, d), lambda j, i: (0, 0)),
                pl.BlockSpec((d, n), lambda j, i: (0, 0))]
    args = [x, shift, scale, g, w_bf16]
    if rope:
        in_specs += [pl.BlockSpec((tm, 128), lambda j, i: (j, 0))] * 3
        args += list(rope_tabs)

    def tok(width, dtype):
        return (jax.ShapeDtypeStruct((b, l, width), dtype),
                pl.BlockSpec((1, tm, width), lambda j, i: (i, j, 0)))

    if kv_only:
        outs = [tok(n, BF16)]
    else:
        outs = [tok(OFF_POOL, F32), tok(W_GROUP, F32), tok(W_GROUP, BF16), tok(W_GROUP, BF16),
                tok(KV_WIDTH, BF16)]
    res = pl.pallas_call(
        functools.partial(_inproj_kernel, rope=rope, kv_only=kv_only),
        out_shape=[o[0] for o in outs],
        grid=(l // tm, b),
        in_specs=in_specs,
        out_specs=[o[1] for o in outs],
        compiler_params=_cparams(("parallel", "parallel")),
        name="inproj_kv" if kv_only else "inproj",
    )(*args)
    return res[0] if kv_only else res


def _rope_tables(s):
    pos = jnp.arange(s)
    p2 = jnp.stack([pos // GRID_W, pos % GRID_W], axis=-1).astype(F32)
    inv = ROPE_BASE ** (-jnp.arange(16, dtype=F32) / 16)
    lane = np.arange(HEAD_DIM)
    ang = p2[:, lane // 32] * inv[lane % 16][None, :]
    first = jnp.asarray((lane % 32) < 16)[None, :]
    cos, sin = jnp.cos(ang), jnp.sin(ang)
    s_up = jnp.where(first, -sin, 0.0)
    s_dn = jnp.where(first, 0.0, sin)
    return tuple(jnp.tile(t, (1, 2)) for t in (cos, s_up, s_dn))


def _softmax_parts(parts, extra=None):
    m = parts[0].max(axis=-1, keepdims=True)
    for s in parts[1:]:
        m = jnp.maximum(m, s.max(axis=-1, keepdims=True))
    if extra is not None:
        m = jnp.maximum(m, extra)
    ps = [jnp.exp(s - m) for s in parts]
    den = ps[0].sum(axis=-1, keepdims=True)
    for p in ps[1:]:
        den = den + p.sum(axis=-1, keepdims=True)
    if extra is not None:
        den = den + jnp.exp(extra - m)
    return ps, den


def _wattn_kernel(sink_ref, q_ref, kp_ref, ko_ref, kn_ref, vp_ref, vo_ref, vn_ref, kc_ref, vc_ref, o_ref, *, nb):
    n = pl.program_id(1)
    blk = WA_BLOCK
    nq = WA_STEP_BLOCKS * blk
    q = q_ref[0]
    kall = jnp.concatenate([kp_ref[0], ko_ref[0], kn_ref[0]], axis=0)
    vall = jnp.concatenate([vp_ref[0], vo_ref[0], vn_ref[0]], axis=0)
    kc, vc = kc_ref[0], vc_ref[0]
    i = lax.broadcasted_iota(jnp.int32, (2 * blk, 3 * blk), 0) & (blk - 1)
    j = lax.broadcasted_iota(jnp.int32, (2 * blk, 3 * blk), 1)
    band = (j >= i) & (j <= i + 2 * blk)
    kvs = [slice(kh * HEAD_DIM, (kh + 1) * HEAD_DIM) for kh in range(N_WA_KV)]
    ctx = []
    for kh, sl in enumerate(kvs):
        q2 = jnp.concatenate([q[:, (2 * kh) * HEAD_DIM:(2 * kh + 1) * HEAD_DIM],
                              q[:, (2 * kh + 1) * HEAD_DIM:(2 * kh + 2) * HEAD_DIM]], axis=0)
        row = lax.broadcasted_iota(jnp.int32, (2 * nq, 1), 0)
        snk = jnp.where(row < nq, sink_ref[2 * kh], sink_ref[2 * kh + 1])
        s_ctx = _nt_dot(q2, kc[:, sl])
        m_ctx = jnp.maximum(s_ctx.max(axis=-1, keepdims=True), snk)
        p_ctx = jnp.exp(s_ctx - m_ctx)
        l_ctx = p_ctx.sum(axis=-1, keepdims=True) + jnp.exp(snk - m_ctx)
        ctx.append((q2, m_ctx, l_ctx, jnp.dot(p_ctx.astype(BF16), vc[:, sl], preferred_element_type=F32)))
    for sb in range(WA_STEP_BLOCKS):
        gb = n * WA_STEP_BLOCKS + sb
        jlo = jnp.where(gb == 0, blk, 0)
        jhi = jnp.where(gb == nb - 1, 2 * blk, 3 * blk)
        valid = band & (j >= jlo) & (j < jhi)
        keys = slice(sb * blk, (sb + 3) * blk)
        outs = []
        for kh, sl in enumerate(kvs):
            q2, m_ctx, l_ctx, o_ctx = ctx[kh]

            def rows(t):
                return jnp.concatenate([t[sb * blk:(sb + 1) * blk], t[nq + sb * blk:nq + (sb + 1) * blk]], axis=0)

            m_c = rows(m_ctx)
            s_loc = jnp.where(valid, _nt_dot(rows(q2), kall[keys, sl]), NEG_INF)
            m = jnp.maximum(s_loc.max(axis=-1, keepdims=True), m_c)
            p_loc = jnp.exp(s_loc - m)
            w_ctx = jnp.exp(m_c - m)
            den = p_loc.sum(axis=-1, keepdims=True) + rows(l_ctx) * w_ctx
            o = (jnp.dot(p_loc.astype(BF16), vall[keys, sl], preferred_element_type=F32) + rows(o_ctx) * w_ctx) / den
            outs += [o[:blk], o[blk:]]
        o_ref[0, sb * blk:(sb + 1) * blk, :] = jnp.concatenate(outs, axis=-1).astype(BF16)


def _window_attn(q, kv, ckv, sink):
    b, s, _ = q.shape
    lc = ckv.shape[1]
    nb = s // WA_BLOCK
    sbk = WA_STEP_BLOCKS

    def halo_spec(col, off):
        return pl.BlockSpec((1, WA_BLOCK, KV_WA),
                            lambda i, n: (i, jnp.clip(n * sbk + off, 0, nb - 1), col))

    def own_spec(col):
        return pl.BlockSpec((1, sbk * WA_BLOCK, KV_WA), lambda i, n: (i, n, col))

    return pl.pallas_call(
        functools.partial(_wattn_kernel, nb=nb),
        out_shape=jax.ShapeDtypeStruct((b, s, W_GROUP), BF16),
        grid=(b, nb // sbk),
        in_specs=[pl.BlockSpec(memory_space=pltpu.SMEM),
                  pl.BlockSpec((1, sbk * WA_BLOCK, W_GROUP), lambda i, n: (i, n, 0)),
                  halo_spec(0, -1), own_spec(0), halo_spec(0, sbk),
                  halo_spec(1, -1), own_spec(1), halo_spec(1, sbk),
                  pl.BlockSpec((1, lc, KV_WA), lambda i, n: (i, 0, 0)),
                  pl.BlockSpec((1, lc, KV_WA), lambda i, n: (i, 0, 1))],
        out_specs=pl.BlockSpec((1, sbk * WA_BLOCK, W_GROUP), lambda i, n: (i, n, 0)),
        compiler_params=_cparams(("parallel", "parallel")),
        name="window_attn",
    )(sink, q, kv, kv, kv, kv, kv, kv, ckv, ckv)


def _nattn_kernel(q_ref, k_ref, v_ref, kc_ref, vc_ref, bias_ref, o_ref, *, rows):
    blk = pl.program_id(1)
    nk = NA_ROWS * GRID_W
    q = q_ref[0]
    kc, vc = kc_ref[0], vc_ref[0]
    heads = [slice(h * HEAD_DIM, (h + 1) * HEAD_DIM) for h in range(N_NA_HEADS)]
    ctx = []
    for sl in heads:
        s_ctx = _nt_dot(q[:, sl], kc[:, sl])
        m_ctx = s_ctx.max(axis=-1, keepdims=True)
        p_ctx = jnp.exp(s_ctx - m_ctx)
        ctx.append((m_ctx, p_ctx.sum(axis=-1, keepdims=True),
                    jnp.dot(p_ctx.astype(BF16), vc[:, sl], preferred_element_type=F32)))
    for rr in range(NA_ROW_BLOCK):
        r = blk * NA_ROW_BLOCK + rr
        r0 = jnp.clip(r - NA_ROWS // 2, 0, rows - NA_ROWS)
        var = r - r0
        start = pl.multiple_of(r0 * GRID_W, GRID_W)
        kt = k_ref[0, pl.ds(start, nk), :]
        vt = v_ref[0, pl.ds(start, nk), :]
        qrow = slice(rr * GRID_W, (rr + 1) * GRID_W)
        outs = []
        for h, sl in enumerate(heads):
            m_ctx, l_ctx, o_ctx = (t[qrow] for t in ctx[h])
            s_loc = _nt_dot(q[qrow, sl], kt[:, sl]) + bias_ref[h, var]
            m = jnp.maximum(s_loc.max(axis=-1, keepdims=True), m_ctx)
            p_loc = jnp.exp(s_loc - m)
            w_ctx = jnp.exp(m_ctx - m)
            den = p_loc.sum(axis=-1, keepdims=True) + l_ctx * w_ctx
            o = jnp.dot(p_loc.astype(BF16), vt[:, sl], preferred_element_type=F32) + o_ctx * w_ctx
            outs.append(o / den)
        o_ref[0, qrow, :] = jnp.concatenate(outs, axis=-1).astype(BF16)


def _na_bias(rpb):
    var = np.arange(NA_ROWS)
    j = np.arange(NA_ROWS)
    qc = np.arange(GRID_W)
    kc = np.arange(GRID_W)
    dr = j[None, :] - var[:, None] + NA_ROWS - 1
    dc = np.clip(kc[None, :] - qc[:, None] + NA_COLS - 1, 0, 2 * NA_COLS - 2)
    ws = np.clip(qc - NA_COLS // 2, 0, GRID_W - NA_COLS)
    ok = (kc[None, :] >= ws[:, None]) & (kc[None, :] < ws[:, None] + NA_COLS)
    onehot = (dc[None] == np.arange(2 * NA_COLS - 1)[:, None, None]).astype(np.float32)
    bias = jnp.einsum("hvjd,dqk->hvqjk", rpb.astype(F32)[:, dr], jnp.asarray(onehot),
                      precision=lax.Precision.HIGHEST)
    bias = jnp.where(jnp.asarray(ok)[None, None, :, None, :], bias, NEG_INF)
    return bias.reshape(rpb.shape[0], NA_ROWS, GRID_W, NA_ROWS * GRID_W)


def _neighbourhood_attn(q, kv, ckv, rpb):
    b, s, _ = q.shape
    lc = ckv.shape[1]
    rows = s // GRID_W
    bias = _na_bias(rpb)
    return pl.pallas_call(
        functools.partial(_nattn_kernel, rows=rows),
        out_shape=jax.ShapeDtypeStruct((b, s, W_GROUP), BF16),
        grid=(b, rows // NA_ROW_BLOCK),
        in_specs=[pl.BlockSpec((1, NA_ROW_BLOCK * GRID_W, W_GROUP), lambda i, r: (i, r, 0)),
                  pl.BlockSpec((1, s, W_GROUP), lambda i, r: (i, 0, 1)),
                  pl.BlockSpec((1, s, W_GROUP), lambda i, r: (i, 0, 2)),
                  pl.BlockSpec((1, lc, W_GROUP), lambda i, r: (i, 0, 1)),
                  pl.BlockSpec((1, lc, W_GROUP), lambda i, r: (i, 0, 2)),
                  pl.BlockSpec(bias.shape, lambda i, r: (0, 0, 0, 0))],
        out_specs=pl.BlockSpec((1, NA_ROW_BLOCK * GRID_W, W_GROUP), lambda i, r: (i, r, 0)),
        compiler_params=_cparams(("parallel", "arbitrary")),
        name="neighbourhood_attn",
    )(q, kv, kv, ckv, ckv, bias)


def _cattn_kernel(*refs, n_kv, with_sink):
    if with_sink:
        sink_ref, q_ref, k_ref, v_ref, o_ref = refs
    else:
        q_ref, k_ref, v_ref, o_ref = refs
    q, k, v = q_ref[0], k_ref[0], v_ref[0]
    group = N_WA_HEADS // n_kv
    outs = []
    for h in range(N_WA_HEADS):
        sl = slice((h // group) * HEAD_DIM, (h // group + 1) * HEAD_DIM)
        s = _nt_dot(q[:, h * HEAD_DIM:(h + 1) * HEAD_DIM], k[:, sl])
        extra = jnp.full((s.shape[0], 1), sink_ref[h], F32) if with_sink else None
        (p,), den = _softmax_parts([s], extra)
        outs.append(jnp.dot(p.astype(BF16), v[:, sl], preferred_element_type=F32) / den)
    o_ref[0] = jnp.concatenate(outs, axis=-1).astype(BF16)


def _ctx_attn(q, ckv, k_col, v_col, n_kv, sink):
    b, lc, _ = q.shape
    w = n_kv * HEAD_DIM
    with_sink = sink is not None
    in_specs = [pl.BlockSpec((1, lc, W_GROUP), lambda i: (i, 0, 0)),
                pl.BlockSpec((1, lc, w), lambda i: (i, 0, k_col)),
                pl.BlockSpec((1, lc, w), lambda i: (i, 0, v_col))]
    args = [q, ckv, ckv]
    if with_sink:
        in_specs = [pl.BlockSpec(memory_space=pltpu.SMEM)] + in_specs
        args = [sink] + args
    return pl.pallas_call(
        functools.partial(_cattn_kernel, n_kv=n_kv, with_sink=with_sink),
        out_shape=jax.ShapeDtypeStruct((b, lc, W_GROUP), BF16),
        grid=(b,),
        in_specs=in_specs,
        out_specs=pl.BlockSpec((1, lc, W_GROUP), lambda i: (i, 0, 0)),
        compiler_params=_cparams(("parallel",)),
        name="ctx_attn",
    )(*args)


def _pool_kernel(prev_ref, cur_ref, next_ref, w_ref, scale_ref, o_ref, *, seq, tl):
    j = pl.program_id(1)
    nt = seq // tl
    cur = cur_ref[0]
    prev = jnp.where(j == 0, 0.0, prev_ref[0])
    nxt = jnp.where(j == nt - 1, 0.0, next_ref[0])
    e = jnp.concatenate([prev, cur, nxt], axis=0)
    n = tl + 2 * HALO

    def sh(a, d):
        return pltpu.roll(a, d % n, 0)

    s2 = e + sh(e, 1)
    s4 = sh(s2, 1) + sh(s2, -1)
    s8 = sh(s4, 2) + sh(s4, -2)
    s16 = sh(s8, 4) + sh(s8, -4)
    lane = lax.broadcasted_iota(jnp.int32, (tl, W_GROUP), 1)
    t = lax.broadcasted_iota(jnp.int32, (tl, W_GROUP), 0) + j * tl
    g = lane // POOL_GROUP
    lo, hi = HALO, HALO + tl
    ssum = jnp.where(g == 0, s2[lo:hi], jnp.where(g == 1, s4[lo:hi], jnp.where(g == 2, s8[lo:hi], s16[lo:hi])))
    half = jnp.where(g == 0, 1, jnp.where(g == 1, 2, jnp.where(g == 2, 4, 8)))
    cnt = (jnp.minimum(t + half, seq) - jnp.maximum(t - half, 0)).astype(F32)
    d = (ssum / cnt - cur).astype(BF16)
    o_ref[0] = (jnp.dot(d, w_ref[...], preferred_element_type=F32) * scale_ref[...]).astype(BF16)


def _pool_mixer(u, w_bd, scale):
    b, l, c = u.shape
    tl = min(l, 512)
    hb = tl // HALO
    nh = l // HALO
    return pl.pallas_call(
        functools.partial(_pool_kernel, seq=l, tl=tl),
        out_shape=jax.ShapeDtypeStruct((b, l, c), BF16),
        grid=(b, l // tl),
        in_specs=[pl.BlockSpec((1, HALO, c), lambda i, j: (i, jnp.maximum(j * hb - 1, 0), 0)),
                  pl.BlockSpec((1, tl, c), lambda i, j: (i, j, 0)),
                  pl.BlockSpec((1, HALO, c), lambda i, j: (i, jnp.minimum((j + 1) * hb, nh - 1), 0)),
                  pl.BlockSpec((c, c), lambda i, j: (0, 0)),
                  pl.BlockSpec((1, c), lambda i, j: (0, 0))],
        out_specs=pl.BlockSpec((1, tl, c), lambda i, j: (i, j, 0)),
        compiler_params=_cparams(("parallel", "parallel")),
        name="pool_mixer",
    )(u, u, u, w_bd, scale)


def _pool_weight(pool_w):
    z = jnp.zeros((W_GROUP, W_GROUP), F32)
    for g in range(len(POOL_WINDOWS)):
        z = z.at[g * POOL_GROUP:(g + 1) * POOL_GROUP, g * POOL_GROUP:(g + 1) * POOL_GROUP].set(pool_w[g])
    return z.astype(BF16)


def _outproj_kernel(yh_ref, ya_ref, yp_ref, yn_ref, x_ref, gt_ref, w_ref, g2_ref, sh_ref, sc_ref, rw_ref,
                    xo_ref, h_ref, aff_ref):
    y = jnp.concatenate([yh_ref[0], ya_ref[0], yp_ref[0], yn_ref[0]], axis=-1)
    x = x_ref[0] + gt_ref[0] * jnp.dot(y, w_ref[...], preferred_element_type=F32)
    xo_ref[0] = x
    h = _rmsnorm_mod(x, g2_ref[...], sh_ref[0], sc_ref[0])
    h_ref[0] = h
    logits = _nt_dot(rw_ref[...], h.astype(BF16))
    m = logits.max(axis=0, keepdims=True)
    p = jnp.exp(logits - m)
    aff_ref[0] = p / p.sum(axis=0, keepdims=True)


def _outproj(ys, x, gate, w_bf16, g2, shift, scale, rw_t):
    b, l, d = x.shape
    tm = min(l, 512)
    bm = gate.shape[0]
    mod_map = (lambda i, j: (i, 0, 0)) if bm > 1 else (lambda i, j: (0, 0, 0))
    tok = lambda w: pl.BlockSpec((1, tm, w), lambda i, j: (i, j, 0))
    mod = pl.BlockSpec((1, 1, d), mod_map)
    return pl.pallas_call(
        _outproj_kernel,
        out_shape=[jax.ShapeDtypeStruct((b, l, d), F32), jax.ShapeDtypeStruct((b, l, d), F32),
                   jax.ShapeDtypeStruct((b, N_EXPERTS, l), F32)],
        grid=(b, l // tm),
        in_specs=[tok(W_GROUP)] * 4 + [tok(d), mod, pl.BlockSpec((d, d), lambda i, j: (0, 0)),
                                        pl.BlockSpec((1, d), lambda i, j: (0, 0)), mod, mod,
                                        pl.BlockSpec((N_EXPERTS, d), lambda i, j: (0, 0))],
        out_specs=[tok(d), tok(d), pl.BlockSpec((1, N_EXPERTS, tm), lambda i, j: (i, 0, j))],
        compiler_params=_cparams(("parallel", "parallel")),
        name="outproj_router",
    )(*ys, x, gate, w_bf16, g2, shift, scale, rw_t)


def _row_gather_kernel(idx_ref, table_ref, out_ref, sem):
    def issue(r, c):
        pltpu.make_async_copy(table_ref.at[pl.ds(idx_ref[0, 0, r], 1)], out_ref.at[pl.ds(r, 1)], sem).start()
        return c

    lax.fori_loop(0, GATHER_CHUNK, issue, 0, unroll=8)
    pltpu.make_async_copy(table_ref.at[pl.ds(0, GATHER_CHUNK)], out_ref, sem).wait()


def _row_gather(table, rows):
    v, d = table.shape
    n = rows.shape[0]
    nch = n // GATHER_CHUNK
    return pl.pallas_call(
        _row_gather_kernel,
        out_shape=jax.ShapeDtypeStruct((n, d), table.dtype),
        grid=(nch,),
        in_specs=[pl.BlockSpec((1, 1, GATHER_CHUNK), lambda i: (i, 0, 0), memory_space=pltpu.SMEM),
                  pl.BlockSpec(memory_space=pl.ANY)],
        out_specs=pl.BlockSpec((GATHER_CHUNK, d), lambda i: (i, 0)),
        scratch_shapes=[pltpu.SemaphoreType.DMA(())],
        compiler_params=pltpu.CompilerParams(dimension_semantics=("arbitrary",), vmem_limit_bytes=VMEM_LIMIT,
                                             disable_bounds_checks=True),
        name="moe_row_gather",
    )(rows.reshape(nch, 1, GATHER_CHUNK), table)


def _moe_kernel(x_ref, g_ref, wg_ref, wu_ref, wd_ref, o_ref):
    f = pl.program_id(2)
    x = x_ref[0].astype(BF16)
    a = jnp.dot(x, wg_ref[0, 0].astype(BF16), preferred_element_type=F32)
    u = jnp.dot(x, wu_ref[0, 0].astype(BF16), preferred_element_type=F32)
    hid = (a * jax.nn.sigmoid(a) * u).astype(BF16)
    part = jnp.dot(hid, wd_ref[0, 0].astype(BF16), preferred_element_type=F32)

    @pl.when(f == 0)
    def _():
        o_ref[0] = part

    @pl.when(f != 0)
    def _():
        o_ref[0] += part

    @pl.when(f == pl.num_programs(2) - 1)
    def _():
        o_ref[0] = o_ref[0] * g_ref[0]


def _moe_ffn(xg, gate, w_gate, w_up, w_down, lyr):
    e, m, d = xg.shape
    f = w_gate.shape[3]
    tm = min(m, 1024)
    tf = 512
    return pl.pallas_call(
        _moe_kernel,
        out_shape=jax.ShapeDtypeStruct((e, m, d), F32),
        grid=(e, m // tm, f // tf),
        in_specs=[pl.BlockSpec((1, tm, d), lambda i, j, k: (i, j, 0)),
                  pl.BlockSpec((1, tm, 1), lambda i, j, k: (i, j, 0)),
                  pl.BlockSpec((1, 1, d, tf), lambda i, j, k: (lyr, i, 0, k)),
                  pl.BlockSpec((1, 1, d, tf), lambda i, j, k: (lyr, i, 0, k)),
                  pl.BlockSpec((1, 1, tf, d), lambda i, j, k: (lyr, i, k, 0))],
        out_specs=pl.BlockSpec((1, tm, d), lambda i, j, k: (i, j, 0)),
        compiler_params=_cparams(("parallel", "parallel", "arbitrary")),
        name="moe_ffn",
    )(xg, gate, w_gate, w_up, w_down)


def _expert_choice_ffn(h, aff_t, w_gate, w_up, w_down, lyr):
    b, l, d = h.shape
    cap = EC_CAPACITY * l // N_EXPERTS
    m = b * cap
    g, idx = lax.top_k(aff_t, cap)
    idx_t = jnp.swapaxes(idx, 0, 1)
    rows = (idx_t + (jnp.arange(b, dtype=jnp.int32) * l)[None, :, None]).reshape(-1)
    xg = _row_gather(h.reshape(b * l, d), rows).reshape(N_EXPERTS, m, d)
    gate = jnp.swapaxes(g, 0, 1).reshape(N_EXPERTS, m, 1)
    o = _moe_ffn(xg, gate, w_gate, w_up, w_down, lyr).reshape(N_EXPERTS * m, d)
    src = (jnp.arange(N_EXPERTS, dtype=jnp.int32)[None, :, None] * m
           + jnp.arange(b, dtype=jnp.int32)[:, None, None] * cap
           + jnp.arange(cap, dtype=jnp.int32)[None, None, :])
    tok_s, src_s = lax.sort((idx.reshape(b, -1), src.reshape(b, -1)), dimension=1, num_keys=1)
    return o, tok_s, src_s


def _combine_kernel(*refs, final, tile, nt):
    if final:
        off_ref, src_ref, tok_ref, x_ref, gt_ref, g_ref, o_hbm, out_ref, stage, sems = refs
    else:
        off_ref, src_ref, tok_ref, x_ref, gt_ref, o_hbm, out_ref, stage, sems = refs
    bi, j = pl.program_id(0), pl.program_id(1)
    step = bi * nt + j
    cur = step % 2
    cap_rows = stage.shape[1]
    sub = COMBINE_SUB

    def row_copy(src, dst, s):
        return pltpu.make_async_copy(o_hbm.at[pl.ds(src, 1)], stage.at[s, pl.ds(dst, 1)], sems.at[s])

    def issue(s, start, cnt):
        def group(q, c):
            for r in range(SUBLANES):
                row_copy(src_ref[0, 0, start + q * SUBLANES + r], q * SUBLANES + r, s).start()
            return c
        lax.fori_loop(0, cnt // SUBLANES, group, 0)

    def wait(s, cnt):
        @pl.when(cnt > 0)
        def _():
            rows = pl.ds(0, pl.multiple_of(cnt, SUBLANES))
            pltpu.make_async_copy(o_hbm.at[rows], stage.at[s, rows], sems.at[s]).wait()

    def span(jj):
        start = (off_ref[bi, jj] // LANES) * LANES
        n = off_ref[bi, jj + 1] - start
        return start, ((n + SUBLANES - 1) // SUBLANES) * SUBLANES

    def reduce(s, start, cnt):
        t_ids = lax.broadcasted_iota(jnp.int32, (tile, sub), 0) + j * tile

        def one(k, acc):
            base = pl.multiple_of(k * sub, sub)
            rows = stage[s, pl.ds(base, sub), :]
            toks = tok_ref[0, :, pl.ds(pl.multiple_of(start + base, LANES), sub)]
            onehot = jnp.where(toks == t_ids, 1.0, 0.0).astype(BF16)
            hi = rows.astype(BF16)
            lo = (rows - hi.astype(F32)).astype(BF16)
            return (acc + jnp.dot(onehot, hi, preferred_element_type=F32)
                    + jnp.dot(onehot, lo, preferred_element_type=F32))

        return lax.fori_loop(0, (cnt + sub - 1) // sub, one, jnp.zeros((tile, out_ref.shape[2]), F32))

    @pl.when(step == 0)
    def _():
        stage[...] = jnp.zeros(stage.shape, F32)

    start, n = span(j)
    n0 = jnp.minimum(n, cap_rows)

    @pl.when(j == 0)
    def _():
        issue(cur, start, n0)

    @pl.when(j + 1 < nt)
    def _():
        nstart, nn = span(jnp.minimum(j + 1, nt - 1))
        issue(1 - cur, nstart, jnp.minimum(nn, cap_rows))

    wait(cur, n0)
    acc = reduce(cur, start, n0)

    def extra_round(r, acc):
        st = start + r * cap_rows
        cnt = jnp.minimum(n - r * cap_rows, cap_rows)
        issue(cur, st, cnt)
        wait(cur, cnt)
        return acc + reduce(cur, st, cnt)

    acc = lax.fori_loop(1, (n + cap_rows - 1) // cap_rows, extra_round, acc)
    x = x_ref[0] + gt_ref[0] * acc
    if final:
        x = x * lax.rsqrt(jnp.mean(x * x, axis=-1, keepdims=True) + NORM_EPS) * g_ref[...]
    out_ref[0] = x


def _combine(x, o, tok_s, src_s, gate, final_g=None):
    b, l, d = x.shape
    p = tok_s.shape[1]
    tile = min(l, COMBINE_TILE)
    nt = l // tile
    cap_rows = min(COMBINE_ROUND, -(-(p + LANES) // COMBINE_SUB) * COMBINE_SUB)
    pad = cap_rows + COMBINE_SUB
    bounds = jnp.arange(nt + 1, dtype=jnp.int32) * tile
    off = jnp.sum(tok_s[:, None, :] < bounds[None, :, None], axis=-1, dtype=jnp.int32)
    tok_p = jnp.pad(tok_s, ((0, 0), (0, pad)), constant_values=l)[:, None, :]
    src_p = jnp.pad(src_s, ((0, 0), (0, pad)))[:, None, :]
    bm = gate.shape[0]
    mod_map = (lambda i, j: (i, 0, 0)) if bm > 1 else (lambda i, j: (0, 0, 0))
    tok = pl.BlockSpec((1, tile, d), lambda i, j: (i, j, 0))
    final = final_g is not None
    in_specs = [pl.BlockSpec(memory_space=pltpu.SMEM),
                pl.BlockSpec((1, 1, p + pad), lambda i, j: (i, 0, 0), memory_space=pltpu.SMEM),
                pl.BlockSpec((1, 1, p + pad), lambda i, j: (i, 0, 0)),
                tok, pl.BlockSpec((1, 1, d), mod_map)]
    args = [off, src_p, tok_p, x, gate]
    if final:
        in_specs.append(pl.BlockSpec((1, d), lambda i, j: (0, 0)))
        args.append(final_g)
    in_specs.append(pl.BlockSpec(memory_space=pl.ANY))
    args.append(o)
    return pl.pallas_call(
        functools.partial(_combine_kernel, final=final, tile=tile, nt=nt),
        out_shape=jax.ShapeDtypeStruct((b, l, d), F32),
        grid=(b, nt),
        in_specs=in_specs,
        out_specs=tok,
        scratch_shapes=[pltpu.VMEM((2, cap_rows, d), F32), pltpu.SemaphoreType.DMA((2,))],
        compiler_params=pltpu.CompilerParams(dimension_semantics=("arbitrary", "arbitrary"),
                                             vmem_limit_bytes=VMEM_LIMIT, disable_bounds_checks=True),
        name="moe_combine_final" if final else "moe_combine",
    )(*args)


def _hyena_filters(l, w1, b1, w2, b2, w3, freq):
    hp = lax.Precision.HIGHEST
    pos = jnp.arange(l, dtype=F32)
    t = pos / max(l - 1, 1)
    bands = jnp.linspace(1e-4, HY_BANDS - 1, HY_BANDS, dtype=F32)
    ang = (2 * math.pi / l) * pos[:, None] * bands[None, :]
    feats = jnp.concatenate([t[:, None], jnp.cos(ang), -jnp.sin(ang)], axis=-1)
    h = jnp.sin(freq * (jnp.dot(feats, w1, precision=hp) + b1))
    h = jnp.sin(freq * (jnp.dot(h, w2, precision=hp) + b2))
    h = jnp.dot(h, w3, precision=hp).reshape(l, HY_ORDER, 2, W_GROUP)
    deltas = jnp.linspace(HY_DECAY_MIN, HY_DECAY_MAX, W_GROUP, dtype=F32)
    window = jnp.exp(-t[:, None] * deltas[None, :])
    return h * window[:, None, None, :]


def _two_sided(h_fwd, h_bwd):
    lag0 = h_fwd[:1] + h_bwd[:1]
    return jnp.concatenate([lag0, h_fwd[1:], jnp.zeros_like(h_fwd[:1]), h_bwd[:0:-1]], axis=0)


def _short_conv_kernel(prev_ref, cur_ref, next_ref, w_ref, b_ref, o_ref, *, seq, tl):
    j = pl.program_id(1)
    nt = seq // tl
    prev = jnp.where(j == 0, 0.0, prev_ref[0])
    nxt = jnp.where(j == nt - 1, 0.0, next_ref[0])
    e = jnp.concatenate([prev, cur_ref[0], nxt], axis=0)
    n = tl + 2 * HALO
    y = pltpu.roll(e, 1, 0) * w_ref[0:1, :] + e * w_ref[1:2, :] + pltpu.roll(e, n - 1, 0) * w_ref[2:3, :]
    o_ref[0] = y[HALO:HALO + tl] + b_ref[...]


def _short_conv(u, w, b):
    bsz, l, c = u.shape
    tl = min(l, 512)
    hb = tl // HALO
    nh = l // HALO
    return pl.pallas_call(
        functools.partial(_short_conv_kernel, seq=l, tl=tl),
        out_shape=jax.ShapeDtypeStruct((bsz, l, c), F32),
        grid=(bsz, l // tl),
        in_specs=[pl.BlockSpec((1, HALO, c), lambda i, j: (i, jnp.maximum(j * hb - 1, 0), 0)),
                  pl.BlockSpec((1, tl, c), lambda i, j: (i, j, 0)),
                  pl.BlockSpec((1, HALO, c), lambda i, j: (i, jnp.minimum((j + 1) * hb, nh - 1), 0)),
                  pl.BlockSpec((3, c), lambda i, j: (0, 0)),
                  pl.BlockSpec((1, c), lambda i, j: (0, 0))],
        out_specs=pl.BlockSpec((1, tl, c), lambda i, j: (i, j, 0)),
        compiler_params=_cparams(("parallel", "parallel")),
        name="hyena_short_conv",
    )(u, u, u, w, b)


def _cplx_block(e):
    return np.block([[e.real, -e.imag], [e.imag, e.real]])


@functools.lru_cache(maxsize=None)
def _dft_consts(l):
    n = 2 * l
    n1f = n // DFT_N2
    n1h = n1f // 2
    k1, n1, n2 = np.arange(n1f), np.arange(n1h), np.arange(DFT_N2)
    ph = np.outer(k1, n1)[None] / n1f + (n2[:, None, None] * k1[None, :, None]) / n
    e1 = np.exp(-2j * np.pi * ph)
    w1 = np.stack([_cplx_block(e1[i]) for i in range(DFT_N2)])
    e2 = np.exp(-2j * np.pi * np.outer(n2, n2) / DFT_N2)
    f2 = _cplx_block(e2)
    g2 = _cplx_block(np.conj(e2).T)
    g1 = np.stack([_cplx_block(np.conj(e1[i]).T / n) for i in range(DFT_N2)])
    return tuple(jnp.asarray(a, F32).astype(BF16) for a in (w1, f2, g2, g1))


def _lconv_kernel(a_ref, hf_ref, w1_ref, f2_ref, g2_ref, g1_ref, y_ref, xy_ref, s_ref, *, n1h):
    n1f = 2 * n1h

    def copy_in(i, c):
        for ri in range(2):
            xy_ref[ri, pl.ds(pl.multiple_of(i * X_PITCH, 8), DFT_N2), :] = \
                a_ref[0, ri, pl.ds(pl.multiple_of(i * DFT_N2, DFT_N2), DFT_N2), :]
        return c

    lax.fori_loop(0, n1h, copy_in, 0, unroll=4)

    def stage1(n2, c):
        slab = jnp.concatenate([xy_ref[0, pl.ds(n2, n1h, stride=X_PITCH), :],
                                xy_ref[1, pl.ds(n2, n1h, stride=X_PITCH), :]], axis=0).astype(BF16)
        a = jnp.dot(w1_ref[n2], slab, preferred_element_type=F32)
        base = pl.multiple_of(n2 * S_PITCH, 8)
        s_ref[0, pl.ds(base, n1f), :] = a[:n1f]
        s_ref[1, pl.ds(base, n1f), :] = a[n1f:]
        return c

    lax.fori_loop(0, DFT_N2, stage1, 0, unroll=32)

    def stage2(k1, c):
        slab = jnp.concatenate([s_ref[0, pl.ds(k1, DFT_N2, stride=S_PITCH), :],
                                s_ref[1, pl.ds(k1, DFT_N2, stride=S_PITCH), :]], axis=0).astype(BF16)
        x = jnp.dot(f2_ref[...], slab, preferred_element_type=F32)
        xr, xi = x[:DFT_N2], x[DFT_N2:]
        hr, hi = hf_ref[k1, 0], hf_ref[k1, 1]
        y = jnp.concatenate([xr * hr - xi * hi, xr * hi + xi * hr], axis=0).astype(BF16)
        cc = jnp.dot(g2_ref[...], y, preferred_element_type=F32)
        s_ref[0, pl.ds(k1, DFT_N2, stride=S_PITCH), :] = cc[:DFT_N2]
        s_ref[1, pl.ds(k1, DFT_N2, stride=S_PITCH), :] = cc[DFT_N2:]
        return c

    lax.fori_loop(0, n1f, stage2, 0, unroll=16)

    def stage3(n2, c):
        base = pl.multiple_of(n2 * S_PITCH, 8)
        d = jnp.concatenate([s_ref[0, pl.ds(base, n1f), :], s_ref[1, pl.ds(base, n1f), :]], axis=0).astype(BF16)
        yv = jnp.dot(g1_ref[n2], d, preferred_element_type=F32)
        yb = pl.multiple_of(n2 * Y_PITCH, 8)
        xy_ref[0, pl.ds(yb, n1h), :] = yv[:n1h]
        xy_ref[1, pl.ds(yb, n1h), :] = yv[n1h:]
        return c

    lax.fori_loop(0, DFT_N2, stage3, 0, unroll=32)

    def copy_out(i, c):
        for ri in range(2):
            y_ref[0, ri, pl.ds(pl.multiple_of(i * DFT_N2, DFT_N2), DFT_N2), :] = \
                xy_ref[ri, pl.ds(i, DFT_N2, stride=Y_PITCH), :]
        return c

    lax.fori_loop(0, n1h, copy_out, 0, unroll=4)


def _lconv_small_kernel(a_ref, hf_ref, f_ref, g_ref, y_ref):
    l = a_ref.shape[2]
    slab = jnp.concatenate([a_ref[0, 0], a_ref[0, 1]], axis=0).astype(BF16)
    x = jnp.dot(f_ref[...], slab, preferred_element_type=F32)
    xr, xi = x[:2 * l], x[2 * l:]
    hr, hi = hf_ref[0], hf_ref[1]
    y = jnp.concatenate([xr * hr - xi * hi, xr * hi + xi * hr], axis=0).astype(BF16)
    out = jnp.dot(g_ref[...], y, preferred_element_type=F32)
    y_ref[0, 0] = out[:l]
    y_ref[0, 1] = out[l:]


@functools.lru_cache(maxsize=None)
def _dft_consts_small(l):
    n = 2 * l
    e = np.exp(-2j * np.pi * np.outer(np.arange(n), np.arange(l)) / n)
    f = _cplx_block(e)
    g = _cplx_block(np.conj(e).T / n)
    return jnp.asarray(f, F32).astype(BF16), jnp.asarray(g, F32).astype(BF16)


def _long_conv(a, hf):
    b, l, ca = a.shape
    c = W_GROUP
    a4 = a.reshape(b // 2, 2, l, ca)
    io_spec = pl.BlockSpec((1, 2, l, LANES), lambda j, p: (p, 0, 0, j))
    once = dict(pipeline_mode=pl.Buffered(1))
    const = lambda arr: pl.BlockSpec(arr.shape, lambda j, p: (0,) * arr.ndim, **once)
    if l <= DFT_N2 * 2:
        f, g = _dft_consts_small(l)
        y = pl.pallas_call(
            _lconv_small_kernel,
            out_shape=jax.ShapeDtypeStruct((b // 2, 2, l, c), F32),
            grid=(c // LANES, b // 2),
            in_specs=[io_spec, pl.BlockSpec((2, 2 * l, LANES), lambda j, p: (0, 0, j)), const(f), const(g)],
            out_specs=io_spec,
            compiler_params=_cparams(("parallel", "arbitrary")),
            name="long_conv_small",
        )(a4, hf, f, g)
        return y.reshape(b, l, c)
    n1f = 2 * l // DFT_N2
    n1h = n1f // 2
    assert (n1f + 8, n1h + 8) == (S_PITCH, Y_PITCH), "scratch pitches are sized for this sequence length"
    w1, f2, g2, g1 = _dft_consts(l)
    rows_xy = max(n1h * X_PITCH, DFT_N2 * Y_PITCH)
    y = pl.pallas_call(
        functools.partial(_lconv_kernel, n1h=n1h),
        out_shape=jax.ShapeDtypeStruct((b // 2, 2, l, c), F32),
        grid=(c // LANES, b // 2),
        in_specs=[io_spec, pl.BlockSpec((n1f, 2, DFT_N2, LANES), lambda j, p: (0, 0, 0, j), **once),
                  const(w1), const(f2), const(g2), const(g1)],
        out_specs=io_spec,
        scratch_shapes=[pltpu.VMEM((2, rows_xy, LANES), F32), pltpu.VMEM((2, DFT_N2 * S_PITCH, LANES), F32)],
        compiler_params=_cparams(("parallel", "arbitrary"), 52 * 1024 * 1024),
        name="long_conv",
    )(a4, hf, w1, f2, g2, g1)
    return y.reshape(b, l, c)


def _filter_spectrum(h_fwd, h_bwd):
    filt = _two_sided(h_fwd, h_bwd)
    n, c = filt.shape
    ff = jnp.fft.fft(filt, axis=0)
    if n <= DFT_N2 * 4:
        return jnp.stack([ff.real, ff.imag], axis=0).astype(F32)
    n1f = n // DFT_N2
    ff = jnp.swapaxes(ff.reshape(DFT_N2, n1f, c), 0, 1)
    return jnp.stack([ff.real, ff.imag], axis=1).astype(F32)


def _hy_gate_kernel(y_ref, a_ref, m_ref, sk_ref, o_ref):
    o_ref[0] = (m_ref[0] * (y_ref[0] + a_ref[0] * sk_ref[...])).astype(o_ref.dtype)


def _hy_gate(y, a, a_col, m, m_col, sk, out_dtype):
    b, l, c = y.shape
    tl = min(l, 1024)
    spec = lambda col: pl.BlockSpec((1, tl, c), lambda i, j: (i, j, col))
    return pl.pallas_call(
        _hy_gate_kernel,
        out_shape=jax.ShapeDtypeStruct((b, l, c), out_dtype),
        grid=(b, l // tl),
        in_specs=[spec(0), spec(a_col), spec(m_col), pl.BlockSpec((1, c), lambda i, j: (0, 0))],
        out_specs=spec(0),
        compiler_params=_cparams(("parallel", "parallel")),
        name="hyena_gate",
    )(y, a, m, sk)


def _hyena(u, conv_w, conv_b, filt, skip):
    uc = _short_conv(u, conv_w, conv_b[None, :])
    y1 = _long_conv(uc, _filter_spectrum(filt[:, 0, 0], filt[:, 0, 1]))
    z1 = _hy_gate(y1, uc, 0, uc, 1, skip[0:1], F32)
    y2 = _long_conv(z1, _filter_spectrum(filt[:, 1, 0], filt[:, 1, 1]))
    return _hy_gate(y2, z1, 0, uc, 2, skip[1:2], BF16)


def _layer(x, ctx, mod, lyr, rope_tabs, update_ctx, final_g):
    b, s, d = x.shape
    lc = ctx.shape[1]
    mx = [mod[:b, None, i * d:(i + 1) * d] for i in range(6)]
    mc = [mod[b:b + 1, None, i * d:(i + 1) * d] for i in range(6)]
    g1, g2 = lyr["g1"][None, :], lyr["g2"][None, :]
    w_in = lyr["w_in"].astype(BF16)
    w_out = lyr["w_out"].astype(BF16)
    rw_t = lyr["router_w"].T.astype(BF16)
    w_bd = _pool_weight(lyr["pool_w"])
    pscale = lyr["pool_scale"][None, :]
    hy_args = (lyr["hy_w1"], lyr["hy_b1"], lyr["hy_w2"], lyr["hy_b2"], lyr["hy_w3"], lyr["hy_freq"])

    u_hy, u_pool, q_wa, q_na, kv = _inproj(x, mx[0], mx[1], g1, w_in, rope_tabs)
    if update_ctx:
        cu_hy, cu_pool, cq_wa, cq_na, ckv = _inproj(ctx, mc[0], mc[1], g1, w_in)
    else:
        ckv = _inproj(ctx, mc[0], mc[1], g1, w_in[:, OFF_KV:], kv_only=True)

    ys = [_hyena(u_hy, lyr["hy_conv_w"], lyr["hy_conv_b"], _hyena_filters(s, *hy_args), lyr["hy_skip"]),
          _window_attn(q_wa, kv, ckv, lyr["wa_sink"]),
          _pool_mixer(u_pool, w_bd, pscale),
          _neighbourhood_attn(q_na, kv, ckv, lyr["na_rpb"])]
    x, h, aff = _outproj(ys, x, mx[2], w_out, g2, mx[3], mx[4], rw_t)
    moe = _expert_choice_ffn(h, aff, *lyr["experts"], lyr["index"])
    x = _combine(x, *moe, mx[5], final_g)

    if update_ctx:
        ycs = [_hyena(cu_hy, lyr["hy_conv_w"], lyr["hy_conv_b"], _hyena_filters(lc, *hy_args), lyr["hy_skip"]),
               _ctx_attn(cq_wa, ckv, 0, 1, N_WA_KV, lyr["wa_sink"]),
               _pool_mixer(cu_pool, w_bd, pscale),
               _ctx_attn(cq_na, ckv, 1, 2, N_NA_HEADS, None)]
        ctx, hc, affc = _outproj(ycs, ctx, mc[2], w_out, g2, mc[3], mc[4], rw_t)
        moe_c = _expert_choice_ffn(hc, affc, *lyr["experts"], lyr["index"])
        ctx = _combine(ctx, *moe_c, mc[5])
    return x, ctx


def kernel(x, c, ctx, c_ctx, ada_w, ada_b, norm1_g, norm2_g, w_in, hy_conv_w, hy_conv_b, hy_w1, hy_b1, hy_w2,
           hy_b2, hy_w3, hy_freq, hy_skip, wa_sink, pool_w, pool_scale, na_rpb, w_out, router_w, exp_w_gate,
           exp_w_up, exp_w_down, final_norm_g):
    b, s, d = x.shape
    cs = jnp.zeros((MOD_ROWS, d), F32).at[:b].set(c).at[b].set(c_ctx)
    mods = _ada_mod(cs, ada_w, ada_b[:, None, :])
    rope_tabs = _rope_tables(s)
    params = dict(g1=norm1_g, g2=norm2_g, w_in=w_in, hy_conv_w=hy_conv_w, hy_conv_b=hy_conv_b, hy_w1=hy_w1,
                  hy_b1=hy_b1, hy_w2=hy_w2, hy_b2=hy_b2, hy_w3=hy_w3, hy_freq=hy_freq, hy_skip=hy_skip,
                  wa_sink=wa_sink, pool_w=pool_w, pool_scale=pool_scale, na_rpb=na_rpb, w_out=w_out,
                  router_w=router_w)
    for l in range(DEPTH):
        lyr = {k: v[l] for k, v in params.items()}
        lyr.update(index=l, experts=(exp_w_gate, exp_w_up, exp_w_down))
        last = l == DEPTH - 1
        x, ctx = _layer(x, ctx, mods[l], lyr, rope_tabs, update_ctx=not last,
                        final_g=final_norm_g[None, :] if last else None)
    return x
```

```python
import functools
import math

import jax
import jax.numpy as jnp
import numpy as np
from jax import lax
from jax.experimental import pallas as pl
from jax.experimental.pallas import tpu as pltpu

F32 = jnp.float32
BF16 = jnp.bfloat16

D_MODEL = 1024
DEPTH = 2
GRID_W = 64
HEAD_DIM = 64
W_GROUP = 256
N_WA_HEADS = 4
N_WA_KV = 2
N_NA_HEADS = 4
KV_WA = 128
OFF_POOL = 768
OFF_WA_Q = 1024
OFF_NA_Q = 1280
OFF_KV = 1536
IN_WIDTH = 2304
KV_WIDTH = IN_WIDTH - OFF_KV
HY_ORDER = 2
HY_BANDS = 16
HY_DECAY_MIN = abs(math.log(1e-2) / 1.5)
HY_DECAY_MAX = abs(math.log(1e-2) / 0.3)
WA_BLOCK = 128
WA_STEP_BLOCKS = 4
WA_HEAD_ORDER = (0, 2, 1, 3)
POOL_WINDOWS = (2, 4, 8, 16)
POOL_GROUP = 64
HALO = 8
LANES = 128
SUBLANES = 8
DFT_N2 = 128
X_PITCH = DFT_N2 + 8
S_PITCH = 64 + 8
Y_PITCH = 32 + 8
NA_ROWS = 8
NA_COLS = 16
NA_ROW_BLOCK = 8
ROPE_BASE = 10000.0
N_EXPERTS = 16
EXPERT_HIDDEN = 2048
EC_CAPACITY = 2
NORM_EPS = 1e-6
NEG_INF = -1e30
Q_SCALE = HEAD_DIM ** -0.5

GATHER_CHUNK = 512
COMBINE_TILE = 256
COMBINE_SUB = 256
COMBINE_STRIP = 64
MOD_ROWS = 16
VMEM_LIMIT = 48 * 1024 * 1024


def _cparams(sem, vmem=VMEM_LIMIT):
    return pltpu.CompilerParams(dimension_semantics=sem, vmem_limit_bytes=vmem)


def _nt_dot(a, b):
    return lax.dot_general(a, b, (((1,), (1,)), ((), ())), preferred_element_type=F32)


def _rmsnorm_mod(x, g, shift, scale):
    y = x * lax.rsqrt(jnp.mean(x * x, axis=-1, keepdims=True) + NORM_EPS) * g
    return y * (1.0 + scale) + shift


def _ada_kernel(c_ref, w_ref, b_ref, o_ref):
    c = c_ref[...]
    s = (c * jax.nn.sigmoid(c)).astype(BF16)
    o_ref[0] = jnp.dot(s, w_ref[0].astype(BF16), preferred_element_type=F32) + b_ref[0]


def _ada_mod(cs, ada_w, ada_b):
    nl, d, n = ada_w.shape
    tn = 1024
    return pl.pallas_call(
        _ada_kernel,
        out_shape=jax.ShapeDtypeStruct((nl, MOD_ROWS, n), F32),
        grid=(nl, n // tn),
        in_specs=[pl.BlockSpec((MOD_ROWS, d), lambda l, j: (0, 0)),
                  pl.BlockSpec((1, d, tn), lambda l, j: (l, 0, j)),
                  pl.BlockSpec((1, 1, tn), lambda l, j: (l, 0, j))],
        out_specs=pl.BlockSpec((1, MOD_ROWS, tn), lambda l, j: (l, 0, j)),
        compiler_params=_cparams(("parallel", "parallel")),
        name="ada_mod",
    )(cs, ada_w, ada_b)


def _rope(t, cos, s_up, s_dn):
    w = t.shape[-1]
    if w > 128:
        cos, s_up, s_dn = (jnp.concatenate([a] * (w // 128), axis=-1) for a in (cos, s_up, s_dn))
    up = pltpu.roll(t, w - 16, 1)
    dn = pltpu.roll(t, 16, 1)
    return t * cos + up * s_up + dn * s_dn


def _inproj_kernel(*refs, rope, kv_only):
    if rope:
        x_ref, sh_ref, sc_ref, g_ref, w_ref, cos_ref, sup_ref, sdn_ref, *outs = refs
    else:
        x_ref, sh_ref, sc_ref, g_ref, w_ref, *outs = refs
    h = _rmsnorm_mod(x_ref[0], g_ref[...], sh_ref[0], sc_ref[0])
    p = jnp.dot(h.astype(BF16), w_ref[...], preferred_element_type=F32)
    if kv_only:
        outs[0][0] = p.astype(BF16)
        return
    hy_ref, pool_ref, qwa_ref, qna_ref, kv_ref = outs
    hy_ref[0] = p[:, :OFF_POOL]
    pool_ref[0] = p[:, OFF_POOL:OFF_WA_Q]
    qwa = p[:, OFF_WA_Q:OFF_NA_Q]
    kwa = p[:, OFF_KV:OFF_KV + KV_WA]
    if rope:
        tabs = (cos_ref[...], sup_ref[...], sdn_ref[...])
        qwa = _rope(qwa, *tabs)
        kwa = _rope(kwa, *tabs)
    qwa_ref[0] = (qwa * Q_SCALE).astype(BF16)
    qna_ref[0] = (p[:, OFF_NA_Q:OFF_KV] * Q_SCALE).astype(BF16)
    kv_ref[0] = jnp.concatenate([kwa, p[:, OFF_KV + KV_WA:]], axis=-1).astype(BF16)


def _inproj(x, shift, scale, g, w_bf16, rope_tabs=None, kv_only=False):
    b, l, d = x.shape
    n = w_bf16.shape[1]
    tm = min(l, 512)
    rope = rope_tabs is not None
    bm = shift.shape[0]
    mod_map = (lambda j, i: (i, 0, 0)) if bm > 1 else (lambda j, i: (0, 0, 0))
    in_specs = [pl.BlockSpec((1, tm, d), lambda j, i: (i, j, 0)),
                pl.BlockSpec((1, 1, d), mod_map),
                pl.BlockSpec((1, 1, d), mod_map),
                pl.BlockSpec((1, d), lambda j, i: (0, 0)),
                pl.BlockSpec((d, n), lambda j, i: (0, 0))]
    args = [x, shift, scale, g, w_bf16]
    if rope:
        in_specs += [pl.BlockSpec((tm, 128), lambda j, i: (j, 0))] * 3
        args += list(rope_tabs)

    def tok(width, dtype):
        return (jax.ShapeDtypeStruct((b, l, width), dtype),
                pl.BlockSpec((1, tm, width), lambda j, i: (i, j, 0)))

    if kv_only:
        outs = [tok(n, BF16)]
    else:
        outs = [tok(OFF_POOL, F32), tok(W_GROUP, F32), tok(W_GROUP, BF16), tok(W_GROUP, BF16),
                tok(KV_WIDTH, BF16)]
    res = pl.pallas_call(
        functools.partial(_inproj_kernel, rope=rope, kv_only=kv_only),
        out_shape=[o[0] for o in outs],
        grid=(l // tm, b),
        in_specs=in_specs,
        out_specs=[o[1] for o in outs],
        compiler_params=_cparams(("parallel", "parallel")),
        name="inproj_kv" if kv_only else "inproj",
    )(*args)
    return res[0] if kv_only else res


def _rope_tables(s):
    pos = jnp.arange(s)
    p2 = jnp.stack([pos // GRID_W, pos % GRID_W], axis=-1).astype(F32)
    inv = ROPE_BASE ** (-jnp.arange(16, dtype=F32) / 16)
    lane = np.arange(HEAD_DIM)
    ang = p2[:, lane // 32] * inv[lane % 16][None, :]
    first = jnp.asarray((lane % 32) < 16)[None, :]
    cos, sin = jnp.cos(ang), jnp.sin(ang)
    s_up = jnp.where(first, -sin, 0.0)
    s_dn = jnp.where(first, 0.0, sin)
    return tuple(jnp.tile(t, (1, 2)) for t in (cos, s_up, s_dn))


def _softmax_parts(parts, extra=None):
    m = parts[0].max(axis=-1, keepdims=True)
    for s in parts[1:]:
        m = jnp.maximum(m, s.max(axis=-1, keepdims=True))
    if extra is not None:
        m = jnp.maximum(m, extra)
    ps = [jnp.exp(s - m) for s in parts]
    den = ps[0].sum(axis=-1, keepdims=True)
    for p in ps[1:]:
        den = den + p.sum(axis=-1, keepdims=True)
    if extra is not None:
        den = den + jnp.exp(extra - m)
    return ps, den


def _head_stack(t, masks):
    return jnp.concatenate([jnp.where(m, t, jnp.zeros_like(t)) for m in masks], axis=0)


def _wattn_kernel(sink_ref, q_ref, kp_ref, ko_ref, kn_ref, vp_ref, vo_ref, vn_ref, kc_ref, vc_ref, o_ref, *, nb):
    n = pl.program_id(1)
    blk = WA_BLOCK
    q = q_ref[0]
    kall = jnp.concatenate([kp_ref[0], ko_ref[0], kn_ref[0]], axis=0)
    vall = jnp.concatenate([vp_ref[0], vo_ref[0], vn_ref[0]], axis=0)
    kc, vc = kc_ref[0], vc_ref[0]
    rows = N_WA_HEADS * blk
    i = lax.broadcasted_iota(jnp.int32, (rows, 3 * blk), 0) & (blk - 1)
    j = lax.broadcasted_iota(jnp.int32, (rows, 3 * blk), 1)
    band = (j >= i) & (j <= i + 2 * blk)
    kv0 = lax.broadcasted_iota(jnp.int32, (blk, KV_WA), 1) < HEAD_DIM
    head = lax.broadcasted_iota(jnp.int32, (rows, 1), 0) // blk
    snk = jnp.where(head == 0, sink_ref[0],
                    jnp.where(head == 1, sink_ref[1], jnp.where(head == 2, sink_ref[2], sink_ref[3])))
    for sb in range(WA_STEP_BLOCKS):
        gb = n * WA_STEP_BLOCKS + sb
        jlo = jnp.where(gb == 0, blk, 0)
        jhi = jnp.where(gb == nb - 1, 2 * blk, 3 * blk)
        valid = band & (j >= jlo) & (j < jhi)
        keys = slice(sb * blk, (sb + 3) * blk)
        qa = q[sb * blk:(sb + 1) * blk, :KV_WA]
        qb = q[sb * blk:(sb + 1) * blk, KV_WA:]
        zero = jnp.zeros_like(qa)
        qs = jnp.concatenate([jnp.where(kv0, qa, zero), jnp.where(kv0, qb, zero),
                              jnp.where(kv0, zero, qa), jnp.where(kv0, zero, qb)], axis=0)
        s_loc = jnp.where(valid, _nt_dot(qs, kall[keys]), NEG_INF)
        s_ctx = _nt_dot(qs, kc)
        (p_loc, p_ctx), den = _softmax_parts([s_loc, s_ctx], snk)
        o = (jnp.dot(p_loc.astype(BF16), vall[keys], preferred_element_type=F32)
             + jnp.dot(p_ctx.astype(BF16), vc, preferred_element_type=F32)) / den
        o_ref[0, sb * blk:(sb + 1) * blk, :] = jnp.concatenate(
            [jnp.where(kv0, o[:blk], o[2 * blk:3 * blk]), jnp.where(kv0, o[blk:2 * blk], o[3 * blk:])],
            axis=-1).astype(BF16)


def _window_attn(q, kv, ckv, sink):
    b, s, _ = q.shape
    lc = ckv.shape[1]
    nb = s // WA_BLOCK
    sbk = WA_STEP_BLOCKS

    def halo_spec(col, off):
        return pl.BlockSpec((1, WA_BLOCK, KV_WA),
                            lambda i, n: (i, jnp.clip(n * sbk + off, 0, nb - 1), col))

    def own_spec(col):
        return pl.BlockSpec((1, sbk * WA_BLOCK, KV_WA), lambda i, n: (i, n, col))

    return pl.pallas_call(
        functools.partial(_wattn_kernel, nb=nb),
        out_shape=jax.ShapeDtypeStruct((b, s, W_GROUP), BF16),
        grid=(b, nb // sbk),
        in_specs=[pl.BlockSpec(memory_space=pltpu.SMEM),
                  pl.BlockSpec((1, sbk * WA_BLOCK, W_GROUP), lambda i, n: (i, n, 0)),
                  halo_spec(0, -1), own_spec(0), halo_spec(0, sbk),
                  halo_spec(1, -1), own_spec(1), halo_spec(1, sbk),
                  pl.BlockSpec((1, lc, KV_WA), lambda i, n: (i, 0, 0)),
                  pl.BlockSpec((1, lc, KV_WA), lambda i, n: (i, 0, 1))],
        out_specs=pl.BlockSpec((1, sbk * WA_BLOCK, W_GROUP), lambda i, n: (i, n, 0)),
        compiler_params=_cparams(("parallel", "parallel")),
        name="window_attn",
    )(sink, q, kv, kv, kv, kv, kv, kv, ckv, ckv)


def _nattn_kernel(q_ref, k_ref, v_ref, kc_ref, vc_ref, bias_ref, o_ref, *, rows):
    blk = pl.program_id(1)
    nk = NA_ROWS * GRID_W
    kc, vc = kc_ref[0], vc_ref[0]
    lane_head = lax.broadcasted_iota(jnp.int32, (GRID_W, W_GROUP), 1) // HEAD_DIM
    masks = [lane_head == h for h in range(N_NA_HEADS)]
    for rr in range(NA_ROW_BLOCK):
        r = blk * NA_ROW_BLOCK + rr
        r0 = jnp.clip(r - NA_ROWS // 2, 0, rows - NA_ROWS)
        var = r - r0
        start = pl.multiple_of(r0 * GRID_W, GRID_W)
        kt = k_ref[0, pl.ds(start, nk), :]
        vt = v_ref[0, pl.ds(start, nk), :]
        qrow = slice(rr * GRID_W, (rr + 1) * GRID_W)
        qs = _head_stack(q_ref[0, qrow, :], masks)
        bias = bias_ref[:, var].reshape(N_NA_HEADS * GRID_W, nk)
        (p_loc, p_ctx), den = _softmax_parts([_nt_dot(qs, kt) + bias, _nt_dot(qs, kc)])
        o = (jnp.dot(p_loc.astype(BF16), vt, preferred_element_type=F32)
             + jnp.dot(p_ctx.astype(BF16), vc, preferred_element_type=F32)) / den
        out = o[(N_NA_HEADS - 1) * GRID_W:]
        for h in range(N_NA_HEADS - 2, -1, -1):
            out = jnp.where(masks[h], o[h * GRID_W:(h + 1) * GRID_W], out)
        o_ref[0, qrow, :] = out.astype(BF16)


def _na_bias(rpb):
    var = np.arange(NA_ROWS)
    j = np.arange(NA_ROWS)
    qc = np.arange(GRID_W)
    kc = np.arange(GRID_W)
    dr = j[None, :] - var[:, None] + NA_ROWS - 1
    dc = np.clip(kc[None, :] - qc[:, None] + NA_COLS - 1, 0, 2 * NA_COLS - 2)
    ws = np.clip(qc - NA_COLS // 2, 0, GRID_W - NA_COLS)
    ok = (kc[None, :] >= ws[:, None]) & (kc[None, :] < ws[:, None] + NA_COLS)
    onehot = (dc[None] == np.arange(2 * NA_COLS - 1)[:, None, None]).astype(np.float32)
    bias = jnp.einsum("hvjd,dqk->hvqjk", rpb.astype(F32)[:, dr], jnp.asarray(onehot),
                      precision=lax.Precision.HIGHEST)
    bias = jnp.where(jnp.asarray(ok)[None, None, :, None, :], bias, NEG_INF)
    return bias.reshape(rpb.shape[0], NA_ROWS, GRID_W, NA_ROWS * GRID_W)


def _neighbourhood_attn(q, kv, ckv, rpb):
    b, s, _ = q.shape
    lc = ckv.shape[1]
    rows = s // GRID_W
    bias = _na_bias(rpb)
    return pl.pallas_call(
        functools.partial(_nattn_kernel, rows=rows),
        out_shape=jax.ShapeDtypeStruct((b, s, W_GROUP), BF16),
        grid=(b, rows // NA_ROW_BLOCK),
        in_specs=[pl.BlockSpec((1, NA_ROW_BLOCK * GRID_W, W_GROUP), lambda i, r: (i, r, 0)),
                  pl.BlockSpec((1, s, W_GROUP), lambda i, r: (i, 0, 1)),
                  pl.BlockSpec((1, s, W_GROUP), lambda i, r: (i, 0, 2)),
                  pl.BlockSpec((1, lc, W_GROUP), lambda i, r: (i, 0, 1)),
                  pl.BlockSpec((1, lc, W_GROUP), lambda i, r: (i, 0, 2)),
                  pl.BlockSpec(bias.shape, lambda i, r: (0, 0, 0, 0))],
        out_specs=pl.BlockSpec((1, NA_ROW_BLOCK * GRID_W, W_GROUP), lambda i, r: (i, r, 0)),
        compiler_params=_cparams(("parallel", "arbitrary")),
        name="neighbourhood_attn",
    )(q, kv, kv, ckv, ckv, bias)


def _cattn_kernel(*refs, n_kv, with_sink, head_order):
    if with_sink:
        sink_ref, q_ref, k_ref, v_ref, o_ref = refs
    else:
        q_ref, k_ref, v_ref, o_ref = refs
    q, k, v = q_ref[0], k_ref[0], v_ref[0]
    group = N_WA_HEADS // n_kv
    outs = []
    for pos, h in enumerate(head_order):
        sl = slice((h // group) * HEAD_DIM, (h // group + 1) * HEAD_DIM)
        s = _nt_dot(q[:, pos * HEAD_DIM:(pos + 1) * HEAD_DIM], k[:, sl])
        extra = jnp.full((s.shape[0], 1), sink_ref[h], F32) if with_sink else None
        (p,), den = _softmax_parts([s], extra)
        outs.append(jnp.dot(p.astype(BF16), v[:, sl], preferred_element_type=F32) / den)
    o_ref[0] = jnp.concatenate(outs, axis=-1).astype(BF16)


def _ctx_attn(q, ckv, k_col, v_col, n_kv, sink, head_order=(0, 1, 2, 3)):
    b, lc, _ = q.shape
    w = n_kv * HEAD_DIM
    with_sink = sink is not None
    in_specs = [pl.BlockSpec((1, lc, W_GROUP), lambda i: (i, 0, 0)),
                pl.BlockSpec((1, lc, w), lambda i: (i, 0, k_col)),
                pl.BlockSpec((1, lc, w), lambda i: (i, 0, v_col))]
    args = [q, ckv, ckv]
    if with_sink:
        in_specs = [pl.BlockSpec(memory_space=pltpu.SMEM)] + in_specs
        args = [sink] + args
    return pl.pallas_call(
        functools.partial(_cattn_kernel, n_kv=n_kv, with_sink=with_sink, head_order=head_order),
        out_shape=jax.ShapeDtypeStruct((b, lc, W_GROUP), BF16),
        grid=(b,),
        in_specs=in_specs,
        out_specs=pl.BlockSpec((1, lc, W_GROUP), lambda i: (i, 0, 0)),
        compiler_params=_cparams(("parallel",)),
        name="ctx_attn",
    )(*args)


def _pool_kernel(prev_ref, cur_ref, next_ref, w_ref, scale_ref, o_ref, *, seq, tl):
    j = pl.program_id(1)
    nt = seq // tl
    cur = cur_ref[0]
    prev = jnp.where(j == 0, 0.0, prev_ref[0])
    nxt = jnp.where(j == nt - 1, 0.0, next_ref[0])
    e = jnp.concatenate([prev, cur, nxt], axis=0)
    n = tl + 2 * HALO

    def sh(a, d):
        return pltpu.roll(a, d % n, 0)

    s2 = e + sh(e, 1)
    s4 = sh(s2, 1) + sh(s2, -1)
    s8 = sh(s4, 2) + sh(s4, -2)
    s16 = sh(s8, 4) + sh(s8, -4)
    lane = lax.broadcasted_iota(jnp.int32, (tl, W_GROUP), 1)
    t = lax.broadcasted_iota(jnp.int32, (tl, W_GROUP), 0) + j * tl
    g = lane // POOL_GROUP
    lo, hi = HALO, HALO + tl
    ssum = jnp.where(g == 0, s2[lo:hi], jnp.where(g == 1, s4[lo:hi], jnp.where(g == 2, s8[lo:hi], s16[lo:hi])))
    half = jnp.where(g == 0, 1, jnp.where(g == 1, 2, jnp.where(g == 2, 4, 8)))
    cnt = (jnp.minimum(t + half, seq) - jnp.maximum(t - half, 0)).astype(F32)
    d = (ssum / cnt - cur).astype(BF16)
    o_ref[0] = (jnp.dot(d, w_ref[...], preferred_element_type=F32) * scale_ref[...]).astype(BF16)


def _pool_mixer(u, w_bd, scale):
    b, l, c = u.shape
    tl = min(l, 512)
    hb = tl // HALO
    nh = l // HALO
    return pl.pallas_call(
        functools.partial(_pool_kernel, seq=l, tl=tl),
        out_shape=jax.ShapeDtypeStruct((b, l, c), BF16),
        grid=(b, l // tl),
        in_specs=[pl.BlockSpec((1, HALO, c), lambda i, j: (i, jnp.maximum(j * hb - 1, 0), 0)),
                  pl.BlockSpec((1, tl, c), lambda i, j: (i, j, 0)),
                  pl.BlockSpec((1, HALO, c), lambda i, j: (i, jnp.minimum((j + 1) * hb, nh - 1), 0)),
                  pl.BlockSpec((c, c), lambda i, j: (0, 0)),
                  pl.BlockSpec((1, c), lambda i, j: (0, 0))],
        out_specs=pl.BlockSpec((1, tl, c), lambda i, j: (i, j, 0)),
        compiler_params=_cparams(("parallel", "parallel")),
        name="pool_mixer",
    )(u, u, u, w_bd, scale)


def _pool_weight(pool_w):
    z = jnp.zeros((W_GROUP, W_GROUP), F32)
    for g in range(len(POOL_WINDOWS)):
        z = z.at[g * POOL_GROUP:(g + 1) * POOL_GROUP, g * POOL_GROUP:(g + 1) * POOL_GROUP].set(pool_w[g])
    return z.astype(BF16)


def _outproj_kernel(yh_ref, ya_ref, yp_ref, yn_ref, x_ref, gt_ref, w_ref, g2_ref, sh_ref, sc_ref, rw_ref,
                    xo_ref, h_ref, aff_ref):
    y = jnp.concatenate([yh_ref[0], ya_ref[0], yp_ref[0], yn_ref[0]], axis=-1)
    x = x_ref[0] + gt_ref[0] * jnp.dot(y, w_ref[...], preferred_element_type=F32)
    xo_ref[0] = x
    h = _rmsnorm_mod(x, g2_ref[...], sh_ref[0], sc_ref[0])
    h_ref[0] = h
    logits = _nt_dot(rw_ref[...], h.astype(BF16))
    m = logits.max(axis=0, keepdims=True)
    p = jnp.exp(logits - m)
    aff_ref[0] = p / p.sum(axis=0, keepdims=True)


def _outproj(ys, x, gate, w_bf16, g2, shift, scale, rw_t):
    b, l, d = x.shape
    tm = min(l, 512)
    bm = gate.shape[0]
    mod_map = (lambda i, j: (i, 0, 0)) if bm > 1 else (lambda i, j: (0, 0, 0))
    tok = lambda w: pl.BlockSpec((1, tm, w), lambda i, j: (i, j, 0))
    mod = pl.BlockSpec((1, 1, d), mod_map)
    return pl.pallas_call(
        _outproj_kernel,
        out_shape=[jax.ShapeDtypeStruct((b, l, d), F32), jax.ShapeDtypeStruct((b, l, d), F32),
                   jax.ShapeDtypeStruct((b, N_EXPERTS, l), F32)],
        grid=(b, l // tm),
        in_specs=[tok(W_GROUP)] * 4 + [tok(d), mod, pl.BlockSpec((d, d), lambda i, j: (0, 0)),
                                        pl.BlockSpec((1, d), lambda i, j: (0, 0)), mod, mod,
                                        pl.BlockSpec((N_EXPERTS, d), lambda i, j: (0, 0))],
        out_specs=[tok(d), tok(d), pl.BlockSpec((1, N_EXPERTS, tm), lambda i, j: (i, 0, j))],
        compiler_params=_cparams(("parallel", "parallel")),
        name="outproj_router",
    )(*ys, x, gate, w_bf16, g2, shift, scale, rw_t)


def _row_gather_kernel(idx_ref, table_ref, out_ref, sem):
    def issue(r, c):
        pltpu.make_async_copy(table_ref.at[pl.ds(idx_ref[0, 0, r], 1)], out_ref.at[pl.ds(r, 1)], sem).start()
        return c

    lax.fori_loop(0, GATHER_CHUNK, issue, 0, unroll=8)
    pltpu.make_async_copy(table_ref.at[pl.ds(0, GATHER_CHUNK)], out_ref, sem).wait()


def _row_gather(table, rows):
    v, d = table.shape
    n = rows.shape[0]
    nch = n // GATHER_CHUNK
    return pl.pallas_call(
        _row_gather_kernel,
        out_shape=jax.ShapeDtypeStruct((n, d), table.dtype),
        grid=(nch,),
        in_specs=[pl.BlockSpec((1, 1, GATHER_CHUNK), lambda i: (i, 0, 0), memory_space=pltpu.SMEM),
                  pl.BlockSpec(memory_space=pl.ANY)],
        out_specs=pl.BlockSpec((GATHER_CHUNK, d), lambda i: (i, 0)),
        scratch_shapes=[pltpu.SemaphoreType.DMA(())],
        compiler_params=pltpu.CompilerParams(dimension_semantics=("arbitrary",), vmem_limit_bytes=VMEM_LIMIT,
                                             disable_bounds_checks=True),
        name="moe_row_gather",
    )(rows.reshape(nch, 1, GATHER_CHUNK), table)


def _moe_kernel(x_ref, g_ref, wg_ref, wu_ref, wd_ref, o_ref):
    f = pl.program_id(2)
    x = x_ref[0].astype(BF16)
    a = jnp.dot(x, wg_ref[0, 0].astype(BF16), preferred_element_type=F32)
    u = jnp.dot(x, wu_ref[0, 0].astype(BF16), preferred_element_type=F32)
    hid = (a * jax.nn.sigmoid(a) * u).astype(BF16)
    part = jnp.dot(hid, wd_ref[0, 0].astype(BF16), preferred_element_type=F32)

    @pl.when(f == 0)
    def _():
        o_ref[0] = part

    @pl.when(f != 0)
    def _():
        o_ref[0] += part

    @pl.when(f == pl.num_programs(2) - 1)
    def _():
        o_ref[0] = o_ref[0] * g_ref[0]


def _moe_ffn(xg, gate, w_gate, w_up, w_down, lyr):
    e, m, d = xg.shape
    f = w_gate.shape[3]
    tm = min(m, 1024)
    tf = 512
    return pl.pallas_call(
        _moe_kernel,
        out_shape=jax.ShapeDtypeStruct((e, m, d), F32),
        grid=(e, m // tm, f // tf),
        in_specs=[pl.BlockSpec((1, tm, d), lambda i, j, k: (i, j, 0)),
                  pl.BlockSpec((1, tm, 1), lambda i, j, k: (i, j, 0)),
                  pl.BlockSpec((1, 1, d, tf), lambda i, j, k: (lyr, i, 0, k)),
                  pl.BlockSpec((1, 1, d, tf), lambda i, j, k: (lyr, i, 0, k)),
                  pl.BlockSpec((1, 1, tf, d), lambda i, j, k: (lyr, i, k, 0))],
        out_specs=pl.BlockSpec((1, tm, d), lambda i, j, k: (i, j, 0)),
        compiler_params=_cparams(("parallel", "parallel", "arbitrary")),
        name="moe_ffn",
    )(xg, gate, w_gate, w_up, w_down)


def _expert_choice_ffn(h, aff_t, w_gate, w_up, w_down, lyr):
    b, l, d = h.shape
    cap = EC_CAPACITY * l // N_EXPERTS
    m = b * cap
    g, idx = lax.top_k(aff_t, cap)
    idx, g = lax.sort((idx, g), dimension=2, num_keys=1)
    idx_t = jnp.swapaxes(idx, 0, 1)
    rows = (idx_t + (jnp.arange(b, dtype=jnp.int32) * l)[None, :, None]).reshape(-1)
    xg = _row_gather(h.reshape(b * l, d), rows).reshape(N_EXPERTS, m, d)
    gate = jnp.swapaxes(g, 0, 1).reshape(N_EXPERTS, m, 1)
    o = _moe_ffn(xg, gate, w_gate, w_up, w_down, lyr).reshape(N_EXPERTS * m, d)
    return o, idx


def _combine_kernel(*refs, final, tile, nt, cap, strip, m):
    if final:
        lo_ref, idx_ref, x_ref, gt_ref, g_ref, o_hbm, out_ref, stage, sems = refs
    else:
        lo_ref, idx_ref, x_ref, gt_ref, o_hbm, out_ref, stage, sems = refs
    bi, j = pl.program_id(0), pl.program_id(1)
    step = bi * nt + j
    cur = step % 2
    ne = N_EXPERTS
    gps = strip // SUBLANES
    per_sub = COMBINE_SUB // strip
    win = 2 * LANES
    sentinel = tile * nt

    def runs(jj):
        out = []
        for e in range(ne):
            lo = lo_ref[bi, e * (nt + 1) + jj]
            hi = lo_ref[bi, e * (nt + 1) + jj + 1]
            a8 = (lo // SUBLANES) * SUBLANES
            out.append((a8, jnp.where(hi > lo, hi - a8, 0)))
        return out

    def groups(run, r):
        return [(a8 + strip * r, jnp.clip((n - strip * r + SUBLANES - 1) // SUBLANES, 0, gps)) for a8, n in run]

    def issue(s, grp):
        for e, (s0, ng) in enumerate(grp):
            base = e * m + bi * cap + s0

            def one(k, c, base=base, e=e):
                pltpu.make_async_copy(
                    o_hbm.at[pl.ds(pl.multiple_of(base + k * SUBLANES, SUBLANES), SUBLANES)],
                    stage.at[s, pl.ds(pl.multiple_of(e * strip + k * SUBLANES, SUBLANES), SUBLANES)],
                    sems.at[s]).start()
                return c
            lax.fori_loop(0, ng, one, 0)

    def wait(s, grp):
        cnt = sum(ng for _, ng in grp) * SUBLANES

        @pl.when(cnt > 0)
        def _():
            rows = pl.ds(0, pl.multiple_of(cnt, SUBLANES))
            pltpu.make_async_copy(o_hbm.at[rows], stage.at[s, rows], sems.at[s]).wait()

    def reduce(s, grp):
        t_ids = lax.broadcasted_iota(jnp.int32, (tile, COMBINE_SUB), 0) + j * tile
        acc = jnp.zeros((tile, out_ref.shape[2]), F32)
        for sub in range(ne // per_sub):
            pieces = []
            for e in range(sub * per_sub, (sub + 1) * per_sub):
                s0, ng = grp[e]
                a = jnp.minimum((s0 // LANES) * LANES, idx_ref.shape[2] - win)
                w = idx_ref[0, e:e + 1, pl.ds(pl.multiple_of(a, LANES), win)]
                w = pltpu.roll(w, (win - (s0 - a)) % win, 1)
                pieces.append(jnp.where(ng > 0, w[:, :strip], sentinel))
            toks = jnp.concatenate(pieces, axis=1)
            rows = stage[s, sub * COMBINE_SUB:(sub + 1) * COMBINE_SUB, :]
            onehot = jnp.where(toks == t_ids, 1.0, 0.0).astype(BF16)
            hi = rows.astype(BF16)
            lo = (rows - hi.astype(F32)).astype(BF16)
            acc = (acc + jnp.dot(onehot, hi, preferred_element_type=F32)
                   + jnp.dot(onehot, lo, preferred_element_type=F32))
        return acc

    @pl.when(step == 0)
    def _():
        stage[...] = jnp.zeros(stage.shape, F32)

    run = runs(j)
    first = groups(run, 0)

    @pl.when(j == 0)
    def _():
        issue(cur, first)

    @pl.when(j + 1 < nt)
    def _():
        issue(1 - cur, groups(runs(jnp.minimum(j + 1, nt - 1)), 0))

    wait(cur, first)
    acc = reduce(cur, first)
    longest = run[0][1]
    for _, n in run[1:]:
        longest = jnp.maximum(longest, n)

    def extra_round(r, acc):
        grp = groups(run, r)
        issue(cur, grp)
        wait(cur, grp)
        return acc + reduce(cur, grp)

    acc = lax.fori_loop(1, (longest + strip - 1) // strip, extra_round, acc)
    x = x_ref[0] + gt_ref[0] * acc
    if final:
        x = x * lax.rsqrt(jnp.mean(x * x, axis=-1, keepdims=True) + NORM_EPS) * g_ref[...]
    out_ref[0] = x


def _combine(x, o, idx, gate, final_g=None):
    b, l, d = x.shape
    ne, cap = idx.shape[1], idx.shape[2]
    tile = min(l, COMBINE_TILE)
    nt = l // tile
    strip = min(COMBINE_STRIP, cap)
    bounds = jnp.arange(nt + 1, dtype=jnp.int32) * tile
    lo = jnp.sum(idx[:, :, None, :] < bounds[None, None, :, None], axis=-1, dtype=jnp.int32)
    cp = -(-cap // LANES) * LANES + LANES
    idx_p = jnp.pad(idx, ((0, 0), (0, 0), (0, cp - cap)), constant_values=l)
    bm = gate.shape[0]
    mod_map = (lambda i, j: (i, 0, 0)) if bm > 1 else (lambda i, j: (0, 0, 0))
    tok = pl.BlockSpec((1, tile, d), lambda i, j: (i, j, 0))
    final = final_g is not None
    in_specs = [pl.BlockSpec(memory_space=pltpu.SMEM),
                pl.BlockSpec((1, ne, cp), lambda i, j: (i, 0, 0)),
                tok, pl.BlockSpec((1, 1, d), mod_map)]
    args = [lo.reshape(b, ne * (nt + 1)), idx_p, x, gate]
    if final:
        in_specs.append(pl.BlockSpec((1, d), lambda i, j: (0, 0)))
        args.append(final_g)
    in_specs.append(pl.BlockSpec(memory_space=pl.ANY))
    args.append(o)
    return pl.pallas_call(
        functools.partial(_combine_kernel, final=final, tile=tile, nt=nt, cap=cap, strip=strip, m=b * cap),
        out_shape=jax.ShapeDtypeStruct((b, l, d), F32),
        grid=(b, nt),
        in_specs=in_specs,
        out_specs=tok,
        scratch_shapes=[pltpu.VMEM((2, ne * strip, d), F32), pltpu.SemaphoreType.DMA((2,))],
        compiler_params=pltpu.CompilerParams(dimension_semantics=("arbitrary", "arbitrary"),
                                             vmem_limit_bytes=VMEM_LIMIT, disable_bounds_checks=True),
        name="moe_combine_final" if final else "moe_combine",
    )(*args)


def _hyena_filters(l, w1, b1, w2, b2, w3, freq):
    hp = lax.Precision.HIGHEST
    pos = jnp.arange(l, dtype=F32)
    t = pos / max(l - 1, 1)
    bands = jnp.linspace(1e-4, HY_BANDS - 1, HY_BANDS, dtype=F32)
    ang = (2 * math.pi / l) * pos[:, None] * bands[None, :]
    feats = jnp.concatenate([t[:, None], jnp.cos(ang), -jnp.sin(ang)], axis=-1)
    h = jnp.sin(freq * (jnp.dot(feats, w1, precision=hp) + b1))
    h = jnp.sin(freq * (jnp.dot(h, w2, precision=hp) + b2))
    h = jnp.dot(h, w3, precision=hp).reshape(l, HY_ORDER, 2, W_GROUP)
    deltas = jnp.linspace(HY_DECAY_MIN, HY_DECAY_MAX, W_GROUP, dtype=F32)
    window = jnp.exp(-t[:, None] * deltas[None, :])
    return h * window[:, None, None, :]


def _two_sided(h_fwd, h_bwd):
    lag0 = h_fwd[:1] + h_bwd[:1]
    return jnp.concatenate([lag0, h_fwd[1:], jnp.zeros_like(h_fwd[:1]), h_bwd[:0:-1]], axis=0)


def _short_conv_kernel(prev_ref, cur_ref, next_ref, w_ref, b_ref, o_ref, *, seq, tl):
    j = pl.program_id(1)
    nt = seq // tl
    prev = jnp.where(j == 0, 0.0, prev_ref[0])
    nxt = jnp.where(j == nt - 1, 0.0, next_ref[0])
    e = jnp.concatenate([prev, cur_ref[0], nxt], axis=0)
    n = tl + 2 * HALO
    y = pltpu.roll(e, 1, 0) * w_ref[0:1, :] + e * w_ref[1:2, :] + pltpu.roll(e, n - 1, 0) * w_ref[2:3, :]
    o_ref[0] = y[HALO:HALO + tl] + b_ref[...]


def _short_conv(u, w, b):
    bsz, l, c = u.shape
    tl = min(l, 512)
    hb = tl // HALO
    nh = l // HALO
    return pl.pallas_call(
        functools.partial(_short_conv_kernel, seq=l, tl=tl),
        out_shape=jax.ShapeDtypeStruct((bsz, l, c), F32),
        grid=(bsz, l // tl),
        in_specs=[pl.BlockSpec((1, HALO, c), lambda i, j: (i, jnp.maximum(j * hb - 1, 0), 0)),
                  pl.BlockSpec((1, tl, c), lambda i, j: (i, j, 0)),
                  pl.BlockSpec((1, HALO, c), lambda i, j: (i, jnp.minimum((j + 1) * hb, nh - 1), 0)),
                  pl.BlockSpec((3, c), lambda i, j: (0, 0)),
                  pl.BlockSpec((1, c), lambda i, j: (0, 0))],
        out_specs=pl.BlockSpec((1, tl, c), lambda i, j: (i, j, 0)),
        compiler_params=_cparams(("parallel", "parallel")),
        name="hyena_short_conv",
    )(u, u, u, w, b)


def _cplx_block(e):
    return np.block([[e.real, -e.imag], [e.imag, e.real]])


@functools.lru_cache(maxsize=None)
def _dft_consts(l):
    n = 2 * l
    n1f = n // DFT_N2
    n1h = n1f // 2
    k1, n1, n2 = np.arange(n1f), np.arange(n1h), np.arange(DFT_N2)
    ph = np.outer(k1, n1)[None] / n1f + (n2[:, None, None] * k1[None, :, None]) / n
    e1 = np.exp(-2j * np.pi * ph)
    w1 = np.stack([_cplx_block(e1[i]) for i in range(DFT_N2)])
    e2 = np.exp(-2j * np.pi * np.outer(n2, n2) / DFT_N2)
    f2 = _cplx_block(e2)
    g2 = _cplx_block(np.conj(e2).T)
    g1 = np.stack([_cplx_block(np.conj(e1[i]).T / n) for i in range(DFT_N2)])
    return tuple(jnp.asarray(a, F32).astype(BF16) for a in (w1, f2, g2, g1))


def _lconv_kernel(a_ref, hf_ref, w1_ref, f2_ref, g2_ref, g1_ref, y_ref, xy_ref, s_ref, *, n1h):
    n1f = 2 * n1h

    def copy_in(i, c):
        for ri in range(2):
            xy_ref[ri, pl.ds(pl.multiple_of(i * X_PITCH, 8), DFT_N2), :] = \
                a_ref[0, ri, pl.ds(pl.multiple_of(i * DFT_N2, DFT_N2), DFT_N2), :]
        return c

    lax.fori_loop(0, n1h, copy_in, 0, unroll=4)

    def stage1(n2, c):
        slab = jnp.concatenate([xy_ref[0, pl.ds(n2, n1h, stride=X_PITCH), :],
                                xy_ref[1, pl.ds(n2, n1h, stride=X_PITCH), :]], axis=0).astype(BF16)
        a = jnp.dot(w1_ref[n2], slab, preferred_element_type=F32)
        base = pl.multiple_of(n2 * S_PITCH, 8)
        s_ref[0, pl.ds(base, n1f), :] = a[:n1f]
        s_ref[1, pl.ds(base, n1f), :] = a[n1f:]
        return c

    lax.fori_loop(0, DFT_N2, stage1, 0, unroll=32)

    def stage2(k1, c):
        slab = jnp.concatenate([s_ref[0, pl.ds(k1, DFT_N2, stride=S_PITCH), :],
                                s_ref[1, pl.ds(k1, DFT_N2, stride=S_PITCH), :]], axis=0).astype(BF16)
        x = jnp.dot(f2_ref[...], slab, preferred_element_type=F32)
        xr, xi = x[:DFT_N2], x[DFT_N2:]
        hr, hi = hf_ref[k1, 0], hf_ref[k1, 1]
        y = jnp.concatenate([xr * hr - xi * hi, xr * hi + xi * hr], axis=0).astype(BF16)
        cc = jnp.dot(g2_ref[...], y, preferred_element_type=F32)
        s_ref[0, pl.ds(k1, DFT_N2, stride=S_PITCH), :] = cc[:DFT_N2]
        s_ref[1, pl.ds(k1, DFT_N2, stride=S_PITCH), :] = cc[DFT_N2:]
        return c

    lax.fori_loop(0, n1f, stage2, 0, unroll=16)

    def stage3(n2, c):
        base = pl.multiple_of(n2 * S_PITCH, 8)
        d = jnp.concatenate([s_ref[0, pl.ds(base, n1f), :], s_ref[1, pl.ds(base, n1f), :]], axis=0).astype(BF16)
        yv = jnp.dot(g1_ref[n2], d, preferred_element_type=F32)
        yb = pl.multiple_of(n2 * Y_PITCH, 8)
        xy_ref[0, pl.ds(yb, n1h), :] = yv[:n1h]
        xy_ref[1, pl.ds(yb, n1h), :] = yv[n1h:]
        return c

    lax.fori_loop(0, DFT_N2, stage3, 0, unroll=32)

    def copy_out(i, c):
        for ri in range(2):
            y_ref[0, ri, pl.ds(pl.multiple_of(i * DFT_N2, DFT_N2), DFT_N2), :] = \
                xy_ref[ri, pl.ds(i, DFT_N2, stride=Y_PITCH), :]
        return c

    lax.fori_loop(0, n1h, copy_out, 0, unroll=4)


def _lconv_small_kernel(a_ref, hf_ref, f_ref, g_ref, y_ref):
    l = a_ref.shape[2]
    slab = jnp.concatenate([a_ref[0, 0], a_ref[0, 1]], axis=0).astype(BF16)
    x = jnp.dot(f_ref[...], slab, preferred_element_type=F32)
    xr, xi = x[:2 * l], x[2 * l:]
    hr, hi = hf_ref[0], hf_ref[1]
    y = jnp.concatenate([xr * hr - xi * hi, xr * hi + xi * hr], axis=0).astype(BF16)
    out = jnp.dot(g_ref[...], y, preferred_element_type=F32)
    y_ref[0, 0] = out[:l]
    y_ref[0, 1] = out[l:]


@functools.lru_cache(maxsize=None)
def _dft_consts_small(l):
    n = 2 * l
    e = np.exp(-2j * np.pi * np.outer(np.arange(n), np.arange(l)) / n)
    f = _cplx_block(e)
    g = _cplx_block(np.conj(e).T / n)
    return jnp.asarray(f, F32).astype(BF16), jnp.asarray(g, F32).astype(BF16)


def _long_conv(a, hf):
    b, l, ca = a.shape
    c = W_GROUP
    a4 = a.reshape(b // 2, 2, l, ca)
    io_spec = pl.BlockSpec((1, 2, l, LANES), lambda j, p: (p, 0, 0, j))
    once = dict(pipeline_mode=pl.Buffered(1))
    const = lambda arr: pl.BlockSpec(arr.shape, lambda j, p: (0,) * arr.ndim, **once)
    if l <= DFT_N2 * 2:
        f, g = _dft_consts_small(l)
        y = pl.pallas_call(
            _lconv_small_kernel,
            out_shape=jax.ShapeDtypeStruct((b // 2, 2, l, c), F32),
            grid=(c // LANES, b // 2),
            in_specs=[io_spec, pl.BlockSpec((2, 2 * l, LANES), lambda j, p: (0, 0, j)), const(f), const(g)],
            out_specs=io_spec,
            compiler_params=_cparams(("parallel", "arbitrary")),
            name="long_conv_small",
        )(a4, hf, f, g)
        return y.reshape(b, l, c)
    n1f = 2 * l // DFT_N2
    n1h = n1f // 2
    assert (n1f + 8, n1h + 8) == (S_PITCH, Y_PITCH), "scratch pitches are sized for this sequence length"
    w1, f2, g2, g1 = _dft_consts(l)
    rows_xy = max(n1h * X_PITCH, DFT_N2 * Y_PITCH)
    y = pl.pallas_call(
        functools.partial(_lconv_kernel, n1h=n1h),
        out_shape=jax.ShapeDtypeStruct((b // 2, 2, l, c), F32),
        grid=(c // LANES, b // 2),
        in_specs=[io_spec, pl.BlockSpec((n1f, 2, DFT_N2, LANES), lambda j, p: (0, 0, 0, j), **once),
                  const(w1), const(f2), const(g2), const(g1)],
        out_specs=io_spec,
        scratch_shapes=[pltpu.VMEM((2, rows_xy, LANES), F32), pltpu.VMEM((2, DFT_N2 * S_PITCH, LANES), F32)],
        compiler_params=_cparams(("parallel", "arbitrary"), 52 * 1024 * 1024),
        name="long_conv",
    )(a4, hf, w1, f2, g2, g1)
    return y.reshape(b, l, c)


def _filter_spectrum(h_fwd, h_bwd):
    filt = _two_sided(h_fwd, h_bwd)
    n, c = filt.shape
    ff = jnp.fft.fft(filt, axis=0)
    if n <= DFT_N2 * 4:
        return jnp.stack([ff.real, ff.imag], axis=0).astype(F32)
    n1f = n // DFT_N2
    ff = jnp.swapaxes(ff.reshape(DFT_N2, n1f, c), 0, 1)
    return jnp.stack([ff.real, ff.imag], axis=1).astype(F32)


def _hy_gate_kernel(y_ref, a_ref, m_ref, sk_ref, o_ref):
    o_ref[0] = (m_ref[0] * (y_ref[0] + a_ref[0] * sk_ref[...])).astype(o_ref.dtype)


def _hy_gate(y, a, a_col, m, m_col, sk, out_dtype):
    b, l, c = y.shape
    tl = min(l, 1024)
    spec = lambda col: pl.BlockSpec((1, tl, c), lambda i, j: (i, j, col))
    return pl.pallas_call(
        _hy_gate_kernel,
        out_shape=jax.ShapeDtypeStruct((b, l, c), out_dtype),
        grid=(b, l // tl),
        in_specs=[spec(0), spec(a_col), spec(m_col), pl.BlockSpec((1, c), lambda i, j: (0, 0))],
        out_specs=spec(0),
        compiler_params=_cparams(("parallel", "parallel")),
        name="hyena_gate",
    )(y, a, m, sk)


def _hyena(u, conv_w, conv_b, filt, skip):
    uc = _short_conv(u, conv_w, conv_b[None, :])
    y1 = _long_conv(uc, _filter_spectrum(filt[:, 0, 0], filt[:, 0, 1]))
    z1 = _hy_gate(y1, uc, 0, uc, 1, skip[0:1], F32)
    y2 = _long_conv(z1, _filter_spectrum(filt[:, 1, 0], filt[:, 1, 1]))
    return _hy_gate(y2, z1, 0, uc, 2, skip[1:2], BF16)


def _permute_wa_heads(w, start, axis):
    idx = np.arange(w.shape[axis])
    blocks = [np.arange(start + h * HEAD_DIM, start + (h + 1) * HEAD_DIM) for h in WA_HEAD_ORDER]
    idx[start:start + N_WA_HEADS * HEAD_DIM] = np.concatenate(blocks)
    return jnp.take(w, idx, axis=axis)


def _layer(x, ctx, mod, lyr, rope_tabs, update_ctx, final_g):
    b, s, d = x.shape
    lc = ctx.shape[1]
    mx = [mod[:b, None, i * d:(i + 1) * d] for i in range(6)]
    mc = [mod[b:b + 1, None, i * d:(i + 1) * d] for i in range(6)]
    g1, g2 = lyr["g1"][None, :], lyr["g2"][None, :]
    w_in = _permute_wa_heads(lyr["w_in"], OFF_WA_Q, axis=1).astype(BF16)
    w_out = _permute_wa_heads(lyr["w_out"], W_GROUP, axis=0).astype(BF16)
    rw_t = lyr["router_w"].T.astype(BF16)
    w_bd = _pool_weight(lyr["pool_w"])
    pscale = lyr["pool_scale"][None, :]
    hy_args = (lyr["hy_w1"], lyr["hy_b1"], lyr["hy_w2"], lyr["hy_b2"], lyr["hy_w3"], lyr["hy_freq"])

    u_hy, u_pool, q_wa, q_na, kv = _inproj(x, mx[0], mx[1], g1, w_in, rope_tabs)
    if update_ctx:
        cu_hy, cu_pool, cq_wa, cq_na, ckv = _inproj(ctx, mc[0], mc[1], g1, w_in)
    else:
        ckv = _inproj(ctx, mc[0], mc[1], g1, w_in[:, OFF_KV:], kv_only=True)

    ys = [_hyena(u_hy, lyr["hy_conv_w"], lyr["hy_conv_b"], _hyena_filters(s, *hy_args), lyr["hy_skip"]),
          _window_attn(q_wa, kv, ckv, lyr["wa_sink"]),
          _pool_mixer(u_pool, w_bd, pscale),
          _neighbourhood_attn(q_na, kv, ckv, lyr["na_rpb"])]
    x, h, aff = _outproj(ys, x, mx[2], w_out, g2, mx[3], mx[4], rw_t)
    moe = _expert_choice_ffn(h, aff, *lyr["experts"], lyr["index"])
    x = _combine(x, *moe, mx[5], final_g)

    if update_ctx:
        ycs = [_hyena(cu_hy, lyr["hy_conv_w"], lyr["hy_conv_b"], _hyena_filters(lc, *hy_args), lyr["hy_skip"]),
               _ctx_attn(cq_wa, ckv, 0, 1, N_WA_KV, lyr["wa_sink"], WA_HEAD_ORDER),
               _pool_mixer(cu_pool, w_bd, pscale),
               _ctx_attn(cq_na, ckv, 1, 2, N_NA_HEADS, None)]
        ctx, hc, affc = _outproj(ycs, ctx, mc[2], w_out, g2, mc[3], mc[4], rw_t)
        moe_c = _expert_choice_ffn(hc, affc, *lyr["experts"], lyr["index"])
        ctx = _combine(ctx, *moe_c, mc[5])
    return x, ctx


def kernel(x, c, ctx, c_ctx, ada_w, ada_b, norm1_g, norm2_g, w_in, hy_conv_w, hy_conv_b, hy_w1, hy_b1, hy_w2,
           hy_b2, hy_w3, hy_freq, hy_skip, wa_sink, pool_w, pool_scale, na_rpb, w_out, router_w, exp_w_gate,
           exp_w_up, exp_w_down, final_norm_g):
    b, s, d = x.shape
    cs = jnp.zeros((MOD_ROWS, d), F32).at[:b].set(c).at[b].set(c_ctx)
    mods = _ada_mod(cs, ada_w, ada_b[:, None, :])
    rope_tabs = _rope_tables(s)
    params = dict(g1=norm1_g, g2=norm2_g, w_in=w_in, hy_conv_w=hy_conv_w, hy_conv_b=hy_conv_b, hy_w1=hy_w1,
                  hy_b1=hy_b1, hy_w2=hy_w2, hy_b2=hy_b2, hy_w3=hy_w3, hy_freq=hy_freq, hy_skip=hy_skip,
                  wa_sink=wa_sink, pool_w=pool_w, pool_scale=pool_scale, na_rpb=na_rpb, w_out=w_out,
                  router_w=router_w)
    for l in range(DEPTH):
        lyr = {k: v[l] for k, v in params.items()}
        lyr.update(index=l, experts=(exp_w_gate, exp_w_up, exp_w_down))
        last = l == DEPTH - 1
        x, ctx = _layer(x, ctx, mods[l], lyr, rope_tabs, update_ctx=not last,
                        final_g=final_norm_g[None, :] if last else None)
    return x
```

```python
import functools
import math

import jax
import jax.numpy as jnp
import numpy as np
from jax import lax
from jax.experimental import pallas as pl
from jax.experimental.pallas import tpu as pltpu

F32 = jnp.float32
BF16 = jnp.bfloat16

D_MODEL = 1024
DEPTH = 2
GRID_W = 64
HEAD_DIM = 64
W_GROUP = 256
N_WA_HEADS = 4
N_WA_KV = 2
N_NA_HEADS = 4
KV_WA = 128
OFF_POOL = 768
OFF_WA_Q = 1024
OFF_NA_Q = 1280
OFF_KV = 1536
IN_WIDTH = 2304
KV_WIDTH = IN_WIDTH - OFF_KV
HY_ORDER = 2
HY_BANDS = 16
HY_DECAY_MIN = abs(math.log(1e-2) / 1.5)
HY_DECAY_MAX = abs(math.log(1e-2) / 0.3)
WA_BLOCK = 128
WA_STEP_BLOCKS = 4
WA_HEAD_ORDER = (0, 2, 1, 3)
POOL_WINDOWS = (2, 4, 8, 16)
POOL_GROUP = 64
HALO = 8
LANES = 128
SUBLANES = 8
DFT_N2 = 128
X_PITCH = DFT_N2 + 8
S_PITCH = 64 + 8
Y_PITCH = 32 + 8
NA_ROWS = 8
NA_COLS = 16
NA_ROW_BLOCK = 8
ROPE_BASE = 10000.0
N_EXPERTS = 16
EXPERT_HIDDEN = 2048
EC_CAPACITY = 2
NORM_EPS = 1e-6
NEG_INF = -1e30
Q_SCALE = HEAD_DIM ** -0.5

GATHER_CHUNK = 2048
COMBINE_TILE = 256
COMBINE_SUB = 256
COMBINE_STRIP = 64
MOD_ROWS = 16
VMEM_LIMIT = 48 * 1024 * 1024


def _cparams(sem, vmem=VMEM_LIMIT):
    return pltpu.CompilerParams(dimension_semantics=sem, vmem_limit_bytes=vmem)


def _nt_dot(a, b):
    return lax.dot_general(a, b, (((1,), (1,)), ((), ())), preferred_element_type=F32)


def _rmsnorm_mod(x, g, shift, scale):
    y = x * lax.rsqrt(jnp.mean(x * x, axis=-1, keepdims=True) + NORM_EPS) * g
    return y * (1.0 + scale) + shift


def _ada_kernel(c_ref, w_ref, b_ref, o_ref):
    c = c_ref[...]
    s = (c * jax.nn.sigmoid(c)).astype(BF16)
    o_ref[0] = jnp.dot(s, w_ref[0].astype(BF16), preferred_element_type=F32) + b_ref[0]


def _ada_mod(cs, ada_w, ada_b):
    nl, d, n = ada_w.shape
    tn = 1024
    return pl.pallas_call(
        _ada_kernel,
        out_shape=jax.ShapeDtypeStruct((nl, MOD_ROWS, n), F32),
        grid=(nl, n // tn),
        in_specs=[pl.BlockSpec((MOD_ROWS, d), lambda l, j: (0, 0)),
                  pl.BlockSpec((1, d, tn), lambda l, j: (l, 0, j)),
                  pl.BlockSpec((1, 1, tn), lambda l, j: (l, 0, j))],
        out_specs=pl.BlockSpec((1, MOD_ROWS, tn), lambda l, j: (l, 0, j)),
        compiler_params=_cparams(("parallel", "parallel")),
        name="ada_mod",
    )(cs, ada_w, ada_b)


def _rope(t, cos, s_up, s_dn):
    w = t.shape[-1]
    if w > 128:
        cos, s_up, s_dn = (jnp.concatenate([a] * (w // 128), axis=-1) for a in (cos, s_up, s_dn))
    up = pltpu.roll(t, w - 16, 1)
    dn = pltpu.roll(t, 16, 1)
    return t * cos + up * s_up + dn * s_dn


def _inproj_kernel(*refs, rope, kv_only):
    if rope:
        x_ref, sh_ref, sc_ref, g_ref, w_ref, cos_ref, sup_ref, sdn_ref, *outs = refs
    else:
        x_ref, sh_ref, sc_ref, g_ref, w_ref, *outs = refs
    h = _rmsnorm_mod(x_ref[0], g_ref[...], sh_ref[0], sc_ref[0])
    p = jnp.dot(h.astype(BF16), w_ref[...], preferred_element_type=F32)
    if kv_only:
        outs[0][0] = p.astype(BF16)
        return
    hy_ref, pool_ref, qwa_ref, qna_ref, kv_ref = outs
    hy_ref[0] = p[:, :OFF_POOL]
    pool_ref[0] = p[:, OFF_POOL:OFF_WA_Q]
    qwa = p[:, OFF_WA_Q:OFF_NA_Q]
    kwa = p[:, OFF_KV:OFF_KV + KV_WA]
    if rope:
        tabs = (cos_ref[...], sup_ref[...], sdn_ref[...])
        qwa = _rope(qwa, *tabs)
        kwa = _rope(kwa, *tabs)
    qwa_ref[0] = (qwa * Q_SCALE).astype(BF16)
    qna_ref[0] = (p[:, OFF_NA_Q:OFF_KV] * Q_SCALE).astype(BF16)
    kv_ref[0] = jnp.concatenate([kwa, p[:, OFF_KV + KV_WA:]], axis=-1).astype(BF16)


def _inproj(x, shift, scale, g, w_bf16, rope_tabs=None, kv_only=False):
    b, l, d = x.shape
    n = w_bf16.shape[1]
    tm = min(l, 512)
    rope = rope_tabs is not None
    bm = shift.shape[0]
    mod_map = (lambda j, i: (i, 0, 0)) if bm > 1 else (lambda j, i: (0, 0, 0))
    in_specs = [pl.BlockSpec((1, tm, d), lambda j, i: (i, j, 0)),
                pl.BlockSpec((1, 1, d), mod_map),
                pl.BlockSpec((1, 1, d), mod_map),
                pl.BlockSpec((1, d), lambda j, i: (0, 0)),
                pl.BlockSpec((d, n), lambda j, i: (0, 0))]
    args = [x, shift, scale, g, w_bf16]
    if rope:
        in_specs += [pl.BlockSpec((tm, 128), lambda j, i: (j, 0))] * 3
        args += list(rope_tabs)

    def tok(width, dtype):
        return (jax.ShapeDtypeStruct((b, l, width), dtype),
                pl.BlockSpec((1, tm, width), lambda j, i: (i, j, 0)))

    if kv_only:
        outs = [tok(n, BF16)]
    else:
        outs = [tok(OFF_POOL, F32), tok(W_GROUP, F32), tok(W_GROUP, BF16), tok(W_GROUP, BF16),
                tok(KV_WIDTH, BF16)]
    res = pl.pallas_call(
        functools.partial(_inproj_kernel, rope=rope, kv_only=kv_only),
        out_shape=[o[0] for o in outs],
        grid=(l // tm, b),
        in_specs=in_specs,
        out_specs=[o[1] for o in outs],
        compiler_params=_cparams(("parallel", "parallel")),
        name="inproj_kv" if kv_only else "inproj",
    )(*args)
    return res[0] if kv_only else res


def _rope_tables(s):
    pos = jnp.arange(s)
    p2 = jnp.stack([pos // GRID_W, pos % GRID_W], axis=-1).astype(F32)
    inv = ROPE_BASE ** (-jnp.arange(16, dtype=F32) / 16)
    lane = np.arange(HEAD_DIM)
    ang = p2[:, lane // 32] * inv[lane % 16][None, :]
    first = jnp.asarray((lane % 32) < 16)[None, :]
    cos, sin = jnp.cos(ang), jnp.sin(ang)
    s_up = jnp.where(first, -sin, 0.0)
    s_dn = jnp.where(first, 0.0, sin)
    return tuple(jnp.tile(t, (1, 2)) for t in (cos, s_up, s_dn))


def _softmax_parts(parts, extra=None):
    m = parts[0].max(axis=-1, keepdims=True)
    for s in parts[1:]:
        m = jnp.maximum(m, s.max(axis=-1, keepdims=True))
    if extra is not None:
        m = jnp.maximum(m, extra)
    ps = [jnp.exp(s - m) for s in parts]
    den = ps[0].sum(axis=-1, keepdims=True)
    for p in ps[1:]:
        den = den + p.sum(axis=-1, keepdims=True)
    if extra is not None:
        den = den + jnp.exp(extra - m)
    return ps, den


def _head_stack(t, masks):
    return jnp.concatenate([jnp.where(m, t, jnp.zeros_like(t)) for m in masks], axis=0)


def _wattn_kernel(sink_ref, q_ref, kp_ref, ko_ref, kn_ref, vp_ref, vo_ref, vn_ref, kc_ref, vc_ref, o_ref, *, nb):
    n = pl.program_id(1)
    blk = WA_BLOCK
    q = q_ref[0]
    kall = jnp.concatenate([kp_ref[0], ko_ref[0], kn_ref[0]], axis=0)
    vall = jnp.concatenate([vp_ref[0], vo_ref[0], vn_ref[0]], axis=0)
    kc, vc = kc_ref[0], vc_ref[0]
    rows = N_WA_HEADS * blk
    i = lax.broadcasted_iota(jnp.int32, (rows, 3 * blk), 0) & (blk - 1)
    j = lax.broadcasted_iota(jnp.int32, (rows, 3 * blk), 1)
    band = (j >= i) & (j <= i + 2 * blk)
    kv0 = lax.broadcasted_iota(jnp.int32, (blk, KV_WA), 1) < HEAD_DIM
    head = lax.broadcasted_iota(jnp.int32, (rows, 1), 0) // blk
    snk = jnp.where(head == 0, sink_ref[0],
                    jnp.where(head == 1, sink_ref[1], jnp.where(head == 2, sink_ref[2], sink_ref[3])))
    for sb in range(WA_STEP_BLOCKS):
        gb = n * WA_STEP_BLOCKS + sb
        jlo = jnp.where(gb == 0, blk, 0)
        jhi = jnp.where(gb == nb - 1, 2 * blk, 3 * blk)
        valid = band & (j >= jlo) & (j < jhi)
        keys = slice(sb * blk, (sb + 3) * blk)
        qa = q[sb * blk:(sb + 1) * blk, :KV_WA]
        qb = q[sb * blk:(sb + 1) * blk, KV_WA:]
        zero = jnp.zeros_like(qa)
        qs = jnp.concatenate([jnp.where(kv0, qa, zero), jnp.where(kv0, qb, zero),
                              jnp.where(kv0, zero, qa), jnp.where(kv0, zero, qb)], axis=0)
        s_loc = jnp.where(valid, _nt_dot(qs, kall[keys]), NEG_INF)
        s_ctx = _nt_dot(qs, kc)
        (p_loc, p_ctx), den = _softmax_parts([s_loc, s_ctx], snk)
        o = (jnp.dot(p_loc.astype(BF16), vall[keys], preferred_element_type=F32)
             + jnp.dot(p_ctx.astype(BF16), vc, preferred_element_type=F32)) / den
        o_ref[0, sb * blk:(sb + 1) * blk, :] = jnp.concatenate(
            [jnp.where(kv0, o[:blk], o[2 * blk:3 * blk]), jnp.where(kv0, o[blk:2 * blk], o[3 * blk:])],
            axis=-1).astype(BF16)


def _window_attn(q, kv, ckv, sink):
    b, s, _ = q.shape
    lc = ckv.shape[1]
    nb = s // WA_BLOCK
    sbk = WA_STEP_BLOCKS

    def halo_spec(col, off):
        return pl.BlockSpec((1, WA_BLOCK, KV_WA),
                            lambda i, n: (i, jnp.clip(n * sbk + off, 0, nb - 1), col))

    def own_spec(col):
        return pl.BlockSpec((1, sbk * WA_BLOCK, KV_WA), lambda i, n: (i, n, col))

    return pl.pallas_call(
        functools.partial(_wattn_kernel, nb=nb),
        out_shape=jax.ShapeDtypeStruct((b, s, W_GROUP), BF16),
        grid=(b, nb // sbk),
        in_specs=[pl.BlockSpec(memory_space=pltpu.SMEM),
                  pl.BlockSpec((1, sbk * WA_BLOCK, W_GROUP), lambda i, n: (i, n, 0)),
                  halo_spec(0, -1), own_spec(0), halo_spec(0, sbk),
                  halo_spec(1, -1), own_spec(1), halo_spec(1, sbk),
                  pl.BlockSpec((1, lc, KV_WA), lambda i, n: (i, 0, 0)),
                  pl.BlockSpec((1, lc, KV_WA), lambda i, n: (i, 0, 1))],
        out_specs=pl.BlockSpec((1, sbk * WA_BLOCK, W_GROUP), lambda i, n: (i, n, 0)),
        compiler_params=_cparams(("parallel", "parallel")),
        name="window_attn",
    )(sink, q, kv, kv, kv, kv, kv, kv, ckv, ckv)


def _nattn_kernel(q_ref, k_ref, v_ref, kc_ref, vc_ref, bias_ref, o_ref, *, rows):
    blk = pl.program_id(1)
    nk = NA_ROWS * GRID_W
    kc, vc = kc_ref[0], vc_ref[0]
    lane_head = lax.broadcasted_iota(jnp.int32, (GRID_W, W_GROUP), 1) // HEAD_DIM
    masks = [lane_head == h for h in range(N_NA_HEADS)]
    for rr in range(NA_ROW_BLOCK):
        r = blk * NA_ROW_BLOCK + rr
        r0 = jnp.clip(r - NA_ROWS // 2, 0, rows - NA_ROWS)
        var = r - r0
        start = pl.multiple_of(r0 * GRID_W, GRID_W)
        kt = k_ref[0, pl.ds(start, nk), :]
        vt = v_ref[0, pl.ds(start, nk), :]
        qrow = slice(rr * GRID_W, (rr + 1) * GRID_W)
        qs = _head_stack(q_ref[0, qrow, :], masks)
        bias = bias_ref[:, var].reshape(N_NA_HEADS * GRID_W, nk)
        (p_loc, p_ctx), den = _softmax_parts([_nt_dot(qs, kt) + bias, _nt_dot(qs, kc)])
        o = (jnp.dot(p_loc.astype(BF16), vt, preferred_element_type=F32)
             + jnp.dot(p_ctx.astype(BF16), vc, preferred_element_type=F32)) / den
        out = o[(N_NA_HEADS - 1) * GRID_W:]
        for h in range(N_NA_HEADS - 2, -1, -1):
            out = jnp.where(masks[h], o[h * GRID_W:(h + 1) * GRID_W], out)
        o_ref[0, qrow, :] = out.astype(BF16)


def _na_bias(rpb):
    var = np.arange(NA_ROWS)
    j = np.arange(NA_ROWS)
    qc = np.arange(GRID_W)
    kc = np.arange(GRID_W)
    dr = j[None, :] - var[:, None] + NA_ROWS - 1
    dc = np.clip(kc[None, :] - qc[:, None] + NA_COLS - 1, 0, 2 * NA_COLS - 2)
    ws = np.clip(qc - NA_COLS // 2, 0, GRID_W - NA_COLS)
    ok = (kc[None, :] >= ws[:, None]) & (kc[None, :] < ws[:, None] + NA_COLS)
    onehot = (dc[None] == np.arange(2 * NA_COLS - 1)[:, None, None]).astype(np.float32)
    bias = jnp.einsum("hvjd,dqk->hvqjk", rpb.astype(F32)[:, dr], jnp.asarray(onehot),
                      precision=lax.Precision.HIGHEST)
    bias = jnp.where(jnp.asarray(ok)[None, None, :, None, :], bias, NEG_INF)
    return bias.reshape(rpb.shape[0], NA_ROWS, GRID_W, NA_ROWS * GRID_W)


def _neighbourhood_attn(q, kv, ckv, rpb):
    b, s, _ = q.shape
    lc = ckv.shape[1]
    rows = s // GRID_W
    bias = _na_bias(rpb)
    return pl.pallas_call(
        functools.partial(_nattn_kernel, rows=rows),
        out_shape=jax.ShapeDtypeStruct((b, s, W_GROUP), BF16),
        grid=(b, rows // NA_ROW_BLOCK),
        in_specs=[pl.BlockSpec((1, NA_ROW_BLOCK * GRID_W, W_GROUP), lambda i, r: (i, r, 0)),
                  pl.BlockSpec((1, s, W_GROUP), lambda i, r: (i, 0, 1)),
                  pl.BlockSpec((1, s, W_GROUP), lambda i, r: (i, 0, 2)),
                  pl.BlockSpec((1, lc, W_GROUP), lambda i, r: (i, 0, 1)),
                  pl.BlockSpec((1, lc, W_GROUP), lambda i, r: (i, 0, 2)),
                  pl.BlockSpec(bias.shape, lambda i, r: (0, 0, 0, 0))],
        out_specs=pl.BlockSpec((1, NA_ROW_BLOCK * GRID_W, W_GROUP), lambda i, r: (i, r, 0)),
        compiler_params=_cparams(("parallel", "arbitrary")),
        name="neighbourhood_attn",
    )(q, kv, kv, ckv, ckv, bias)


def _cattn_kernel(*refs, n_kv, with_sink, head_order):
    if with_sink:
        sink_ref, q_ref, k_ref, v_ref, o_ref = refs
    else:
        q_ref, k_ref, v_ref, o_ref = refs
    q, k, v = q_ref[0], k_ref[0], v_ref[0]
    group = N_WA_HEADS // n_kv
    outs = []
    for pos, h in enumerate(head_order):
        sl = slice((h // group) * HEAD_DIM, (h // group + 1) * HEAD_DIM)
        s = _nt_dot(q[:, pos * HEAD_DIM:(pos + 1) * HEAD_DIM], k[:, sl])
        extra = jnp.full((s.shape[0], 1), sink_ref[h], F32) if with_sink else None
        (p,), den = _softmax_parts([s], extra)
        outs.append(jnp.dot(p.astype(BF16), v[:, sl], preferred_element_type=F32) / den)
    o_ref[0] = jnp.concatenate(outs, axis=-1).astype(BF16)


def _ctx_attn(q, ckv, k_col, v_col, n_kv, sink, head_order=(0, 1, 2, 3)):
    b, lc, _ = q.shape
    w = n_kv * HEAD_DIM
    with_sink = sink is not None
    in_specs = [pl.BlockSpec((1, lc, W_GROUP), lambda i: (i, 0, 0)),
                pl.BlockSpec((1, lc, w), lambda i: (i, 0, k_col)),
                pl.BlockSpec((1, lc, w), lambda i: (i, 0, v_col))]
    args = [q, ckv, ckv]
    if with_sink:
        in_specs = [pl.BlockSpec(memory_space=pltpu.SMEM)] + in_specs
        args = [sink] + args
    return pl.pallas_call(
        functools.partial(_cattn_kernel, n_kv=n_kv, with_sink=with_sink, head_order=head_order),
        out_shape=jax.ShapeDtypeStruct((b, lc, W_GROUP), BF16),
        grid=(b,),
        in_specs=in_specs,
        out_specs=pl.BlockSpec((1, lc, W_GROUP), lambda i: (i, 0, 0)),
        compiler_params=_cparams(("parallel",)),
        name="ctx_attn",
    )(*args)


def _pool_kernel(prev_ref, cur_ref, next_ref, w_ref, scale_ref, o_ref, *, seq, tl):
    j = pl.program_id(1)
    nt = seq // tl
    cur = cur_ref[0]
    prev = jnp.where(j == 0, 0.0, prev_ref[0])
    nxt = jnp.where(j == nt - 1, 0.0, next_ref[0])
    e = jnp.concatenate([prev, cur, nxt], axis=0)
    n = tl + 2 * HALO

    def sh(a, d):
        return pltpu.roll(a, d % n, 0)

    s2 = e + sh(e, 1)
    s4 = sh(s2, 1) + sh(s2, -1)
    s8 = sh(s4, 2) + sh(s4, -2)
    s16 = sh(s8, 4) + sh(s8, -4)
    lane = lax.broadcasted_iota(jnp.int32, (tl, W_GROUP), 1)
    t = lax.broadcasted_iota(jnp.int32, (tl, W_GROUP), 0) + j * tl
    g = lane // POOL_GROUP
    lo, hi = HALO, HALO + tl
    ssum = jnp.where(g == 0, s2[lo:hi], jnp.where(g == 1, s4[lo:hi], jnp.where(g == 2, s8[lo:hi], s16[lo:hi])))
    half = jnp.where(g == 0, 1, jnp.where(g == 1, 2, jnp.where(g == 2, 4, 8)))
    cnt = (jnp.minimum(t + half, seq) - jnp.maximum(t - half, 0)).astype(F32)
    d = (ssum / cnt - cur).astype(BF16)
    o_ref[0] = (jnp.dot(d, w_ref[...], preferred_element_type=F32) * scale_ref[...]).astype(BF16)


def _pool_mixer(u, w_bd, scale):
    b, l, c = u.shape
    tl = min(l, 512)
    hb = tl // HALO
    nh = l // HALO
    return pl.pallas_call(
        functools.partial(_pool_kernel, seq=l, tl=tl),
        out_shape=jax.ShapeDtypeStruct((b, l, c), BF16),
        grid=(b, l // tl),
        in_specs=[pl.BlockSpec((1, HALO, c), lambda i, j: (i, jnp.maximum(j * hb - 1, 0), 0)),
                  pl.BlockSpec((1, tl, c), lambda i, j: (i, j, 0)),
                  pl.BlockSpec((1, HALO, c), lambda i, j: (i, jnp.minimum((j + 1) * hb, nh - 1), 0)),
                  pl.BlockSpec((c, c), lambda i, j: (0, 0)),
                  pl.BlockSpec((1, c), lambda i, j: (0, 0))],
        out_specs=pl.BlockSpec((1, tl, c), lambda i, j: (i, j, 0)),
        compiler_params=_cparams(("parallel", "parallel")),
        name="pool_mixer",
    )(u, u, u, w_bd, scale)


def _pool_weight(pool_w):
    z = jnp.zeros((W_GROUP, W_GROUP), F32)
    for g in range(len(POOL_WINDOWS)):
        z = z.at[g * POOL_GROUP:(g + 1) * POOL_GROUP, g * POOL_GROUP:(g + 1) * POOL_GROUP].set(pool_w[g])
    return z.astype(BF16)


def _outproj_kernel(yh_ref, ya_ref, yp_ref, yn_ref, x_ref, gt_ref, w_ref, g2_ref, sh_ref, sc_ref, rw_ref,
                    xo_ref, h_ref, aff_ref):
    y = jnp.concatenate([yh_ref[0], ya_ref[0], yp_ref[0], yn_ref[0]], axis=-1)
    x = x_ref[0] + gt_ref[0] * jnp.dot(y, w_ref[...], preferred_element_type=F32)
    xo_ref[0] = x
    h = _rmsnorm_mod(x, g2_ref[...], sh_ref[0], sc_ref[0])
    h_ref[0] = h
    logits = _nt_dot(rw_ref[...], h.astype(BF16))
    m = logits.max(axis=0, keepdims=True)
    p = jnp.exp(logits - m)
    aff_ref[0] = p / p.sum(axis=0, keepdims=True)


def _outproj(ys, x, gate, w_bf16, g2, shift, scale, rw_t):
    b, l, d = x.shape
    tm = min(l, 512)
    bm = gate.shape[0]
    mod_map = (lambda i, j: (i, 0, 0)) if bm > 1 else (lambda i, j: (0, 0, 0))
    tok = lambda w: pl.BlockSpec((1, tm, w), lambda i, j: (i, j, 0))
    mod = pl.BlockSpec((1, 1, d), mod_map)
    return pl.pallas_call(
        _outproj_kernel,
        out_shape=[jax.ShapeDtypeStruct((b, l, d), F32), jax.ShapeDtypeStruct((b, l, d), F32),
                   jax.ShapeDtypeStruct((b, N_EXPERTS, l), F32)],
        grid=(b, l // tm),
        in_specs=[tok(W_GROUP)] * 4 + [tok(d), mod, pl.BlockSpec((d, d), lambda i, j: (0, 0)),
                                        pl.BlockSpec((1, d), lambda i, j: (0, 0)), mod, mod,
                                        pl.BlockSpec((N_EXPERTS, d), lambda i, j: (0, 0))],
        out_specs=[tok(d), tok(d), pl.BlockSpec((1, N_EXPERTS, tm), lambda i, j: (i, 0, j))],
        compiler_params=_cparams(("parallel", "parallel")),
        name="outproj_router",
    )(*ys, x, gate, w_bf16, g2, shift, scale, rw_t)


def _row_gather_kernel(idx_ref, table_ref, out_ref, sem):
    def issue(r, c):
        pltpu.make_async_copy(table_ref.at[pl.ds(idx_ref[0, 0, r], 1)], out_ref.at[pl.ds(r, 1)], sem).start()
        return c

    lax.fori_loop(0, GATHER_CHUNK, issue, 0, unroll=8)
    pltpu.make_async_copy(table_ref.at[pl.ds(0, GATHER_CHUNK)], out_ref, sem).wait()


def _row_gather(table, rows):
    v, d = table.shape
    n = rows.shape[0]
    nch = n // GATHER_CHUNK
    return pl.pallas_call(
        _row_gather_kernel,
        out_shape=jax.ShapeDtypeStruct((n, d), table.dtype),
        grid=(nch,),
        in_specs=[pl.BlockSpec((1, 1, GATHER_CHUNK), lambda i: (i, 0, 0), memory_space=pltpu.SMEM),
                  pl.BlockSpec(memory_space=pl.ANY)],
        out_specs=pl.BlockSpec((GATHER_CHUNK, d), lambda i: (i, 0)),
        scratch_shapes=[pltpu.SemaphoreType.DMA(())],
        compiler_params=pltpu.CompilerParams(dimension_semantics=("arbitrary",), vmem_limit_bytes=VMEM_LIMIT,
                                             disable_bounds_checks=True),
        name="moe_row_gather",
    )(rows.reshape(nch, 1, GATHER_CHUNK), table)


def _moe_kernel(x_ref, g_ref, wg_ref, wu_ref, wd_ref, o_ref):
    f = pl.program_id(2)
    x = x_ref[0].astype(BF16)
    a = jnp.dot(x, wg_ref[0, 0].astype(BF16), preferred_element_type=F32)
    u = jnp.dot(x, wu_ref[0, 0].astype(BF16), preferred_element_type=F32)
    hid = (a * jax.nn.sigmoid(a) * u).astype(BF16)
    part = jnp.dot(hid, wd_ref[0, 0].astype(BF16), preferred_element_type=F32)

    @pl.when(f == 0)
    def _():
        o_ref[0] = part

    @pl.when(f != 0)
    def _():
        o_ref[0] += part

    @pl.when(f == pl.num_programs(2) - 1)
    def _():
        o_ref[0] = o_ref[0] * g_ref[0]


def _moe_ffn(xg, gate, w_gate, w_up, w_down, lyr):
    e, m, d = xg.shape
    f = w_gate.shape[3]
    tm = min(m, 1024)
    tf = 512
    return pl.pallas_call(
        _moe_kernel,
        out_shape=jax.ShapeDtypeStruct((e, m, d), F32),
        grid=(e, m // tm, f // tf),
        in_specs=[pl.BlockSpec((1, tm, d), lambda i, j, k: (i, j, 0)),
                  pl.BlockSpec((1, tm, 1), lambda i, j, k: (i, j, 0)),
                  pl.BlockSpec((1, 1, d, tf), lambda i, j, k: (lyr, i, 0, k)),
                  pl.BlockSpec((1, 1, d, tf), lambda i, j, k: (lyr, i, 0, k)),
                  pl.BlockSpec((1, 1, tf, d), lambda i, j, k: (lyr, i, k, 0))],
        out_specs=pl.BlockSpec((1, tm, d), lambda i, j, k: (i, j, 0)),
        compiler_params=_cparams(("parallel", "parallel", "arbitrary")),
        name="moe_ffn",
    )(xg, gate, w_gate, w_up, w_down)


def _expert_choice_ffn(h, aff_t, w_gate, w_up, w_down, lyr):
    b, l, d = h.shape
    cap = EC_CAPACITY * l // N_EXPERTS
    m = b * cap
    g, idx = lax.top_k(aff_t, cap)
    idx, g = lax.sort((idx, g), dimension=2, num_keys=1)
    idx_t = jnp.swapaxes(idx, 0, 1)
    rows = (idx_t + (jnp.arange(b, dtype=jnp.int32) * l)[None, :, None]).reshape(-1)
    xg = _row_gather(h.reshape(b * l, d), rows).reshape(N_EXPERTS, m, d)
    gate = jnp.swapaxes(g, 0, 1).reshape(N_EXPERTS, m, 1)
    o = _moe_ffn(xg, gate, w_gate, w_up, w_down, lyr).reshape(N_EXPERTS * m, d)
    return o, idx


def _combine_kernel(*refs, final, tile, nt, cap, strip, m):
    if final:
        lo_ref, idx_ref, x_ref, gt_ref, g_ref, o_hbm, out_ref, stage, sems = refs
    else:
        lo_ref, idx_ref, x_ref, gt_ref, o_hbm, out_ref, stage, sems = refs
    bi, j = pl.program_id(0), pl.program_id(1)
    step = bi * nt + j
    cur = step % 2
    ne = N_EXPERTS
    gps = strip // SUBLANES
    per_sub = COMBINE_SUB // strip
    win = 2 * LANES
    sentinel = tile * nt

    def runs(jj):
        out = []
        for e in range(ne):
            lo = lo_ref[bi, e * (nt + 1) + jj]
            hi = lo_ref[bi, e * (nt + 1) + jj + 1]
            a8 = (lo // SUBLANES) * SUBLANES
            out.append((a8, jnp.where(hi > lo, hi - a8, 0)))
        return out

    def groups(run, r):
        return [(a8 + strip * r, jnp.clip((n - strip * r + SUBLANES - 1) // SUBLANES, 0, gps)) for a8, n in run]

    def issue(s, grp):
        for e, (s0, ng) in enumerate(grp):
            base = e * m + bi * cap + s0

            def one(k, c, base=base, e=e):
                pltpu.make_async_copy(
                    o_hbm.at[pl.ds(pl.multiple_of(base + k * SUBLANES, SUBLANES), SUBLANES)],
                    stage.at[s, pl.ds(pl.multiple_of(e * strip + k * SUBLANES, SUBLANES), SUBLANES)],
                    sems.at[s]).start()
                return c
            lax.fori_loop(0, ng, one, 0)

    def wait(s, grp):
        cnt = sum(ng for _, ng in grp) * SUBLANES

        @pl.when(cnt > 0)
        def _():
            rows = pl.ds(0, pl.multiple_of(cnt, SUBLANES))
            pltpu.make_async_copy(o_hbm.at[rows], stage.at[s, rows], sems.at[s]).wait()

    def reduce(s, grp):
        t_ids = lax.broadcasted_iota(jnp.int32, (tile, COMBINE_SUB), 0) + j * tile
        acc = jnp.zeros((tile, out_ref.shape[2]), F32)
        for sub in range(ne // per_sub):
            pieces = []
            for e in range(sub * per_sub, (sub + 1) * per_sub):
                s0, ng = grp[e]
                a = jnp.minimum((s0 // LANES) * LANES, idx_ref.shape[2] - win)
                w = idx_ref[0, e:e + 1, pl.ds(pl.multiple_of(a, LANES), win)]
                w = pltpu.roll(w, (win - (s0 - a)) % win, 1)
                pieces.append(jnp.where(ng > 0, w[:, :strip], sentinel))
            toks = jnp.concatenate(pieces, axis=1)
            rows = stage[s, sub * COMBINE_SUB:(sub + 1) * COMBINE_SUB, :]
            onehot = jnp.where(toks == t_ids, 1.0, 0.0).astype(BF16)
            hi = rows.astype(BF16)
            lo = (rows - hi.astype(F32)).astype(BF16)
            acc = (acc + jnp.dot(onehot, hi, preferred_element_type=F32)
                   + jnp.dot(onehot, lo, preferred_element_type=F32))
        return acc

    @pl.when(step == 0)
    def _():
        stage[...] = jnp.zeros(stage.shape, F32)

    run = runs(j)
    first = groups(run, 0)

    @pl.when(j == 0)
    def _():
        issue(cur, first)

    @pl.when(j + 1 < nt)
    def _():
        issue(1 - cur, groups(runs(jnp.minimum(j + 1, nt - 1)), 0))

    wait(cur, first)
    acc = reduce(cur, first)
    longest = run[0][1]
    for _, n in run[1:]:
        longest = jnp.maximum(longest, n)

    def extra_round(r, acc):
        grp = groups(run, r)
        issue(cur, grp)
        wait(cur, grp)
        return acc + reduce(cur, grp)

    acc = lax.fori_loop(1, (longest + strip - 1) // strip, extra_round, acc)
    x = x_ref[0] + gt_ref[0] * acc
    if final:
        x = x * lax.rsqrt(jnp.mean(x * x, axis=-1, keepdims=True) + NORM_EPS) * g_ref[...]
    out_ref[0] = x


def _combine(x, o, idx, gate, final_g=None):
    b, l, d = x.shape
    ne, cap = idx.shape[1], idx.shape[2]
    tile = min(l, COMBINE_TILE)
    nt = l // tile
    strip = min(COMBINE_STRIP, cap)
    bounds = jnp.arange(nt + 1, dtype=jnp.int32) * tile
    lo = jnp.sum(idx[:, :, None, :] < bounds[None, None, :, None], axis=-1, dtype=jnp.int32)
    cp = -(-cap // LANES) * LANES + LANES
    idx_p = jnp.pad(idx, ((0, 0), (0, 0), (0, cp - cap)), constant_values=l)
    bm = gate.shape[0]
    mod_map = (lambda i, j: (i, 0, 0)) if bm > 1 else (lambda i, j: (0, 0, 0))
    tok = pl.BlockSpec((1, tile, d), lambda i, j: (i, j, 0))
    final = final_g is not None
    in_specs = [pl.BlockSpec(memory_space=pltpu.SMEM),
                pl.BlockSpec((1, ne, cp), lambda i, j: (i, 0, 0)),
                tok, pl.BlockSpec((1, 1, d), mod_map)]
    args = [lo.reshape(b, ne * (nt + 1)), idx_p, x, gate]
    if final:
        in_specs.append(pl.BlockSpec((1, d), lambda i, j: (0, 0)))
        args.append(final_g)
    in_specs.append(pl.BlockSpec(memory_space=pl.ANY))
    args.append(o)
    return pl.pallas_call(
        functools.partial(_combine_kernel, final=final, tile=tile, nt=nt, cap=cap, strip=strip, m=b * cap),
        out_shape=jax.ShapeDtypeStruct((b, l, d), F32),
        grid=(b, nt),
        in_specs=in_specs,
        out_specs=tok,
        scratch_shapes=[pltpu.VMEM((2, ne * strip, d), F32), pltpu.SemaphoreType.DMA((2,))],
        compiler_params=pltpu.CompilerParams(dimension_semantics=("arbitrary", "arbitrary"),
                                             vmem_limit_bytes=VMEM_LIMIT, disable_bounds_checks=True),
        name="moe_combine_final" if final else "moe_combine",
    )(*args)


def _hyena_filters(l, w1, b1, w2, b2, w3, freq):
    hp = lax.Precision.HIGHEST
    pos = jnp.arange(l, dtype=F32)
    t = pos / max(l - 1, 1)
    bands = jnp.linspace(1e-4, HY_BANDS - 1, HY_BANDS, dtype=F32)
    ang = (2 * math.pi / l) * pos[:, None] * bands[None, :]
    feats = jnp.concatenate([t[:, None], jnp.cos(ang), -jnp.sin(ang)], axis=-1)
    h = jnp.sin(freq * (jnp.dot(feats, w1, precision=hp) + b1))
    h = jnp.sin(freq * (jnp.dot(h, w2, precision=hp) + b2))
    h = jnp.dot(h, w3, precision=hp).reshape(l, HY_ORDER, 2, W_GROUP)
    deltas = jnp.linspace(HY_DECAY_MIN, HY_DECAY_MAX, W_GROUP, dtype=F32)
    window = jnp.exp(-t[:, None] * deltas[None, :])
    return h * window[:, None, None, :]


def _two_sided(h_fwd, h_bwd):
    lag0 = h_fwd[:1] + h_bwd[:1]
    return jnp.concatenate([lag0, h_fwd[1:], jnp.zeros_like(h_fwd[:1]), h_bwd[:0:-1]], axis=0)


def _short_conv_kernel(prev_ref, cur_ref, next_ref, w_ref, b_ref, o_ref, *, seq, tl):
    j = pl.program_id(1)
    nt = seq // tl
    prev = jnp.where(j == 0, 0.0, prev_ref[0])
    nxt = jnp.where(j == nt - 1, 0.0, next_ref[0])
    e = jnp.concatenate([prev, cur_ref[0], nxt], axis=0)
    n = tl + 2 * HALO
    y = pltpu.roll(e, 1, 0) * w_ref[0:1, :] + e * w_ref[1:2, :] + pltpu.roll(e, n - 1, 0) * w_ref[2:3, :]
    o_ref[0] = y[HALO:HALO + tl] + b_ref[...]


def _short_conv(u, w, b):
    bsz, l, c = u.shape
    tl = min(l, 512)
    hb = tl // HALO
    nh = l // HALO
    return pl.pallas_call(
        functools.partial(_short_conv_kernel, seq=l, tl=tl),
        out_shape=jax.ShapeDtypeStruct((bsz, l, c), F32),
        grid=(bsz, l // tl),
        in_specs=[pl.BlockSpec((1, HALO, c), lambda i, j: (i, jnp.maximum(j * hb - 1, 0), 0)),
                  pl.BlockSpec((1, tl, c), lambda i, j: (i, j, 0)),
                  pl.BlockSpec((1, HALO, c), lambda i, j: (i, jnp.minimum((j + 1) * hb, nh - 1), 0)),
                  pl.BlockSpec((3, c), lambda i, j: (0, 0)),
                  pl.BlockSpec((1, c), lambda i, j: (0, 0))],
        out_specs=pl.BlockSpec((1, tl, c), lambda i, j: (i, j, 0)),
        compiler_params=_cparams(("parallel", "parallel")),
        name="hyena_short_conv",
    )(u, u, u, w, b)


def _cplx_block(e):
    return np.block([[e.real, -e.imag], [e.imag, e.real]])


@functools.lru_cache(maxsize=None)
def _dft_consts(l):
    n = 2 * l
    n1f = n // DFT_N2
    n1h = n1f // 2
    k1, n1, n2 = np.arange(n1f), np.arange(n1h), np.arange(DFT_N2)
    ph = np.outer(k1, n1)[None] / n1f + (n2[:, None, None] * k1[None, :, None]) / n
    e1 = np.exp(-2j * np.pi * ph)
    w1 = np.stack([_cplx_block(e1[i]) for i in range(DFT_N2)])
    e2 = np.exp(-2j * np.pi * np.outer(n2, n2) / DFT_N2)
    f2 = _cplx_block(e2)
    g2 = _cplx_block(np.conj(e2).T)
    g1 = np.stack([_cplx_block(np.conj(e1[i]).T / n) for i in range(DFT_N2)])
    return tuple(jnp.asarray(a, F32).astype(BF16) for a in (w1, f2, g2, g1))


def _lconv_kernel(a_ref, hf_ref, w1_ref, f2_ref, g2_ref, g1_ref, y_ref, xy_ref, s_ref, *, n1h):
    n1f = 2 * n1h

    def copy_in(i, c):
        for ri in range(2):
            xy_ref[ri, pl.ds(pl.multiple_of(i * X_PITCH, 8), DFT_N2), :] = \
                a_ref[0, ri, pl.ds(pl.multiple_of(i * DFT_N2, DFT_N2), DFT_N2), :]
        return c

    lax.fori_loop(0, n1h, copy_in, 0, unroll=4)

    def stage1(n2, c):
        slab = jnp.concatenate([xy_ref[0, pl.ds(n2, n1h, stride=X_PITCH), :],
                                xy_ref[1, pl.ds(n2, n1h, stride=X_PITCH), :]], axis=0).astype(BF16)
        a = jnp.dot(w1_ref[n2], slab, preferred_element_type=F32)
        base = pl.multiple_of(n2 * S_PITCH, 8)
        s_ref[0, pl.ds(base, n1f), :] = a[:n1f]
        s_ref[1, pl.ds(base, n1f), :] = a[n1f:]
        return c

    lax.fori_loop(0, DFT_N2, stage1, 0, unroll=32)

    def stage2(k1, c):
        slab = jnp.concatenate([s_ref[0, pl.ds(k1, DFT_N2, stride=S_PITCH), :],
                                s_ref[1, pl.ds(k1, DFT_N2, stride=S_PITCH), :]], axis=0).astype(BF16)
        x = jnp.dot(f2_ref[...], slab, preferred_element_type=F32)
        xr, xi = x[:DFT_N2], x[DFT_N2:]
        hr, hi = hf_ref[0, k1, 0], hf_ref[0, k1, 1]
        y = jnp.concatenate([xr * hr - xi * hi, xr * hi + xi * hr], axis=0).astype(BF16)
        cc = jnp.dot(g2_ref[...], y, preferred_element_type=F32)
        s_ref[0, pl.ds(k1, DFT_N2, stride=S_PITCH), :] = cc[:DFT_N2]
        s_ref[1, pl.ds(k1, DFT_N2, stride=S_PITCH), :] = cc[DFT_N2:]
        return c

    lax.fori_loop(0, n1f, stage2, 0, unroll=16)

    def stage3(n2, c):
        base = pl.multiple_of(n2 * S_PITCH, 8)
        d = jnp.concatenate([s_ref[0, pl.ds(base, n1f), :], s_ref[1, pl.ds(base, n1f), :]], axis=0).astype(BF16)
        yv = jnp.dot(g1_ref[n2], d, preferred_element_type=F32)
        yb = pl.multiple_of(n2 * Y_PITCH, 8)
        xy_ref[0, pl.ds(yb, n1h), :] = yv[:n1h]
        xy_ref[1, pl.ds(yb, n1h), :] = yv[n1h:]
        return c

    lax.fori_loop(0, DFT_N2, stage3, 0, unroll=32)

    def copy_out(i, c):
        for ri in range(2):
            y_ref[0, ri, pl.ds(pl.multiple_of(i * DFT_N2, DFT_N2), DFT_N2), :] = \
                xy_ref[ri, pl.ds(i, DFT_N2, stride=Y_PITCH), :]
        return c

    lax.fori_loop(0, n1h, copy_out, 0, unroll=4)


def _lconv_small_kernel(a_ref, hf_ref, f_ref, g_ref, y_ref):
    l = a_ref.shape[2]
    slab = jnp.concatenate([a_ref[0, 0], a_ref[0, 1]], axis=0).astype(BF16)
    x = jnp.dot(f_ref[...], slab, preferred_element_type=F32)
    xr, xi = x[:2 * l], x[2 * l:]
    hr, hi = hf_ref[0, 0], hf_ref[0, 1]
    y = jnp.concatenate([xr * hr - xi * hi, xr * hi + xi * hr], axis=0).astype(BF16)
    out = jnp.dot(g_ref[...], y, preferred_element_type=F32)
    y_ref[0, 0] = out[:l]
    y_ref[0, 1] = out[l:]


@functools.lru_cache(maxsize=None)
def _dft_consts_small(l):
    n = 2 * l
    e = np.exp(-2j * np.pi * np.outer(np.arange(n), np.arange(l)) / n)
    f = _cplx_block(e)
    g = _cplx_block(np.conj(e).T / n)
    return jnp.asarray(f, F32).astype(BF16), jnp.asarray(g, F32).astype(BF16)


def _long_conv(a, hf, order):
    b, l, ca = a.shape
    c = W_GROUP
    a4 = a.reshape(b // 2, 2, l, ca)
    io_spec = pl.BlockSpec((1, 2, l, LANES), lambda j, p: (p, 0, 0, j))
    once = dict(pipeline_mode=pl.Buffered(1))
    const = lambda arr: pl.BlockSpec(arr.shape, lambda j, p: (0,) * arr.ndim, **once)
    if l <= DFT_N2 * 2:
        f, g = _dft_consts_small(l)
        y = pl.pallas_call(
            _lconv_small_kernel,
            out_shape=jax.ShapeDtypeStruct((b // 2, 2, l, c), F32),
            grid=(c // LANES, b // 2),
            in_specs=[io_spec, pl.BlockSpec((1, 2, 2 * l, LANES), lambda j, p: (order, 0, 0, j)), const(f), const(g)],
            out_specs=io_spec,
            compiler_params=_cparams(("parallel", "arbitrary")),
            name="long_conv_small",
        )(a4, hf, f, g)
        return y.reshape(b, l, c)
    n1f = 2 * l // DFT_N2
    n1h = n1f // 2
    assert (n1f + 8, n1h + 8) == (S_PITCH, Y_PITCH), "scratch pitches are sized for this sequence length"
    w1, f2, g2, g1 = _dft_consts(l)
    rows_xy = max(n1h * X_PITCH, DFT_N2 * Y_PITCH)
    y = pl.pallas_call(
        functools.partial(_lconv_kernel, n1h=n1h),
        out_shape=jax.ShapeDtypeStruct((b // 2, 2, l, c), F32),
        grid=(c // LANES, b // 2),
        in_specs=[io_spec, pl.BlockSpec((1, n1f, 2, DFT_N2, LANES), lambda j, p: (order, 0, 0, 0, j), **once),
                  const(w1), const(f2), const(g2), const(g1)],
        out_specs=io_spec,
        scratch_shapes=[pltpu.VMEM((2, rows_xy, LANES), F32), pltpu.VMEM((2, DFT_N2 * S_PITCH, LANES), F32)],
        compiler_params=_cparams(("parallel", "arbitrary"), 52 * 1024 * 1024),
        name="long_conv",
    )(a4, hf, w1, f2, g2, g1)
    return y.reshape(b, l, c)


@functools.lru_cache(maxsize=None)
def _fspec_consts(n):
    n1f = n // DFT_N2
    k1, n1, n2 = np.arange(n1f), np.arange(n1f), np.arange(DFT_N2)
    ph = np.outer(k1, n1)[None] / n1f + (n2[:, None, None] * k1[None, :, None]) / n
    e1 = np.exp(-2j * np.pi * ph)
    w1 = np.concatenate([e1.real, e1.imag], axis=1)
    f2 = _cplx_block(np.exp(-2j * np.pi * np.outer(n2, n2) / DFT_N2))
    return jnp.asarray(w1, F32).astype(BF16), jnp.asarray(f2, F32).astype(BF16)


def _fspec_kernel(f_ref, w1_ref, f2_ref, hf_ref, x_ref, s_ref, *, n1f):
    def copy_in(i, c):
        x_ref[pl.ds(pl.multiple_of(i * X_PITCH, 8), DFT_N2), :] = \
            f_ref[0, pl.ds(pl.multiple_of(i * DFT_N2, DFT_N2), DFT_N2), :]
        return c

    lax.fori_loop(0, n1f, copy_in, 0, unroll=4)

    def stage1(n2, c):
        slab = x_ref[pl.ds(n2, n1f, stride=X_PITCH), :].astype(BF16)
        a = jnp.dot(w1_ref[n2], slab, preferred_element_type=F32)
        base = pl.multiple_of(n2 * S_PITCH, 8)
        s_ref[0, pl.ds(base, n1f), :] = a[:n1f]
        s_ref[1, pl.ds(base, n1f), :] = a[n1f:]
        return c

    lax.fori_loop(0, DFT_N2, stage1, 0, unroll=32)

    def stage2(k1, c):
        slab = jnp.concatenate([s_ref[0, pl.ds(k1, DFT_N2, stride=S_PITCH), :],
                                s_ref[1, pl.ds(k1, DFT_N2, stride=S_PITCH), :]], axis=0).astype(BF16)
        x = jnp.dot(f2_ref[...], slab, preferred_element_type=F32)
        hf_ref[0, k1, 0] = x[:DFT_N2]
        hf_ref[0, k1, 1] = x[DFT_N2:]
        return c

    lax.fori_loop(0, n1f, stage2, 0, unroll=16)


def _fspec_small_kernel(f_ref, w_ref, hf_ref):
    n = f_ref.shape[1]
    x = jnp.dot(w_ref[...], f_ref[0].astype(BF16), preferred_element_type=F32)
    hf_ref[0, 0] = x[:n]
    hf_ref[0, 1] = x[n:]


@functools.lru_cache(maxsize=None)
def _fspec_consts_small(n):
    e = np.exp(-2j * np.pi * np.outer(np.arange(n), np.arange(n)) / n)
    return jnp.asarray(np.concatenate([e.real, e.imag], axis=0), F32).astype(BF16)


def _filter_spectra(filt):
    l, orders, _, c = filt.shape
    n = 2 * l
    f2s = jnp.stack([_two_sided(filt[:, o, 0], filt[:, o, 1]) for o in range(orders)])
    once = dict(pipeline_mode=pl.Buffered(1))
    const = lambda arr: pl.BlockSpec(arr.shape, lambda o, j: (0,) * arr.ndim, **once)
    in_spec = pl.BlockSpec((1, n, LANES), lambda o, j: (o, 0, j))
    if l <= DFT_N2 * 2:
        w = _fspec_consts_small(n)
        return pl.pallas_call(
            _fspec_small_kernel,
            out_shape=jax.ShapeDtypeStruct((orders, 2, n, c), F32),
            grid=(orders, c // LANES),
            in_specs=[in_spec, const(w)],
            out_specs=pl.BlockSpec((1, 2, n, LANES), lambda o, j: (o, 0, 0, j)),
            compiler_params=_cparams(("parallel", "parallel")),
            name="filter_spectrum_small",
        )(f2s, w)
    n1f = n // DFT_N2
    assert n1f + 8 == S_PITCH, "scratch pitches are sized for this sequence length"
    w1, f2 = _fspec_consts(n)
    return pl.pallas_call(
        functools.partial(_fspec_kernel, n1f=n1f),
        out_shape=jax.ShapeDtypeStruct((orders, n1f, 2, DFT_N2, c), F32),
        grid=(orders, c // LANES),
        in_specs=[in_spec, const(w1), const(f2)],
        out_specs=pl.BlockSpec((1, n1f, 2, DFT_N2, LANES), lambda o, j: (o, 0, 0, 0, j)),
        scratch_shapes=[pltpu.VMEM((n1f * X_PITCH, LANES), F32), pltpu.VMEM((2, DFT_N2 * S_PITCH, LANES), F32)],
        compiler_params=_cparams(("parallel", "parallel"), 52 * 1024 * 1024),
        name="filter_spectrum",
    )(f2s, w1, f2)


def _hy_gate_kernel(y_ref, a_ref, m_ref, sk_ref, o_ref):
    o_ref[0] = (m_ref[0] * (y_ref[0] + a_ref[0] * sk_ref[...])).astype(o_ref.dtype)


def _hy_gate(y, a, a_col, m, m_col, sk, out_dtype):
    b, l, c = y.shape
    tl = min(l, 1024)
    spec = lambda col: pl.BlockSpec((1, tl, c), lambda i, j: (i, j, col))
    return pl.pallas_call(
        _hy_gate_kernel,
        out_shape=jax.ShapeDtypeStruct((b, l, c), out_dtype),
        grid=(b, l // tl),
        in_specs=[spec(0), spec(a_col), spec(m_col), pl.BlockSpec((1, c), lambda i, j: (0, 0))],
        out_specs=spec(0),
        compiler_params=_cparams(("parallel", "parallel")),
        name="hyena_gate",
    )(y, a, m, sk)


def _hyena(u, conv_w, conv_b, filt, skip):
    uc = _short_conv(u, conv_w, conv_b[None, :])
    hf = _filter_spectra(filt)
    y1 = _long_conv(uc, hf, 0)
    z1 = _hy_gate(y1, uc, 0, uc, 1, skip[0:1], F32)
    y2 = _long_conv(z1, hf, 1)
    return _hy_gate(y2, z1, 0, uc, 2, skip[1:2], BF16)


def _permute_wa_heads(w, start, axis):
    idx = np.arange(w.shape[axis])
    blocks = [np.arange(start + h * HEAD_DIM, start + (h + 1) * HEAD_DIM) for h in WA_HEAD_ORDER]
    idx[start:start + N_WA_HEADS * HEAD_DIM] = np.concatenate(blocks)
    return jnp.take(w, idx, axis=axis)


def _layer(x, ctx, mod, lyr, rope_tabs, update_ctx, final_g):
    b, s, d = x.shape
    lc = ctx.shape[1]
    mx = [mod[:b, None, i * d:(i + 1) * d] for i in range(6)]
    mc = [mod[b:b + 1, None, i * d:(i + 1) * d] for i in range(6)]
    g1, g2 = lyr["g1"][None, :], lyr["g2"][None, :]
    w_in = _permute_wa_heads(lyr["w_in"], OFF_WA_Q, axis=1).astype(BF16)
    w_out = _permute_wa_heads(lyr["w_out"], W_GROUP, axis=0).astype(BF16)
    rw_t = lyr["router_w"].T.astype(BF16)
    w_bd = _pool_weight(lyr["pool_w"])
    pscale = lyr["pool_scale"][None, :]
    hy_args = (lyr["hy_w1"], lyr["hy_b1"], lyr["hy_w2"], lyr["hy_b2"], lyr["hy_w3"], lyr["hy_freq"])

    u_hy, u_pool, q_wa, q_na, kv = _inproj(x, mx[0], mx[1], g1, w_in, rope_tabs)
    if update_ctx:
        cu_hy, cu_pool, cq_wa, cq_na, ckv = _inproj(ctx, mc[0], mc[1], g1, w_in)
    else:
        ckv = _inproj(ctx, mc[0], mc[1], g1, w_in[:, OFF_KV:], kv_only=True)

    ys = [_hyena(u_hy, lyr["hy_conv_w"], lyr["hy_conv_b"], _hyena_filters(s, *hy_args), lyr["hy_skip"]),
          _window_attn(q_wa, kv, ckv, lyr["wa_sink"]),
          _pool_mixer(u_pool, w_bd, pscale),
          _neighbourhood_attn(q_na, kv, ckv, lyr["na_rpb"])]
    x, h, aff = _outproj(ys, x, mx[2], w_out, g2, mx[3], mx[4], rw_t)
    moe = _expert_choice_ffn(h, aff, *lyr["experts"], lyr["index"])
    x = _combine(x, *moe, mx[5], final_g)

    if update_ctx:
        ycs = [_hyena(cu_hy, lyr["hy_conv_w"], lyr["hy_conv_b"], _hyena_filters(lc, *hy_args), lyr["hy_skip"]),
               _ctx_attn(cq_wa, ckv, 0, 1, N_WA_KV, lyr["wa_sink"], WA_HEAD_ORDER),
               _pool_mixer(cu_pool, w_bd, pscale),
               _ctx_attn(cq_na, ckv, 1, 2, N_NA_HEADS, None)]
        ctx, hc, affc = _outproj(ycs, ctx, mc[2], w_out, g2, mc[3], mc[4], rw_t)
        moe_c = _expert_choice_ffn(hc, affc, *lyr["experts"], lyr["index"])
        ctx = _combine(ctx, *moe_c, mc[5])
    return x, ctx


def kernel(x, c, ctx, c_ctx, ada_w, ada_b, norm1_g, norm2_g, w_in, hy_conv_w, hy_conv_b, hy_w1, hy_b1, hy_w2,
           hy_b2, hy_w3, hy_freq, hy_skip, wa_sink, pool_w, pool_scale, na_rpb, w_out, router_w, exp_w_gate,
           exp_w_up, exp_w_down, final_norm_g):
    b, s, d = x.shape
    cs = jnp.zeros((MOD_ROWS, d), F32).at[:b].set(c).at[b].set(c_ctx)
    mods = _ada_mod(cs, ada_w, ada_b[:, None, :])
    rope_tabs = _rope_tables(s)
    params = dict(g1=norm1_g, g2=norm2_g, w_in=w_in, hy_conv_w=hy_conv_w, hy_conv_b=hy_conv_b, hy_w1=hy_w1,
                  hy_b1=hy_b1, hy_w2=hy_w2, hy_b2=hy_b2, hy_w3=hy_w3, hy_freq=hy_freq, hy_skip=hy_skip,
                  wa_sink=wa_sink, pool_w=pool_w, pool_scale=pool_scale, na_rpb=na_rpb, w_out=w_out,
                  router_w=router_w)
    for l in range(DEPTH):
        lyr = {k: v[l] for k, v in params.items()}
        lyr.update(index=l, experts=(exp_w_gate, exp_w_up, exp_w_down))
        last = l == DEPTH - 1
        x, ctx = _layer(x, ctx, mods[l], lyr, rope_tabs, update_ctx=not last,
                        final_g=final_norm_g[None, :] if last else None)
    return x
```

```python
import functools
import math

import jax
import jax.numpy as jnp
import numpy as np
from jax import lax
from jax.experimental import pallas as pl
from jax.experimental.pallas import tpu as pltpu

F32 = jnp.float32
BF16 = jnp.bfloat16

D_MODEL = 1024
DEPTH = 2
GRID_W = 64
HEAD_DIM = 64
W_GROUP = 256
N_WA_HEADS = 4
N_WA_KV = 2
N_NA_HEADS = 4
KV_WA = 128
OFF_POOL = 768
OFF_WA_Q = 1024
OFF_NA_Q = 1280
OFF_KV = 1536
IN_WIDTH = 2304
KV_WIDTH = IN_WIDTH - OFF_KV
HY_ORDER = 2
HY_BANDS = 16
HY_DECAY_MIN = abs(math.log(1e-2) / 1.5)
HY_DECAY_MAX = abs(math.log(1e-2) / 0.3)
WA_BLOCK = 128
WA_STEP_BLOCKS = 4
WA_HEAD_ORDER = (0, 2, 1, 3)
POOL_WINDOWS = (2, 4, 8, 16)
POOL_GROUP = 64
HALO = 8
LANES = 128
SUBLANES = 8
DFT_N2 = 128
X_PITCH = DFT_N2 + 8
S_PITCH = 64 + 8
Y_PITCH = 32 + 8
NA_ROWS = 8
NA_COLS = 16
NA_ROW_BLOCK = 8
ROPE_BASE = 10000.0
N_EXPERTS = 16
EXPERT_HIDDEN = 2048
EC_CAPACITY = 2
NORM_EPS = 1e-6
NEG_INF = -1e30
Q_SCALE = HEAD_DIM ** -0.5

GATHER_CHUNK = 2048
COMBINE_TILE = 256
COMBINE_SUB = 256
COMBINE_STRIP = 64
MOD_ROWS = 16
VMEM_LIMIT = 48 * 1024 * 1024


def _cparams(sem, vmem=VMEM_LIMIT):
    return pltpu.CompilerParams(dimension_semantics=sem, vmem_limit_bytes=vmem)


def _nt_dot(a, b):
    return lax.dot_general(a, b, (((1,), (1,)), ((), ())), preferred_element_type=F32)


def _rmsnorm_mod(x, g, shift, scale):
    y = x * lax.rsqrt(jnp.mean(x * x, axis=-1, keepdims=True) + NORM_EPS) * g
    return y * (1.0 + scale) + shift


def _ada_kernel(c_ref, w_ref, b_ref, o_ref):
    c = c_ref[...]
    s = (c * jax.nn.sigmoid(c)).astype(BF16)
    o_ref[0] = jnp.dot(s, w_ref[0].astype(BF16), preferred_element_type=F32) + b_ref[0]


def _ada_mod(cs, ada_w, ada_b):
    nl, d, n = ada_w.shape
    tn = 1024
    return pl.pallas_call(
        _ada_kernel,
        out_shape=jax.ShapeDtypeStruct((nl, MOD_ROWS, n), F32),
        grid=(nl, n // tn),
        in_specs=[pl.BlockSpec((MOD_ROWS, d), lambda l, j: (0, 0)),
                  pl.BlockSpec((1, d, tn), lambda l, j: (l, 0, j)),
                  pl.BlockSpec((1, 1, tn), lambda l, j: (l, 0, j))],
        out_specs=pl.BlockSpec((1, MOD_ROWS, tn), lambda l, j: (l, 0, j)),
        compiler_params=_cparams(("parallel", "parallel")),
        name="ada_mod",
    )(cs, ada_w, ada_b)


def _rope(t, cos, s_up, s_dn):
    w = t.shape[-1]
    if w > 128:
        cos, s_up, s_dn = (jnp.concatenate([a] * (w // 128), axis=-1) for a in (cos, s_up, s_dn))
    up = pltpu.roll(t, w - 16, 1)
    dn = pltpu.roll(t, 16, 1)
    return t * cos + up * s_up + dn * s_dn


def _inproj_kernel(*refs, rope, kv_only):
    if kv_only:
        x_ref, sh_ref, sc_ref, g_ref, w_ref, o_ref = refs
    elif rope:
        x_ref, xp_ref, xn_ref, sh_ref, sc_ref, g_ref, w_ref, cw_ref, cb_ref, cos_ref, sup_ref, sdn_ref, *outs = refs
    else:
        x_ref, xp_ref, xn_ref, sh_ref, sc_ref, g_ref, w_ref, cw_ref, cb_ref, *outs = refs
    h = _rmsnorm_mod(x_ref[0], g_ref[...], sh_ref[0], sc_ref[0])
    p = jnp.dot(h.astype(BF16), w_ref[...], preferred_element_type=F32)
    if kv_only:
        o_ref[0] = p.astype(BF16)
        return
    hy_ref, pool_ref, qwa_ref, qna_ref, kv_ref = outs
    j, nt, tm = pl.program_id(0), pl.num_programs(0), x_ref.shape[1]
    halo = _rmsnorm_mod(jnp.concatenate([xp_ref[0], xn_ref[0]], axis=0), g_ref[...], sh_ref[0], sc_ref[0])
    ph = jnp.dot(halo.astype(BF16), w_ref[:, :OFF_POOL], preferred_element_type=F32)
    e = jnp.concatenate([jnp.where(j == 0, 0.0, ph[:HALO]), p[:, :OFF_POOL],
                         jnp.where(j == nt - 1, 0.0, ph[HALO:])], axis=0)
    n = tm + 2 * HALO
    uc = pltpu.roll(e, 1, 0) * cw_ref[0:1, :] + e * cw_ref[1:2, :] + pltpu.roll(e, n - 1, 0) * cw_ref[2:3, :]
    hy_ref[0] = uc[HALO:HALO + tm] + cb_ref[...]
    pool_ref[0] = p[:, OFF_POOL:OFF_WA_Q]
    qwa = p[:, OFF_WA_Q:OFF_NA_Q]
    kwa = p[:, OFF_KV:OFF_KV + KV_WA]
    if rope:
        tabs = (cos_ref[...], sup_ref[...], sdn_ref[...])
        qwa = _rope(qwa, *tabs)
        kwa = _rope(kwa, *tabs)
    qwa_ref[0] = (qwa * Q_SCALE).astype(BF16)
    qna_ref[0] = (p[:, OFF_NA_Q:OFF_KV] * Q_SCALE).astype(BF16)
    kv_ref[0] = jnp.concatenate([kwa, p[:, OFF_KV + KV_WA:]], axis=-1).astype(BF16)


def _inproj(x, shift, scale, g, w_bf16, conv=None, rope_tabs=None, kv_only=False):
    b, l, d = x.shape
    n = w_bf16.shape[1]
    tm = min(l, 512)
    hb, nh = tm // HALO, l // HALO
    rope = rope_tabs is not None
    bm = shift.shape[0]
    mod_map = (lambda j, i: (i, 0, 0)) if bm > 1 else (lambda j, i: (0, 0, 0))
    in_specs = [pl.BlockSpec((1, tm, d), lambda j, i: (i, j, 0))]
    args = [x]
    if not kv_only:
        in_specs += [pl.BlockSpec((1, HALO, d), lambda j, i: (i, jnp.maximum(j * hb - 1, 0), 0)),
                     pl.BlockSpec((1, HALO, d), lambda j, i: (i, jnp.minimum((j + 1) * hb, nh - 1), 0))]
        args += [x, x]
    in_specs += [pl.BlockSpec((1, 1, d), mod_map),
                 pl.BlockSpec((1, 1, d), mod_map),
                 pl.BlockSpec((1, d), lambda j, i: (0, 0)),
                 pl.BlockSpec((d, n), lambda j, i: (0, 0))]
    args += [shift, scale, g, w_bf16]
    if not kv_only:
        in_specs += [pl.BlockSpec((3, OFF_POOL), lambda j, i: (0, 0)), pl.BlockSpec((1, OFF_POOL), lambda j, i: (0, 0))]
        args += list(conv)
    if rope:
        in_specs += [pl.BlockSpec((tm, 128), lambda j, i: (j, 0))] * 3
        args += list(rope_tabs)

    def tok(width, dtype):
        return (jax.ShapeDtypeStruct((b, l, width), dtype),
                pl.BlockSpec((1, tm, width), lambda j, i: (i, j, 0)))

    if kv_only:
        outs = [tok(n, BF16)]
    else:
        outs = [tok(OFF_POOL, F32), tok(W_GROUP, F32), tok(W_GROUP, BF16), tok(W_GROUP, BF16),
                tok(KV_WIDTH, BF16)]
    res = pl.pallas_call(
        functools.partial(_inproj_kernel, rope=rope, kv_only=kv_only),
        out_shape=[o[0] for o in outs],
        grid=(l // tm, b),
        in_specs=in_specs,
        out_specs=[o[1] for o in outs],
        compiler_params=_cparams(("parallel", "parallel")),
        name="inproj_kv" if kv_only else "inproj",
    )(*args)
    return res[0] if kv_only else res


def _rope_tables(s):
    pos = jnp.arange(s)
    p2 = jnp.stack([pos // GRID_W, pos % GRID_W], axis=-1).astype(F32)
    inv = ROPE_BASE ** (-jnp.arange(16, dtype=F32) / 16)
    lane = np.arange(HEAD_DIM)
    ang = p2[:, lane // 32] * inv[lane % 16][None, :]
    first = jnp.asarray((lane % 32) < 16)[None, :]
    cos, sin = jnp.cos(ang), jnp.sin(ang)
    s_up = jnp.where(first, -sin, 0.0)
    s_dn = jnp.where(first, 0.0, sin)
    return tuple(jnp.tile(t, (1, 2)) for t in (cos, s_up, s_dn))


def _softmax_parts(parts, extra=None):
    m = parts[0].max(axis=-1, keepdims=True)
    for s in parts[1:]:
        m = jnp.maximum(m, s.max(axis=-1, keepdims=True))
    if extra is not None:
        m = jnp.maximum(m, extra)
    ps = [jnp.exp(s - m) for s in parts]
    den = ps[0].sum(axis=-1, keepdims=True)
    for p in ps[1:]:
        den = den + p.sum(axis=-1, keepdims=True)
    if extra is not None:
        den = den + jnp.exp(extra - m)
    return ps, den


def _head_stack(t, masks):
    return jnp.concatenate([jnp.where(m, t, jnp.zeros_like(t)) for m in masks], axis=0)


def _wattn_kernel(sink_ref, q_ref, kp_ref, ko_ref, kn_ref, vp_ref, vo_ref, vn_ref, kc_ref, vc_ref, o_ref, *, nb):
    n = pl.program_id(1)
    blk = WA_BLOCK
    q = q_ref[0]
    kall = jnp.concatenate([kp_ref[0], ko_ref[0], kn_ref[0]], axis=0)
    vall = jnp.concatenate([vp_ref[0], vo_ref[0], vn_ref[0]], axis=0)
    kc, vc = kc_ref[0], vc_ref[0]
    rows = N_WA_HEADS * blk
    i = lax.broadcasted_iota(jnp.int32, (rows, 3 * blk), 0) & (blk - 1)
    j = lax.broadcasted_iota(jnp.int32, (rows, 3 * blk), 1)
    band = (j >= i) & (j <= i + 2 * blk)
    kv0 = lax.broadcasted_iota(jnp.int32, (blk, KV_WA), 1) < HEAD_DIM
    head = lax.broadcasted_iota(jnp.int32, (rows, 1), 0) // blk
    snk = jnp.where(head == 0, sink_ref[0],
                    jnp.where(head == 1, sink_ref[1], jnp.where(head == 2, sink_ref[2], sink_ref[3])))
    for sb in range(WA_STEP_BLOCKS):
        gb = n * WA_STEP_BLOCKS + sb
        jlo = jnp.where(gb == 0, blk, 0)
        jhi = jnp.where(gb == nb - 1, 2 * blk, 3 * blk)
        valid = band & (j >= jlo) & (j < jhi)
        keys = slice(sb * blk, (sb + 3) * blk)
        qa = q[sb * blk:(sb + 1) * blk, :KV_WA]
        qb = q[sb * blk:(sb + 1) * blk, KV_WA:]
        zero = jnp.zeros_like(qa)
        qs = jnp.concatenate([jnp.where(kv0, qa, zero), jnp.where(kv0, qb, zero),
                              jnp.where(kv0, zero, qa), jnp.where(kv0, zero, qb)], axis=0)
        s_loc = jnp.where(valid, _nt_dot(qs, kall[keys]), NEG_INF)
        s_ctx = _nt_dot(qs, kc)
        (p_loc, p_ctx), den = _softmax_parts([s_loc, s_ctx], snk)
        o = (jnp.dot(p_loc.astype(BF16), vall[keys], preferred_element_type=F32)
             + jnp.dot(p_ctx.astype(BF16), vc, preferred_element_type=F32)) / den
        o_ref[0, sb * blk:(sb + 1) * blk, :] = jnp.concatenate(
            [jnp.where(kv0, o[:blk], o[2 * blk:3 * blk]), jnp.where(kv0, o[blk:2 * blk], o[3 * blk:])],
            axis=-1).astype(BF16)


def _window_attn(q, kv, ckv, sink):
    b, s, _ = q.shape
    lc = ckv.shape[1]
    nb = s // WA_BLOCK
    sbk = WA_STEP_BLOCKS

    def halo_spec(col, off):
        return pl.BlockSpec((1, WA_BLOCK, KV_WA),
                            lambda i, n: (i, jnp.clip(n * sbk + off, 0, nb - 1), col))

    def own_spec(col):
        return pl.BlockSpec((1, sbk * WA_BLOCK, KV_WA), lambda i, n: (i, n, col))

    return pl.pallas_call(
        functools.partial(_wattn_kernel, nb=nb),
        out_shape=jax.ShapeDtypeStruct((b, s, W_GROUP), BF16),
        grid=(b, nb // sbk),
        in_specs=[pl.BlockSpec(memory_space=pltpu.SMEM),
                  pl.BlockSpec((1, sbk * WA_BLOCK, W_GROUP), lambda i, n: (i, n, 0)),
                  halo_spec(0, -1), own_spec(0), halo_spec(0, sbk),
                  halo_spec(1, -1), own_spec(1), halo_spec(1, sbk),
                  pl.BlockSpec((1, lc, KV_WA), lambda i, n: (i, 0, 0)),
                  pl.BlockSpec((1, lc, KV_WA), lambda i, n: (i, 0, 1))],
        out_specs=pl.BlockSpec((1, sbk * WA_BLOCK, W_GROUP), lambda i, n: (i, n, 0)),
        compiler_params=_cparams(("parallel", "parallel")),
        name="window_attn",
    )(sink, q, kv, kv, kv, kv, kv, kv, ckv, ckv)


def _nattn_kernel(q_ref, k_ref, v_ref, kc_ref, vc_ref, bias_ref, o_ref, *, rows):
    blk = pl.program_id(1)
    nk = NA_ROWS * GRID_W
    kc, vc = kc_ref[0], vc_ref[0]
    lane_head = lax.broadcasted_iota(jnp.int32, (GRID_W, W_GROUP), 1) // HEAD_DIM
    masks = [lane_head == h for h in range(N_NA_HEADS)]
    for rr in range(NA_ROW_BLOCK):
        r = blk * NA_ROW_BLOCK + rr
        r0 = jnp.clip(r - NA_ROWS // 2, 0, rows - NA_ROWS)
        var = r - r0
        start = pl.multiple_of(r0 * GRID_W, GRID_W)
        kt = k_ref[0, pl.ds(start, nk), :]
        vt = v_ref[0, pl.ds(start, nk), :]
        qrow = slice(rr * GRID_W, (rr + 1) * GRID_W)
        qs = _head_stack(q_ref[0, qrow, :], masks)
        bias = bias_ref[:, var].reshape(N_NA_HEADS * GRID_W, nk)
        (p_loc, p_ctx), den = _softmax_parts([_nt_dot(qs, kt) + bias, _nt_dot(qs, kc)])
        o = (jnp.dot(p_loc.astype(BF16), vt, preferred_element_type=F32)
             + jnp.dot(p_ctx.astype(BF16), vc, preferred_element_type=F32)) / den
        out = o[(N_NA_HEADS - 1) * GRID_W:]
        for h in range(N_NA_HEADS - 2, -1, -1):
            out = jnp.where(masks[h], o[h * GRID_W:(h + 1) * GRID_W], out)
        o_ref[0, qrow, :] = out.astype(BF16)


def _na_bias(rpb):
    var = np.arange(NA_ROWS)
    j = np.arange(NA_ROWS)
    qc = np.arange(GRID_W)
    kc = np.arange(GRID_W)
    dr = j[None, :] - var[:, None] + NA_ROWS - 1
    dc = np.clip(kc[None, :] - qc[:, None] + NA_COLS - 1, 0, 2 * NA_COLS - 2)
    ws = np.clip(qc - NA_COLS // 2, 0, GRID_W - NA_COLS)
    ok = (kc[None, :] >= ws[:, None]) & (kc[None, :] < ws[:, None] + NA_COLS)
    onehot = (dc[None] == np.arange(2 * NA_COLS - 1)[:, None, None]).astype(np.float32)
    bias = jnp.einsum("hvjd,dqk->hvqjk", rpb.astype(F32)[:, dr], jnp.asarray(onehot),
                      precision=lax.Precision.HIGHEST)
    bias = jnp.where(jnp.asarray(ok)[None, None, :, None, :], bias, NEG_INF)
    return bias.reshape(rpb.shape[0], NA_ROWS, GRID_W, NA_ROWS * GRID_W)


def _neighbourhood_attn(q, kv, ckv, rpb):
    b, s, _ = q.shape
    lc = ckv.shape[1]
    rows = s // GRID_W
    bias = _na_bias(rpb)
    return pl.pallas_call(
        functools.partial(_nattn_kernel, rows=rows),
        out_shape=jax.ShapeDtypeStruct((b, s, W_GROUP), BF16),
        grid=(b, rows // NA_ROW_BLOCK),
        in_specs=[pl.BlockSpec((1, NA_ROW_BLOCK * GRID_W, W_GROUP), lambda i, r: (i, r, 0)),
                  pl.BlockSpec((1, s, W_GROUP), lambda i, r: (i, 0, 1)),
                  pl.BlockSpec((1, s, W_GROUP), lambda i, r: (i, 0, 2)),
                  pl.BlockSpec((1, lc, W_GROUP), lambda i, r: (i, 0, 1)),
                  pl.BlockSpec((1, lc, W_GROUP), lambda i, r: (i, 0, 2)),
                  pl.BlockSpec(bias.shape, lambda i, r: (0, 0, 0, 0))],
        out_specs=pl.BlockSpec((1, NA_ROW_BLOCK * GRID_W, W_GROUP), lambda i, r: (i, r, 0)),
        compiler_params=_cparams(("parallel", "arbitrary")),
        name="neighbourhood_attn",
    )(q, kv, kv, ckv, ckv, bias)


def _cattn_kernel(*refs, n_kv, with_sink, head_order):
    if with_sink:
        sink_ref, q_ref, k_ref, v_ref, o_ref = refs
    else:
        q_ref, k_ref, v_ref, o_ref = refs
    q, k, v = q_ref[0], k_ref[0], v_ref[0]
    group = N_WA_HEADS // n_kv
    outs = []
    for pos, h in enumerate(head_order):
        sl = slice((h // group) * HEAD_DIM, (h // group + 1) * HEAD_DIM)
        s = _nt_dot(q[:, pos * HEAD_DIM:(pos + 1) * HEAD_DIM], k[:, sl])
        extra = jnp.full((s.shape[0], 1), sink_ref[h], F32) if with_sink else None
        (p,), den = _softmax_parts([s], extra)
        outs.append(jnp.dot(p.astype(BF16), v[:, sl], preferred_element_type=F32) / den)
    o_ref[0] = jnp.concatenate(outs, axis=-1).astype(BF16)


def _ctx_attn(q, ckv, k_col, v_col, n_kv, sink, head_order=(0, 1, 2, 3)):
    b, lc, _ = q.shape
    w = n_kv * HEAD_DIM
    with_sink = sink is not None
    in_specs = [pl.BlockSpec((1, lc, W_GROUP), lambda i: (i, 0, 0)),
                pl.BlockSpec((1, lc, w), lambda i: (i, 0, k_col)),
                pl.BlockSpec((1, lc, w), lambda i: (i, 0, v_col))]
    args = [q, ckv, ckv]
    if with_sink:
        in_specs = [pl.BlockSpec(memory_space=pltpu.SMEM)] + in_specs
        args = [sink] + args
    return pl.pallas_call(
        functools.partial(_cattn_kernel, n_kv=n_kv, with_sink=with_sink, head_order=head_order),
        out_shape=jax.ShapeDtypeStruct((b, lc, W_GROUP), BF16),
        grid=(b,),
        in_specs=in_specs,
        out_specs=pl.BlockSpec((1, lc, W_GROUP), lambda i: (i, 0, 0)),
        compiler_params=_cparams(("parallel",)),
        name="ctx_attn",
    )(*args)


def _pool_kernel(prev_ref, cur_ref, next_ref, w_ref, scale_ref, o_ref, *, seq, tl):
    j = pl.program_id(1)
    nt = seq // tl
    cur = cur_ref[0]
    prev = jnp.where(j == 0, 0.0, prev_ref[0])
    nxt = jnp.where(j == nt - 1, 0.0, next_ref[0])
    e = jnp.concatenate([prev, cur, nxt], axis=0)
    n = tl + 2 * HALO

    def sh(a, d):
        return pltpu.roll(a, d % n, 0)

    s2 = e + sh(e, 1)
    s4 = sh(s2, 1) + sh(s2, -1)
    s8 = sh(s4, 2) + sh(s4, -2)
    s16 = sh(s8, 4) + sh(s8, -4)
    lane = lax.broadcasted_iota(jnp.int32, (tl, W_GROUP), 1)
    t = lax.broadcasted_iota(jnp.int32, (tl, W_GROUP), 0) + j * tl
    g = lane // POOL_GROUP
    lo, hi = HALO, HALO + tl
    ssum = jnp.where(g == 0, s2[lo:hi], jnp.where(g == 1, s4[lo:hi], jnp.where(g == 2, s8[lo:hi], s16[lo:hi])))
    half = jnp.where(g == 0, 1, jnp.where(g == 1, 2, jnp.where(g == 2, 4, 8)))
    cnt = (jnp.minimum(t + half, seq) - jnp.maximum(t - half, 0)).astype(F32)
    d = (ssum / cnt - cur).astype(BF16)
    o_ref[0] = (jnp.dot(d, w_ref[...], preferred_element_type=F32) * scale_ref[...]).astype(BF16)


def _pool_mixer(u, w_bd, scale):
    b, l, c = u.shape
    tl = min(l, 512)
    hb = tl // HALO
    nh = l // HALO
    return pl.pallas_call(
        functools.partial(_pool_kernel, seq=l, tl=tl),
        out_shape=jax.ShapeDtypeStruct((b, l, c), BF16),
        grid=(b, l // tl),
        in_specs=[pl.BlockSpec((1, HALO, c), lambda i, j: (i, jnp.maximum(j * hb - 1, 0), 0)),
                  pl.BlockSpec((1, tl, c), lambda i, j: (i, j, 0)),
                  pl.BlockSpec((1, HALO, c), lambda i, j: (i, jnp.minimum((j + 1) * hb, nh - 1), 0)),
                  pl.BlockSpec((c, c), lambda i, j: (0, 0)),
                  pl.BlockSpec((1, c), lambda i, j: (0, 0))],
        out_specs=pl.BlockSpec((1, tl, c), lambda i, j: (i, j, 0)),
        compiler_params=_cparams(("parallel", "parallel")),
        name="pool_mixer",
    )(u, u, u, w_bd, scale)


def _pool_weight(pool_w):
    z = jnp.zeros((W_GROUP, W_GROUP), F32)
    for g in range(len(POOL_WINDOWS)):
        z = z.at[g * POOL_GROUP:(g + 1) * POOL_GROUP, g * POOL_GROUP:(g + 1) * POOL_GROUP].set(pool_w[g])
    return z.astype(BF16)


def _outproj_kernel(yh_ref, ya_ref, yp_ref, yn_ref, x_ref, gt_ref, w_ref, g2_ref, sh_ref, sc_ref, rw_ref,
                    xo_ref, h_ref, aff_ref):
    y = jnp.concatenate([yh_ref[0], ya_ref[0], yp_ref[0], yn_ref[0]], axis=-1)
    x = x_ref[0] + gt_ref[0] * jnp.dot(y, w_ref[...], preferred_element_type=F32)
    xo_ref[0] = x
    h = _rmsnorm_mod(x, g2_ref[...], sh_ref[0], sc_ref[0])
    h_ref[0] = h
    logits = _nt_dot(rw_ref[...], h.astype(BF16))
    m = logits.max(axis=0, keepdims=True)
    p = jnp.exp(logits - m)
    aff_ref[0] = p / p.sum(axis=0, keepdims=True)


def _outproj(ys, x, gate, w_bf16, g2, shift, scale, rw_t):
    b, l, d = x.shape
    tm = min(l, 512)
    bm = gate.shape[0]
    mod_map = (lambda i, j: (i, 0, 0)) if bm > 1 else (lambda i, j: (0, 0, 0))
    tok = lambda w: pl.BlockSpec((1, tm, w), lambda i, j: (i, j, 0))
    mod = pl.BlockSpec((1, 1, d), mod_map)
    return pl.pallas_call(
        _outproj_kernel,
        out_shape=[jax.ShapeDtypeStruct((b, l, d), F32), jax.ShapeDtypeStruct((b, l, d), F32),
                   jax.ShapeDtypeStruct((b, N_EXPERTS, l), F32)],
        grid=(b, l // tm),
        in_specs=[tok(W_GROUP)] * 4 + [tok(d), mod, pl.BlockSpec((d, d), lambda i, j: (0, 0)),
                                        pl.BlockSpec((1, d), lambda i, j: (0, 0)), mod, mod,
                                        pl.BlockSpec((N_EXPERTS, d), lambda i, j: (0, 0))],
        out_specs=[tok(d), tok(d), pl.BlockSpec((1, N_EXPERTS, tm), lambda i, j: (i, 0, j))],
        compiler_params=_cparams(("parallel", "parallel")),
        name="outproj_router",
    )(*ys, x, gate, w_bf16, g2, shift, scale, rw_t)


def _row_gather_kernel(idx_ref, table_ref, out_ref, sem):
    def issue(r, c):
        pltpu.make_async_copy(table_ref.at[pl.ds(idx_ref[0, 0, r], 1)], out_ref.at[pl.ds(r, 1)], sem).start()
        return c

    lax.fori_loop(0, GATHER_CHUNK, issue, 0, unroll=8)
    pltpu.make_async_copy(table_ref.at[pl.ds(0, GATHER_CHUNK)], out_ref, sem).wait()


def _row_gather(table, rows):
    v, d = table.shape
    n = rows.shape[0]
    nch = n // GATHER_CHUNK
    return pl.pallas_call(
        _row_gather_kernel,
        out_shape=jax.ShapeDtypeStruct((n, d), table.dtype),
        grid=(nch,),
        in_specs=[pl.BlockSpec((1, 1, GATHER_CHUNK), lambda i: (i, 0, 0), memory_space=pltpu.SMEM),
                  pl.BlockSpec(memory_space=pl.ANY)],
        out_specs=pl.BlockSpec((GATHER_CHUNK, d), lambda i: (i, 0)),
        scratch_shapes=[pltpu.SemaphoreType.DMA(())],
        compiler_params=pltpu.CompilerParams(dimension_semantics=("arbitrary",), vmem_limit_bytes=VMEM_LIMIT,
                                             disable_bounds_checks=True),
        name="moe_row_gather",
    )(rows.reshape(nch, 1, GATHER_CHUNK), table)


def _moe_kernel(x_ref, g_ref, wg_ref, wu_ref, wd_ref, o_ref):
    f = pl.program_id(2)

    @pl.when(f == 0)
    def _():
        o_ref[0] = jnp.zeros(o_ref.shape[1:], F32)

    x = x_ref[0].astype(BF16)
    a = jnp.dot(x, wg_ref[0, 0].astype(BF16), preferred_element_type=F32)
    u = jnp.dot(x, wu_ref[0, 0].astype(BF16), preferred_element_type=F32)
    hid = (a * jax.nn.sigmoid(a) * u).astype(BF16)
    o_ref[0] += jnp.dot(hid, wd_ref[0, 0].astype(BF16), preferred_element_type=F32)

    @pl.when(f == pl.num_programs(2) - 1)
    def _():
        o_ref[0] = o_ref[0] * g_ref[0]


def _moe_ffn(xg, gate, w_gate, w_up, w_down, lyr):
    e, m, d = xg.shape
    f = w_gate.shape[3]
    tm = min(m, 1024)
    tf = 512
    return pl.pallas_call(
        _moe_kernel,
        out_shape=jax.ShapeDtypeStruct((e, m, d), F32),
        grid=(e, m // tm, f // tf),
        in_specs=[pl.BlockSpec((1, tm, d), lambda i, j, k: (i, j, 0)),
                  pl.BlockSpec((1, tm, 1), lambda i, j, k: (i, j, 0)),
                  pl.BlockSpec((1, 1, d, tf), lambda i, j, k: (lyr, i, 0, k)),
                  pl.BlockSpec((1, 1, d, tf), lambda i, j, k: (lyr, i, 0, k)),
                  pl.BlockSpec((1, 1, tf, d), lambda i, j, k: (lyr, i, k, 0))],
        out_specs=pl.BlockSpec((1, tm, d), lambda i, j, k: (i, j, 0)),
        compiler_params=_cparams(("parallel", "parallel", "arbitrary")),
        name="moe_ffn",
    )(xg, gate, w_gate, w_up, w_down)


def _expert_choice_ffn(h, aff_t, w_gate, w_up, w_down, lyr):
    b, l, d = h.shape
    cap = EC_CAPACITY * l // N_EXPERTS
    m = b * cap
    g, idx = lax.top_k(aff_t, cap)
    idx, g = lax.sort((idx, g), dimension=2, num_keys=1)
    idx_t = jnp.swapaxes(idx, 0, 1)
    rows = (idx_t + (jnp.arange(b, dtype=jnp.int32) * l)[None, :, None]).reshape(-1)
    xg = _row_gather(h.reshape(b * l, d), rows).reshape(N_EXPERTS, m, d)
    gate = jnp.swapaxes(g, 0, 1).reshape(N_EXPERTS, m, 1)
    o = _moe_ffn(xg, gate, w_gate, w_up, w_down, lyr).reshape(N_EXPERTS * m, d)
    return o, idx


def _combine_kernel(*refs, final, tile, nt, cap, strip, m):
    if final:
        lo_ref, idx_ref, x_ref, gt_ref, g_ref, o_hbm, out_ref, stage, sems = refs
    else:
        lo_ref, idx_ref, x_ref, gt_ref, o_hbm, out_ref, stage, sems = refs
    bi, j = pl.program_id(0), pl.program_id(1)
    step = bi * nt + j
    cur = step % 2
    ne = N_EXPERTS
    gps = strip // SUBLANES
    per_sub = COMBINE_SUB // strip
    win = 2 * LANES
    sentinel = tile * nt

    def runs(jj):
        out = []
        for e in range(ne):
            lo = lo_ref[bi, e * (nt + 1) + jj]
            hi = lo_ref[bi, e * (nt + 1) + jj + 1]
            a8 = (lo // SUBLANES) * SUBLANES
            out.append((a8, jnp.where(hi > lo, hi - a8, 0)))
        return out

    def groups(run, r):
        return [(a8 + strip * r, jnp.clip((n - strip * r + SUBLANES - 1) // SUBLANES, 0, gps)) for a8, n in run]

    def issue(s, grp):
        for e, (s0, ng) in enumerate(grp):
            base = e * m + bi * cap + s0

            def one(k, c, base=base, e=e):
                pltpu.make_async_copy(
                    o_hbm.at[pl.ds(pl.multiple_of(base + k * SUBLANES, SUBLANES), SUBLANES)],
                    stage.at[s, pl.ds(pl.multiple_of(e * strip + k * SUBLANES, SUBLANES), SUBLANES)],
                    sems.at[s]).start()
                return c
            lax.fori_loop(0, ng, one, 0)

    def wait(s, grp):
        cnt = sum(ng for _, ng in grp) * SUBLANES

        @pl.when(cnt > 0)
        def _():
            rows = pl.ds(0, pl.multiple_of(cnt, SUBLANES))
            pltpu.make_async_copy(o_hbm.at[rows], stage.at[s, rows], sems.at[s]).wait()

    def reduce(s, grp):
        t_ids = lax.broadcasted_iota(jnp.int32, (tile, COMBINE_SUB), 0) + j * tile
        acc = jnp.zeros((tile, out_ref.shape[2]), F32)
        for sub in range(ne // per_sub):
            pieces = []
            for e in range(sub * per_sub, (sub + 1) * per_sub):
                s0, ng = grp[e]
                a = jnp.minimum((s0 // LANES) * LANES, idx_ref.shape[2] - win)
                w = idx_ref[0, e:e + 1, pl.ds(pl.multiple_of(a, LANES), win)]
                w = pltpu.roll(w, (win - (s0 - a)) % win, 1)
                pieces.append(jnp.where(ng > 0, w[:, :strip], sentinel))
            toks = jnp.concatenate(pieces, axis=1)
            rows = stage[s, sub * COMBINE_SUB:(sub + 1) * COMBINE_SUB, :]
            onehot = jnp.where(toks == t_ids, 1.0, 0.0).astype(BF16)
            hi = rows.astype(BF16)
            lo = (rows - hi.astype(F32)).astype(BF16)
            acc = (acc + jnp.dot(onehot, hi, preferred_element_type=F32)
                   + jnp.dot(onehot, lo, preferred_element_type=F32))
        return acc

    @pl.when(step == 0)
    def _():
        stage[...] = jnp.zeros(stage.shape, F32)

    run = runs(j)
    first = groups(run, 0)

    @pl.when(j == 0)
    def _():
        issue(cur, first)

    @pl.when(j + 1 < nt)
    def _():
        issue(1 - cur, groups(runs(jnp.minimum(j + 1, nt - 1)), 0))

    wait(cur, first)
    acc = reduce(cur, first)
    longest = run[0][1]
    for _, n in run[1:]:
        longest = jnp.maximum(longest, n)

    def extra_round(r, acc):
        grp = groups(run, r)
        issue(cur, grp)
        wait(cur, grp)
        return acc + reduce(cur, grp)

    acc = lax.fori_loop(1, (longest + strip - 1) // strip, extra_round, acc)
    x = x_ref[0] + gt_ref[0] * acc
    if final:
        x = x * lax.rsqrt(jnp.mean(x * x, axis=-1, keepdims=True) + NORM_EPS) * g_ref[...]
    out_ref[0] = x


def _combine(x, o, idx, gate, final_g=None):
    b, l, d = x.shape
    ne, cap = idx.shape[1], idx.shape[2]
    tile = min(l, COMBINE_TILE)
    nt = l // tile
    strip = min(COMBINE_STRIP, cap)
    bounds = jnp.arange(nt + 1, dtype=jnp.int32) * tile
    lo = jnp.sum(idx[:, :, None, :] < bounds[None, None, :, None], axis=-1, dtype=jnp.int32)
    cp = -(-cap // LANES) * LANES + LANES
    idx_p = jnp.pad(idx, ((0, 0), (0, 0), (0, cp - cap)), constant_values=l)
    bm = gate.shape[0]
    mod_map = (lambda i, j: (i, 0, 0)) if bm > 1 else (lambda i, j: (0, 0, 0))
    tok = pl.BlockSpec((1, tile, d), lambda i, j: (i, j, 0))
    final = final_g is not None
    in_specs = [pl.BlockSpec(memory_space=pltpu.SMEM),
                pl.BlockSpec((1, ne, cp), lambda i, j: (i, 0, 0)),
                tok, pl.BlockSpec((1, 1, d), mod_map)]
    args = [lo.reshape(b, ne * (nt + 1)), idx_p, x, gate]
    if final:
        in_specs.append(pl.BlockSpec((1, d), lambda i, j: (0, 0)))
        args.append(final_g)
    in_specs.append(pl.BlockSpec(memory_space=pl.ANY))
    args.append(o)
    return pl.pallas_call(
        functools.partial(_combine_kernel, final=final, tile=tile, nt=nt, cap=cap, strip=strip, m=b * cap),
        out_shape=jax.ShapeDtypeStruct((b, l, d), F32),
        grid=(b, nt),
        in_specs=in_specs,
        out_specs=tok,
        scratch_shapes=[pltpu.VMEM((2, ne * strip, d), F32), pltpu.SemaphoreType.DMA((2,))],
        compiler_params=pltpu.CompilerParams(dimension_semantics=("arbitrary", "arbitrary"),
                                             vmem_limit_bytes=VMEM_LIMIT, disable_bounds_checks=True),
        name="moe_combine_final" if final else "moe_combine",
    )(*args)


def _hyena_filters(l, w1, b1, w2, b2, w3, freq):
    hp = lax.Precision.HIGHEST
    bands = jnp.linspace(1e-4, HY_BANDS - 1, HY_BANDS, dtype=F32)
    deltas = jnp.linspace(HY_DECAY_MIN, HY_DECAY_MAX, W_GROUP, dtype=F32)

    def at(pos):
        t = pos / max(l - 1, 1)
        ang = (2 * math.pi / l) * pos[:, None] * bands[None, :]
        feats = jnp.concatenate([t[:, None], jnp.cos(ang), -jnp.sin(ang)], axis=-1)
        h = jnp.sin(freq * (jnp.dot(feats, w1, precision=hp) + b1))
        h = jnp.sin(freq * (jnp.dot(h, w2, precision=hp) + b2))
        h = jnp.dot(h, w3, precision=hp).reshape(l, HY_ORDER, 2, W_GROUP)
        return h * jnp.exp(-t[:, None] * deltas[None, :])[:, None, None, :]

    lag = jnp.arange(l)
    fwd = at(lag.astype(F32))
    rev = at(((l - lag) % l).astype(F32))
    first = (lag == 0)[:, None, None]
    half0 = fwd[:, :, 0] + jnp.where(first, fwd[:, :, 1], 0.0)
    half1 = jnp.where(first, 0.0, rev[:, :, 1])
    return jnp.transpose(jnp.stack([half0, half1], axis=0), (2, 0, 1, 3))


def _cplx_block(e):
    return np.block([[e.real, -e.imag], [e.imag, e.real]])


@functools.lru_cache(maxsize=None)
def _dft_consts(l):
    n = 2 * l
    n1f = n // DFT_N2
    n1h = n1f // 2
    k1, n1, n2 = np.arange(n1f), np.arange(n1h), np.arange(DFT_N2)
    ph = np.outer(k1, n1)[None] / n1f + (n2[:, None, None] * k1[None, :, None]) / n
    e1 = np.exp(-2j * np.pi * ph)
    w1 = np.stack([_cplx_block(e1[i]) for i in range(DFT_N2)])
    e2 = np.exp(-2j * np.pi * np.outer(n2, n2) / DFT_N2)
    f2 = _cplx_block(e2)
    g2 = _cplx_block(np.conj(e2).T)
    g1 = np.stack([_cplx_block(np.conj(e1[i]).T / n) for i in range(DFT_N2)])
    return tuple(jnp.asarray(a, F32).astype(BF16) for a in (w1, f2, g2, g1))


def _lconv_kernel(a_ref, hf_ref, w1_ref, f2_ref, g2_ref, g1_ref, y_ref, xy_ref, s_ref, *, n1h):
    n1f = 2 * n1h

    def copy_in(i, c):
        for ri in range(2):
            xy_ref[ri, pl.ds(pl.multiple_of(i * X_PITCH, 8), DFT_N2), :] = \
                a_ref[0, ri, pl.ds(pl.multiple_of(i * DFT_N2, DFT_N2), DFT_N2), :]
        return c

    lax.fori_loop(0, n1h, copy_in, 0, unroll=4)

    def stage1(n2, c):
        slab = jnp.concatenate([xy_ref[0, pl.ds(n2, n1h, stride=X_PITCH), :],
                                xy_ref[1, pl.ds(n2, n1h, stride=X_PITCH), :]], axis=0).astype(BF16)
        a = jnp.dot(w1_ref[n2], slab, preferred_element_type=F32)
        base = pl.multiple_of(n2 * S_PITCH, 8)
        s_ref[0, pl.ds(base, n1f), :] = a[:n1f]
        s_ref[1, pl.ds(base, n1f), :] = a[n1f:]
        return c

    lax.fori_loop(0, DFT_N2, stage1, 0, unroll=32)

    def stage2(k1, c):
        slab = jnp.concatenate([s_ref[0, pl.ds(k1, DFT_N2, stride=S_PITCH), :],
                                s_ref[1, pl.ds(k1, DFT_N2, stride=S_PITCH), :]], axis=0).astype(BF16)
        x = jnp.dot(f2_ref[...], slab, preferred_element_type=F32)
        xr, xi = x[:DFT_N2], x[DFT_N2:]
        hr, hi = hf_ref[0, k1, 0], hf_ref[0, k1, 1]
        y = jnp.concatenate([xr * hr - xi * hi, xr * hi + xi * hr], axis=0).astype(BF16)
        cc = jnp.dot(g2_ref[...], y, preferred_element_type=F32)
        s_ref[0, pl.ds(k1, DFT_N2, stride=S_PITCH), :] = cc[:DFT_N2]
        s_ref[1, pl.ds(k1, DFT_N2, stride=S_PITCH), :] = cc[DFT_N2:]
        return c

    lax.fori_loop(0, n1f, stage2, 0, unroll=16)

    def stage3(n2, c):
        base = pl.multiple_of(n2 * S_PITCH, 8)
        d = jnp.concatenate([s_ref[0, pl.ds(base, n1f), :], s_ref[1, pl.ds(base, n1f), :]], axis=0).astype(BF16)
        yv = jnp.dot(g1_ref[n2], d, preferred_element_type=F32)
        yb = pl.multiple_of(n2 * Y_PITCH, 8)
        xy_ref[0, pl.ds(yb, n1h), :] = yv[:n1h]
        xy_ref[1, pl.ds(yb, n1h), :] = yv[n1h:]
        return c

    lax.fori_loop(0, DFT_N2, stage3, 0, unroll=32)

    def copy_out(i, c):
        for ri in range(2):
            y_ref[0, ri, pl.ds(pl.multiple_of(i * DFT_N2, DFT_N2), DFT_N2), :] = \
                xy_ref[ri, pl.ds(i, DFT_N2, stride=Y_PITCH), :]
        return c

    lax.fori_loop(0, n1h, copy_out, 0, unroll=4)


def _lconv_small_kernel(a_ref, hf_ref, f_ref, g_ref, y_ref):
    l = a_ref.shape[2]
    slab = jnp.concatenate([a_ref[0, 0], a_ref[0, 1]], axis=0).astype(BF16)
    x = jnp.dot(f_ref[...], slab, preferred_element_type=F32)
    xr, xi = x[:2 * l], x[2 * l:]
    hr, hi = hf_ref[0, 0], hf_ref[0, 1]
    y = jnp.concatenate([xr * hr - xi * hi, xr * hi + xi * hr], axis=0).astype(BF16)
    out = jnp.dot(g_ref[...], y, preferred_element_type=F32)
    y_ref[0, 0] = out[:l]
    y_ref[0, 1] = out[l:]


@functools.lru_cache(maxsize=None)
def _dft_consts_small(l):
    n = 2 * l
    e = np.exp(-2j * np.pi * np.outer(np.arange(n), np.arange(l)) / n)
    f = _cplx_block(e)
    g = _cplx_block(np.conj(e).T / n)
    return jnp.asarray(f, F32).astype(BF16), jnp.asarray(g, F32).astype(BF16)


def _long_conv(a, hf, order):
    b, l, ca = a.shape
    c = W_GROUP
    a4 = a.reshape(b // 2, 2, l, ca)
    io_spec = pl.BlockSpec((1, 2, l, LANES), lambda j, p: (p, 0, 0, j))
    once = dict(pipeline_mode=pl.Buffered(1))
    const = lambda arr: pl.BlockSpec(arr.shape, lambda j, p: (0,) * arr.ndim, **once)
    if l <= DFT_N2 * 2:
        f, g = _dft_consts_small(l)
        y = pl.pallas_call(
            _lconv_small_kernel,
            out_shape=jax.ShapeDtypeStruct((b // 2, 2, l, c), F32),
            grid=(c // LANES, b // 2),
            in_specs=[io_spec, pl.BlockSpec((1, 2, 2 * l, LANES), lambda j, p: (order, 0, 0, j)), const(f), const(g)],
            out_specs=io_spec,
            compiler_params=_cparams(("parallel", "arbitrary")),
            name="long_conv_small",
        )(a4, hf, f, g)
        return y.reshape(b, l, c)
    n1f = 2 * l // DFT_N2
    n1h = n1f // 2
    assert (n1f + 8, n1h + 8) == (S_PITCH, Y_PITCH), "scratch pitches are sized for this sequence length"
    w1, f2, g2, g1 = _dft_consts(l)
    rows_xy = max(n1h * X_PITCH, DFT_N2 * Y_PITCH)
    y = pl.pallas_call(
        functools.partial(_lconv_kernel, n1h=n1h),
        out_shape=jax.ShapeDtypeStruct((b // 2, 2, l, c), F32),
        grid=(c // LANES, b // 2),
        in_specs=[io_spec, pl.BlockSpec((1, n1f, 2, DFT_N2, LANES), lambda j, p: (order, 0, 0, 0, j), **once),
                  const(w1), const(f2), const(g2), const(g1)],
        out_specs=io_spec,
        scratch_shapes=[pltpu.VMEM((2, rows_xy, LANES), F32), pltpu.VMEM((2, DFT_N2 * S_PITCH, LANES), F32)],
        compiler_params=_cparams(("parallel", "arbitrary"), 52 * 1024 * 1024),
        name="long_conv",
    )(a4, hf, w1, f2, g2, g1)
    return y.reshape(b, l, c)


@functools.lru_cache(maxsize=None)
def _fspec_consts(n):
    n1f = n // DFT_N2
    k1, n1, n2 = np.arange(n1f), np.arange(n1f), np.arange(DFT_N2)
    ph = np.outer(k1, n1)[None] / n1f + (n2[:, None, None] * k1[None, :, None]) / n
    e1 = np.exp(-2j * np.pi * ph)
    w1 = np.concatenate([e1.real, e1.imag], axis=1)
    f2 = _cplx_block(np.exp(-2j * np.pi * np.outer(n2, n2) / DFT_N2))
    return jnp.asarray(w1, F32).astype(BF16), jnp.asarray(f2, F32).astype(BF16)


def _fspec_kernel(f_ref, w1_ref, f2_ref, hf_ref, x_ref, s_ref, *, n1f):
    n1h = n1f // 2

    def copy_in(i, c):
        for half in range(2):
            x_ref[pl.ds(pl.multiple_of((half * n1h + i) * X_PITCH, 8), DFT_N2), :] = \
                f_ref[0, half, pl.ds(pl.multiple_of(i * DFT_N2, DFT_N2), DFT_N2), :]
        return c

    lax.fori_loop(0, n1h, copy_in, 0, unroll=4)

    def stage1(n2, c):
        slab = x_ref[pl.ds(n2, n1f, stride=X_PITCH), :].astype(BF16)
        a = jnp.dot(w1_ref[n2], slab, preferred_element_type=F32)
        base = pl.multiple_of(n2 * S_PITCH, 8)
        s_ref[0, pl.ds(base, n1f), :] = a[:n1f]
        s_ref[1, pl.ds(base, n1f), :] = a[n1f:]
        return c

    lax.fori_loop(0, DFT_N2, stage1, 0, unroll=32)

    def stage2(k1, c):
        slab = jnp.concatenate([s_ref[0, pl.ds(k1, DFT_N2, stride=S_PITCH), :],
                                s_ref[1, pl.ds(k1, DFT_N2, stride=S_PITCH), :]], axis=0).astype(BF16)
        x = jnp.dot(f2_ref[...], slab, preferred_element_type=F32)
        hf_ref[0, k1, 0] = x[:DFT_N2]
        hf_ref[0, k1, 1] = x[DFT_N2:]
        return c

    lax.fori_loop(0, n1f, stage2, 0, unroll=16)


def _fspec_small_kernel(f_ref, w_ref, hf_ref):
    n = 2 * f_ref.shape[2]
    filt = jnp.concatenate([f_ref[0, 0], f_ref[0, 1]], axis=0).astype(BF16)
    x = jnp.dot(w_ref[...], filt, preferred_element_type=F32)
    hf_ref[0, 0] = x[:n]
    hf_ref[0, 1] = x[n:]


@functools.lru_cache(maxsize=None)
def _fspec_consts_small(n):
    e = np.exp(-2j * np.pi * np.outer(np.arange(n), np.arange(n)) / n)
    return jnp.asarray(np.concatenate([e.real, e.imag], axis=0), F32).astype(BF16)


def _filter_spectra(filt):
    orders, _, l, c = filt.shape
    n = 2 * l
    f2s = filt
    once = dict(pipeline_mode=pl.Buffered(1))
    const = lambda arr: pl.BlockSpec(arr.shape, lambda o, j: (0,) * arr.ndim, **once)
    in_spec = pl.BlockSpec((1, 2, l, LANES), lambda o, j: (o, 0, 0, j))
    if l <= DFT_N2 * 2:
        w = _fspec_consts_small(n)
        return pl.pallas_call(
            _fspec_small_kernel,
            out_shape=jax.ShapeDtypeStruct((orders, 2, n, c), F32),
            grid=(orders, c // LANES),
            in_specs=[in_spec, const(w)],
            out_specs=pl.BlockSpec((1, 2, n, LANES), lambda o, j: (o, 0, 0, j)),
            compiler_params=_cparams(("parallel", "parallel")),
            name="filter_spectrum_small",
        )(f2s, w)
    n1f = n // DFT_N2
    assert n1f + 8 == S_PITCH, "scratch pitches are sized for this sequence length"
    w1, f2 = _fspec_consts(n)
    return pl.pallas_call(
        functools.partial(_fspec_kernel, n1f=n1f),
        out_shape=jax.ShapeDtypeStruct((orders, n1f, 2, DFT_N2, c), F32),
        grid=(orders, c // LANES),
        in_specs=[in_spec, const(w1), const(f2)],
        out_specs=pl.BlockSpec((1, n1f, 2, DFT_N2, LANES), lambda o, j: (o, 0, 0, 0, j)),
        scratch_shapes=[pltpu.VMEM((n1f * X_PITCH, LANES), F32), pltpu.VMEM((2, DFT_N2 * S_PITCH, LANES), F32)],
        compiler_params=_cparams(("parallel", "parallel"), 52 * 1024 * 1024),
        name="filter_spectrum",
    )(f2s, w1, f2)


def _hy_gate_kernel(y_ref, a_ref, m_ref, sk_ref, o_ref):
    o_ref[0] = (m_ref[0] * (y_ref[0] + a_ref[0] * sk_ref[...])).astype(o_ref.dtype)


def _hy_gate(y, a, a_col, m, m_col, sk, out_dtype):
    b, l, c = y.shape
    tl = min(l, 1024)
    spec = lambda col: pl.BlockSpec((1, tl, c), lambda i, j: (i, j, col))
    return pl.pallas_call(
        _hy_gate_kernel,
        out_shape=jax.ShapeDtypeStruct((b, l, c), out_dtype),
        grid=(b, l // tl),
        in_specs=[spec(0), spec(a_col), spec(m_col), pl.BlockSpec((1, c), lambda i, j: (0, 0))],
        out_specs=spec(0),
        compiler_params=_cparams(("parallel", "parallel")),
        name="hyena_gate",
    )(y, a, m, sk)


def _hyena(uc, filt, skip):
    hf = _filter_spectra(filt)
    y1 = _long_conv(uc, hf, 0)
    z1 = _hy_gate(y1, uc, 0, uc, 1, skip[0:1], F32)
    y2 = _long_conv(z1, hf, 1)
    return _hy_gate(y2, z1, 0, uc, 2, skip[1:2], BF16)


def _permute_wa_heads(w, start, axis):
    idx = np.arange(w.shape[axis])
    blocks = [np.arange(start + h * HEAD_DIM, start + (h + 1) * HEAD_DIM) for h in WA_HEAD_ORDER]
    idx[start:start + N_WA_HEADS * HEAD_DIM] = np.concatenate(blocks)
    return jnp.take(w, idx, axis=axis)


def _layer(x, ctx, mod, lyr, rope_tabs, update_ctx, final_g):
    b, s, d = x.shape
    lc = ctx.shape[1]
    mx = [mod[:b, None, i * d:(i + 1) * d] for i in range(6)]
    mc = [mod[b:b + 1, None, i * d:(i + 1) * d] for i in range(6)]
    g1, g2 = lyr["g1"][None, :], lyr["g2"][None, :]
    w_in = _permute_wa_heads(lyr["w_in"], OFF_WA_Q, axis=1).astype(BF16)
    w_out = _permute_wa_heads(lyr["w_out"], W_GROUP, axis=0).astype(BF16)
    rw_t = lyr["router_w"].T.astype(BF16)
    w_bd = _pool_weight(lyr["pool_w"])
    pscale = lyr["pool_scale"][None, :]
    hy_args = (lyr["hy_w1"], lyr["hy_b1"], lyr["hy_w2"], lyr["hy_b2"], lyr["hy_w3"], lyr["hy_freq"])

    conv = (lyr["hy_conv_w"], lyr["hy_conv_b"][None, :])
    u_hy, u_pool, q_wa, q_na, kv = _inproj(x, mx[0], mx[1], g1, w_in, conv, rope_tabs)
    if update_ctx:
        cu_hy, cu_pool, cq_wa, cq_na, ckv = _inproj(ctx, mc[0], mc[1], g1, w_in, conv)
    else:
        ckv = _inproj(ctx, mc[0], mc[1], g1, w_in[:, OFF_KV:], kv_only=True)

    ys = [_hyena(u_hy, _hyena_filters(s, *hy_args), lyr["hy_skip"]),
          _window_attn(q_wa, kv, ckv, lyr["wa_sink"]),
          _pool_mixer(u_pool, w_bd, pscale),
          _neighbourhood_attn(q_na, kv, ckv, lyr["na_rpb"])]
    x, h, aff = _outproj(ys, x, mx[2], w_out, g2, mx[3], mx[4], rw_t)
    moe = _expert_choice_ffn(h, aff, *lyr["experts"], lyr["index"])
    x = _combine(x, *moe, mx[5], final_g)

    if update_ctx:
        ycs = [_hyena(cu_hy, _hyena_filters(lc, *hy_args), lyr["hy_skip"]),
               _ctx_attn(cq_wa, ckv, 0, 1, N_WA_KV, lyr["wa_sink"], WA_HEAD_ORDER),
               _pool_mixer(cu_pool, w_bd, pscale),
               _ctx_attn(cq_na, ckv, 1, 2, N_NA_HEADS, None)]
        ctx, hc, affc = _outproj(ycs, ctx, mc[2], w_out, g2, mc[3], mc[4], rw_t)
        moe_c = _expert_choice_ffn(hc, affc, *lyr["experts"], lyr["index"])
        ctx = _combine(ctx, *moe_c, mc[5])
    return x, ctx


def kernel(x, c, ctx, c_ctx, ada_w, ada_b, norm1_g, norm2_g, w_in, hy_conv_w, hy_conv_b, hy_w1, hy_b1, hy_w2,
           hy_b2, hy_w3, hy_freq, hy_skip, wa_sink, pool_w, pool_scale, na_rpb, w_out, router_w, exp_w_gate,
           exp_w_up, exp_w_down, final_norm_g):
    b, s, d = x.shape
    cs = jnp.zeros((MOD_ROWS, d), F32).at[:b].set(c).at[b].set(c_ctx)
    mods = _ada_mod(cs, ada_w, ada_b[:, None, :])
    rope_tabs = _rope_tables(s)
    params = dict(g1=norm1_g, g2=norm2_g, w_in=w_in, hy_conv_w=hy_conv_w, hy_conv_b=hy_conv_b, hy_w1=hy_w1,
                  hy_b1=hy_b1, hy_w2=hy_w2, hy_b2=hy_b2, hy_w3=hy_w3, hy_freq=hy_freq, hy_skip=hy_skip,
                  wa_sink=wa_sink, pool_w=pool_w, pool_scale=pool_scale, na_rpb=na_rpb, w_out=w_out,
                  router_w=router_w)
    for l in range(DEPTH):
        lyr = {k: v[l] for k, v in params.items()}
        lyr.update(index=l, experts=(exp_w_gate, exp_w_up, exp_w_down))
        last = l == DEPTH - 1
        x, ctx = _layer(x, ctx, mods[l], lyr, rope_tabs, update_ctx=not last,
                        final_g=final_norm_g[None, :] if last else None)
    return x
```

```python
import functools
import math

import jax
import jax.numpy as jnp
import numpy as np
from jax import lax
from jax.experimental import pallas as pl
from jax.experimental.pallas import tpu as pltpu

F32 = jnp.float32
BF16 = jnp.bfloat16

D_MODEL = 1024
DEPTH = 2
GRID_W = 64
HEAD_DIM = 64
W_GROUP = 256
N_WA_HEADS = 4
N_WA_KV = 2
N_NA_HEADS = 4
KV_WA = 128
OFF_POOL = 768
OFF_WA_Q = 1024
OFF_NA_Q = 1280
OFF_KV = 1536
IN_WIDTH = 2304
KV_WIDTH = IN_WIDTH - OFF_KV
HY_ORDER = 2
HY_BANDS = 16
HY_DECAY_MIN = abs(math.log(1e-2) / 1.5)
HY_DECAY_MAX = abs(math.log(1e-2) / 0.3)
WA_BLOCK = 128
WA_STEP_BLOCKS = 4
WA_HEAD_ORDER = (0, 2, 1, 3)
POOL_WINDOWS = (2, 4, 8, 16)
POOL_GROUP = 64
HALO = 8
LANES = 128
SUBLANES = 8
DFT_N2 = 128
X_PITCH = DFT_N2 + 8
S_PITCH = 64 + 8
Y_PITCH = 32 + 8
NA_ROWS = 8
NA_COLS = 16
NA_ROW_BLOCK = 8
ROPE_BASE = 10000.0
N_EXPERTS = 16
EXPERT_HIDDEN = 2048
EC_CAPACITY = 2
NORM_EPS = 1e-6
NEG_INF = -1e30
Q_SCALE = HEAD_DIM ** -0.5

FFN_ROWS = 2048
FFN_HIDDEN_TILE = 256
GATHER_CHUNK = 2048
COMBINE_TILE = 256
COMBINE_SUB = 256
COMBINE_STRIP = 64
MOD_ROWS = 16
VMEM_LIMIT = 48 * 1024 * 1024
DFT_VMEM_LIMIT = 52 * 1024 * 1024
LANE_BITS = 7
F32_VALUE_BITS = 31


def _cparams(sem, vmem=VMEM_LIMIT):
    return pltpu.CompilerParams(dimension_semantics=sem, vmem_limit_bytes=vmem)


def _nt_dot(a, b):
    return lax.dot_general(a, b, (((1,), (1,)), ((), ())), preferred_element_type=F32)


def _rmsnorm_mod(x, g, shift, scale):
    y = x * lax.rsqrt(jnp.mean(x * x, axis=-1, keepdims=True) + NORM_EPS) * g
    return y * (1.0 + scale) + shift


def _ada_kernel(c_ref, w_ref, b_ref, o_ref):
    c = c_ref[...]
    s = (c * jax.nn.sigmoid(c)).astype(BF16)
    o_ref[0] = jnp.dot(s, w_ref[0].astype(BF16), preferred_element_type=F32) + b_ref[0]


def _ada_mod(cs, ada_w, ada_b):
    nl, d, n = ada_w.shape
    tn = 1024
    return pl.pallas_call(
        _ada_kernel,
        out_shape=jax.ShapeDtypeStruct((nl, MOD_ROWS, n), F32),
        grid=(nl, n // tn),
        in_specs=[pl.BlockSpec((MOD_ROWS, d), lambda l, j: (0, 0)),
                  pl.BlockSpec((1, d, tn), lambda l, j: (l, 0, j)),
                  pl.BlockSpec((1, 1, tn), lambda l, j: (l, 0, j))],
        out_specs=pl.BlockSpec((1, MOD_ROWS, tn), lambda l, j: (l, 0, j)),
        compiler_params=_cparams(("parallel", "parallel")),
        name="ada_mod",
    )(cs, ada_w, ada_b)


def _rope(t, cos, s_up, s_dn):
    w = t.shape[-1]
    if w > 128:
        cos, s_up, s_dn = (jnp.concatenate([a] * (w // 128), axis=-1) for a in (cos, s_up, s_dn))
    up = pltpu.roll(t, w - 16, 1)
    dn = pltpu.roll(t, 16, 1)
    return t * cos + up * s_up + dn * s_dn


def _inproj_kernel(*refs, rope, kv_only):
    if kv_only:
        x_ref, sh_ref, sc_ref, g_ref, w_ref, o_ref = refs
    elif rope:
        x_ref, xp_ref, xn_ref, sh_ref, sc_ref, g_ref, w_ref, cw_ref, cb_ref, cos_ref, sup_ref, sdn_ref, *outs = refs
    else:
        x_ref, xp_ref, xn_ref, sh_ref, sc_ref, g_ref, w_ref, cw_ref, cb_ref, *outs = refs
    h = _rmsnorm_mod(x_ref[0], g_ref[...], sh_ref[0], sc_ref[0])
    p = jnp.dot(h.astype(BF16), w_ref[...], preferred_element_type=F32)
    if kv_only:
        o_ref[0] = p.astype(BF16)
        return
    hy_ref, pool_ref, qwa_ref, qna_ref, kv_ref = outs
    j, nt, tm = pl.program_id(0), pl.num_programs(0), x_ref.shape[1]
    halo = _rmsnorm_mod(jnp.concatenate([xp_ref[0], xn_ref[0]], axis=0), g_ref[...], sh_ref[0], sc_ref[0])
    ph = jnp.dot(halo.astype(BF16), w_ref[:, :OFF_POOL], preferred_element_type=F32)
    e = jnp.concatenate([jnp.where(j == 0, 0.0, ph[:HALO]), p[:, :OFF_POOL],
                         jnp.where(j == nt - 1, 0.0, ph[HALO:])], axis=0)
    n = tm + 2 * HALO
    uc = pltpu.roll(e, 1, 0) * cw_ref[0:1, :] + e * cw_ref[1:2, :] + pltpu.roll(e, n - 1, 0) * cw_ref[2:3, :]
    hy_ref[0] = uc[HALO:HALO + tm] + cb_ref[...]
    pool_ref[0] = p[:, OFF_POOL:OFF_WA_Q]
    qwa = p[:, OFF_WA_Q:OFF_NA_Q]
    kwa = p[:, OFF_KV:OFF_KV + KV_WA]
    if rope:
        tabs = (cos_ref[...], sup_ref[...], sdn_ref[...])
        qwa = _rope(qwa, *tabs)
        kwa = _rope(kwa, *tabs)
    qwa_ref[0] = (qwa * Q_SCALE).astype(BF16)
    qna_ref[0] = (p[:, OFF_NA_Q:OFF_KV] * Q_SCALE).astype(BF16)
    kv_ref[0] = jnp.concatenate([kwa, p[:, OFF_KV + KV_WA:]], axis=-1).astype(BF16)


def _inproj(x, shift, scale, g, w_bf16, conv=None, rope_tabs=None, kv_only=False):
    b, l, d = x.shape
    n = w_bf16.shape[1]
    tm = min(l, 512)
    hb, nh = tm // HALO, l // HALO
    rope = rope_tabs is not None
    bm = shift.shape[0]
    mod_map = (lambda j, i: (i, 0, 0)) if bm > 1 else (lambda j, i: (0, 0, 0))
    in_specs = [pl.BlockSpec((1, tm, d), lambda j, i: (i, j, 0))]
    args = [x]
    if not kv_only:
        in_specs += [pl.BlockSpec((1, HALO, d), lambda j, i: (i, jnp.maximum(j * hb - 1, 0), 0)),
                     pl.BlockSpec((1, HALO, d), lambda j, i: (i, jnp.minimum((j + 1) * hb, nh - 1), 0))]
        args += [x, x]
    in_specs += [pl.BlockSpec((1, 1, d), mod_map),
                 pl.BlockSpec((1, 1, d), mod_map),
                 pl.BlockSpec((1, d), lambda j, i: (0, 0)),
                 pl.BlockSpec((d, n), lambda j, i: (0, 0))]
    args += [shift, scale, g, w_bf16]
    if not kv_only:
        in_specs += [pl.BlockSpec((3, OFF_POOL), lambda j, i: (0, 0)), pl.BlockSpec((1, OFF_POOL), lambda j, i: (0, 0))]
        args += list(conv)
    if rope:
        in_specs += [pl.BlockSpec((tm, 128), lambda j, i: (j, 0))] * 3
        args += list(rope_tabs)

    def tok(width, dtype):
        return (jax.ShapeDtypeStruct((b, l, width), dtype),
                pl.BlockSpec((1, tm, width), lambda j, i: (i, j, 0)))

    if kv_only:
        outs = [tok(n, BF16)]
    else:
        outs = [tok(OFF_POOL, F32), tok(W_GROUP, F32), tok(W_GROUP, BF16), tok(W_GROUP, BF16),
                tok(KV_WIDTH, BF16)]
    res = pl.pallas_call(
        functools.partial(_inproj_kernel, rope=rope, kv_only=kv_only),
        out_shape=[o[0] for o in outs],
        grid=(l // tm, b),
        in_specs=in_specs,
        out_specs=[o[1] for o in outs],
        compiler_params=_cparams(("parallel", "parallel")),
        name="inproj_kv" if kv_only else "inproj",
    )(*args)
    return res[0] if kv_only else res


def _rope_tables(s):
    pos = jnp.arange(s)
    p2 = jnp.stack([pos // GRID_W, pos % GRID_W], axis=-1).astype(F32)
    inv = ROPE_BASE ** (-jnp.arange(16, dtype=F32) / 16)
    lane = np.arange(HEAD_DIM)
    ang = p2[:, lane // 32] * inv[lane % 16][None, :]
    first = jnp.asarray((lane % 32) < 16)[None, :]
    cos, sin = jnp.cos(ang), jnp.sin(ang)
    s_up = jnp.where(first, -sin, 0.0)
    s_dn = jnp.where(first, 0.0, sin)
    return tuple(jnp.tile(t, (1, 2)) for t in (cos, s_up, s_dn))


def _softmax_parts(parts, extra=None):
    m = parts[0].max(axis=-1, keepdims=True)
    for s in parts[1:]:
        m = jnp.maximum(m, s.max(axis=-1, keepdims=True))
    if extra is not None:
        m = jnp.maximum(m, extra)
    ps = [jnp.exp(s - m) for s in parts]
    den = ps[0].sum(axis=-1, keepdims=True)
    for p in ps[1:]:
        den = den + p.sum(axis=-1, keepdims=True)
    if extra is not None:
        den = den + jnp.exp(extra - m)
    return ps, den


def _head_stack(t, masks):
    return jnp.concatenate([jnp.where(m, t, jnp.zeros_like(t)) for m in masks], axis=0)


def _wattn_kernel(sink_ref, q_ref, kp_ref, ko_ref, kn_ref, vp_ref, vo_ref, vn_ref, kc_ref, vc_ref, o_ref, *, nb):
    n = pl.program_id(1)
    blk = WA_BLOCK
    q = q_ref[0]
    kall = jnp.concatenate([kp_ref[0], ko_ref[0], kn_ref[0]], axis=0)
    vall = jnp.concatenate([vp_ref[0], vo_ref[0], vn_ref[0]], axis=0)
    kc, vc = kc_ref[0], vc_ref[0]
    rows = N_WA_HEADS * blk
    i = lax.broadcasted_iota(jnp.int32, (rows, 3 * blk), 0) & (blk - 1)
    j = lax.broadcasted_iota(jnp.int32, (rows, 3 * blk), 1)
    band = (j >= i) & (j <= i + 2 * blk)
    kv0 = lax.broadcasted_iota(jnp.int32, (blk, KV_WA), 1) < HEAD_DIM
    head = lax.broadcasted_iota(jnp.int32, (rows, 1), 0) // blk
    snk = jnp.where(head == 0, sink_ref[0],
                    jnp.where(head == 1, sink_ref[1], jnp.where(head == 2, sink_ref[2], sink_ref[3])))
    for sb in range(WA_STEP_BLOCKS):
        gb = n * WA_STEP_BLOCKS + sb
        jlo = jnp.where(gb == 0, blk, 0)
        jhi = jnp.where(gb == nb - 1, 2 * blk, 3 * blk)
        valid = band & (j >= jlo) & (j < jhi)
        keys = slice(sb * blk, (sb + 3) * blk)
        qa = q[sb * blk:(sb + 1) * blk, :KV_WA]
        qb = q[sb * blk:(sb + 1) * blk, KV_WA:]
        zero = jnp.zeros_like(qa)
        qs = jnp.concatenate([jnp.where(kv0, qa, zero), jnp.where(kv0, qb, zero),
                              jnp.where(kv0, zero, qa), jnp.where(kv0, zero, qb)], axis=0)
        s_loc = jnp.where(valid, _nt_dot(qs, kall[keys]), NEG_INF)
        s_ctx = _nt_dot(qs, kc)
        (p_loc, p_ctx), den = _softmax_parts([s_loc, s_ctx], snk)
        o = (jnp.dot(p_loc.astype(BF16), vall[keys], preferred_element_type=F32)
             + jnp.dot(p_ctx.astype(BF16), vc, preferred_element_type=F32)) / den
        o_ref[0, sb * blk:(sb + 1) * blk, :] = jnp.concatenate(
            [jnp.where(kv0, o[:blk], o[2 * blk:3 * blk]), jnp.where(kv0, o[blk:2 * blk], o[3 * blk:])],
            axis=-1).astype(BF16)


def _window_attn(q, kv, ckv, sink):
    b, s, _ = q.shape
    lc = ckv.shape[1]
    nb = s // WA_BLOCK
    sbk = WA_STEP_BLOCKS

    def halo_spec(col, off):
        return pl.BlockSpec((1, WA_BLOCK, KV_WA),
                            lambda i, n: (i, jnp.clip(n * sbk + off, 0, nb - 1), col))

    def own_spec(col):
        return pl.BlockSpec((1, sbk * WA_BLOCK, KV_WA), lambda i, n: (i, n, col))

    return pl.pallas_call(
        functools.partial(_wattn_kernel, nb=nb),
        out_shape=jax.ShapeDtypeStruct((b, s, W_GROUP), BF16),
        grid=(b, nb // sbk),
        in_specs=[pl.BlockSpec(memory_space=pltpu.SMEM),
                  pl.BlockSpec((1, sbk * WA_BLOCK, W_GROUP), lambda i, n: (i, n, 0)),
                  halo_spec(0, -1), own_spec(0), halo_spec(0, sbk),
                  halo_spec(1, -1), own_spec(1), halo_spec(1, sbk),
                  pl.BlockSpec((1, lc, KV_WA), lambda i, n: (i, 0, 0)),
                  pl.BlockSpec((1, lc, KV_WA), lambda i, n: (i, 0, 1))],
        out_specs=pl.BlockSpec((1, sbk * WA_BLOCK, W_GROUP), lambda i, n: (i, n, 0)),
        compiler_params=_cparams(("parallel", "parallel")),
        name="window_attn",
    )(sink, q, kv, kv, kv, kv, kv, kv, ckv, ckv)


def _nattn_kernel(q_ref, k_ref, v_ref, kc_ref, vc_ref, bias_ref, o_ref, *, rows):
    blk = pl.program_id(1)
    nk = NA_ROWS * GRID_W
    kc, vc = kc_ref[0], vc_ref[0]
    lane_head = lax.broadcasted_iota(jnp.int32, (GRID_W, W_GROUP), 1) // HEAD_DIM
    masks = [lane_head == h for h in range(N_NA_HEADS)]
    for rr in range(NA_ROW_BLOCK):
        r = blk * NA_ROW_BLOCK + rr
        r0 = jnp.clip(r - NA_ROWS // 2, 0, rows - NA_ROWS)
        var = r - r0
        start = pl.multiple_of(r0 * GRID_W, GRID_W)
        kt = k_ref[0, pl.ds(start, nk), :]
        vt = v_ref[0, pl.ds(start, nk), :]
        qrow = slice(rr * GRID_W, (rr + 1) * GRID_W)
        qs = _head_stack(q_ref[0, qrow, :], masks)
        bias = bias_ref[:, var].reshape(N_NA_HEADS * GRID_W, nk)
        (p_loc, p_ctx), den = _softmax_parts([_nt_dot(qs, kt) + bias, _nt_dot(qs, kc)])
        o = (jnp.dot(p_loc.astype(BF16), vt, preferred_element_type=F32)
             + jnp.dot(p_ctx.astype(BF16), vc, preferred_element_type=F32)) / den
        out = o[(N_NA_HEADS - 1) * GRID_W:]
        for h in range(N_NA_HEADS - 2, -1, -1):
            out = jnp.where(masks[h], o[h * GRID_W:(h + 1) * GRID_W], out)
        o_ref[0, qrow, :] = out.astype(BF16)


def _na_bias(rpb):
    var = np.arange(NA_ROWS)
    j = np.arange(NA_ROWS)
    qc = np.arange(GRID_W)
    kc = np.arange(GRID_W)
    dr = j[None, :] - var[:, None] + NA_ROWS - 1
    dc = np.clip(kc[None, :] - qc[:, None] + NA_COLS - 1, 0, 2 * NA_COLS - 2)
    ws = np.clip(qc - NA_COLS // 2, 0, GRID_W - NA_COLS)
    ok = (kc[None, :] >= ws[:, None]) & (kc[None, :] < ws[:, None] + NA_COLS)
    onehot = (dc[None] == np.arange(2 * NA_COLS - 1)[:, None, None]).astype(np.float32)
    bias = jnp.einsum("hvjd,dqk->hvqjk", rpb.astype(F32)[:, dr], jnp.asarray(onehot),
                      precision=lax.Precision.HIGHEST)
    bias = jnp.where(jnp.asarray(ok)[None, None, :, None, :], bias, NEG_INF)
    return bias.reshape(rpb.shape[0], NA_ROWS, GRID_W, NA_ROWS * GRID_W)


def _neighbourhood_attn(q, kv, ckv, rpb):
    b, s, _ = q.shape
    lc = ckv.shape[1]
    rows = s // GRID_W
    bias = _na_bias(rpb)
    return pl.pallas_call(
        functools.partial(_nattn_kernel, rows=rows),
        out_shape=jax.ShapeDtypeStruct((b, s, W_GROUP), BF16),
        grid=(b, rows // NA_ROW_BLOCK),
        in_specs=[pl.BlockSpec((1, NA_ROW_BLOCK * GRID_W, W_GROUP), lambda i, r: (i, r, 0)),
                  pl.BlockSpec((1, s, W_GROUP), lambda i, r: (i, 0, 1)),
                  pl.BlockSpec((1, s, W_GROUP), lambda i, r: (i, 0, 2)),
                  pl.BlockSpec((1, lc, W_GROUP), lambda i, r: (i, 0, 1)),
                  pl.BlockSpec((1, lc, W_GROUP), lambda i, r: (i, 0, 2)),
                  pl.BlockSpec(bias.shape, lambda i, r: (0, 0, 0, 0))],
        out_specs=pl.BlockSpec((1, NA_ROW_BLOCK * GRID_W, W_GROUP), lambda i, r: (i, r, 0)),
        compiler_params=_cparams(("parallel", "arbitrary")),
        name="neighbourhood_attn",
    )(q, kv, kv, ckv, ckv, bias)


def _cattn_kernel(*refs, n_kv, with_sink, head_order):
    if with_sink:
        sink_ref, q_ref, k_ref, v_ref, o_ref = refs
    else:
        q_ref, k_ref, v_ref, o_ref = refs
    q, k, v = q_ref[0], k_ref[0], v_ref[0]
    group = N_WA_HEADS // n_kv
    outs = []
    for pos, h in enumerate(head_order):
        sl = slice((h // group) * HEAD_DIM, (h // group + 1) * HEAD_DIM)
        s = _nt_dot(q[:, pos * HEAD_DIM:(pos + 1) * HEAD_DIM], k[:, sl])
        extra = jnp.full((s.shape[0], 1), sink_ref[h], F32) if with_sink else None
        (p,), den = _softmax_parts([s], extra)
        outs.append(jnp.dot(p.astype(BF16), v[:, sl], preferred_element_type=F32) / den)
    o_ref[0] = jnp.concatenate(outs, axis=-1).astype(BF16)


def _ctx_attn(q, ckv, k_col, v_col, n_kv, sink, head_order=(0, 1, 2, 3)):
    b, lc, _ = q.shape
    w = n_kv * HEAD_DIM
    with_sink = sink is not None
    in_specs = [pl.BlockSpec((1, lc, W_GROUP), lambda i: (i, 0, 0)),
                pl.BlockSpec((1, lc, w), lambda i: (i, 0, k_col)),
                pl.BlockSpec((1, lc, w), lambda i: (i, 0, v_col))]
    args = [q, ckv, ckv]
    if with_sink:
        in_specs = [pl.BlockSpec(memory_space=pltpu.SMEM)] + in_specs
        args = [sink] + args
    return pl.pallas_call(
        functools.partial(_cattn_kernel, n_kv=n_kv, with_sink=with_sink, head_order=head_order),
        out_shape=jax.ShapeDtypeStruct((b, lc, W_GROUP), BF16),
        grid=(b,),
        in_specs=in_specs,
        out_specs=pl.BlockSpec((1, lc, W_GROUP), lambda i: (i, 0, 0)),
        compiler_params=_cparams(("parallel",)),
        name="ctx_attn",
    )(*args)


def _pool_kernel(prev_ref, cur_ref, next_ref, w_ref, scale_ref, o_ref, *, seq, tl):
    j = pl.program_id(1)
    nt = seq // tl
    cur = cur_ref[0]
    prev = jnp.where(j == 0, 0.0, prev_ref[0])
    nxt = jnp.where(j == nt - 1, 0.0, next_ref[0])
    e = jnp.concatenate([prev, cur, nxt], axis=0)
    n = tl + 2 * HALO

    def sh(a, d):
        return pltpu.roll(a, d % n, 0)

    s2 = e + sh(e, 1)
    s4 = sh(s2, 1) + sh(s2, -1)
    s8 = sh(s4, 2) + sh(s4, -2)
    s16 = sh(s8, 4) + sh(s8, -4)
    lane = lax.broadcasted_iota(jnp.int32, (tl, W_GROUP), 1)
    t = lax.broadcasted_iota(jnp.int32, (tl, W_GROUP), 0) + j * tl
    g = lane // POOL_GROUP
    lo, hi = HALO, HALO + tl
    ssum = jnp.where(g == 0, s2[lo:hi], jnp.where(g == 1, s4[lo:hi], jnp.where(g == 2, s8[lo:hi], s16[lo:hi])))
    half = jnp.where(g == 0, 1, jnp.where(g == 1, 2, jnp.where(g == 2, 4, 8)))
    cnt = (jnp.minimum(t + half, seq) - jnp.maximum(t - half, 0)).astype(F32)
    d = (ssum / cnt - cur).astype(BF16)
    o_ref[0] = (jnp.dot(d, w_ref[...], preferred_element_type=F32) * scale_ref[...]).astype(BF16)


def _pool_mixer(u, w_bd, scale):
    b, l, c = u.shape
    tl = min(l, 512)
    hb = tl // HALO
    nh = l // HALO
    return pl.pallas_call(
        functools.partial(_pool_kernel, seq=l, tl=tl),
        out_shape=jax.ShapeDtypeStruct((b, l, c), BF16),
        grid=(b, l // tl),
        in_specs=[pl.BlockSpec((1, HALO, c), lambda i, j: (i, jnp.maximum(j * hb - 1, 0), 0)),
                  pl.BlockSpec((1, tl, c), lambda i, j: (i, j, 0)),
                  pl.BlockSpec((1, HALO, c), lambda i, j: (i, jnp.minimum((j + 1) * hb, nh - 1), 0)),
                  pl.BlockSpec((c, c), lambda i, j: (0, 0)),
                  pl.BlockSpec((1, c), lambda i, j: (0, 0))],
        out_specs=pl.BlockSpec((1, tl, c), lambda i, j: (i, j, 0)),
        compiler_params=_cparams(("parallel", "parallel")),
        name="pool_mixer",
    )(u, u, u, w_bd, scale)


def _pool_weight(pool_w):
    z = jnp.zeros((W_GROUP, W_GROUP), F32)
    for g in range(len(POOL_WINDOWS)):
        z = z.at[g * POOL_GROUP:(g + 1) * POOL_GROUP, g * POOL_GROUP:(g + 1) * POOL_GROUP].set(pool_w[g])
    return z.astype(BF16)


def _outproj_kernel(yh_ref, ya_ref, yp_ref, yn_ref, x_ref, gt_ref, w_ref, g2_ref, sh_ref, sc_ref, rw_ref,
                    xo_ref, h_ref, aff_ref):
    y = jnp.concatenate([yh_ref[0], ya_ref[0], yp_ref[0], yn_ref[0]], axis=-1)
    x = x_ref[0] + gt_ref[0] * jnp.dot(y, w_ref[...], preferred_element_type=F32)
    xo_ref[0] = x
    h = _rmsnorm_mod(x, g2_ref[...], sh_ref[0], sc_ref[0])
    h_ref[0] = h
    logits = _nt_dot(rw_ref[...], h.astype(BF16))
    m = logits.max(axis=0, keepdims=True)
    p = jnp.exp(logits - m)
    aff_ref[0] = p / p.sum(axis=0, keepdims=True)


def _outproj(ys, x, gate, w_bf16, g2, shift, scale, rw_t):
    b, l, d = x.shape
    tm = min(l, 512)
    bm = gate.shape[0]
    mod_map = (lambda i, j: (i, 0, 0)) if bm > 1 else (lambda i, j: (0, 0, 0))
    tok = lambda w: pl.BlockSpec((1, tm, w), lambda i, j: (i, j, 0))
    mod = pl.BlockSpec((1, 1, d), mod_map)
    return pl.pallas_call(
        _outproj_kernel,
        out_shape=[jax.ShapeDtypeStruct((b, l, d), F32), jax.ShapeDtypeStruct((b, l, d), F32),
                   jax.ShapeDtypeStruct((b, N_EXPERTS, l), F32)],
        grid=(b, l // tm),
        in_specs=[tok(W_GROUP)] * 4 + [tok(d), mod, pl.BlockSpec((d, d), lambda i, j: (0, 0)),
                                        pl.BlockSpec((1, d), lambda i, j: (0, 0)), mod, mod,
                                        pl.BlockSpec((N_EXPERTS, d), lambda i, j: (0, 0))],
        out_specs=[tok(d), tok(d), pl.BlockSpec((1, N_EXPERTS, tm), lambda i, j: (i, 0, j))],
        compiler_params=_cparams(("parallel", "parallel")),
        name="outproj_router",
    )(*ys, x, gate, w_bf16, g2, shift, scale, rw_t)


def _row_gather_kernel(idx_ref, nxt_ref, table_ref, out_ref, stage, sems):
    i, n = pl.program_id(0), pl.num_programs(0)
    cur = i % 2
    chunk = out_ref.shape[0]

    def issue(ref, s):
        def one(r, c):
            pltpu.make_async_copy(table_ref.at[pl.ds(ref[0, 0, r], 1)], stage.at[s, pl.ds(r, 1)], sems.at[s]).start()
            return c
        lax.fori_loop(0, chunk, one, 0, unroll=8)

    @pl.when(i == 0)
    def _():
        issue(idx_ref, cur)

    @pl.when(i + 1 < n)
    def _():
        issue(nxt_ref, 1 - cur)

    pltpu.make_async_copy(table_ref.at[pl.ds(0, chunk)], stage.at[cur], sems.at[cur]).wait()
    out_ref[...] = stage[cur].astype(BF16)


def _row_gather(table, rows):
    v, d = table.shape
    n = rows.shape[0]
    chunk = min(GATHER_CHUNK, n)
    nch = n // chunk
    idx_spec = lambda off: pl.BlockSpec((1, 1, chunk), lambda i: (jnp.minimum(i + off, nch - 1), 0, 0),
                                        memory_space=pltpu.SMEM)
    rows3 = rows.reshape(nch, 1, chunk)
    return pl.pallas_call(
        _row_gather_kernel,
        out_shape=jax.ShapeDtypeStruct((n, d), BF16),
        grid=(nch,),
        in_specs=[idx_spec(0), idx_spec(1), pl.BlockSpec(memory_space=pl.ANY)],
        out_specs=pl.BlockSpec((chunk, d), lambda i: (i, 0)),
        scratch_shapes=[pltpu.VMEM((2, chunk, d), table.dtype), pltpu.SemaphoreType.DMA((2,))],
        compiler_params=pltpu.CompilerParams(dimension_semantics=("arbitrary",), vmem_limit_bytes=VMEM_LIMIT,
                                             disable_bounds_checks=True),
        name="moe_row_gather",
    )(rows3, rows3, table)


def _moe_kernel(x_ref, g_ref, wg_ref, wu_ref, wd_ref, o_ref):
    f = pl.program_id(2)

    @pl.when(f == 0)
    def _():
        o_ref[0] = jnp.zeros(o_ref.shape[1:], F32)

    x = x_ref[0]
    a = jnp.dot(x, wg_ref[0, 0].astype(BF16), preferred_element_type=F32)
    u = jnp.dot(x, wu_ref[0, 0].astype(BF16), preferred_element_type=F32)
    hid = (a * jax.nn.sigmoid(a) * u).astype(BF16)
    o_ref[0] += jnp.dot(hid, wd_ref[0, 0].astype(BF16), preferred_element_type=F32)

    @pl.when(f == pl.num_programs(2) - 1)
    def _():
        o_ref[0] = o_ref[0] * g_ref[0]


def _moe_ffn(xg, gate, w_gate, w_up, w_down, lyr):
    e, m, d = xg.shape
    f = w_gate.shape[3]
    tm = min(m, FFN_ROWS)
    tf = FFN_HIDDEN_TILE
    return pl.pallas_call(
        _moe_kernel,
        out_shape=jax.ShapeDtypeStruct((e, m, d), F32),
        grid=(e, m // tm, f // tf),
        in_specs=[pl.BlockSpec((1, tm, d), lambda i, j, k: (i, j, 0)),
                  pl.BlockSpec((1, tm, 1), lambda i, j, k: (i, j, 0)),
                  pl.BlockSpec((1, 1, d, tf), lambda i, j, k: (lyr, i, 0, k)),
                  pl.BlockSpec((1, 1, d, tf), lambda i, j, k: (lyr, i, 0, k)),
                  pl.BlockSpec((1, 1, tf, d), lambda i, j, k: (lyr, i, k, 0))],
        out_specs=pl.BlockSpec((1, tm, d), lambda i, j, k: (i, j, 0)),
        compiler_params=_cparams(("parallel", "parallel", "arbitrary")),
        name="moe_ffn",
    )(xg, gate, w_gate, w_up, w_down)


def _route_kernel(aff_ref, tri_ref, idx_ref, g_ref, *, k):
    x = aff_ref[...]
    r, l = x.shape
    nb = l // LANES
    nq = idx_ref.shape[1] // LANES
    def count(mask):
        return jnp.sum(jnp.where(mask, 1.0, 0.0), axis=-1, keepdims=True)

    def as_float(pattern):
        return pltpu.bitcast(jnp.broadcast_to(pattern, (r, LANES)), F32)[:, :1]

    def bisect(i, t):
        cand = t | jnp.left_shift(jnp.int32(1), F32_VALUE_BITS - 1 - i)
        return jnp.where(count(x >= as_float(cand)) >= k, cand, t)

    kth = lax.fori_loop(0, F32_VALUE_BITS, bisect, jnp.zeros((r, 1), jnp.int32))
    gt = x >= as_float(kth + 1)
    eq = (x >= as_float(kth)) & jnp.logical_not(gt)
    need = k - count(gt)
    tri = tri_ref[...]
    lane = lax.broadcasted_iota(jnp.int32, (r, LANES), 1)

    def block_prefix(mask, j):
        m = jnp.where(mask[:, j * LANES:(j + 1) * LANES], 1.0, 0.0)
        return jnp.dot(m.astype(BF16), tri, preferred_element_type=F32), m

    out_tok = [jnp.zeros((r, LANES), jnp.int32) for _ in range(nq)]
    out_g = [jnp.zeros((r, LANES), F32) for _ in range(nq)]
    eq_before = jnp.zeros((r, 1), F32)
    start = jnp.zeros((r, 1), jnp.int32)
    for j in range(nb):
        blk = slice(j * LANES, (j + 1) * LANES)
        eq_inc, eq_f = block_prefix(eq, j)
        sel = gt[:, blk] | (eq[:, blk] & (eq_inc - eq_f + eq_before < need))
        eq_before = eq_before + eq_inc[:, LANES - 1:]
        sel_f = jnp.where(sel, 1.0, 0.0)
        sel_inc = jnp.dot(sel_f.astype(BF16), tri, preferred_element_type=F32)
        cnt = sel_inc[:, LANES - 1:].astype(jnp.int32)
        d = jnp.where(sel, lane - (sel_inc - sel_f).astype(jnp.int32), -1)
        tok = lane + j * LANES
        gv = x[:, blk]
        for kb in range(LANE_BITS):
            sh = LANES - (1 << kb)
            rd, rt, rg = pltpu.roll(d, sh, 1), pltpu.roll(tok, sh, 1), pltpu.roll(gv, sh, 1)
            arrive = (rd >= 0) & (((rd >> kb) & 1) == 1)
            stay = (d >= 0) & (((d >> kb) & 1) == 0)
            tok = jnp.where(arrive, rt, tok)
            gv = jnp.where(arrive, rg, gv)
            d = jnp.where(arrive, rd, jnp.where(stay, d, -1))
        s, q = start & (LANES - 1), start >> LANE_BITS
        for kb in range(LANE_BITS):
            mv = ((s >> kb) & 1) == 1
            tok = jnp.where(mv, pltpu.roll(tok, 1 << kb, 1), tok)
            gv = jnp.where(mv, pltpu.roll(gv, 1 << kb, 1), gv)
        end = s + cnt
        here = (lane >= s) & (lane < end)
        wrapped = lane < end - LANES
        for qq in range(nq):
            hit = (here & (q == qq)) | (wrapped & (q + 1 == qq))
            out_tok[qq] = jnp.where(hit, tok, out_tok[qq])
            out_g[qq] = jnp.where(hit, gv, out_g[qq])
        start = start + cnt
    idx_ref[...] = jnp.concatenate(out_tok, axis=1)
    g_ref[...] = jnp.concatenate(out_g, axis=1)


def _route(aff_t, k):
    b, e, l = aff_t.shape
    r = b * e
    width = -(-k // LANES) * LANES
    tri = jnp.asarray(np.triu(np.ones((LANES, LANES), np.float32))).astype(BF16)
    idx, g = pl.pallas_call(
        functools.partial(_route_kernel, k=k),
        out_shape=[jax.ShapeDtypeStruct((r, width), jnp.int32), jax.ShapeDtypeStruct((r, width), F32)],
        compiler_params=pltpu.CompilerParams(vmem_limit_bytes=VMEM_LIMIT),
        name="expert_choice_route",
    )(aff_t.reshape(r, l), tri)
    return idx[:, :k].reshape(b, e, k), g[:, :k].reshape(b, e, k)


def _expert_choice_ffn(h, aff_t, w_gate, w_up, w_down, lyr):
    b, l, d = h.shape
    cap = EC_CAPACITY * l // N_EXPERTS
    m = b * cap
    idx, g = _route(aff_t, cap)
    idx_t = jnp.swapaxes(idx, 0, 1)
    rows = (idx_t + (jnp.arange(b, dtype=jnp.int32) * l)[None, :, None]).reshape(-1)
    xg = _row_gather(h.reshape(b * l, d), rows).reshape(N_EXPERTS, m, d)
    gate = jnp.swapaxes(g, 0, 1).reshape(N_EXPERTS, m, 1)
    o = _moe_ffn(xg, gate, w_gate, w_up, w_down, lyr).reshape(N_EXPERTS * m, d)
    return o, idx


def _combine_kernel(*refs, final, tile, nt, cap, strip, m):
    if final:
        lo_ref, idx_ref, x_ref, gt_ref, g_ref, o_hbm, out_ref, stage, sems = refs
    else:
        lo_ref, idx_ref, x_ref, gt_ref, o_hbm, out_ref, stage, sems = refs
    bi, j = pl.program_id(0), pl.program_id(1)
    step = bi * nt + j
    cur = step % 2
    ne = N_EXPERTS
    gps = strip // SUBLANES
    per_sub = COMBINE_SUB // strip
    win = 2 * LANES
    sentinel = tile * nt

    def runs(jj):
        out = []
        for e in range(ne):
            lo = lo_ref[bi, e * (nt + 1) + jj]
            hi = lo_ref[bi, e * (nt + 1) + jj + 1]
            a8 = (lo // SUBLANES) * SUBLANES
            out.append((a8, jnp.where(hi > lo, hi - a8, 0)))
        return out

    def groups(run, r):
        return [(a8 + strip * r, jnp.clip((n - strip * r + SUBLANES - 1) // SUBLANES, 0, gps)) for a8, n in run]

    def issue(s, grp):
        for e, (s0, ng) in enumerate(grp):
            base = e * m + bi * cap + s0

            def one(k, c, base=base, e=e):
                pltpu.make_async_copy(
                    o_hbm.at[pl.ds(pl.multiple_of(base + k * SUBLANES, SUBLANES), SUBLANES)],
                    stage.at[s, pl.ds(pl.multiple_of(e * strip + k * SUBLANES, SUBLANES), SUBLANES)],
                    sems.at[s]).start()
                return c
            lax.fori_loop(0, ng, one, 0)

    def wait(s, grp):
        cnt = sum(ng for _, ng in grp) * SUBLANES

        @pl.when(cnt > 0)
        def _():
            rows = pl.ds(0, pl.multiple_of(cnt, SUBLANES))
            pltpu.make_async_copy(o_hbm.at[rows], stage.at[s, rows], sems.at[s]).wait()

    def reduce(s, grp):
        t_ids = lax.broadcasted_iota(jnp.int32, (tile, COMBINE_SUB), 0) + j * tile
        acc = jnp.zeros((tile, out_ref.shape[2]), F32)
        for sub in range(ne // per_sub):
            pieces = []
            for e in range(sub * per_sub, (sub + 1) * per_sub):
                s0, ng = grp[e]
                a = jnp.minimum((s0 // LANES) * LANES, idx_ref.shape[2] - win)
                w = idx_ref[0, e:e + 1, pl.ds(pl.multiple_of(a, LANES), win)]
                w = pltpu.roll(w, (win - (s0 - a)) % win, 1)
                pieces.append(jnp.where(ng > 0, w[:, :strip], sentinel))
            toks = jnp.concatenate(pieces, axis=1)
            rows = stage[s, sub * COMBINE_SUB:(sub + 1) * COMBINE_SUB, :]
            onehot = jnp.where(toks == t_ids, 1.0, 0.0).astype(BF16)
            hi = rows.astype(BF16)
            lo = (rows - hi.astype(F32)).astype(BF16)
            acc = (acc + jnp.dot(onehot, hi, preferred_element_type=F32)
                   + jnp.dot(onehot, lo, preferred_element_type=F32))
        return acc

    @pl.when(step == 0)
    def _():
        stage[...] = jnp.zeros(stage.shape, F32)

    run = runs(j)
    first = groups(run, 0)

    @pl.when(j == 0)
    def _():
        issue(cur, first)

    @pl.when(j + 1 < nt)
    def _():
        issue(1 - cur, groups(runs(jnp.minimum(j + 1, nt - 1)), 0))

    wait(cur, first)
    acc = reduce(cur, first)
    longest = run[0][1]
    for _, n in run[1:]:
        longest = jnp.maximum(longest, n)

    def extra_round(r, acc):
        grp = groups(run, r)
        issue(cur, grp)
        wait(cur, grp)
        return acc + reduce(cur, grp)

    acc = lax.fori_loop(1, (longest + strip - 1) // strip, extra_round, acc)
    x = x_ref[0] + gt_ref[0] * acc
    if final:
        x = x * lax.rsqrt(jnp.mean(x * x, axis=-1, keepdims=True) + NORM_EPS) * g_ref[...]
    out_ref[0] = x


def _combine(x, o, idx, gate, final_g=None):
    b, l, d = x.shape
    ne, cap = idx.shape[1], idx.shape[2]
    tile = min(l, COMBINE_TILE)
    nt = l // tile
    strip = min(COMBINE_STRIP, cap)
    bounds = jnp.arange(nt + 1, dtype=jnp.int32) * tile
    lo = jnp.sum(idx[:, :, None, :] < bounds[None, None, :, None], axis=-1, dtype=jnp.int32)
    cp = -(-cap // LANES) * LANES + LANES
    idx_p = jnp.pad(idx, ((0, 0), (0, 0), (0, cp - cap)), constant_values=l)
    bm = gate.shape[0]
    mod_map = (lambda i, j: (i, 0, 0)) if bm > 1 else (lambda i, j: (0, 0, 0))
    tok = pl.BlockSpec((1, tile, d), lambda i, j: (i, j, 0))
    final = final_g is not None
    in_specs = [pl.BlockSpec(memory_space=pltpu.SMEM),
                pl.BlockSpec((1, ne, cp), lambda i, j: (i, 0, 0)),
                tok, pl.BlockSpec((1, 1, d), mod_map)]
    args = [lo.reshape(b, ne * (nt + 1)), idx_p, x, gate]
    if final:
        in_specs.append(pl.BlockSpec((1, d), lambda i, j: (0, 0)))
        args.append(final_g)
    in_specs.append(pl.BlockSpec(memory_space=pl.ANY))
    args.append(o)
    return pl.pallas_call(
        functools.partial(_combine_kernel, final=final, tile=tile, nt=nt, cap=cap, strip=strip, m=b * cap),
        out_shape=jax.ShapeDtypeStruct((b, l, d), F32),
        grid=(b, nt),
        in_specs=in_specs,
        out_specs=tok,
        scratch_shapes=[pltpu.VMEM((2, ne * strip, d), F32), pltpu.SemaphoreType.DMA((2,))],
        compiler_params=pltpu.CompilerParams(dimension_semantics=("arbitrary", "arbitrary"),
                                             vmem_limit_bytes=VMEM_LIMIT, disable_bounds_checks=True),
        name="moe_combine_final" if final else "moe_combine",
    )(*args)


def _hyena_filters(l, w1, b1, w2, b2, w3, freq):
    hp = lax.Precision.HIGHEST
    bands = jnp.linspace(1e-4, HY_BANDS - 1, HY_BANDS, dtype=F32)
    deltas = jnp.linspace(HY_DECAY_MIN, HY_DECAY_MAX, W_GROUP, dtype=F32)

    def at(pos):
        t = pos / max(l - 1, 1)
        ang = (2 * math.pi / l) * pos[:, None] * bands[None, :]
        feats = jnp.concatenate([t[:, None], jnp.cos(ang), -jnp.sin(ang)], axis=-1)
        h = jnp.sin(freq * (jnp.dot(feats, w1, precision=hp) + b1))
        h = jnp.sin(freq * (jnp.dot(h, w2, precision=hp) + b2))
        h = jnp.dot(h, w3, precision=hp).reshape(l, HY_ORDER, 2, W_GROUP)
        return h * jnp.exp(-t[:, None] * deltas[None, :])[:, None, None, :]

    lag = jnp.arange(l)
    fwd = at(lag.astype(F32))
    rev = at(((l - lag) % l).astype(F32))
    first = (lag == 0)[:, None, None]
    half0 = fwd[:, :, 0] + jnp.where(first, fwd[:, :, 1], 0.0)
    half1 = jnp.where(first, 0.0, rev[:, :, 1])
    return jnp.transpose(jnp.stack([half0, half1], axis=0), (2, 0, 1, 3))


def _cplx_block(e):
    return np.block([[e.real, -e.imag], [e.imag, e.real]])


@functools.lru_cache(maxsize=None)
def _dft_consts(l):
    n = 2 * l
    n1f = n // DFT_N2
    n1h = n1f // 2
    k1, n1, n2 = np.arange(n1f), np.arange(n1h), np.arange(DFT_N2)
    ph = np.outer(k1, n1)[None] / n1f + (n2[:, None, None] * k1[None, :, None]) / n
    e1 = np.exp(-2j * np.pi * ph)
    w1 = np.stack([_cplx_block(e1[i]) for i in range(DFT_N2)])
    e2 = np.exp(-2j * np.pi * np.outer(n2, n2) / DFT_N2)
    f2 = _cplx_block(e2)
    g2 = _cplx_block(np.conj(e2).T)
    g1 = np.stack([_cplx_block(np.conj(e1[i]).T / n) for i in range(DFT_N2)])
    return tuple(jnp.asarray(a, F32).astype(BF16) for a in (w1, f2, g2, g1))


def _lconv_kernel(a_ref, hf_ref, w1_ref, f2_ref, g2_ref, g1_ref, y_ref, xy_ref, s_ref, *, n1h):
    n1f = 2 * n1h

    def copy_in(i, c):
        for ri in range(2):
            xy_ref[ri, pl.ds(pl.multiple_of(i * X_PITCH, 8), DFT_N2), :] = \
                a_ref[0, ri, pl.ds(pl.multiple_of(i * DFT_N2, DFT_N2), DFT_N2), :]
        return c

    lax.fori_loop(0, n1h, copy_in, 0, unroll=4)

    def stage1(n2, c):
        slab = jnp.concatenate([xy_ref[0, pl.ds(n2, n1h, stride=X_PITCH), :],
                                xy_ref[1, pl.ds(n2, n1h, stride=X_PITCH), :]], axis=0).astype(BF16)
        a = jnp.dot(w1_ref[n2], slab, preferred_element_type=F32)
        base = pl.multiple_of(n2 * S_PITCH, 8)
        s_ref[0, pl.ds(base, n1f), :] = a[:n1f]
        s_ref[1, pl.ds(base, n1f), :] = a[n1f:]
        return c

    lax.fori_loop(0, DFT_N2, stage1, 0, unroll=32)

    def stage2(k1, c):
        slab = jnp.concatenate([s_ref[0, pl.ds(k1, DFT_N2, stride=S_PITCH), :],
                                s_ref[1, pl.ds(k1, DFT_N2, stride=S_PITCH), :]], axis=0).astype(BF16)
        x = jnp.dot(f2_ref[...], slab, preferred_element_type=F32)
        xr, xi = x[:DFT_N2], x[DFT_N2:]
        hr, hi = hf_ref[0, k1, 0], hf_ref[0, k1, 1]
        y = jnp.concatenate([xr * hr - xi * hi, xr * hi + xi * hr], axis=0).astype(BF16)
        cc = jnp.dot(g2_ref[...], y, preferred_element_type=F32)
        s_ref[0, pl.ds(k1, DFT_N2, stride=S_PITCH), :] = cc[:DFT_N2]
        s_ref[1, pl.ds(k1, DFT_N2, stride=S_PITCH), :] = cc[DFT_N2:]
        return c

    lax.fori_loop(0, n1f, stage2, 0, unroll=16)

    def stage3(n2, c):
        base = pl.multiple_of(n2 * S_PITCH, 8)
        d = jnp.concatenate([s_ref[0, pl.ds(base, n1f), :], s_ref[1, pl.ds(base, n1f), :]], axis=0).astype(BF16)
        yv = jnp.dot(g1_ref[n2], d, preferred_element_type=F32)
        yb = pl.multiple_of(n2 * Y_PITCH, 8)
        xy_ref[0, pl.ds(yb, n1h), :] = yv[:n1h]
        xy_ref[1, pl.ds(yb, n1h), :] = yv[n1h:]
        return c

    lax.fori_loop(0, DFT_N2, stage3, 0, unroll=32)

    def copy_out(i, c):
        for ri in range(2):
            y_ref[0, ri, pl.ds(pl.multiple_of(i * DFT_N2, DFT_N2), DFT_N2), :] = \
                xy_ref[ri, pl.ds(i, DFT_N2, stride=Y_PITCH), :]
        return c

    lax.fori_loop(0, n1h, copy_out, 0, unroll=4)


def _lconv_small_kernel(a_ref, hf_ref, f_ref, g_ref, y_ref):
    l = a_ref.shape[2]
    slab = jnp.concatenate([a_ref[0, 0], a_ref[0, 1]], axis=0).astype(BF16)
    x = jnp.dot(f_ref[...], slab, preferred_element_type=F32)
    xr, xi = x[:2 * l], x[2 * l:]
    hr, hi = hf_ref[0, 0], hf_ref[0, 1]
    y = jnp.concatenate([xr * hr - xi * hi, xr * hi + xi * hr], axis=0).astype(BF16)
    out = jnp.dot(g_ref[...], y, preferred_element_type=F32)
    y_ref[0, 0] = out[:l]
    y_ref[0, 1] = out[l:]


@functools.lru_cache(maxsize=None)
def _dft_consts_small(l):
    n = 2 * l
    e = np.exp(-2j * np.pi * np.outer(np.arange(n), np.arange(l)) / n)
    f = _cplx_block(e)
    g = _cplx_block(np.conj(e).T / n)
    return jnp.asarray(f, F32).astype(BF16), jnp.asarray(g, F32).astype(BF16)


def _long_conv(a, hf, order):
    b, l, ca = a.shape
    c = W_GROUP
    a4 = a.reshape(b // 2, 2, l, ca)
    io_spec = pl.BlockSpec((1, 2, l, LANES), lambda j, p: (p, 0, 0, j))
    once = dict(pipeline_mode=pl.Buffered(1))
    const = lambda arr: pl.BlockSpec(arr.shape, lambda j, p: (0,) * arr.ndim, **once)
    if l <= DFT_N2 * 2:
        f, g = _dft_consts_small(l)
        y = pl.pallas_call(
            _lconv_small_kernel,
            out_shape=jax.ShapeDtypeStruct((b // 2, 2, l, c), F32),
            grid=(c // LANES, b // 2),
            in_specs=[io_spec, pl.BlockSpec((1, 2, 2 * l, LANES), lambda j, p: (order, 0, 0, j)), const(f), const(g)],
            out_specs=io_spec,
            compiler_params=_cparams(("parallel", "arbitrary")),
            name="long_conv_small",
        )(a4, hf, f, g)
        return y.reshape(b, l, c)
    n1f = 2 * l // DFT_N2
    n1h = n1f // 2
    assert (n1f + 8, n1h + 8) == (S_PITCH, Y_PITCH), "scratch pitches are sized for this sequence length"
    w1, f2, g2, g1 = _dft_consts(l)
    rows_xy = max(n1h * X_PITCH, DFT_N2 * Y_PITCH)
    y = pl.pallas_call(
        functools.partial(_lconv_kernel, n1h=n1h),
        out_shape=jax.ShapeDtypeStruct((b // 2, 2, l, c), F32),
        grid=(c // LANES, b // 2),
        in_specs=[io_spec, pl.BlockSpec((1, n1f, 2, DFT_N2, LANES), lambda j, p: (order, 0, 0, 0, j), **once),
                  const(w1), const(f2), const(g2), const(g1)],
        out_specs=io_spec,
        scratch_shapes=[pltpu.VMEM((2, rows_xy, LANES), F32), pltpu.VMEM((2, DFT_N2 * S_PITCH, LANES), F32)],
        compiler_params=_cparams(("parallel", "arbitrary"), DFT_VMEM_LIMIT),
        name="long_conv",
    )(a4, hf, w1, f2, g2, g1)
    return y.reshape(b, l, c)


@functools.lru_cache(maxsize=None)
def _fspec_consts(n):
    n1f = n // DFT_N2
    k1, n1, n2 = np.arange(n1f), np.arange(n1f), np.arange(DFT_N2)
    ph = np.outer(k1, n1)[None] / n1f + (n2[:, None, None] * k1[None, :, None]) / n
    e1 = np.exp(-2j * np.pi * ph)
    w1 = np.concatenate([e1.real, e1.imag], axis=1)
    f2 = _cplx_block(np.exp(-2j * np.pi * np.outer(n2, n2) / DFT_N2))
    return jnp.asarray(w1, F32).astype(BF16), jnp.asarray(f2, F32).astype(BF16)


def _fspec_kernel(f_ref, w1_ref, f2_ref, hf_ref, x_ref, s_ref, *, n1f):
    n1h = n1f // 2

    def copy_in(i, c):
        for half in range(2):
            x_ref[pl.ds(pl.multiple_of((half * n1h + i) * X_PITCH, 8), DFT_N2), :] = \
                f_ref[0, half, pl.ds(pl.multiple_of(i * DFT_N2, DFT_N2), DFT_N2), :]
        return c

    lax.fori_loop(0, n1h, copy_in, 0, unroll=4)

    def stage1(n2, c):
        slab = x_ref[pl.ds(n2, n1f, stride=X_PITCH), :].astype(BF16)
        a = jnp.dot(w1_ref[n2], slab, preferred_element_type=F32)
        base = pl.multiple_of(n2 * S_PITCH, 8)
        s_ref[0, pl.ds(base, n1f), :] = a[:n1f]
        s_ref[1, pl.ds(base, n1f), :] = a[n1f:]
        return c

    lax.fori_loop(0, DFT_N2, stage1, 0, unroll=32)

    def stage2(k1, c):
        slab = jnp.concatenate([s_ref[0, pl.ds(k1, DFT_N2, stride=S_PITCH), :],
                                s_ref[1, pl.ds(k1, DFT_N2, stride=S_PITCH), :]], axis=0).astype(BF16)
        x = jnp.dot(f2_ref[...], slab, preferred_element_type=F32)
        hf_ref[0, k1, 0] = x[:DFT_N2]
        hf_ref[0, k1, 1] = x[DFT_N2:]
        return c

    lax.fori_loop(0, n1f, stage2, 0, unroll=16)


def _fspec_small_kernel(f_ref, w_ref, hf_ref):
    n = 2 * f_ref.shape[2]
    filt = jnp.concatenate([f_ref[0, 0], f_ref[0, 1]], axis=0).astype(BF16)
    x = jnp.dot(w_ref[...], filt, preferred_element_type=F32)
    hf_ref[0, 0] = x[:n]
    hf_ref[0, 1] = x[n:]


@functools.lru_cache(maxsize=None)
def _fspec_consts_small(n):
    e = np.exp(-2j * np.pi * np.outer(np.arange(n), np.arange(n)) / n)
    return jnp.asarray(np.concatenate([e.real, e.imag], axis=0), F32).astype(BF16)


def _filter_spectra(filt):
    orders, _, l, c = filt.shape
    n = 2 * l
    f2s = filt
    once = dict(pipeline_mode=pl.Buffered(1))
    const = lambda arr: pl.BlockSpec(arr.shape, lambda o, j: (0,) * arr.ndim, **once)
    in_spec = pl.BlockSpec((1, 2, l, LANES), lambda o, j: (o, 0, 0, j))
    if l <= DFT_N2 * 2:
        w = _fspec_consts_small(n)
        return pl.pallas_call(
            _fspec_small_kernel,
            out_shape=jax.ShapeDtypeStruct((orders, 2, n, c), F32),
            grid=(orders, c // LANES),
            in_specs=[in_spec, const(w)],
            out_specs=pl.BlockSpec((1, 2, n, LANES), lambda o, j: (o, 0, 0, j)),
            compiler_params=_cparams(("parallel", "parallel")),
            name="filter_spectrum_small",
        )(f2s, w)
    n1f = n // DFT_N2
    assert n1f + 8 == S_PITCH, "scratch pitches are sized for this sequence length"
    w1, f2 = _fspec_consts(n)
    return pl.pallas_call(
        functools.partial(_fspec_kernel, n1f=n1f),
        out_shape=jax.ShapeDtypeStruct((orders, n1f, 2, DFT_N2, c), F32),
        grid=(orders, c // LANES),
        in_specs=[in_spec, const(w1), const(f2)],
        out_specs=pl.BlockSpec((1, n1f, 2, DFT_N2, LANES), lambda o, j: (o, 0, 0, 0, j)),
        scratch_shapes=[pltpu.VMEM((n1f * X_PITCH, LANES), F32), pltpu.VMEM((2, DFT_N2 * S_PITCH, LANES), F32)],
        compiler_params=_cparams(("parallel", "parallel"), DFT_VMEM_LIMIT),
        name="filter_spectrum",
    )(f2s, w1, f2)


def _hy_gate_kernel(y_ref, a_ref, m_ref, sk_ref, o_ref):
    o_ref[0] = (m_ref[0] * (y_ref[0] + a_ref[0] * sk_ref[...])).astype(o_ref.dtype)


def _hy_gate(y, a, a_col, m, m_col, sk, out_dtype):
    b, l, c = y.shape
    tl = min(l, 1024)
    spec = lambda col: pl.BlockSpec((1, tl, c), lambda i, j: (i, j, col))
    return pl.pallas_call(
        _hy_gate_kernel,
        out_shape=jax.ShapeDtypeStruct((b, l, c), out_dtype),
        grid=(b, l // tl),
        in_specs=[spec(0), spec(a_col), spec(m_col), pl.BlockSpec((1, c), lambda i, j: (0, 0))],
        out_specs=spec(0),
        compiler_params=_cparams(("parallel", "parallel")),
        name="hyena_gate",
    )(y, a, m, sk)


def _hyena(uc, filt, skip):
    hf = _filter_spectra(filt)
    y1 = _long_conv(uc, hf, 0)
    z1 = _hy_gate(y1, uc, 0, uc, 1, skip[0:1], F32)
    y2 = _long_conv(z1, hf, 1)
    return _hy_gate(y2, z1, 0, uc, 2, skip[1:2], BF16)


def _permute_wa_heads(w, start, axis):
    idx = np.arange(w.shape[axis])
    blocks = [np.arange(start + h * HEAD_DIM, start + (h + 1) * HEAD_DIM) for h in WA_HEAD_ORDER]
    idx[start:start + N_WA_HEADS * HEAD_DIM] = np.concatenate(blocks)
    return jnp.take(w, idx, axis=axis)


def _layer(x, ctx, mod, lyr, rope_tabs, update_ctx, final_g):
    b, s, d = x.shape
    lc = ctx.shape[1]
    mx = [mod[:b, None, i * d:(i + 1) * d] for i in range(6)]
    mc = [mod[b:b + 1, None, i * d:(i + 1) * d] for i in range(6)]
    g1, g2 = lyr["g1"][None, :], lyr["g2"][None, :]
    w_in = _permute_wa_heads(lyr["w_in"], OFF_WA_Q, axis=1).astype(BF16)
    w_out = _permute_wa_heads(lyr["w_out"], W_GROUP, axis=0).astype(BF16)
    rw_t = lyr["router_w"].T.astype(BF16)
    w_bd = _pool_weight(lyr["pool_w"])
    pscale = lyr["pool_scale"][None, :]
    hy_args = (lyr["hy_w1"], lyr["hy_b1"], lyr["hy_w2"], lyr["hy_b2"], lyr["hy_w3"], lyr["hy_freq"])

    conv = (lyr["hy_conv_w"], lyr["hy_conv_b"][None, :])
    u_hy, u_pool, q_wa, q_na, kv = _inproj(x, mx[0], mx[1], g1, w_in, conv, rope_tabs)
    if update_ctx:
        cu_hy, cu_pool, cq_wa, cq_na, ckv = _inproj(ctx, mc[0], mc[1], g1, w_in, conv)
    else:
        ckv = _inproj(ctx, mc[0], mc[1], g1, w_in[:, OFF_KV:], kv_only=True)

    ys = [_hyena(u_hy, _hyena_filters(s, *hy_args), lyr["hy_skip"]),
          _window_attn(q_wa, kv, ckv, lyr["wa_sink"]),
          _pool_mixer(u_pool, w_bd, pscale),
          _neighbourhood_attn(q_na, kv, ckv, lyr["na_rpb"])]
    x, h, aff = _outproj(ys, x, mx[2], w_out, g2, mx[3], mx[4], rw_t)
    moe = _expert_choice_ffn(h, aff, *lyr["experts"], lyr["index"])
    x = _combine(x, *moe, mx[5], final_g)

    if update_ctx:
        ycs = [_hyena(cu_hy, _hyena_filters(lc, *hy_args), lyr["hy_skip"]),
               _ctx_attn(cq_wa, ckv, 0, 1, N_WA_KV, lyr["wa_sink"], WA_HEAD_ORDER),
               _pool_mixer(cu_pool, w_bd, pscale),
               _ctx_attn(cq_na, ckv, 1, 2, N_NA_HEADS, None)]
        ctx, hc, affc = _outproj(ycs, ctx, mc[2], w_out, g2, mc[3], mc[4], rw_t)
        moe_c = _expert_choice_ffn(hc, affc, *lyr["experts"], lyr["index"])
        ctx = _combine(ctx, *moe_c, mc[5])
    return x, ctx


def kernel(x, c, ctx, c_ctx, ada_w, ada_b, norm1_g, norm2_g, w_in, hy_conv_w, hy_conv_b, hy_w1, hy_b1, hy_w2,
           hy_b2, hy_w3, hy_freq, hy_skip, wa_sink, pool_w, pool_scale, na_rpb, w_out, router_w, exp_w_gate,
           exp_w_up, exp_w_down, final_norm_g):
    b, s, d = x.shape
    cs = jnp.zeros((MOD_ROWS, d), F32).at[:b].set(c).at[b].set(c_ctx)
    mods = _ada_mod(cs, ada_w, ada_b[:, None, :])
    rope_tabs = _rope_tables(s)
    params = dict(g1=norm1_g, g2=norm2_g, w_in=w_in, hy_conv_w=hy_conv_w, hy_conv_b=hy_conv_b, hy_w1=hy_w1,
                  hy_b1=hy_b1, hy_w2=hy_w2, hy_b2=hy_b2, hy_w3=hy_w3, hy_freq=hy_freq, hy_skip=hy_skip,
                  wa_sink=wa_sink, pool_w=pool_w, pool_scale=pool_scale, na_rpb=na_rpb, w_out=w_out,
                  router_w=router_w)
    for l in range(DEPTH):
        lyr = {k: v[l] for k, v in params.items()}
        lyr.update(index=l, experts=(exp_w_gate, exp_w_up, exp_w_down))
        last = l == DEPTH - 1
        x, ctx = _layer(x, ctx, mods[l], lyr, rope_tabs, update_ctx=not last,
                        final_g=final_norm_g[None, :] if last else None)
    return x
```

```python
import functools
import math

import jax
import jax.numpy as jnp
import numpy as np
from jax import lax
from jax.experimental import pallas as pl
from jax.experimental.pallas import tpu as pltpu

F32 = jnp.float32
BF16 = jnp.bfloat16

D_MODEL = 1024
DEPTH = 2
GRID_W = 64
HEAD_DIM = 64
W_GROUP = 256
N_WA_HEADS = 4
N_WA_KV = 2
N_NA_HEADS = 4
KV_WA = 128
OFF_POOL = 768
OFF_WA_Q = 1024
OFF_NA_Q = 1280
OFF_KV = 1536
IN_WIDTH = 2304
KV_WIDTH = IN_WIDTH - OFF_KV
HY_ORDER = 2
HY_BANDS = 16
HY_DECAY_MIN = abs(math.log(1e-2) / 1.5)
HY_DECAY_MAX = abs(math.log(1e-2) / 0.3)
WA_BLOCK = 128
WA_STEP_BLOCKS = 4
WA_HEAD_ORDER = (0, 2, 1, 3)
POOL_WINDOWS = (2, 4, 8, 16)
POOL_GROUP = 64
HALO = 8
LANES = 128
SUBLANES = 8
DFT_N2 = 128
X_PITCH = DFT_N2 + 8
S_PITCH = 64 + 8
Y_PITCH = 32 + 8
NA_ROWS = 8
NA_COLS = 16
NA_ROW_BLOCK = 8
ROPE_BASE = 10000.0
N_EXPERTS = 16
EXPERT_HIDDEN = 2048
EC_CAPACITY = 2
NORM_EPS = 1e-6
NEG_INF = -1e30
Q_SCALE = HEAD_DIM ** -0.5

FFN_ROWS = 2048
FFN_HIDDEN_TILE = 256
GATHER_CHUNK = 2048
COMBINE_TILE = 256
COMBINE_SUB = 256
COMBINE_STRIP = 64
MOD_ROWS = 16
VMEM_LIMIT = 48 * 1024 * 1024
DFT_VMEM_LIMIT = 52 * 1024 * 1024
LANE_BITS = 7
F32_VALUE_BITS = 31


def _cparams(sem, vmem=VMEM_LIMIT):
    return pltpu.CompilerParams(dimension_semantics=sem, vmem_limit_bytes=vmem)


def _nt_dot(a, b):
    return lax.dot_general(a, b, (((1,), (1,)), ((), ())), preferred_element_type=F32)


def _rmsnorm_mod(x, g, shift, scale):
    y = x * lax.rsqrt(jnp.mean(x * x, axis=-1, keepdims=True) + NORM_EPS) * g
    return y * (1.0 + scale) + shift


def _ada_kernel(c_ref, w_ref, b_ref, o_ref):
    c = c_ref[...]
    s = (c * jax.nn.sigmoid(c)).astype(BF16)
    o_ref[0] = jnp.dot(s, w_ref[0].astype(BF16), preferred_element_type=F32) + b_ref[0]


def _ada_mod(cs, ada_w, ada_b):
    nl, d, n = ada_w.shape
    tn = 1024
    return pl.pallas_call(
        _ada_kernel,
        out_shape=jax.ShapeDtypeStruct((nl, MOD_ROWS, n), F32),
        grid=(nl, n // tn),
        in_specs=[pl.BlockSpec((MOD_ROWS, d), lambda l, j: (0, 0)),
                  pl.BlockSpec((1, d, tn), lambda l, j: (l, 0, j)),
                  pl.BlockSpec((1, 1, tn), lambda l, j: (l, 0, j))],
        out_specs=pl.BlockSpec((1, MOD_ROWS, tn), lambda l, j: (l, 0, j)),
        compiler_params=_cparams(("parallel", "parallel")),
        name="ada_mod",
    )(cs, ada_w, ada_b)


def _rope(t, cos, s_up, s_dn):
    w = t.shape[-1]
    if w > 128:
        cos, s_up, s_dn = (jnp.concatenate([a] * (w // 128), axis=-1) for a in (cos, s_up, s_dn))
    up = pltpu.roll(t, w - 16, 1)
    dn = pltpu.roll(t, 16, 1)
    return t * cos + up * s_up + dn * s_dn


def _inproj_kernel(*refs, rope, kv_only):
    if kv_only:
        x_ref, sh_ref, sc_ref, g_ref, w_ref, o_ref = refs
    elif rope:
        x_ref, xp_ref, xn_ref, sh_ref, sc_ref, g_ref, w_ref, cw_ref, cb_ref, cos_ref, sup_ref, sdn_ref, *outs = refs
    else:
        x_ref, xp_ref, xn_ref, sh_ref, sc_ref, g_ref, w_ref, cw_ref, cb_ref, *outs = refs
    h = _rmsnorm_mod(x_ref[0], g_ref[...], sh_ref[0], sc_ref[0])
    p = jnp.dot(h.astype(BF16), w_ref[...], preferred_element_type=F32)
    if kv_only:
        o_ref[0] = p.astype(BF16)
        return
    hy_ref, pool_ref, qwa_ref, qna_ref, kv_ref = outs
    j, nt, tm = pl.program_id(0), pl.num_programs(0), x_ref.shape[1]
    halo = _rmsnorm_mod(jnp.concatenate([xp_ref[0], xn_ref[0]], axis=0), g_ref[...], sh_ref[0], sc_ref[0])
    ph = jnp.dot(halo.astype(BF16), w_ref[:, :OFF_POOL], preferred_element_type=F32)
    e = jnp.concatenate([jnp.where(j == 0, 0.0, ph[:HALO]), p[:, :OFF_POOL],
                         jnp.where(j == nt - 1, 0.0, ph[HALO:])], axis=0)
    n = tm + 2 * HALO
    uc = pltpu.roll(e, 1, 0) * cw_ref[0:1, :] + e * cw_ref[1:2, :] + pltpu.roll(e, n - 1, 0) * cw_ref[2:3, :]
    hy_ref[0] = uc[HALO:HALO + tm] + cb_ref[...]
    pool_ref[0] = p[:, OFF_POOL:OFF_WA_Q]
    qwa = p[:, OFF_WA_Q:OFF_NA_Q]
    kwa = p[:, OFF_KV:OFF_KV + KV_WA]
    if rope:
        tabs = (cos_ref[...], sup_ref[...], sdn_ref[...])
        qwa = _rope(qwa, *tabs)
        kwa = _rope(kwa, *tabs)
    qwa_ref[0] = (qwa * Q_SCALE).astype(BF16)
    qna_ref[0] = (p[:, OFF_NA_Q:OFF_KV] * Q_SCALE).astype(BF16)
    kv_ref[0] = jnp.concatenate([kwa, p[:, OFF_KV + KV_WA:]], axis=-1).astype(BF16)


def _inproj(x, shift, scale, g, w_bf16, conv=None, rope_tabs=None, kv_only=False):
    b, l, d = x.shape
    n = w_bf16.shape[1]
    tm = min(l, 512)
    hb, nh = tm // HALO, l // HALO
    rope = rope_tabs is not None
    bm = shift.shape[0]
    mod_map = (lambda j, i: (i, 0, 0)) if bm > 1 else (lambda j, i: (0, 0, 0))
    in_specs = [pl.BlockSpec((1, tm, d), lambda j, i: (i, j, 0))]
    args = [x]
    if not kv_only:
        in_specs += [pl.BlockSpec((1, HALO, d), lambda j, i: (i, jnp.maximum(j * hb - 1, 0), 0)),
                     pl.BlockSpec((1, HALO, d), lambda j, i: (i, jnp.minimum((j + 1) * hb, nh - 1), 0))]
        args += [x, x]
    in_specs += [pl.BlockSpec((1, 1, d), mod_map),
                 pl.BlockSpec((1, 1, d), mod_map),
                 pl.BlockSpec((1, d), lambda j, i: (0, 0)),
                 pl.BlockSpec((d, n), lambda j, i: (0, 0))]
    args += [shift, scale, g, w_bf16]
    if not kv_only:
        in_specs += [pl.BlockSpec((3, OFF_POOL), lambda j, i: (0, 0)), pl.BlockSpec((1, OFF_POOL), lambda j, i: (0, 0))]
        args += list(conv)
    if rope:
        in_specs += [pl.BlockSpec((tm, 128), lambda j, i: (j, 0))] * 3
        args += list(rope_tabs)

    def tok(width, dtype):
        return (jax.ShapeDtypeStruct((b, l, width), dtype),
                pl.BlockSpec((1, tm, width), lambda j, i: (i, j, 0)))

    if kv_only:
        outs = [tok(n, BF16)]
    else:
        outs = [tok(OFF_POOL, F32), tok(W_GROUP, F32), tok(W_GROUP, BF16), tok(W_GROUP, BF16),
                tok(KV_WIDTH, BF16)]
    res = pl.pallas_call(
        functools.partial(_inproj_kernel, rope=rope, kv_only=kv_only),
        out_shape=[o[0] for o in outs],
        grid=(l // tm, b),
        in_specs=in_specs,
        out_specs=[o[1] for o in outs],
        compiler_params=_cparams(("parallel", "parallel")),
        name="inproj_kv" if kv_only else "inproj",
    )(*args)
    return res[0] if kv_only else res


def _rope_tables(s):
    pos = jnp.arange(s)
    p2 = jnp.stack([pos // GRID_W, pos % GRID_W], axis=-1).astype(F32)
    inv = ROPE_BASE ** (-jnp.arange(16, dtype=F32) / 16)
    lane = np.arange(HEAD_DIM)
    ang = p2[:, lane // 32] * inv[lane % 16][None, :]
    first = jnp.asarray((lane % 32) < 16)[None, :]
    cos, sin = jnp.cos(ang), jnp.sin(ang)
    s_up = jnp.where(first, -sin, 0.0)
    s_dn = jnp.where(first, 0.0, sin)
    return tuple(jnp.tile(t, (1, 2)) for t in (cos, s_up, s_dn))


def _softmax_parts(parts, extra=None):
    m = parts[0].max(axis=-1, keepdims=True)
    for s in parts[1:]:
        m = jnp.maximum(m, s.max(axis=-1, keepdims=True))
    if extra is not None:
        m = jnp.maximum(m, extra)
    ps = [jnp.exp(s - m) for s in parts]
    den = ps[0].sum(axis=-1, keepdims=True)
    for p in ps[1:]:
        den = den + p.sum(axis=-1, keepdims=True)
    if extra is not None:
        den = den + jnp.exp(extra - m)
    return ps, den


def _head_stack(t, masks):
    return jnp.concatenate([jnp.where(m, t, jnp.zeros_like(t)) for m in masks], axis=0)


def _wattn_kernel(sink_ref, q_ref, kp_ref, ko_ref, kn_ref, vp_ref, vo_ref, vn_ref, kc_ref, vc_ref, o_ref, *, nb):
    n = pl.program_id(1)
    blk = WA_BLOCK
    q = q_ref[0]
    kall = jnp.concatenate([kp_ref[0], ko_ref[0], kn_ref[0]], axis=0)
    vall = jnp.concatenate([vp_ref[0], vo_ref[0], vn_ref[0]], axis=0)
    kc, vc = kc_ref[0], vc_ref[0]
    rows = N_WA_HEADS * blk
    i = lax.broadcasted_iota(jnp.int32, (rows, 3 * blk), 0) & (blk - 1)
    j = lax.broadcasted_iota(jnp.int32, (rows, 3 * blk), 1)
    band = (j >= i) & (j <= i + 2 * blk)
    kv0 = lax.broadcasted_iota(jnp.int32, (blk, KV_WA), 1) < HEAD_DIM
    head = lax.broadcasted_iota(jnp.int32, (rows, 1), 0) // blk
    snk = jnp.where(head == 0, sink_ref[0],
                    jnp.where(head == 1, sink_ref[1], jnp.where(head == 2, sink_ref[2], sink_ref[3])))
    for sb in range(WA_STEP_BLOCKS):
        gb = n * WA_STEP_BLOCKS + sb
        jlo = jnp.where(gb == 0, blk, 0)
        jhi = jnp.where(gb == nb - 1, 2 * blk, 3 * blk)
        valid = band & (j >= jlo) & (j < jhi)
        keys = slice(sb * blk, (sb + 3) * blk)
        qa = q[sb * blk:(sb + 1) * blk, :KV_WA]
        qb = q[sb * blk:(sb + 1) * blk, KV_WA:]
        zero = jnp.zeros_like(qa)
        qs = jnp.concatenate([jnp.where(kv0, qa, zero), jnp.where(kv0, qb, zero),
                              jnp.where(kv0, zero, qa), jnp.where(kv0, zero, qb)], axis=0)
        s_loc = jnp.where(valid, _nt_dot(qs, kall[keys]), NEG_INF)
        s_ctx = _nt_dot(qs, kc)
        (p_loc, p_ctx), den = _softmax_parts([s_loc, s_ctx], snk)
        o = (jnp.dot(p_loc.astype(BF16), vall[keys], preferred_element_type=F32)
             + jnp.dot(p_ctx.astype(BF16), vc, preferred_element_type=F32)) / den
        o_ref[0, sb * blk:(sb + 1) * blk, :] = jnp.concatenate(
            [jnp.where(kv0, o[:blk], o[2 * blk:3 * blk]), jnp.where(kv0, o[blk:2 * blk], o[3 * blk:])],
            axis=-1).astype(BF16)


def _window_attn(q, kv, ckv, sink):
    b, s, _ = q.shape
    lc = ckv.shape[1]
    nb = s // WA_BLOCK
    sbk = WA_STEP_BLOCKS

    def halo_spec(col, off):
        return pl.BlockSpec((1, WA_BLOCK, KV_WA),
                            lambda i, n: (i, jnp.clip(n * sbk + off, 0, nb - 1), col))

    def own_spec(col):
        return pl.BlockSpec((1, sbk * WA_BLOCK, KV_WA), lambda i, n: (i, n, col))

    return pl.pallas_call(
        functools.partial(_wattn_kernel, nb=nb),
        out_shape=jax.ShapeDtypeStruct((b, s, W_GROUP), BF16),
        grid=(b, nb // sbk),
        in_specs=[pl.BlockSpec(memory_space=pltpu.SMEM),
                  pl.BlockSpec((1, sbk * WA_BLOCK, W_GROUP), lambda i, n: (i, n, 0)),
                  halo_spec(0, -1), own_spec(0), halo_spec(0, sbk),
                  halo_spec(1, -1), own_spec(1), halo_spec(1, sbk),
                  pl.BlockSpec((1, lc, KV_WA), lambda i, n: (i, 0, 0)),
                  pl.BlockSpec((1, lc, KV_WA), lambda i, n: (i, 0, 1))],
        out_specs=pl.BlockSpec((1, sbk * WA_BLOCK, W_GROUP), lambda i, n: (i, n, 0)),
        compiler_params=_cparams(("parallel", "parallel")),
        name="window_attn",
    )(sink, q, kv, kv, kv, kv, kv, kv, ckv, ckv)


def _nattn_kernel(q_ref, k_ref, v_ref, kc_ref, vc_ref, bias_ref, o_ref, *, rows):
    blk = pl.program_id(1)
    nk = NA_ROWS * GRID_W
    kc, vc = kc_ref[0], vc_ref[0]
    lane_head = lax.broadcasted_iota(jnp.int32, (GRID_W, W_GROUP), 1) // HEAD_DIM
    masks = [lane_head == h for h in range(N_NA_HEADS)]
    for rr in range(NA_ROW_BLOCK):
        r = blk * NA_ROW_BLOCK + rr
        r0 = jnp.clip(r - NA_ROWS // 2, 0, rows - NA_ROWS)
        var = r - r0
        start = pl.multiple_of(r0 * GRID_W, GRID_W)
        kt = k_ref[0, pl.ds(start, nk), :]
        vt = v_ref[0, pl.ds(start, nk), :]
        qrow = slice(rr * GRID_W, (rr + 1) * GRID_W)
        qs = _head_stack(q_ref[0, qrow, :], masks)
        bias = bias_ref[:, var].reshape(N_NA_HEADS * GRID_W, nk)
        (p_loc, p_ctx), den = _softmax_parts([_nt_dot(qs, kt) + bias, _nt_dot(qs, kc)])
        o = (jnp.dot(p_loc.astype(BF16), vt, preferred_element_type=F32)
             + jnp.dot(p_ctx.astype(BF16), vc, preferred_element_type=F32)) / den
        out = o[(N_NA_HEADS - 1) * GRID_W:]
        for h in range(N_NA_HEADS - 2, -1, -1):
            out = jnp.where(masks[h], o[h * GRID_W:(h + 1) * GRID_W], out)
        o_ref[0, qrow, :] = out.astype(BF16)


def _na_bias(rpb):
    var = np.arange(NA_ROWS)
    j = np.arange(NA_ROWS)
    qc = np.arange(GRID_W)
    kc = np.arange(GRID_W)
    dr = j[None, :] - var[:, None] + NA_ROWS - 1
    dc = np.clip(kc[None, :] - qc[:, None] + NA_COLS - 1, 0, 2 * NA_COLS - 2)
    ws = np.clip(qc - NA_COLS // 2, 0, GRID_W - NA_COLS)
    ok = (kc[None, :] >= ws[:, None]) & (kc[None, :] < ws[:, None] + NA_COLS)
    onehot = (dc[None] == np.arange(2 * NA_COLS - 1)[:, None, None]).astype(np.float32)
    bias = jnp.einsum("hvjd,dqk->hvqjk", rpb.astype(F32)[:, dr], jnp.asarray(onehot),
                      precision=lax.Precision.HIGHEST)
    bias = jnp.where(jnp.asarray(ok)[None, None, :, None, :], bias, NEG_INF)
    return bias.reshape(rpb.shape[0], NA_ROWS, GRID_W, NA_ROWS * GRID_W)


def _neighbourhood_attn(q, kv, ckv, rpb):
    b, s, _ = q.shape
    lc = ckv.shape[1]
    rows = s // GRID_W
    bias = _na_bias(rpb)
    return pl.pallas_call(
        functools.partial(_nattn_kernel, rows=rows),
        out_shape=jax.ShapeDtypeStruct((b, s, W_GROUP), BF16),
        grid=(b, rows // NA_ROW_BLOCK),
        in_specs=[pl.BlockSpec((1, NA_ROW_BLOCK * GRID_W, W_GROUP), lambda i, r: (i, r, 0)),
                  pl.BlockSpec((1, s, W_GROUP), lambda i, r: (i, 0, 1)),
                  pl.BlockSpec((1, s, W_GROUP), lambda i, r: (i, 0, 2)),
                  pl.BlockSpec((1, lc, W_GROUP), lambda i, r: (i, 0, 1)),
                  pl.BlockSpec((1, lc, W_GROUP), lambda i, r: (i, 0, 2)),
                  pl.BlockSpec(bias.shape, lambda i, r: (0, 0, 0, 0))],
        out_specs=pl.BlockSpec((1, NA_ROW_BLOCK * GRID_W, W_GROUP), lambda i, r: (i, r, 0)),
        compiler_params=_cparams(("parallel", "arbitrary")),
        name="neighbourhood_attn",
    )(q, kv, kv, ckv, ckv, bias)


def _cattn_kernel(*refs, n_kv, with_sink, head_order):
    if with_sink:
        sink_ref, q_ref, k_ref, v_ref, o_ref = refs
    else:
        q_ref, k_ref, v_ref, o_ref = refs
    q, k, v = q_ref[0], k_ref[0], v_ref[0]
    group = N_WA_HEADS // n_kv
    outs = []
    for pos, h in enumerate(head_order):
        sl = slice((h // group) * HEAD_DIM, (h // group + 1) * HEAD_DIM)
        s = _nt_dot(q[:, pos * HEAD_DIM:(pos + 1) * HEAD_DIM], k[:, sl])
        extra = jnp.full((s.shape[0], 1), sink_ref[h], F32) if with_sink else None
        (p,), den = _softmax_parts([s], extra)
        outs.append(jnp.dot(p.astype(BF16), v[:, sl], preferred_element_type=F32) / den)
    o_ref[0] = jnp.concatenate(outs, axis=-1).astype(BF16)


def _ctx_attn(q, ckv, k_col, v_col, n_kv, sink, head_order=(0, 1, 2, 3)):
    b, lc, _ = q.shape
    w = n_kv * HEAD_DIM
    with_sink = sink is not None
    in_specs = [pl.BlockSpec((1, lc, W_GROUP), lambda i: (i, 0, 0)),
                pl.BlockSpec((1, lc, w), lambda i: (i, 0, k_col)),
                pl.BlockSpec((1, lc, w), lambda i: (i, 0, v_col))]
    args = [q, ckv, ckv]
    if with_sink:
        in_specs = [pl.BlockSpec(memory_space=pltpu.SMEM)] + in_specs
        args = [sink] + args
    return pl.pallas_call(
        functools.partial(_cattn_kernel, n_kv=n_kv, with_sink=with_sink, head_order=head_order),
        out_shape=jax.ShapeDtypeStruct((b, lc, W_GROUP), BF16),
        grid=(b,),
        in_specs=in_specs,
        out_specs=pl.BlockSpec((1, lc, W_GROUP), lambda i: (i, 0, 0)),
        compiler_params=_cparams(("parallel",)),
        name="ctx_attn",
    )(*args)


def _pool_kernel(prev_ref, cur_ref, next_ref, w_ref, scale_ref, o_ref, *, seq, tl):
    j = pl.program_id(1)
    nt = seq // tl
    cur = cur_ref[0]
    prev = jnp.where(j == 0, 0.0, prev_ref[0])
    nxt = jnp.where(j == nt - 1, 0.0, next_ref[0])
    e = jnp.concatenate([prev, cur, nxt], axis=0)
    n = tl + 2 * HALO

    def sh(a, d):
        return pltpu.roll(a, d % n, 0)

    s2 = e + sh(e, 1)
    s4 = sh(s2, 1) + sh(s2, -1)
    s8 = sh(s4, 2) + sh(s4, -2)
    s16 = sh(s8, 4) + sh(s8, -4)
    lane = lax.broadcasted_iota(jnp.int32, (tl, W_GROUP), 1)
    t = lax.broadcasted_iota(jnp.int32, (tl, W_GROUP), 0) + j * tl
    g = lane // POOL_GROUP
    lo, hi = HALO, HALO + tl
    ssum = jnp.where(g == 0, s2[lo:hi], jnp.where(g == 1, s4[lo:hi], jnp.where(g == 2, s8[lo:hi], s16[lo:hi])))
    half = jnp.where(g == 0, 1, jnp.where(g == 1, 2, jnp.where(g == 2, 4, 8)))
    cnt = (jnp.minimum(t + half, seq) - jnp.maximum(t - half, 0)).astype(F32)
    d = (ssum / cnt - cur).astype(BF16)
    o_ref[0] = (jnp.dot(d, w_ref[...], preferred_element_type=F32) * scale_ref[...]).astype(BF16)


def _pool_mixer(u, w_bd, scale):
    b, l, c = u.shape
    tl = min(l, 512)
    hb = tl // HALO
    nh = l // HALO
    return pl.pallas_call(
        functools.partial(_pool_kernel, seq=l, tl=tl),
        out_shape=jax.ShapeDtypeStruct((b, l, c), BF16),
        grid=(b, l // tl),
        in_specs=[pl.BlockSpec((1, HALO, c), lambda i, j: (i, jnp.maximum(j * hb - 1, 0), 0)),
                  pl.BlockSpec((1, tl, c), lambda i, j: (i, j, 0)),
                  pl.BlockSpec((1, HALO, c), lambda i, j: (i, jnp.minimum((j + 1) * hb, nh - 1), 0)),
                  pl.BlockSpec((c, c), lambda i, j: (0, 0)),
                  pl.BlockSpec((1, c), lambda i, j: (0, 0))],
        out_specs=pl.BlockSpec((1, tl, c), lambda i, j: (i, j, 0)),
        compiler_params=_cparams(("parallel", "parallel")),
        name="pool_mixer",
    )(u, u, u, w_bd, scale)


def _pool_weight(pool_w):
    z = jnp.zeros((W_GROUP, W_GROUP), F32)
    for g in range(len(POOL_WINDOWS)):
        z = z.at[g * POOL_GROUP:(g + 1) * POOL_GROUP, g * POOL_GROUP:(g + 1) * POOL_GROUP].set(pool_w[g])
    return z.astype(BF16)


def _outproj_kernel(yh_ref, ya_ref, yp_ref, yn_ref, x_ref, gt_ref, w_ref, g2_ref, sh_ref, sc_ref, rw_ref,
                    xo_ref, h_ref, aff_ref):
    y = jnp.concatenate([yh_ref[0], ya_ref[0], yp_ref[0], yn_ref[0]], axis=-1)
    x = x_ref[0] + gt_ref[0] * jnp.dot(y, w_ref[...], preferred_element_type=F32)
    xo_ref[0] = x
    h = _rmsnorm_mod(x, g2_ref[...], sh_ref[0], sc_ref[0])
    nsl = h.shape[1] // LANES
    for s in range(nsl):
        h_ref[0, pl.ds(s, h.shape[0], stride=nsl), :] = h[:, s * LANES:(s + 1) * LANES]
    logits = _nt_dot(rw_ref[...], h.astype(BF16))
    m = logits.max(axis=0, keepdims=True)
    p = jnp.exp(logits - m)
    aff_ref[0] = p / p.sum(axis=0, keepdims=True)


def _outproj(ys, x, gate, w_bf16, g2, shift, scale, rw_t):
    b, l, d = x.shape
    tm = min(l, 512)
    bm = gate.shape[0]
    mod_map = (lambda i, j: (i, 0, 0)) if bm > 1 else (lambda i, j: (0, 0, 0))
    tok = lambda w: pl.BlockSpec((1, tm, w), lambda i, j: (i, j, 0))
    mod = pl.BlockSpec((1, 1, d), mod_map)
    return pl.pallas_call(
        _outproj_kernel,
        out_shape=[jax.ShapeDtypeStruct((b, l, d), F32), jax.ShapeDtypeStruct((b, l * d // LANES, LANES), F32),
                   jax.ShapeDtypeStruct((b, N_EXPERTS, l), F32)],
        grid=(b, l // tm),
        in_specs=[tok(W_GROUP)] * 4 + [tok(d), mod, pl.BlockSpec((d, d), lambda i, j: (0, 0)),
                                        pl.BlockSpec((1, d), lambda i, j: (0, 0)), mod, mod,
                                        pl.BlockSpec((N_EXPERTS, d), lambda i, j: (0, 0))],
        out_specs=[tok(d), pl.BlockSpec((1, tm * d // LANES, LANES), lambda i, j: (i, j, 0)),
                   pl.BlockSpec((1, N_EXPERTS, tm), lambda i, j: (i, 0, j))],
        compiler_params=_cparams(("parallel", "parallel")),
        name="outproj_router",
    )(*ys, x, gate, w_bf16, g2, shift, scale, rw_t)


def _row_gather_kernel(idx_ref, nxt_ref, table_ref, out_ref, stage, sems):
    i, n = pl.program_id(0), pl.num_programs(0)
    cur = i % 2
    chunk, d = out_ref.shape
    nsl = d // LANES

    def issue(ref, s):
        def one(r, c):
            pltpu.make_async_copy(table_ref.at[pl.ds(pl.multiple_of(ref[0, 0, r] * nsl, nsl), nsl)],
                                  stage.at[s, pl.ds(pl.multiple_of(r * nsl, nsl), nsl)], sems.at[s]).start()
            return c
        lax.fori_loop(0, chunk, one, 0, unroll=8)

    @pl.when(i == 0)
    def _():
        issue(idx_ref, cur)

    @pl.when(i + 1 < n)
    def _():
        issue(nxt_ref, 1 - cur)

    pltpu.make_async_copy(table_ref.at[pl.ds(0, chunk * nsl)], stage.at[cur], sems.at[cur]).wait()
    for s in range(nsl):
        out_ref[:, s * LANES:(s + 1) * LANES] = stage[cur, pl.ds(s, chunk, stride=nsl), :].astype(BF16)


def _row_gather(table, rows, d):
    n = rows.shape[0]
    chunk = min(GATHER_CHUNK, n)
    nch = n // chunk
    idx_spec = lambda off: pl.BlockSpec((1, 1, chunk), lambda i: (jnp.minimum(i + off, nch - 1), 0, 0),
                                        memory_space=pltpu.SMEM)
    rows3 = rows.reshape(nch, 1, chunk)
    return pl.pallas_call(
        _row_gather_kernel,
        out_shape=jax.ShapeDtypeStruct((n, d), BF16),
        grid=(nch,),
        in_specs=[idx_spec(0), idx_spec(1), pl.BlockSpec(memory_space=pl.ANY)],
        out_specs=pl.BlockSpec((chunk, d), lambda i: (i, 0)),
        scratch_shapes=[pltpu.VMEM((2, chunk * d // LANES, LANES), table.dtype), pltpu.SemaphoreType.DMA((2,))],
        compiler_params=pltpu.CompilerParams(dimension_semantics=("arbitrary",), vmem_limit_bytes=VMEM_LIMIT,
                                             disable_bounds_checks=True),
        name="moe_row_gather",
    )(rows3, rows3, table)


def _moe_kernel(x_ref, g_ref, wg_ref, wu_ref, wd_ref, o_ref):
    f = pl.program_id(2)

    @pl.when(f == 0)
    def _():
        o_ref[0] = jnp.zeros(o_ref.shape[1:], F32)

    x = x_ref[0]
    a = jnp.dot(x, wg_ref[0, 0].astype(BF16), preferred_element_type=F32)
    u = jnp.dot(x, wu_ref[0, 0].astype(BF16), preferred_element_type=F32)
    hid = (a * jax.nn.sigmoid(a) * u).astype(BF16)
    o_ref[0] += jnp.dot(hid, wd_ref[0, 0].astype(BF16), preferred_element_type=F32)

    @pl.when(f == pl.num_programs(2) - 1)
    def _():
        o_ref[0] = o_ref[0] * g_ref[0]


def _moe_ffn(xg, gate, w_gate, w_up, w_down, lyr):
    e, m, d = xg.shape
    f = w_gate.shape[3]
    tm = min(m, FFN_ROWS)
    tf = FFN_HIDDEN_TILE
    return pl.pallas_call(
        _moe_kernel,
        out_shape=jax.ShapeDtypeStruct((e, m, d), F32),
        grid=(e, m // tm, f // tf),
        in_specs=[pl.BlockSpec((1, tm, d), lambda i, j, k: (i, j, 0)),
                  pl.BlockSpec((1, tm, 1), lambda i, j, k: (i, j, 0)),
                  pl.BlockSpec((1, 1, d, tf), lambda i, j, k: (lyr, i, 0, k)),
                  pl.BlockSpec((1, 1, d, tf), lambda i, j, k: (lyr, i, 0, k)),
                  pl.BlockSpec((1, 1, tf, d), lambda i, j, k: (lyr, i, k, 0))],
        out_specs=pl.BlockSpec((1, tm, d), lambda i, j, k: (i, j, 0)),
        compiler_params=_cparams(("parallel", "parallel", "arbitrary")),
        name="moe_ffn",
    )(xg, gate, w_gate, w_up, w_down)


def _route_kernel(aff_ref, tri_ref, idx_ref, g_ref, *, k):
    x = aff_ref[...]
    r, l = x.shape
    nb = l // LANES
    nq = idx_ref.shape[1] // LANES
    def count(mask):
        return jnp.sum(jnp.where(mask, 1.0, 0.0), axis=-1, keepdims=True)

    def as_float(pattern):
        return pltpu.bitcast(jnp.broadcast_to(pattern, (r, LANES)), F32)[:, :1]

    def bisect(i, t):
        cand = t | jnp.left_shift(jnp.int32(1), F32_VALUE_BITS - 1 - i)
        return jnp.where(count(x >= as_float(cand)) >= k, cand, t)

    kth = lax.fori_loop(0, F32_VALUE_BITS, bisect, jnp.zeros((r, 1), jnp.int32))
    gt = x >= as_float(kth + 1)
    eq = (x >= as_float(kth)) & jnp.logical_not(gt)
    need = k - count(gt)
    tri = tri_ref[...]
    lane = lax.broadcasted_iota(jnp.int32, (r, LANES), 1)

    def block_prefix(mask, j):
        m = jnp.where(mask[:, j * LANES:(j + 1) * LANES], 1.0, 0.0)
        return jnp.dot(m.astype(BF16), tri, preferred_element_type=F32), m

    out_tok = [jnp.zeros((r, LANES), jnp.int32) for _ in range(nq)]
    out_g = [jnp.zeros((r, LANES), F32) for _ in range(nq)]
    eq_before = jnp.zeros((r, 1), F32)
    start = jnp.zeros((r, 1), jnp.int32)
    for j in range(nb):
        blk = slice(j * LANES, (j + 1) * LANES)
        eq_inc, eq_f = block_prefix(eq, j)
        sel = gt[:, blk] | (eq[:, blk] & (eq_inc - eq_f + eq_before < need))
        eq_before = eq_before + eq_inc[:, LANES - 1:]
        sel_f = jnp.where(sel, 1.0, 0.0)
        sel_inc = jnp.dot(sel_f.astype(BF16), tri, preferred_element_type=F32)
        cnt = sel_inc[:, LANES - 1:].astype(jnp.int32)
        d = jnp.where(sel, lane - (sel_inc - sel_f).astype(jnp.int32), -1)
        tok = lane + j * LANES
        gv = x[:, blk]
        for kb in range(LANE_BITS):
            sh = LANES - (1 << kb)
            rd, rt, rg = pltpu.roll(d, sh, 1), pltpu.roll(tok, sh, 1), pltpu.roll(gv, sh, 1)
            arrive = (rd >= 0) & (((rd >> kb) & 1) == 1)
            stay = (d >= 0) & (((d >> kb) & 1) == 0)
            tok = jnp.where(arrive, rt, tok)
            gv = jnp.where(arrive, rg, gv)
            d = jnp.where(arrive, rd, jnp.where(stay, d, -1))
        s, q = start & (LANES - 1), start >> LANE_BITS
        for kb in range(LANE_BITS):
            mv = ((s >> kb) & 1) == 1
            tok = jnp.where(mv, pltpu.roll(tok, 1 << kb, 1), tok)
            gv = jnp.where(mv, pltpu.roll(gv, 1 << kb, 1), gv)
        end = s + cnt
        here = (lane >= s) & (lane < end)
        wrapped = lane < end - LANES
        for qq in range(nq):
            hit = (here & (q == qq)) | (wrapped & (q + 1 == qq))
            out_tok[qq] = jnp.where(hit, tok, out_tok[qq])
            out_g[qq] = jnp.where(hit, gv, out_g[qq])
        start = start + cnt
    idx_ref[...] = jnp.concatenate(out_tok, axis=1)
    g_ref[...] = jnp.concatenate(out_g, axis=1)


def _route(aff_t, k):
    b, e, l = aff_t.shape
    r = b * e
    width = -(-k // LANES) * LANES
    tri = jnp.asarray(np.triu(np.ones((LANES, LANES), np.float32))).astype(BF16)
    idx, g = pl.pallas_call(
        functools.partial(_route_kernel, k=k),
        out_shape=[jax.ShapeDtypeStruct((r, width), jnp.int32), jax.ShapeDtypeStruct((r, width), F32)],
        compiler_params=pltpu.CompilerParams(vmem_limit_bytes=VMEM_LIMIT),
        name="expert_choice_route",
    )(aff_t.reshape(r, l), tri)
    return idx[:, :k].reshape(b, e, k), g[:, :k].reshape(b, e, k)


def _expert_choice_ffn(h, aff_t, w_gate, w_up, w_down, lyr):
    b, _, l = aff_t.shape
    d = h.shape[1] * LANES // l
    cap = EC_CAPACITY * l // N_EXPERTS
    m = b * cap
    idx, g = _route(aff_t, cap)
    idx_t = jnp.swapaxes(idx, 0, 1)
    rows = (idx_t + (jnp.arange(b, dtype=jnp.int32) * l)[None, :, None]).reshape(-1)
    xg = _row_gather(h.reshape(-1, LANES), rows, d).reshape(N_EXPERTS, m, d)
    gate = jnp.swapaxes(g, 0, 1).reshape(N_EXPERTS, m, 1)
    o = _moe_ffn(xg, gate, w_gate, w_up, w_down, lyr).reshape(N_EXPERTS * m, d)
    return o, idx


def _combine_kernel(*refs, final, tile, nt, cap, strip, m):
    if final:
        lo_ref, idx_ref, x_ref, gt_ref, g_ref, o_hbm, out_ref, stage, sems = refs
    else:
        lo_ref, idx_ref, x_ref, gt_ref, o_hbm, out_ref, stage, sems = refs
    bi, j = pl.program_id(0), pl.program_id(1)
    step = bi * nt + j
    cur = step % 2
    ne = N_EXPERTS
    gps = strip // SUBLANES
    per_sub = COMBINE_SUB // strip
    win = 2 * LANES
    sentinel = tile * nt

    def runs(jj):
        out = []
        for e in range(ne):
            lo = lo_ref[bi, e * (nt + 1) + jj]
            hi = lo_ref[bi, e * (nt + 1) + jj + 1]
            a8 = (lo // SUBLANES) * SUBLANES
            out.append((a8, jnp.where(hi > lo, hi - a8, 0)))
        return out

    def groups(run, r):
        return [(a8 + strip * r, jnp.clip((n - strip * r + SUBLANES - 1) // SUBLANES, 0, gps)) for a8, n in run]

    def issue(s, grp):
        for e, (s0, ng) in enumerate(grp):
            base = e * m + bi * cap + s0

            def one(k, c, base=base, e=e):
                pltpu.make_async_copy(
                    o_hbm.at[pl.ds(pl.multiple_of(base + k * SUBLANES, SUBLANES), SUBLANES)],
                    stage.at[s, pl.ds(pl.multiple_of(e * strip + k * SUBLANES, SUBLANES), SUBLANES)],
                    sems.at[s]).start()
                return c
            lax.fori_loop(0, ng, one, 0)

    def wait(s, grp):
        cnt = sum(ng for _, ng in grp) * SUBLANES

        @pl.when(cnt > 0)
        def _():
            rows = pl.ds(0, pl.multiple_of(cnt, SUBLANES))
            pltpu.make_async_copy(o_hbm.at[rows], stage.at[s, rows], sems.at[s]).wait()

    def reduce(s, grp):
        t_ids = lax.broadcasted_iota(jnp.int32, (tile, COMBINE_SUB), 0) + j * tile
        acc = jnp.zeros((tile, out_ref.shape[2]), F32)
        for sub in range(ne // per_sub):
            pieces = []
            for e in range(sub * per_sub, (sub + 1) * per_sub):
                s0, ng = grp[e]
                a = jnp.minimum((s0 // LANES) * LANES, idx_ref.shape[2] - win)
                w = idx_ref[0, e:e + 1, pl.ds(pl.multiple_of(a, LANES), win)]
                w = pltpu.roll(w, (win - (s0 - a)) % win, 1)
                pieces.append(jnp.where(ng > 0, w[:, :strip], sentinel))
            toks = jnp.concatenate(pieces, axis=1)
            rows = stage[s, sub * COMBINE_SUB:(sub + 1) * COMBINE_SUB, :]
            onehot = jnp.where(toks == t_ids, 1.0, 0.0).astype(BF16)
            hi = rows.astype(BF16)
            lo = (rows - hi.astype(F32)).astype(BF16)
            acc = (acc + jnp.dot(onehot, hi, preferred_element_type=F32)
                   + jnp.dot(onehot, lo, preferred_element_type=F32))
        return acc

    @pl.when(step == 0)
    def _():
        stage[...] = jnp.zeros(stage.shape, F32)

    run = runs(j)
    first = groups(run, 0)

    @pl.when(j == 0)
    def _():
        issue(cur, first)

    @pl.when(j + 1 < nt)
    def _():
        issue(1 - cur, groups(runs(jnp.minimum(j + 1, nt - 1)), 0))

    wait(cur, first)
    acc = reduce(cur, first)
    longest = run[0][1]
    for _, n in run[1:]:
        longest = jnp.maximum(longest, n)

    def extra_round(r, acc):
        grp = groups(run, r)
        issue(cur, grp)
        wait(cur, grp)
        return acc + reduce(cur, grp)

    acc = lax.fori_loop(1, (longest + strip - 1) // strip, extra_round, acc)
    x = x_ref[0] + gt_ref[0] * acc
    if final:
        x = x * lax.rsqrt(jnp.mean(x * x, axis=-1, keepdims=True) + NORM_EPS) * g_ref[...]
    out_ref[0] = x


def _combine(x, o, idx, gate, final_g=None):
    b, l, d = x.shape
    ne, cap = idx.shape[1], idx.shape[2]
    tile = min(l, COMBINE_TILE)
    nt = l // tile
    strip = min(COMBINE_STRIP, cap)
    bounds = jnp.arange(nt + 1, dtype=jnp.int32) * tile
    lo = jnp.sum(idx[:, :, None, :] < bounds[None, None, :, None], axis=-1, dtype=jnp.int32)
    cp = -(-cap // LANES) * LANES + LANES
    idx_p = jnp.pad(idx, ((0, 0), (0, 0), (0, cp - cap)), constant_values=l)
    bm = gate.shape[0]
    mod_map = (lambda i, j: (i, 0, 0)) if bm > 1 else (lambda i, j: (0, 0, 0))
    tok = pl.BlockSpec((1, tile, d), lambda i, j: (i, j, 0))
    final = final_g is not None
    in_specs = [pl.BlockSpec(memory_space=pltpu.SMEM),
                pl.BlockSpec((1, ne, cp), lambda i, j: (i, 0, 0)),
                tok, pl.BlockSpec((1, 1, d), mod_map)]
    args = [lo.reshape(b, ne * (nt + 1)), idx_p, x, gate]
    if final:
        in_specs.append(pl.BlockSpec((1, d), lambda i, j: (0, 0)))
        args.append(final_g)
    in_specs.append(pl.BlockSpec(memory_space=pl.ANY))
    args.append(o)
    return pl.pallas_call(
        functools.partial(_combine_kernel, final=final, tile=tile, nt=nt, cap=cap, strip=strip, m=b * cap),
        out_shape=jax.ShapeDtypeStruct((b, l, d), F32),
        grid=(b, nt),
        in_specs=in_specs,
        out_specs=tok,
        scratch_shapes=[pltpu.VMEM((2, ne * strip, d), F32), pltpu.SemaphoreType.DMA((2,))],
        compiler_params=pltpu.CompilerParams(dimension_semantics=("arbitrary", "arbitrary"),
                                             vmem_limit_bytes=VMEM_LIMIT, disable_bounds_checks=True),
        name="moe_combine_final" if final else "moe_combine",
    )(*args)


def _hyena_filters(l, w1, b1, w2, b2, w3, freq):
    hp = lax.Precision.HIGHEST
    bands = jnp.linspace(1e-4, HY_BANDS - 1, HY_BANDS, dtype=F32)
    deltas = jnp.linspace(HY_DECAY_MIN, HY_DECAY_MAX, W_GROUP, dtype=F32)

    def at(pos):
        t = pos / max(l - 1, 1)
        ang = (2 * math.pi / l) * pos[:, None] * bands[None, :]
        feats = jnp.concatenate([t[:, None], jnp.cos(ang), -jnp.sin(ang)], axis=-1)
        h = jnp.sin(freq * (jnp.dot(feats, w1, precision=hp) + b1))
        h = jnp.sin(freq * (jnp.dot(h, w2, precision=hp) + b2))
        h = jnp.dot(h, w3, precision=hp).reshape(l, HY_ORDER, 2, W_GROUP)
        return h * jnp.exp(-t[:, None] * deltas[None, :])[:, None, None, :]

    lag = jnp.arange(l)
    fwd = at(lag.astype(F32))
    rev = at(((l - lag) % l).astype(F32))
    first = (lag == 0)[:, None, None]
    half0 = fwd[:, :, 0] + jnp.where(first, fwd[:, :, 1], 0.0)
    half1 = jnp.where(first, 0.0, rev[:, :, 1])
    return jnp.transpose(jnp.stack([half0, half1], axis=0), (2, 0, 1, 3))


def _cplx_block(e):
    return np.block([[e.real, -e.imag], [e.imag, e.real]])


@functools.lru_cache(maxsize=None)
def _dft_consts(l):
    n = 2 * l
    n1f = n // DFT_N2
    n1h = n1f // 2
    k1, n1, n2 = np.arange(n1f), np.arange(n1h), np.arange(DFT_N2)
    ph = np.outer(k1, n1)[None] / n1f + (n2[:, None, None] * k1[None, :, None]) / n
    e1 = np.exp(-2j * np.pi * ph)
    w1 = np.stack([_cplx_block(e1[i]) for i in range(DFT_N2)])
    e2 = np.exp(-2j * np.pi * np.outer(n2, n2) / DFT_N2)
    f2 = _cplx_block(e2)
    g2 = _cplx_block(np.conj(e2).T)
    g1 = np.stack([_cplx_block(np.conj(e1[i]).T / n) for i in range(DFT_N2)])
    return tuple(jnp.asarray(a, F32).astype(BF16) for a in (w1, f2, g2, g1))


def _lconv_kernel(a_ref, hf_ref, w1_ref, f2_ref, g2_ref, g1_ref, y_ref, xy_ref, s_ref, *, n1h):
    n1f = 2 * n1h

    def copy_in(i, c):
        for ri in range(2):
            xy_ref[ri, pl.ds(pl.multiple_of(i * X_PITCH, 8), DFT_N2), :] = \
                a_ref[0, ri, pl.ds(pl.multiple_of(i * DFT_N2, DFT_N2), DFT_N2), :]
        return c

    lax.fori_loop(0, n1h, copy_in, 0, unroll=4)

    def stage1(n2, c):
        slab = jnp.concatenate([xy_ref[0, pl.ds(n2, n1h, stride=X_PITCH), :],
                                xy_ref[1, pl.ds(n2, n1h, stride=X_PITCH), :]], axis=0).astype(BF16)
        a = jnp.dot(w1_ref[n2], slab, preferred_element_type=F32)
        base = pl.multiple_of(n2 * S_PITCH, 8)
        s_ref[0, pl.ds(base, n1f), :] = a[:n1f]
        s_ref[1, pl.ds(base, n1f), :] = a[n1f:]
        return c

    lax.fori_loop(0, DFT_N2, stage1, 0, unroll=32)

    def stage2(k1, c):
        slab = jnp.concatenate([s_ref[0, pl.ds(k1, DFT_N2, stride=S_PITCH), :],
                                s_ref[1, pl.ds(k1, DFT_N2, stride=S_PITCH), :]], axis=0).astype(BF16)
        x = jnp.dot(f2_ref[...], slab, preferred_element_type=F32)
        xr, xi = x[:DFT_N2], x[DFT_N2:]
        hr, hi = hf_ref[0, k1, 0], hf_ref[0, k1, 1]
        y = jnp.concatenate([xr * hr - xi * hi, xr * hi + xi * hr], axis=0).astype(BF16)
        cc = jnp.dot(g2_ref[...], y, preferred_element_type=F32)
        s_ref[0, pl.ds(k1, DFT_N2, stride=S_PITCH), :] = cc[:DFT_N2]
        s_ref[1, pl.ds(k1, DFT_N2, stride=S_PITCH), :] = cc[DFT_N2:]
        return c

    lax.fori_loop(0, n1f, stage2, 0, unroll=16)

    def stage3(n2, c):
        base = pl.multiple_of(n2 * S_PITCH, 8)
        d = jnp.concatenate([s_ref[0, pl.ds(base, n1f), :], s_ref[1, pl.ds(base, n1f), :]], axis=0).astype(BF16)
        yv = jnp.dot(g1_ref[n2], d, preferred_element_type=F32)
        yb = pl.multiple_of(n2 * Y_PITCH, 8)
        xy_ref[0, pl.ds(yb, n1h), :] = yv[:n1h]
        xy_ref[1, pl.ds(yb, n1h), :] = yv[n1h:]
        return c

    lax.fori_loop(0, DFT_N2, stage3, 0, unroll=32)

    def copy_out(i, c):
        for ri in range(2):
            y_ref[0, ri, pl.ds(pl.multiple_of(i * DFT_N2, DFT_N2), DFT_N2), :] = \
                xy_ref[ri, pl.ds(i, DFT_N2, stride=Y_PITCH), :]
        return c

    lax.fori_loop(0, n1h, copy_out, 0, unroll=4)


def _lconv_small_kernel(a_ref, hf_ref, f_ref, g_ref, y_ref):
    l = a_ref.shape[2]
    slab = jnp.concatenate([a_ref[0, 0], a_ref[0, 1]], axis=0).astype(BF16)
    x = jnp.dot(f_ref[...], slab, preferred_element_type=F32)
    xr, xi = x[:2 * l], x[2 * l:]
    hr, hi = hf_ref[0, 0], hf_ref[0, 1]
    y = jnp.concatenate([xr * hr - xi * hi, xr * hi + xi * hr], axis=0).astype(BF16)
    out = jnp.dot(g_ref[...], y, preferred_element_type=F32)
    y_ref[0, 0] = out[:l]
    y_ref[0, 1] = out[l:]


@functools.lru_cache(maxsize=None)
def _dft_consts_small(l):
    n = 2 * l
    e = np.exp(-2j * np.pi * np.outer(np.arange(n), np.arange(l)) / n)
    f = _cplx_block(e)
    g = _cplx_block(np.conj(e).T / n)
    return jnp.asarray(f, F32).astype(BF16), jnp.asarray(g, F32).astype(BF16)


def _long_conv(a, hf, order):
    b, l, ca = a.shape
    c = W_GROUP
    a4 = a.reshape(b // 2, 2, l, ca)
    io_spec = pl.BlockSpec((1, 2, l, LANES), lambda j, p: (p, 0, 0, j))
    once = dict(pipeline_mode=pl.Buffered(1))
    const = lambda arr: pl.BlockSpec(arr.shape, lambda j, p: (0,) * arr.ndim, **once)
    if l <= DFT_N2 * 2:
        f, g = _dft_consts_small(l)
        y = pl.pallas_call(
            _lconv_small_kernel,
            out_shape=jax.ShapeDtypeStruct((b // 2, 2, l, c), F32),
            grid=(c // LANES, b // 2),
            in_specs=[io_spec, pl.BlockSpec((1, 2, 2 * l, LANES), lambda j, p: (order, 0, 0, j)), const(f), const(g)],
            out_specs=io_spec,
            compiler_params=_cparams(("parallel", "arbitrary")),
            name="long_conv_small",
        )(a4, hf, f, g)
        return y.reshape(b, l, c)
    n1f = 2 * l // DFT_N2
    n1h = n1f // 2
    assert (n1f + 8, n1h + 8) == (S_PITCH, Y_PITCH), "scratch pitches are sized for this sequence length"
    w1, f2, g2, g1 = _dft_consts(l)
    rows_xy = max(n1h * X_PITCH, DFT_N2 * Y_PITCH)
    y = pl.pallas_call(
        functools.partial(_lconv_kernel, n1h=n1h),
        out_shape=jax.ShapeDtypeStruct((b // 2, 2, l, c), F32),
        grid=(c // LANES, b // 2),
        in_specs=[io_spec, pl.BlockSpec((1, n1f, 2, DFT_N2, LANES), lambda j, p: (order, 0, 0, 0, j), **once),
                  const(w1), const(f2), const(g2), const(g1)],
        out_specs=io_spec,
        scratch_shapes=[pltpu.VMEM((2, rows_xy, LANES), F32), pltpu.VMEM((2, DFT_N2 * S_PITCH, LANES), F32)],
        compiler_params=_cparams(("parallel", "arbitrary"), DFT_VMEM_LIMIT),
        name="long_conv",
    )(a4, hf, w1, f2, g2, g1)
    return y.reshape(b, l, c)


@functools.lru_cache(maxsize=None)
def _fspec_consts(n):
    n1f = n // DFT_N2
    k1, n1, n2 = np.arange(n1f), np.arange(n1f), np.arange(DFT_N2)
    ph = np.outer(k1, n1)[None] / n1f + (n2[:, None, None] * k1[None, :, None]) / n
    e1 = np.exp(-2j * np.pi * ph)
    w1 = np.concatenate([e1.real, e1.imag], axis=1)
    f2 = _cplx_block(np.exp(-2j * np.pi * np.outer(n2, n2) / DFT_N2))
    return jnp.asarray(w1, F32).astype(BF16), jnp.asarray(f2, F32).astype(BF16)


def _fspec_kernel(f_ref, w1_ref, f2_ref, hf_ref, x_ref, s_ref, *, n1f):
    n1h = n1f // 2

    def copy_in(i, c):
        for half in range(2):
            x_ref[pl.ds(pl.multiple_of((half * n1h + i) * X_PITCH, 8), DFT_N2), :] = \
                f_ref[0, half, pl.ds(pl.multiple_of(i * DFT_N2, DFT_N2), DFT_N2), :]
        return c

    lax.fori_loop(0, n1h, copy_in, 0, unroll=4)

    def stage1(n2, c):
        slab = x_ref[pl.ds(n2, n1f, stride=X_PITCH), :].astype(BF16)
        a = jnp.dot(w1_ref[n2], slab, preferred_element_type=F32)
        base = pl.multiple_of(n2 * S_PITCH, 8)
        s_ref[0, pl.ds(base, n1f), :] = a[:n1f]
        s_ref[1, pl.ds(base, n1f), :] = a[n1f:]
        return c

    lax.fori_loop(0, DFT_N2, stage1, 0, unroll=32)

    def stage2(k1, c):
        slab = jnp.concatenate([s_ref[0, pl.ds(k1, DFT_N2, stride=S_PITCH), :],
                                s_ref[1, pl.ds(k1, DFT_N2, stride=S_PITCH), :]], axis=0).astype(BF16)
        x = jnp.dot(f2_ref[...], slab, preferred_element_type=F32)
        hf_ref[0, k1, 0] = x[:DFT_N2]
        hf_ref[0, k1, 1] = x[DFT_N2:]
        return c

    lax.fori_loop(0, n1f, stage2, 0, unroll=16)


def _fspec_small_kernel(f_ref, w_ref, hf_ref):
    n = 2 * f_ref.shape[2]
    filt = jnp.concatenate([f_ref[0, 0], f_ref[0, 1]], axis=0).astype(BF16)
    x = jnp.dot(w_ref[...], filt, preferred_element_type=F32)
    hf_ref[0, 0] = x[:n]
    hf_ref[0, 1] = x[n:]


@functools.lru_cache(maxsize=None)
def _fspec_consts_small(n):
    e = np.exp(-2j * np.pi * np.outer(np.arange(n), np.arange(n)) / n)
    return jnp.asarray(np.concatenate([e.real, e.imag], axis=0), F32).astype(BF16)


def _filter_spectra(filt):
    orders, _, l, c = filt.shape
    n = 2 * l
    f2s = filt
    once = dict(pipeline_mode=pl.Buffered(1))
    const = lambda arr: pl.BlockSpec(arr.shape, lambda o, j: (0,) * arr.ndim, **once)
    in_spec = pl.BlockSpec((1, 2, l, LANES), lambda o, j: (o, 0, 0, j))
    if l <= DFT_N2 * 2:
        w = _fspec_consts_small(n)
        return pl.pallas_call(
            _fspec_small_kernel,
            out_shape=jax.ShapeDtypeStruct((orders, 2, n, c), F32),
            grid=(orders, c // LANES),
            in_specs=[in_spec, const(w)],
            out_specs=pl.BlockSpec((1, 2, n, LANES), lambda o, j: (o, 0, 0, j)),
            compiler_params=_cparams(("parallel", "parallel")),
            name="filter_spectrum_small",
        )(f2s, w)
    n1f = n // DFT_N2
    assert n1f + 8 == S_PITCH, "scratch pitches are sized for this sequence length"
    w1, f2 = _fspec_consts(n)
    return pl.pallas_call(
        functools.partial(_fspec_kernel, n1f=n1f),
        out_shape=jax.ShapeDtypeStruct((orders, n1f, 2, DFT_N2, c), F32),
        grid=(orders, c // LANES),
        in_specs=[in_spec, const(w1), const(f2)],
        out_specs=pl.BlockSpec((1, n1f, 2, DFT_N2, LANES), lambda o, j: (o, 0, 0, 0, j)),
        scratch_shapes=[pltpu.VMEM((n1f * X_PITCH, LANES), F32), pltpu.VMEM((2, DFT_N2 * S_PITCH, LANES), F32)],
        compiler_params=_cparams(("parallel", "parallel"), DFT_VMEM_LIMIT),
        name="filter_spectrum",
    )(f2s, w1, f2)


def _hy_gate_kernel(y_ref, a_ref, m_ref, sk_ref, o_ref):
    o_ref[0] = (m_ref[0] * (y_ref[0] + a_ref[0] * sk_ref[...])).astype(o_ref.dtype)


def _hy_gate(y, a, a_col, m, m_col, sk, out_dtype):
    b, l, c = y.shape
    tl = min(l, 1024)
    spec = lambda col: pl.BlockSpec((1, tl, c), lambda i, j: (i, j, col))
    return pl.pallas_call(
        _hy_gate_kernel,
        out_shape=jax.ShapeDtypeStruct((b, l, c), out_dtype),
        grid=(b, l // tl),
        in_specs=[spec(0), spec(a_col), spec(m_col), pl.BlockSpec((1, c), lambda i, j: (0, 0))],
        out_specs=spec(0),
        compiler_params=_cparams(("parallel", "parallel")),
        name="hyena_gate",
    )(y, a, m, sk)


def _hyena(uc, filt, skip):
    hf = _filter_spectra(filt)
    y1 = _long_conv(uc, hf, 0)
    z1 = _hy_gate(y1, uc, 0, uc, 1, skip[0:1], F32)
    y2 = _long_conv(z1, hf, 1)
    return _hy_gate(y2, z1, 0, uc, 2, skip[1:2], BF16)


def _permute_wa_heads(w, start, axis):
    idx = np.arange(w.shape[axis])
    blocks = [np.arange(start + h * HEAD_DIM, start + (h + 1) * HEAD_DIM) for h in WA_HEAD_ORDER]
    idx[start:start + N_WA_HEADS * HEAD_DIM] = np.concatenate(blocks)
    return jnp.take(w, idx, axis=axis)


def _layer(x, ctx, mod, lyr, rope_tabs, update_ctx, final_g):
    b, s, d = x.shape
    lc = ctx.shape[1]
    mx = [mod[:b, None, i * d:(i + 1) * d] for i in range(6)]
    mc = [mod[b:b + 1, None, i * d:(i + 1) * d] for i in range(6)]
    g1, g2 = lyr["g1"][None, :], lyr["g2"][None, :]
    w_in = _permute_wa_heads(lyr["w_in"], OFF_WA_Q, axis=1).astype(BF16)
    w_out = _permute_wa_heads(lyr["w_out"], W_GROUP, axis=0).astype(BF16)
    rw_t = lyr["router_w"].T.astype(BF16)
    w_bd = _pool_weight(lyr["pool_w"])
    pscale = lyr["pool_scale"][None, :]
    hy_args = (lyr["hy_w1"], lyr["hy_b1"], lyr["hy_w2"], lyr["hy_b2"], lyr["hy_w3"], lyr["hy_freq"])

    conv = (lyr["hy_conv_w"], lyr["hy_conv_b"][None, :])
    u_hy, u_pool, q_wa, q_na, kv = _inproj(x, mx[0], mx[1], g1, w_in, conv, rope_tabs)
    if update_ctx:
        cu_hy, cu_pool, cq_wa, cq_na, ckv = _inproj(ctx, mc[0], mc[1], g1, w_in, conv)
    else:
        ckv = _inproj(ctx, mc[0], mc[1], g1, w_in[:, OFF_KV:], kv_only=True)

    ys = [_hyena(u_hy, _hyena_filters(s, *hy_args), lyr["hy_skip"]),
          _window_attn(q_wa, kv, ckv, lyr["wa_sink"]),
          _pool_mixer(u_pool, w_bd, pscale),
          _neighbourhood_attn(q_na, kv, ckv, lyr["na_rpb"])]
    x, h, aff = _outproj(ys, x, mx[2], w_out, g2, mx[3], mx[4], rw_t)
    moe = _expert_choice_ffn(h, aff, *lyr["experts"], lyr["index"])
    x = _combine(x, *moe, mx[5], final_g)

    if update_ctx:
        ycs = [_hyena(cu_hy, _hyena_filters(lc, *hy_args), lyr["hy_skip"]),
               _ctx_attn(cq_wa, ckv, 0, 1, N_WA_KV, lyr["wa_sink"], WA_HEAD_ORDER),
               _pool_mixer(cu_pool, w_bd, pscale),
               _ctx_attn(cq_na, ckv, 1, 2, N_NA_HEADS, None)]
        ctx, hc, affc = _outproj(ycs, ctx, mc[2], w_out, g2, mc[3], mc[4], rw_t)
        moe_c = _expert_choice_ffn(hc, affc, *lyr["experts"], lyr["index"])
        ctx = _combine(ctx, *moe_c, mc[5])
    return x, ctx


def kernel(x, c, ctx, c_ctx, ada_w, ada_b, norm1_g, norm2_g, w_in, hy_conv_w, hy_conv_b, hy_w1, hy_b1, hy_w2,
           hy_b2, hy_w3, hy_freq, hy_skip, wa_sink, pool_w, pool_scale, na_rpb, w_out, router_w, exp_w_gate,
           exp_w_up, exp_w_down, final_norm_g):
    b, s, d = x.shape
    cs = jnp.zeros((MOD_ROWS, d), F32).at[:b].set(c).at[b].set(c_ctx)
    mods = _ada_mod(cs, ada_w, ada_b[:, None, :])
    rope_tabs = _rope_tables(s)
    params = dict(g1=norm1_g, g2=norm2_g, w_in=w_in, hy_conv_w=hy_conv_w, hy_conv_b=hy_conv_b, hy_w1=hy_w1,
                  hy_b1=hy_b1, hy_w2=hy_w2, hy_b2=hy_b2, hy_w3=hy_w3, hy_freq=hy_freq, hy_skip=hy_skip,
                  wa_sink=wa_sink, pool_w=pool_w, pool_scale=pool_scale, na_rpb=na_rpb, w_out=w_out,
                  router_w=router_w)
    for l in range(DEPTH):
        lyr = {k: v[l] for k, v in params.items()}
        lyr.update(index=l, experts=(exp_w_gate, exp_w_up, exp_w_down))
        last = l == DEPTH - 1
        x, ctx = _layer(x, ctx, mods[l], lyr, rope_tabs, update_ctx=not last,
                        final_g=final_norm_g[None, :] if last else None)
    return x
```

```python
import functools
import math

import jax
import jax.numpy as jnp
import numpy as np
from jax import lax
from jax.experimental import pallas as pl
from jax.experimental.pallas import tpu as pltpu

F32 = jnp.float32
BF16 = jnp.bfloat16

D_MODEL = 1024
DEPTH = 2
GRID_W = 64
HEAD_DIM = 64
W_GROUP = 256
N_WA_HEADS = 4
N_WA_KV = 2
N_NA_HEADS = 4
KV_WA = 128
OFF_POOL = 768
OFF_WA_Q = 1024
OFF_NA_Q = 1280
OFF_KV = 1536
IN_WIDTH = 2304
KV_WIDTH = IN_WIDTH - OFF_KV
HY_ORDER = 2
HY_BANDS = 16
HY_DECAY_MIN = abs(math.log(1e-2) / 1.5)
HY_DECAY_MAX = abs(math.log(1e-2) / 0.3)
WA_BLOCK = 128
WA_STEP_BLOCKS = 4
WA_HEAD_ORDER = (0, 2, 1, 3)
POOL_WINDOWS = (2, 4, 8, 16)
POOL_GROUP = 64
HALO = 8
LANES = 128
SUBLANES = 8
DFT_N2 = 128
X_PITCH = DFT_N2 + 8
S_PITCH = 64 + 8
Y_PITCH = 32 + 8
NA_ROWS = 8
NA_COLS = 16
NA_ROW_BLOCK = 8
ROPE_BASE = 10000.0
N_EXPERTS = 16
EXPERT_HIDDEN = 2048
EC_CAPACITY = 2
NORM_EPS = 1e-6
NEG_INF = -1e30
Q_SCALE = HEAD_DIM ** -0.5

FFN_ROWS = 2048
FFN_HIDDEN_TILE = 256
GATHER_CHUNK = 2048
COMBINE_TILE = 256
COMBINE_SUB = 256
COMBINE_STRIP = 64
MOD_ROWS = 16
VMEM_LIMIT = 48 * 1024 * 1024
DFT_VMEM_LIMIT = 52 * 1024 * 1024
LANE_BITS = 7
F32_VALUE_BITS = 31


def _cparams(sem, vmem=VMEM_LIMIT):
    return pltpu.CompilerParams(dimension_semantics=sem, vmem_limit_bytes=vmem)


def _nt_dot(a, b):
    return lax.dot_general(a, b, (((1,), (1,)), ((), ())), preferred_element_type=F32)


def _rmsnorm_mod(x, g, shift, scale):
    y = x * lax.rsqrt(jnp.mean(x * x, axis=-1, keepdims=True) + NORM_EPS) * g
    return y * (1.0 + scale) + shift


def _ada_kernel(c_ref, w_ref, b_ref, o_ref):
    c = c_ref[...]
    s = (c * jax.nn.sigmoid(c)).astype(BF16)
    o_ref[0] = jnp.dot(s, w_ref[0].astype(BF16), preferred_element_type=F32) + b_ref[0]


def _ada_mod(cs, ada_w, ada_b):
    nl, d, n = ada_w.shape
    tn = 1024
    return pl.pallas_call(
        _ada_kernel,
        out_shape=jax.ShapeDtypeStruct((nl, MOD_ROWS, n), F32),
        grid=(nl, n // tn),
        in_specs=[pl.BlockSpec((MOD_ROWS, d), lambda l, j: (0, 0)),
                  pl.BlockSpec((1, d, tn), lambda l, j: (l, 0, j)),
                  pl.BlockSpec((1, 1, tn), lambda l, j: (l, 0, j))],
        out_specs=pl.BlockSpec((1, MOD_ROWS, tn), lambda l, j: (l, 0, j)),
        compiler_params=_cparams(("parallel", "parallel")),
        name="ada_mod",
    )(cs, ada_w, ada_b)


def _rope(t, cos, s_up, s_dn):
    w = t.shape[-1]
    if w > 128:
        cos, s_up, s_dn = (jnp.concatenate([a] * (w // 128), axis=-1) for a in (cos, s_up, s_dn))
    up = pltpu.roll(t, w - 16, 1)
    dn = pltpu.roll(t, 16, 1)
    return t * cos + up * s_up + dn * s_dn


def _inproj_kernel(*refs, rope, kv_only):
    if kv_only:
        x_ref, sh_ref, sc_ref, g_ref, w_ref, o_ref = refs
    elif rope:
        x_ref, xp_ref, xn_ref, sh_ref, sc_ref, g_ref, w_ref, cw_ref, cb_ref, cos_ref, sup_ref, sdn_ref, *outs = refs
    else:
        x_ref, xp_ref, xn_ref, sh_ref, sc_ref, g_ref, w_ref, cw_ref, cb_ref, *outs = refs
    h = _rmsnorm_mod(x_ref[0], g_ref[...], sh_ref[0], sc_ref[0])
    p = jnp.dot(h.astype(BF16), w_ref[...], preferred_element_type=F32)
    if kv_only:
        o_ref[0] = p.astype(BF16)
        return
    hy_ref, pool_ref, qwa_ref, qna_ref, kv_ref = outs
    j, nt, tm = pl.program_id(0), pl.num_programs(0), x_ref.shape[1]
    halo = _rmsnorm_mod(jnp.concatenate([xp_ref[0], xn_ref[0]], axis=0), g_ref[...], sh_ref[0], sc_ref[0])
    ph = jnp.dot(halo.astype(BF16), w_ref[:, :OFF_POOL], preferred_element_type=F32)
    e = jnp.concatenate([jnp.where(j == 0, 0.0, ph[:HALO]), p[:, :OFF_POOL],
                         jnp.where(j == nt - 1, 0.0, ph[HALO:])], axis=0)
    n = tm + 2 * HALO
    uc = pltpu.roll(e, 1, 0) * cw_ref[0:1, :] + e * cw_ref[1:2, :] + pltpu.roll(e, n - 1, 0) * cw_ref[2:3, :]
    hy_ref[0] = uc[HALO:HALO + tm] + cb_ref[...]
    pool_ref[0] = p[:, OFF_POOL:OFF_WA_Q]
    qwa = p[:, OFF_WA_Q:OFF_NA_Q]
    kwa = p[:, OFF_KV:OFF_KV + KV_WA]
    if rope:
        tabs = (cos_ref[...], sup_ref[...], sdn_ref[...])
        qwa = _rope(qwa, *tabs)
        kwa = _rope(kwa, *tabs)
    qwa_ref[0] = (qwa * Q_SCALE).astype(BF16)
    qna_ref[0] = (p[:, OFF_NA_Q:OFF_KV] * Q_SCALE).astype(BF16)
    kv_ref[0] = jnp.concatenate([kwa, p[:, OFF_KV + KV_WA:]], axis=-1).astype(BF16)


def _inproj(x, shift, scale, g, w_bf16, conv=None, rope_tabs=None, kv_only=False):
    b, l, d = x.shape
    n = w_bf16.shape[1]
    tm = min(l, 512)
    hb, nh = tm // HALO, l // HALO
    rope = rope_tabs is not None
    bm = shift.shape[0]
    mod_map = (lambda j, i: (i, 0, 0)) if bm > 1 else (lambda j, i: (0, 0, 0))
    in_specs = [pl.BlockSpec((1, tm, d), lambda j, i: (i, j, 0))]
    args = [x]
    if not kv_only:
        in_specs += [pl.BlockSpec((1, HALO, d), lambda j, i: (i, jnp.maximum(j * hb - 1, 0), 0)),
                     pl.BlockSpec((1, HALO, d), lambda j, i: (i, jnp.minimum((j + 1) * hb, nh - 1), 0))]
        args += [x, x]
    in_specs += [pl.BlockSpec((1, 1, d), mod_map),
                 pl.BlockSpec((1, 1, d), mod_map),
                 pl.BlockSpec((1, d), lambda j, i: (0, 0)),
                 pl.BlockSpec((d, n), lambda j, i: (0, 0))]
    args += [shift, scale, g, w_bf16]
    if not kv_only:
        in_specs += [pl.BlockSpec((3, OFF_POOL), lambda j, i: (0, 0)), pl.BlockSpec((1, OFF_POOL), lambda j, i: (0, 0))]
        args += list(conv)
    if rope:
        in_specs += [pl.BlockSpec((tm, 128), lambda j, i: (j, 0))] * 3
        args += list(rope_tabs)

    def tok(width, dtype):
        return (jax.ShapeDtypeStruct((b, l, width), dtype),
                pl.BlockSpec((1, tm, width), lambda j, i: (i, j, 0)))

    if kv_only:
        outs = [tok(n, BF16)]
    else:
        outs = [tok(OFF_POOL, F32), tok(W_GROUP, F32), tok(W_GROUP, BF16), tok(W_GROUP, BF16),
                tok(KV_WIDTH, BF16)]
    res = pl.pallas_call(
        functools.partial(_inproj_kernel, rope=rope, kv_only=kv_only),
        out_shape=[o[0] for o in outs],
        grid=(l // tm, b),
        in_specs=in_specs,
        out_specs=[o[1] for o in outs],
        compiler_params=_cparams(("parallel", "parallel")),
        name="inproj_kv" if kv_only else "inproj",
    )(*args)
    return res[0] if kv_only else res


def _rope_tables(s):
    pos = jnp.arange(s)
    p2 = jnp.stack([pos // GRID_W, pos % GRID_W], axis=-1).astype(F32)
    inv = ROPE_BASE ** (-jnp.arange(16, dtype=F32) / 16)
    lane = np.arange(HEAD_DIM)
    ang = p2[:, lane // 32] * inv[lane % 16][None, :]
    first = jnp.asarray((lane % 32) < 16)[None, :]
    cos, sin = jnp.cos(ang), jnp.sin(ang)
    s_up = jnp.where(first, -sin, 0.0)
    s_dn = jnp.where(first, 0.0, sin)
    return tuple(jnp.tile(t, (1, 2)) for t in (cos, s_up, s_dn))


def _softmax_parts(parts, extra=None):
    m = parts[0].max(axis=-1, keepdims=True)
    for s in parts[1:]:
        m = jnp.maximum(m, s.max(axis=-1, keepdims=True))
    if extra is not None:
        m = jnp.maximum(m, extra)
    ps = [jnp.exp(s - m) for s in parts]
    den = ps[0].sum(axis=-1, keepdims=True)
    for p in ps[1:]:
        den = den + p.sum(axis=-1, keepdims=True)
    if extra is not None:
        den = den + jnp.exp(extra - m)
    return ps, den


def _head_stack(t, masks):
    return jnp.concatenate([jnp.where(m, t, jnp.zeros_like(t)) for m in masks], axis=0)


def _wattn_kernel(sink_ref, q_ref, kp_ref, ko_ref, kn_ref, vp_ref, vo_ref, vn_ref, kc_ref, vc_ref, o_ref, *, nb):
    n = pl.program_id(1)
    blk = WA_BLOCK
    q = q_ref[0]
    kall = jnp.concatenate([kp_ref[0], ko_ref[0], kn_ref[0]], axis=0)
    vall = jnp.concatenate([vp_ref[0], vo_ref[0], vn_ref[0]], axis=0)
    kc, vc = kc_ref[0], vc_ref[0]
    rows = N_WA_HEADS * blk
    i = lax.broadcasted_iota(jnp.int32, (rows, 3 * blk), 0) & (blk - 1)
    j = lax.broadcasted_iota(jnp.int32, (rows, 3 * blk), 1)
    band = (j >= i) & (j <= i + 2 * blk)
    kv0 = lax.broadcasted_iota(jnp.int32, (blk, KV_WA), 1) < HEAD_DIM
    head = lax.broadcasted_iota(jnp.int32, (rows, 1), 0) // blk
    snk = jnp.where(head == 0, sink_ref[0],
                    jnp.where(head == 1, sink_ref[1], jnp.where(head == 2, sink_ref[2], sink_ref[3])))
    for sb in range(WA_STEP_BLOCKS):
        gb = n * WA_STEP_BLOCKS + sb
        jlo = jnp.where(gb == 0, blk, 0)
        jhi = jnp.where(gb == nb - 1, 2 * blk, 3 * blk)
        valid = band & (j >= jlo) & (j < jhi)
        keys = slice(sb * blk, (sb + 3) * blk)
        qa = q[sb * blk:(sb + 1) * blk, :KV_WA]
        qb = q[sb * blk:(sb + 1) * blk, KV_WA:]
        zero = jnp.zeros_like(qa)
        qs = jnp.concatenate([jnp.where(kv0, qa, zero), jnp.where(kv0, qb, zero),
                              jnp.where(kv0, zero, qa), jnp.where(kv0, zero, qb)], axis=0)
        s_loc = jnp.where(valid, _nt_dot(qs, kall[keys]), NEG_INF)
        s_ctx = _nt_dot(qs, kc)
        (p_loc, p_ctx), den = _softmax_parts([s_loc, s_ctx], snk)
        o = (jnp.dot(p_loc.astype(BF16), vall[keys], preferred_element_type=F32)
             + jnp.dot(p_ctx.astype(BF16), vc, preferred_element_type=F32)) / den
        o_ref[0, sb * blk:(sb + 1) * blk, :] = jnp.concatenate(
            [jnp.where(kv0, o[:blk], o[2 * blk:3 * blk]), jnp.where(kv0, o[blk:2 * blk], o[3 * blk:])],
            axis=-1).astype(BF16)


def _window_attn(q, kv, ckv, sink):
    b, s, _ = q.shape
    lc = ckv.shape[1]
    nb = s // WA_BLOCK
    sbk = WA_STEP_BLOCKS

    def halo_spec(col, off):
        return pl.BlockSpec((1, WA_BLOCK, KV_WA),
                            lambda i, n: (i, jnp.clip(n * sbk + off, 0, nb - 1), col))

    def own_spec(col):
        return pl.BlockSpec((1, sbk * WA_BLOCK, KV_WA), lambda i, n: (i, n, col))

    return pl.pallas_call(
        functools.partial(_wattn_kernel, nb=nb),
        out_shape=jax.ShapeDtypeStruct((b, s, W_GROUP), BF16),
        grid=(b, nb // sbk),
        in_specs=[pl.BlockSpec(memory_space=pltpu.SMEM),
                  pl.BlockSpec((1, sbk * WA_BLOCK, W_GROUP), lambda i, n: (i, n, 0)),
                  halo_spec(0, -1), own_spec(0), halo_spec(0, sbk),
                  halo_spec(1, -1), own_spec(1), halo_spec(1, sbk),
                  pl.BlockSpec((1, lc, KV_WA), lambda i, n: (i, 0, 0)),
                  pl.BlockSpec((1, lc, KV_WA), lambda i, n: (i, 0, 1))],
        out_specs=pl.BlockSpec((1, sbk * WA_BLOCK, W_GROUP), lambda i, n: (i, n, 0)),
        compiler_params=_cparams(("parallel", "parallel")),
        name="window_attn",
    )(sink, q, kv, kv, kv, kv, kv, kv, ckv, ckv)


def _nattn_kernel(q_ref, k_ref, v_ref, kc_ref, vc_ref, bias_ref, o_ref, *, rows):
    blk = pl.program_id(1)
    nk = NA_ROWS * GRID_W
    kc, vc = kc_ref[0], vc_ref[0]
    lane_head = lax.broadcasted_iota(jnp.int32, (GRID_W, W_GROUP), 1) // HEAD_DIM
    masks = [lane_head == h for h in range(N_NA_HEADS)]
    for rr in range(NA_ROW_BLOCK):
        r = blk * NA_ROW_BLOCK + rr
        r0 = jnp.clip(r - NA_ROWS // 2, 0, rows - NA_ROWS)
        var = r - r0
        start = pl.multiple_of(r0 * GRID_W, GRID_W)
        kt = k_ref[0, pl.ds(start, nk), :]
        vt = v_ref[0, pl.ds(start, nk), :]
        qrow = slice(rr * GRID_W, (rr + 1) * GRID_W)
        qs = _head_stack(q_ref[0, qrow, :], masks)
        bias = bias_ref[:, var].reshape(N_NA_HEADS * GRID_W, nk)
        (p_loc, p_ctx), den = _softmax_parts([_nt_dot(qs, kt) + bias, _nt_dot(qs, kc)])
        o = (jnp.dot(p_loc.astype(BF16), vt, preferred_element_type=F32)
             + jnp.dot(p_ctx.astype(BF16), vc, preferred_element_type=F32)) / den
        out = o[(N_NA_HEADS - 1) * GRID_W:]
        for h in range(N_NA_HEADS - 2, -1, -1):
            out = jnp.where(masks[h], o[h * GRID_W:(h + 1) * GRID_W], out)
        o_ref[0, qrow, :] = out.astype(BF16)


def _na_bias(rpb):
    var = np.arange(NA_ROWS)
    j = np.arange(NA_ROWS)
    qc = np.arange(GRID_W)
    kc = np.arange(GRID_W)
    dr = j[None, :] - var[:, None] + NA_ROWS - 1
    dc = np.clip(kc[None, :] - qc[:, None] + NA_COLS - 1, 0, 2 * NA_COLS - 2)
    ws = np.clip(qc - NA_COLS // 2, 0, GRID_W - NA_COLS)
    ok = (kc[None, :] >= ws[:, None]) & (kc[None, :] < ws[:, None] + NA_COLS)
    onehot = (dc[None] == np.arange(2 * NA_COLS - 1)[:, None, None]).astype(np.float32)
    bias = jnp.einsum("hvjd,dqk->hvqjk", rpb.astype(F32)[:, dr], jnp.asarray(onehot),
                      precision=lax.Precision.HIGHEST)
    bias = jnp.where(jnp.asarray(ok)[None, None, :, None, :], bias, NEG_INF)
    return bias.reshape(rpb.shape[0], NA_ROWS, GRID_W, NA_ROWS * GRID_W)


def _neighbourhood_attn(q, kv, ckv, rpb):
    b, s, _ = q.shape
    lc = ckv.shape[1]
    rows = s // GRID_W
    bias = _na_bias(rpb)
    return pl.pallas_call(
        functools.partial(_nattn_kernel, rows=rows),
        out_shape=jax.ShapeDtypeStruct((b, s, W_GROUP), BF16),
        grid=(b, rows // NA_ROW_BLOCK),
        in_specs=[pl.BlockSpec((1, NA_ROW_BLOCK * GRID_W, W_GROUP), lambda i, r: (i, r, 0)),
                  pl.BlockSpec((1, s, W_GROUP), lambda i, r: (i, 0, 1)),
                  pl.BlockSpec((1, s, W_GROUP), lambda i, r: (i, 0, 2)),
                  pl.BlockSpec((1, lc, W_GROUP), lambda i, r: (i, 0, 1)),
                  pl.BlockSpec((1, lc, W_GROUP), lambda i, r: (i, 0, 2)),
                  pl.BlockSpec(bias.shape, lambda i, r: (0, 0, 0, 0))],
        out_specs=pl.BlockSpec((1, NA_ROW_BLOCK * GRID_W, W_GROUP), lambda i, r: (i, r, 0)),
        compiler_params=_cparams(("parallel", "arbitrary")),
        name="neighbourhood_attn",
    )(q, kv, kv, ckv, ckv, bias)


def _cattn_kernel(*refs, n_kv, with_sink, head_order):
    if with_sink:
        sink_ref, q_ref, k_ref, v_ref, o_ref = refs
    else:
        q_ref, k_ref, v_ref, o_ref = refs
    q, k, v = q_ref[0], k_ref[0], v_ref[0]
    group = N_WA_HEADS // n_kv
    outs = []
    for pos, h in enumerate(head_order):
        sl = slice((h // group) * HEAD_DIM, (h // group + 1) * HEAD_DIM)
        s = _nt_dot(q[:, pos * HEAD_DIM:(pos + 1) * HEAD_DIM], k[:, sl])
        extra = jnp.full((s.shape[0], 1), sink_ref[h], F32) if with_sink else None
        (p,), den = _softmax_parts([s], extra)
        outs.append(jnp.dot(p.astype(BF16), v[:, sl], preferred_element_type=F32) / den)
    o_ref[0] = jnp.concatenate(outs, axis=-1).astype(BF16)


def _ctx_attn(q, ckv, k_col, v_col, n_kv, sink, head_order=(0, 1, 2, 3)):
    b, lc, _ = q.shape
    w = n_kv * HEAD_DIM
    with_sink = sink is not None
    in_specs = [pl.BlockSpec((1, lc, W_GROUP), lambda i: (i, 0, 0)),
                pl.BlockSpec((1, lc, w), lambda i: (i, 0, k_col)),
                pl.BlockSpec((1, lc, w), lambda i: (i, 0, v_col))]
    args = [q, ckv, ckv]
    if with_sink:
        in_specs = [pl.BlockSpec(memory_space=pltpu.SMEM)] + in_specs
        args = [sink] + args
    return pl.pallas_call(
        functools.partial(_cattn_kernel, n_kv=n_kv, with_sink=with_sink, head_order=head_order),
        out_shape=jax.ShapeDtypeStruct((b, lc, W_GROUP), BF16),
        grid=(b,),
        in_specs=in_specs,
        out_specs=pl.BlockSpec((1, lc, W_GROUP), lambda i: (i, 0, 0)),
        compiler_params=_cparams(("parallel",)),
        name="ctx_attn",
    )(*args)


def _pool_kernel(prev_ref, cur_ref, next_ref, w_ref, scale_ref, o_ref, *, seq, tl):
    j = pl.program_id(1)
    nt = seq // tl
    cur = cur_ref[0]
    prev = jnp.where(j == 0, 0.0, prev_ref[0])
    nxt = jnp.where(j == nt - 1, 0.0, next_ref[0])
    e = jnp.concatenate([prev, cur, nxt], axis=0)
    n = tl + 2 * HALO

    def sh(a, d):
        return pltpu.roll(a, d % n, 0)

    s2 = e + sh(e, 1)
    s4 = sh(s2, 1) + sh(s2, -1)
    s8 = sh(s4, 2) + sh(s4, -2)
    s16 = sh(s8, 4) + sh(s8, -4)
    lane = lax.broadcasted_iota(jnp.int32, (tl, W_GROUP), 1)
    t = lax.broadcasted_iota(jnp.int32, (tl, W_GROUP), 0) + j * tl
    g = lane // POOL_GROUP
    lo, hi = HALO, HALO + tl
    ssum = jnp.where(g == 0, s2[lo:hi], jnp.where(g == 1, s4[lo:hi], jnp.where(g == 2, s8[lo:hi], s16[lo:hi])))
    half = jnp.where(g == 0, 1, jnp.where(g == 1, 2, jnp.where(g == 2, 4, 8)))
    cnt = (jnp.minimum(t + half, seq) - jnp.maximum(t - half, 0)).astype(F32)
    d = (ssum / cnt - cur).astype(BF16)
    o_ref[0] = (jnp.dot(d, w_ref[...], preferred_element_type=F32) * scale_ref[...]).astype(BF16)


def _pool_mixer(u, w_bd, scale):
    b, l, c = u.shape
    tl = min(l, 512)
    hb = tl // HALO
    nh = l // HALO
    return pl.pallas_call(
        functools.partial(_pool_kernel, seq=l, tl=tl),
        out_shape=jax.ShapeDtypeStruct((b, l, c), BF16),
        grid=(b, l // tl),
        in_specs=[pl.BlockSpec((1, HALO, c), lambda i, j: (i, jnp.maximum(j * hb - 1, 0), 0)),
                  pl.BlockSpec((1, tl, c), lambda i, j: (i, j, 0)),
                  pl.BlockSpec((1, HALO, c), lambda i, j: (i, jnp.minimum((j + 1) * hb, nh - 1), 0)),
                  pl.BlockSpec((c, c), lambda i, j: (0, 0)),
                  pl.BlockSpec((1, c), lambda i, j: (0, 0))],
        out_specs=pl.BlockSpec((1, tl, c), lambda i, j: (i, j, 0)),
        compiler_params=_cparams(("parallel", "parallel")),
        name="pool_mixer",
    )(u, u, u, w_bd, scale)


def _pool_weight(pool_w):
    z = jnp.zeros((W_GROUP, W_GROUP), F32)
    for g in range(len(POOL_WINDOWS)):
        z = z.at[g * POOL_GROUP:(g + 1) * POOL_GROUP, g * POOL_GROUP:(g + 1) * POOL_GROUP].set(pool_w[g])
    return z.astype(BF16)


def _outproj_kernel(yh_ref, ya_ref, yp_ref, yn_ref, x_ref, gt_ref, w_ref, g2_ref, sh_ref, sc_ref, rw_ref,
                    xo_ref, h_ref, aff_ref):
    y = jnp.concatenate([yh_ref[0], ya_ref[0], yp_ref[0], yn_ref[0]], axis=-1)
    x = x_ref[0] + gt_ref[0] * jnp.dot(y, w_ref[...], preferred_element_type=F32)
    xo_ref[0] = x
    h = _rmsnorm_mod(x, g2_ref[...], sh_ref[0], sc_ref[0])
    nsl = h.shape[1] // LANES
    for s in range(nsl):
        h_ref[0, pl.ds(s, h.shape[0], stride=nsl), :] = h[:, s * LANES:(s + 1) * LANES]
    logits = _nt_dot(rw_ref[...], h.astype(BF16))
    m = logits.max(axis=0, keepdims=True)
    p = jnp.exp(logits - m)
    aff_ref[0] = p / p.sum(axis=0, keepdims=True)


def _outproj(ys, x, gate, w_bf16, g2, shift, scale, rw_t):
    b, l, d = x.shape
    tm = min(l, 512)
    bm = gate.shape[0]
    mod_map = (lambda i, j: (i, 0, 0)) if bm > 1 else (lambda i, j: (0, 0, 0))
    tok = lambda w: pl.BlockSpec((1, tm, w), lambda i, j: (i, j, 0))
    mod = pl.BlockSpec((1, 1, d), mod_map)
    return pl.pallas_call(
        _outproj_kernel,
        out_shape=[jax.ShapeDtypeStruct((b, l, d), F32), jax.ShapeDtypeStruct((b, l * d // LANES, LANES), F32),
                   jax.ShapeDtypeStruct((b, N_EXPERTS, l), F32)],
        grid=(b, l // tm),
        in_specs=[tok(W_GROUP)] * 4 + [tok(d), mod, pl.BlockSpec((d, d), lambda i, j: (0, 0)),
                                        pl.BlockSpec((1, d), lambda i, j: (0, 0)), mod, mod,
                                        pl.BlockSpec((N_EXPERTS, d), lambda i, j: (0, 0))],
        out_specs=[tok(d), pl.BlockSpec((1, tm * d // LANES, LANES), lambda i, j: (i, j, 0)),
                   pl.BlockSpec((1, N_EXPERTS, tm), lambda i, j: (i, 0, j))],
        compiler_params=_cparams(("parallel", "parallel")),
        name="outproj_router",
    )(*ys, x, gate, w_bf16, g2, shift, scale, rw_t)


def _row_gather_kernel(idx_ref, nxt_ref, table_ref, out_ref, stage, sems):
    i, n = pl.program_id(0), pl.num_programs(0)
    cur = i % 2
    chunk, d = out_ref.shape
    nsl = d // LANES

    def issue(ref, s):
        def one(r, c):
            pltpu.make_async_copy(table_ref.at[pl.ds(pl.multiple_of(ref[0, 0, r] * nsl, nsl), nsl)],
                                  stage.at[s, pl.ds(pl.multiple_of(r * nsl, nsl), nsl)], sems.at[s]).start()
            return c
        lax.fori_loop(0, chunk, one, 0, unroll=8)

    @pl.when(i == 0)
    def _():
        issue(idx_ref, cur)

    @pl.when(i + 1 < n)
    def _():
        issue(nxt_ref, 1 - cur)

    pltpu.make_async_copy(table_ref.at[pl.ds(0, chunk * nsl)], stage.at[cur], sems.at[cur]).wait()
    for s in range(nsl):
        out_ref[:, s * LANES:(s + 1) * LANES] = stage[cur, pl.ds(s, chunk, stride=nsl), :].astype(BF16)


def _row_gather(table, rows, d):
    n = rows.shape[0]
    chunk = min(GATHER_CHUNK, n)
    nch = n // chunk
    idx_spec = lambda off: pl.BlockSpec((1, 1, chunk), lambda i: (jnp.minimum(i + off, nch - 1), 0, 0),
                                        memory_space=pltpu.SMEM)
    rows3 = rows.reshape(nch, 1, chunk)
    return pl.pallas_call(
        _row_gather_kernel,
        out_shape=jax.ShapeDtypeStruct((n, d), BF16),
        grid=(nch,),
        in_specs=[idx_spec(0), idx_spec(1), pl.BlockSpec(memory_space=pl.ANY)],
        out_specs=pl.BlockSpec((chunk, d), lambda i: (i, 0)),
        scratch_shapes=[pltpu.VMEM((2, chunk * d // LANES, LANES), table.dtype), pltpu.SemaphoreType.DMA((2,))],
        compiler_params=pltpu.CompilerParams(dimension_semantics=("arbitrary",), vmem_limit_bytes=VMEM_LIMIT,
                                             disable_bounds_checks=True),
        name="moe_row_gather",
    )(rows3, rows3, table)


def _moe_kernel(x_ref, g_ref, wg_ref, wu_ref, wd_ref, o_ref):
    f = pl.program_id(2)

    @pl.when(f == 0)
    def _():
        o_ref[0] = jnp.zeros(o_ref.shape[1:], F32)

    x = x_ref[0]
    a = jnp.dot(x, wg_ref[0, 0].astype(BF16), preferred_element_type=F32)
    u = jnp.dot(x, wu_ref[0, 0].astype(BF16), preferred_element_type=F32)
    hid = (a * jax.nn.sigmoid(a) * u).astype(BF16)
    o_ref[0] += jnp.dot(hid, wd_ref[0, 0].astype(BF16), preferred_element_type=F32)

    @pl.when(f == pl.num_programs(2) - 1)
    def _():
        o_ref[0] = o_ref[0] * g_ref[0]


def _moe_ffn(xg, gate, w_gate, w_up, w_down, lyr):
    e, m, d = xg.shape
    f = w_gate.shape[3]
    tm = min(m, FFN_ROWS)
    tf = FFN_HIDDEN_TILE
    return pl.pallas_call(
        _moe_kernel,
        out_shape=jax.ShapeDtypeStruct((e, m, d), F32),
        grid=(e, m // tm, f // tf),
        in_specs=[pl.BlockSpec((1, tm, d), lambda i, j, k: (i, j, 0)),
                  pl.BlockSpec((1, tm, 1), lambda i, j, k: (i, j, 0)),
                  pl.BlockSpec((1, 1, d, tf), lambda i, j, k: (lyr, i, 0, k)),
                  pl.BlockSpec((1, 1, d, tf), lambda i, j, k: (lyr, i, 0, k)),
                  pl.BlockSpec((1, 1, tf, d), lambda i, j, k: (lyr, i, k, 0))],
        out_specs=pl.BlockSpec((1, tm, d), lambda i, j, k: (i, j, 0)),
        compiler_params=_cparams(("parallel", "parallel", "arbitrary")),
        name="moe_ffn",
    )(xg, gate, w_gate, w_up, w_down)


def _route_kernel(aff_ref, tri_ref, idx_ref, g_ref, *, k):
    x = aff_ref[...]
    r, l = x.shape
    nb = l // LANES
    nq = idx_ref.shape[1] // LANES
    def count(mask):
        return jnp.sum(jnp.where(mask, 1.0, 0.0), axis=-1, keepdims=True)

    def as_float(pattern):
        return pltpu.bitcast(jnp.broadcast_to(pattern, (r, LANES)), F32)[:, :1]

    def bisect(i, t):
        cand = t | jnp.left_shift(jnp.int32(1), F32_VALUE_BITS - 1 - i)
        return jnp.where(count(x >= as_float(cand)) >= k, cand, t)

    kth = lax.fori_loop(0, F32_VALUE_BITS, bisect, jnp.zeros((r, 1), jnp.int32))
    gt = x >= as_float(kth + 1)
    eq = (x >= as_float(kth)) & jnp.logical_not(gt)
    need = k - count(gt)
    tri = tri_ref[...]
    lane = lax.broadcasted_iota(jnp.int32, (r, LANES), 1)

    def block_prefix(mask, j):
        m = jnp.where(mask[:, j * LANES:(j + 1) * LANES], 1.0, 0.0)
        return jnp.dot(m.astype(BF16), tri, preferred_element_type=F32), m

    out_tok = [jnp.zeros((r, LANES), jnp.int32) for _ in range(nq)]
    out_g = [jnp.zeros((r, LANES), F32) for _ in range(nq)]
    eq_before = jnp.zeros((r, 1), F32)
    start = jnp.zeros((r, 1), jnp.int32)
    for j in range(nb):
        blk = slice(j * LANES, (j + 1) * LANES)
        eq_inc, eq_f = block_prefix(eq, j)
        sel = gt[:, blk] | (eq[:, blk] & (eq_inc - eq_f + eq_before < need))
        eq_before = eq_before + eq_inc[:, LANES - 1:]
        sel_f = jnp.where(sel, 1.0, 0.0)
        sel_inc = jnp.dot(sel_f.astype(BF16), tri, preferred_element_type=F32)
        cnt = sel_inc[:, LANES - 1:].astype(jnp.int32)
        d = jnp.where(sel, lane - (sel_inc - sel_f).astype(jnp.int32), -1)
        tok = lane + j * LANES
        gv = x[:, blk]
        for kb in range(LANE_BITS):
            sh = LANES - (1 << kb)
            rd, rt, rg = pltpu.roll(d, sh, 1), pltpu.roll(tok, sh, 1), pltpu.roll(gv, sh, 1)
            arrive = (rd >= 0) & (((rd >> kb) & 1) == 1)
            stay = (d >= 0) & (((d >> kb) & 1) == 0)
            tok = jnp.where(arrive, rt, tok)
            gv = jnp.where(arrive, rg, gv)
            d = jnp.where(arrive, rd, jnp.where(stay, d, -1))
        s, q = start & (LANES - 1), start >> LANE_BITS
        for kb in range(LANE_BITS):
            mv = ((s >> kb) & 1) == 1
            tok = jnp.where(mv, pltpu.roll(tok, 1 << kb, 1), tok)
            gv = jnp.where(mv, pltpu.roll(gv, 1 << kb, 1), gv)
        end = s + cnt
        here = (lane >= s) & (lane < end)
        wrapped = lane < end - LANES
        for qq in range(nq):
            hit = (here & (q == qq)) | (wrapped & (q + 1 == qq))
            out_tok[qq] = jnp.where(hit, tok, out_tok[qq])
            out_g[qq] = jnp.where(hit, gv, out_g[qq])
        start = start + cnt
    idx_ref[...] = jnp.concatenate(out_tok, axis=1)
    g_ref[...] = jnp.concatenate(out_g, axis=1)


def _route(aff_t, k):
    b, e, l = aff_t.shape
    r = b * e
    width = -(-k // LANES) * LANES
    tri = jnp.asarray(np.triu(np.ones((LANES, LANES), np.float32))).astype(BF16)
    idx, g = pl.pallas_call(
        functools.partial(_route_kernel, k=k),
        out_shape=[jax.ShapeDtypeStruct((r, width), jnp.int32), jax.ShapeDtypeStruct((r, width), F32)],
        compiler_params=pltpu.CompilerParams(vmem_limit_bytes=VMEM_LIMIT),
        name="expert_choice_route",
    )(aff_t.reshape(r, l), tri)
    return idx[:, :k].reshape(b, e, k), g[:, :k].reshape(b, e, k)


def _expert_choice_ffn(h, aff_t, w_gate, w_up, w_down, lyr):
    b, _, l = aff_t.shape
    d = h.shape[1] * LANES // l
    cap = EC_CAPACITY * l // N_EXPERTS
    m = b * cap
    idx, g = _route(aff_t, cap)
    idx_t = jnp.swapaxes(idx, 0, 1)
    rows = (idx_t + (jnp.arange(b, dtype=jnp.int32) * l)[None, :, None]).reshape(-1)
    xg = _row_gather(h.reshape(-1, LANES), rows, d).reshape(N_EXPERTS, m, d)
    gate = jnp.swapaxes(g, 0, 1).reshape(N_EXPERTS, m, 1)
    o = _moe_ffn(xg, gate, w_gate, w_up, w_down, lyr).reshape(N_EXPERTS * m, d)
    return o, idx


def _combine_kernel(*refs, final, tile, nt, cap, strip, m):
    if final:
        lo_ref, idx_ref, x_ref, gt_ref, g_ref, o_hbm, out_ref, stage, sems = refs
    else:
        lo_ref, idx_ref, x_ref, gt_ref, o_hbm, out_ref, stage, sems = refs
    bi, j = pl.program_id(0), pl.program_id(1)
    step = bi * nt + j
    cur = step % 2
    ne = N_EXPERTS
    gps = strip // SUBLANES
    per_sub = COMBINE_SUB // strip
    win = 2 * LANES
    sentinel = tile * nt

    def runs(jj):
        out = []
        for e in range(ne):
            lo = lo_ref[bi, e * (nt + 1) + jj]
            hi = lo_ref[bi, e * (nt + 1) + jj + 1]
            a8 = (lo // SUBLANES) * SUBLANES
            out.append((a8, jnp.where(hi > lo, hi - a8, 0)))
        return out

    def groups(run, r):
        return [(a8 + strip * r, jnp.clip((n - strip * r + SUBLANES - 1) // SUBLANES, 0, gps)) for a8, n in run]

    def issue(s, grp):
        for e, (s0, ng) in enumerate(grp):
            base = e * m + bi * cap + s0

            def one(k, c, base=base, e=e):
                pltpu.make_async_copy(
                    o_hbm.at[pl.ds(pl.multiple_of(base + k * SUBLANES, SUBLANES), SUBLANES)],
                    stage.at[s, pl.ds(pl.multiple_of(e * strip + k * SUBLANES, SUBLANES), SUBLANES)],
                    sems.at[s]).start()
                return c
            lax.fori_loop(0, ng, one, 0)

    def wait(s, grp):
        cnt = sum(ng for _, ng in grp) * SUBLANES

        @pl.when(cnt > 0)
        def _():
            rows = pl.ds(0, pl.multiple_of(cnt, SUBLANES))
            pltpu.make_async_copy(o_hbm.at[rows], stage.at[s, rows], sems.at[s]).wait()

    def reduce(s, grp):
        t_ids = lax.broadcasted_iota(jnp.int32, (tile, COMBINE_SUB), 0) + j * tile
        acc = jnp.zeros((tile, out_ref.shape[2]), F32)
        for sub in range(ne // per_sub):
            pieces = []
            for e in range(sub * per_sub, (sub + 1) * per_sub):
                s0, ng = grp[e]
                a = jnp.minimum((s0 // LANES) * LANES, idx_ref.shape[2] - win)
                w = idx_ref[0, e:e + 1, pl.ds(pl.multiple_of(a, LANES), win)]
                w = pltpu.roll(w, (win - (s0 - a)) % win, 1)
                pieces.append(jnp.where(ng > 0, w[:, :strip], sentinel))
            toks = jnp.concatenate(pieces, axis=1)
            rows = stage[s, sub * COMBINE_SUB:(sub + 1) * COMBINE_SUB, :]
            onehot = jnp.where(toks == t_ids, 1.0, 0.0).astype(BF16)
            hi = rows.astype(BF16)
            lo = (rows - hi.astype(F32)).astype(BF16)
            acc = (acc + jnp.dot(onehot, hi, preferred_element_type=F32)
                   + jnp.dot(onehot, lo, preferred_element_type=F32))
        return acc

    @pl.when(step == 0)
    def _():
        stage[...] = jnp.zeros(stage.shape, F32)

    run = runs(j)
    first = groups(run, 0)

    @pl.when(j == 0)
    def _():
        issue(cur, first)

    @pl.when(j + 1 < nt)
    def _():
        issue(1 - cur, groups(runs(jnp.minimum(j + 1, nt - 1)), 0))

    wait(cur, first)
    acc = reduce(cur, first)
    longest = run[0][1]
    for _, n in run[1:]:
        longest = jnp.maximum(longest, n)

    def extra_round(r, acc):
        grp = groups(run, r)
        issue(cur, grp)
        wait(cur, grp)
        return acc + reduce(cur, grp)

    acc = lax.fori_loop(1, (longest + strip - 1) // strip, extra_round, acc)
    x = x_ref[0] + gt_ref[0] * acc
    if final:
        x = x * lax.rsqrt(jnp.mean(x * x, axis=-1, keepdims=True) + NORM_EPS) * g_ref[...]
    out_ref[0] = x


def _combine(x, o, idx, gate, final_g=None):
    b, l, d = x.shape
    ne, cap = idx.shape[1], idx.shape[2]
    tile = min(l, COMBINE_TILE)
    nt = l // tile
    strip = min(COMBINE_STRIP, cap)
    bounds = jnp.arange(nt + 1, dtype=jnp.int32) * tile
    lo = jnp.sum(idx[:, :, None, :] < bounds[None, None, :, None], axis=-1, dtype=jnp.int32)
    cp = -(-cap // LANES) * LANES + LANES
    idx_p = jnp.pad(idx, ((0, 0), (0, 0), (0, cp - cap)), constant_values=l)
    bm = gate.shape[0]
    mod_map = (lambda i, j: (i, 0, 0)) if bm > 1 else (lambda i, j: (0, 0, 0))
    tok = pl.BlockSpec((1, tile, d), lambda i, j: (i, j, 0))
    final = final_g is not None
    in_specs = [pl.BlockSpec(memory_space=pltpu.SMEM),
                pl.BlockSpec((1, ne, cp), lambda i, j: (i, 0, 0)),
                tok, pl.BlockSpec((1, 1, d), mod_map)]
    args = [lo.reshape(b, ne * (nt + 1)), idx_p, x, gate]
    if final:
        in_specs.append(pl.BlockSpec((1, d), lambda i, j: (0, 0)))
        args.append(final_g)
    in_specs.append(pl.BlockSpec(memory_space=pl.ANY))
    args.append(o)
    return pl.pallas_call(
        functools.partial(_combine_kernel, final=final, tile=tile, nt=nt, cap=cap, strip=strip, m=b * cap),
        out_shape=jax.ShapeDtypeStruct((b, l, d), F32),
        grid=(b, nt),
        in_specs=in_specs,
        out_specs=tok,
        scratch_shapes=[pltpu.VMEM((2, ne * strip, d), F32), pltpu.SemaphoreType.DMA((2,))],
        compiler_params=pltpu.CompilerParams(dimension_semantics=("arbitrary", "arbitrary"),
                                             vmem_limit_bytes=VMEM_LIMIT, disable_bounds_checks=True),
        name="moe_combine_final" if final else "moe_combine",
    )(*args)


def _hyena_filters(l, w1, b1, w2, b2, w3, freq):
    hp = lax.Precision.HIGHEST
    bands = jnp.linspace(1e-4, HY_BANDS - 1, HY_BANDS, dtype=F32)
    deltas = jnp.linspace(HY_DECAY_MIN, HY_DECAY_MAX, W_GROUP, dtype=F32)

    def at(pos):
        t = pos / max(l - 1, 1)
        ang = (2 * math.pi / l) * pos[:, None] * bands[None, :]
        feats = jnp.concatenate([t[:, None], jnp.cos(ang), -jnp.sin(ang)], axis=-1)
        h = jnp.sin(freq * (jnp.dot(feats, w1, precision=hp) + b1))
        h = jnp.sin(freq * (jnp.dot(h, w2, precision=hp) + b2))
        h = jnp.dot(h, w3, precision=hp).reshape(l, HY_ORDER, 2, W_GROUP)
        return h * jnp.exp(-t[:, None] * deltas[None, :])[:, None, None, :]

    lag = jnp.arange(l)
    fwd = at(lag.astype(F32))
    rev = at(((l - lag) % l).astype(F32))
    first = (lag == 0)[:, None, None]
    half0 = fwd[:, :, 0] + jnp.where(first, fwd[:, :, 1], 0.0)
    half1 = jnp.where(first, 0.0, rev[:, :, 1])
    return jnp.transpose(jnp.stack([half0, half1], axis=0), (2, 0, 1, 3))


def _cplx_block(e):
    return np.block([[e.real, -e.imag], [e.imag, e.real]])


@functools.lru_cache(maxsize=None)
def _dft_consts(l):
    n = 2 * l
    n1f = n // DFT_N2
    n1h = n1f // 2
    k1, n1, n2 = np.arange(n1f), np.arange(n1h), np.arange(DFT_N2)
    ph = np.outer(k1, n1)[None] / n1f + (n2[:, None, None] * k1[None, :, None]) / n
    e1 = np.exp(-2j * np.pi * ph)
    w1 = np.stack([_cplx_block(e1[i]) for i in range(DFT_N2)])
    e2 = np.exp(-2j * np.pi * np.outer(n2, n2) / DFT_N2)
    f2 = _cplx_block(e2)
    g2 = _cplx_block(np.conj(e2).T)
    g1 = np.stack([_cplx_block(np.conj(e1[i]).T / n) for i in range(DFT_N2)])
    return tuple(jnp.asarray(a, F32).astype(BF16) for a in (w1, f2, g2, g1))


def _lconv_kernel(a_ref, m_ref, sk_ref, hf_ref, w1_ref, f2_ref, g2_ref, g1_ref, y_ref, xy_ref, s_ref, *, n1h):
    n1f = 2 * n1h

    def copy_in(i, c):
        for ri in range(2):
            xy_ref[ri, pl.ds(pl.multiple_of(i * X_PITCH, 8), DFT_N2), :] = \
                a_ref[0, ri, pl.ds(pl.multiple_of(i * DFT_N2, DFT_N2), DFT_N2), :]
        return c

    lax.fori_loop(0, n1h, copy_in, 0, unroll=4)

    def stage1(n2, c):
        slab = jnp.concatenate([xy_ref[0, pl.ds(n2, n1h, stride=X_PITCH), :],
                                xy_ref[1, pl.ds(n2, n1h, stride=X_PITCH), :]], axis=0).astype(BF16)
        a = jnp.dot(w1_ref[n2], slab, preferred_element_type=F32)
        base = pl.multiple_of(n2 * S_PITCH, 8)
        s_ref[0, pl.ds(base, n1f), :] = a[:n1f]
        s_ref[1, pl.ds(base, n1f), :] = a[n1f:]
        return c

    lax.fori_loop(0, DFT_N2, stage1, 0, unroll=32)

    def stage2(k1, c):
        slab = jnp.concatenate([s_ref[0, pl.ds(k1, DFT_N2, stride=S_PITCH), :],
                                s_ref[1, pl.ds(k1, DFT_N2, stride=S_PITCH), :]], axis=0).astype(BF16)
        x = jnp.dot(f2_ref[...], slab, preferred_element_type=F32)
        xr, xi = x[:DFT_N2], x[DFT_N2:]
        hr, hi = hf_ref[0, k1, 0], hf_ref[0, k1, 1]
        y = jnp.concatenate([xr * hr - xi * hi, xr * hi + xi * hr], axis=0).astype(BF16)
        cc = jnp.dot(g2_ref[...], y, preferred_element_type=F32)
        s_ref[0, pl.ds(k1, DFT_N2, stride=S_PITCH), :] = cc[:DFT_N2]
        s_ref[1, pl.ds(k1, DFT_N2, stride=S_PITCH), :] = cc[DFT_N2:]
        return c

    lax.fori_loop(0, n1f, stage2, 0, unroll=16)

    def stage3(n2, c):
        base = pl.multiple_of(n2 * S_PITCH, 8)
        d = jnp.concatenate([s_ref[0, pl.ds(base, n1f), :], s_ref[1, pl.ds(base, n1f), :]], axis=0).astype(BF16)
        yv = jnp.dot(g1_ref[n2], d, preferred_element_type=F32)
        yb = pl.multiple_of(n2 * Y_PITCH, 8)
        xy_ref[0, pl.ds(yb, n1h), :] = yv[:n1h]
        xy_ref[1, pl.ds(yb, n1h), :] = yv[n1h:]
        return c

    lax.fori_loop(0, DFT_N2, stage3, 0, unroll=32)

    def copy_out(i, c):
        for ri in range(2):
            rows = pl.ds(pl.multiple_of(i * DFT_N2, DFT_N2), DFT_N2)
            y = xy_ref[ri, pl.ds(i, DFT_N2, stride=Y_PITCH), :]
            y_ref[0, ri, rows, :] = (m_ref[0, ri, rows, :] * (y + a_ref[0, ri, rows, :] * sk_ref[...])).astype(y_ref.dtype)
        return c

    lax.fori_loop(0, n1h, copy_out, 0, unroll=4)


def _lconv_small_kernel(a_ref, m_ref, sk_ref, hf_ref, f_ref, g_ref, y_ref):
    l = a_ref.shape[2]
    slab = jnp.concatenate([a_ref[0, 0], a_ref[0, 1]], axis=0).astype(BF16)
    x = jnp.dot(f_ref[...], slab, preferred_element_type=F32)
    xr, xi = x[:2 * l], x[2 * l:]
    hr, hi = hf_ref[0, 0], hf_ref[0, 1]
    y = jnp.concatenate([xr * hr - xi * hi, xr * hi + xi * hr], axis=0).astype(BF16)
    out = jnp.dot(g_ref[...], y, preferred_element_type=F32)
    for ri in range(2):
        y_ref[0, ri] = (m_ref[0, ri] * (out[ri * l:(ri + 1) * l] + a_ref[0, ri] * sk_ref[...])).astype(y_ref.dtype)


@functools.lru_cache(maxsize=None)
def _dft_consts_small(l):
    n = 2 * l
    e = np.exp(-2j * np.pi * np.outer(np.arange(n), np.arange(l)) / n)
    f = _cplx_block(e)
    g = _cplx_block(np.conj(e).T / n)
    return jnp.asarray(f, F32).astype(BF16), jnp.asarray(g, F32).astype(BF16)


def _long_conv(a, hf, order, m, m_col, sk, out_dtype):
    b, l, ca = a.shape
    c = W_GROUP
    nl = c // LANES
    a4 = a.reshape(b // 2, 2, l, ca)
    m4 = m.reshape(b // 2, 2, l, m.shape[2])
    io_spec = pl.BlockSpec((1, 2, l, LANES), lambda j, p: (p, 0, 0, j))
    once = dict(pipeline_mode=pl.Buffered(1))
    m_spec = pl.BlockSpec((1, 2, l, LANES), lambda j, p: (p, 0, 0, m_col * nl + j), **once)
    sk_spec = pl.BlockSpec((1, LANES), lambda j, p: (0, j))
    const = lambda arr: pl.BlockSpec(arr.shape, lambda j, p: (0,) * arr.ndim, **once)
    if l <= DFT_N2 * 2:
        f, g = _dft_consts_small(l)
        y = pl.pallas_call(
            _lconv_small_kernel,
            out_shape=jax.ShapeDtypeStruct((b // 2, 2, l, c), out_dtype),
            grid=(nl, b // 2),
            in_specs=[io_spec, m_spec, sk_spec, pl.BlockSpec((1, 2, 2 * l, LANES), lambda j, p: (order, 0, 0, j)),
                      const(f), const(g)],
            out_specs=io_spec,
            compiler_params=_cparams(("parallel", "arbitrary")),
            name="long_conv_small",
        )(a4, m4, sk, hf, f, g)
        return y.reshape(b, l, c)
    n1f = 2 * l // DFT_N2
    n1h = n1f // 2
    assert (n1f + 8, n1h + 8) == (S_PITCH, Y_PITCH), "scratch pitches are sized for this sequence length"
    w1, f2, g2, g1 = _dft_consts(l)
    rows_xy = max(n1h * X_PITCH, DFT_N2 * Y_PITCH)
    y = pl.pallas_call(
        functools.partial(_lconv_kernel, n1h=n1h),
        out_shape=jax.ShapeDtypeStruct((b // 2, 2, l, c), out_dtype),
        grid=(nl, b // 2),
        in_specs=[io_spec, m_spec, sk_spec,
                  pl.BlockSpec((1, n1f, 2, DFT_N2, LANES), lambda j, p: (order, 0, 0, 0, j), **once),
                  const(w1), const(f2), const(g2), const(g1)],
        out_specs=io_spec,
        scratch_shapes=[pltpu.VMEM((2, rows_xy, LANES), F32), pltpu.VMEM((2, DFT_N2 * S_PITCH, LANES), F32)],
        compiler_params=_cparams(("parallel", "arbitrary"), DFT_VMEM_LIMIT),
        name="long_conv",
    )(a4, m4, sk, hf, w1, f2, g2, g1)
    return y.reshape(b, l, c)


@functools.lru_cache(maxsize=None)
def _fspec_consts(n):
    n1f = n // DFT_N2
    k1, n1, n2 = np.arange(n1f), np.arange(n1f), np.arange(DFT_N2)
    ph = np.outer(k1, n1)[None] / n1f + (n2[:, None, None] * k1[None, :, None]) / n
    e1 = np.exp(-2j * np.pi * ph)
    w1 = np.concatenate([e1.real, e1.imag], axis=1)
    f2 = _cplx_block(np.exp(-2j * np.pi * np.outer(n2, n2) / DFT_N2))
    return jnp.asarray(w1, F32).astype(BF16), jnp.asarray(f2, F32).astype(BF16)


def _fspec_kernel(f_ref, w1_ref, f2_ref, hf_ref, x_ref, s_ref, *, n1f):
    n1h = n1f // 2

    def copy_in(i, c):
        for half in range(2):
            x_ref[pl.ds(pl.multiple_of((half * n1h + i) * X_PITCH, 8), DFT_N2), :] = \
                f_ref[0, half, pl.ds(pl.multiple_of(i * DFT_N2, DFT_N2), DFT_N2), :]
        return c

    lax.fori_loop(0, n1h, copy_in, 0, unroll=4)

    def stage1(n2, c):
        slab = x_ref[pl.ds(n2, n1f, stride=X_PITCH), :].astype(BF16)
        a = jnp.dot(w1_ref[n2], slab, preferred_element_type=F32)
        base = pl.multiple_of(n2 * S_PITCH, 8)
        s_ref[0, pl.ds(base, n1f), :] = a[:n1f]
        s_ref[1, pl.ds(base, n1f), :] = a[n1f:]
        return c

    lax.fori_loop(0, DFT_N2, stage1, 0, unroll=32)

    def stage2(k1, c):
        slab = jnp.concatenate([s_ref[0, pl.ds(k1, DFT_N2, stride=S_PITCH), :],
                                s_ref[1, pl.ds(k1, DFT_N2, stride=S_PITCH), :]], axis=0).astype(BF16)
        x = jnp.dot(f2_ref[...], slab, preferred_element_type=F32)
        hf_ref[0, k1, 0] = x[:DFT_N2]
        hf_ref[0, k1, 1] = x[DFT_N2:]
        return c

    lax.fori_loop(0, n1f, stage2, 0, unroll=16)


def _fspec_small_kernel(f_ref, w_ref, hf_ref):
    n = 2 * f_ref.shape[2]
    filt = jnp.concatenate([f_ref[0, 0], f_ref[0, 1]], axis=0).astype(BF16)
    x = jnp.dot(w_ref[...], filt, preferred_element_type=F32)
    hf_ref[0, 0] = x[:n]
    hf_ref[0, 1] = x[n:]


@functools.lru_cache(maxsize=None)
def _fspec_consts_small(n):
    e = np.exp(-2j * np.pi * np.outer(np.arange(n), np.arange(n)) / n)
    return jnp.asarray(np.concatenate([e.real, e.imag], axis=0), F32).astype(BF16)


def _filter_spectra(filt):
    orders, _, l, c = filt.shape
    n = 2 * l
    f2s = filt
    once = dict(pipeline_mode=pl.Buffered(1))
    const = lambda arr: pl.BlockSpec(arr.shape, lambda o, j: (0,) * arr.ndim, **once)
    in_spec = pl.BlockSpec((1, 2, l, LANES), lambda o, j: (o, 0, 0, j))
    if l <= DFT_N2 * 2:
        w = _fspec_consts_small(n)
        return pl.pallas_call(
            _fspec_small_kernel,
            out_shape=jax.ShapeDtypeStruct((orders, 2, n, c), F32),
            grid=(orders, c // LANES),
            in_specs=[in_spec, const(w)],
            out_specs=pl.BlockSpec((1, 2, n, LANES), lambda o, j: (o, 0, 0, j)),
            compiler_params=_cparams(("parallel", "parallel")),
            name="filter_spectrum_small",
        )(f2s, w)
    n1f = n // DFT_N2
    assert n1f + 8 == S_PITCH, "scratch pitches are sized for this sequence length"
    w1, f2 = _fspec_consts(n)
    return pl.pallas_call(
        functools.partial(_fspec_kernel, n1f=n1f),
        out_shape=jax.ShapeDtypeStruct((orders, n1f, 2, DFT_N2, c), F32),
        grid=(orders, c // LANES),
        in_specs=[in_spec, const(w1), const(f2)],
        out_specs=pl.BlockSpec((1, n1f, 2, DFT_N2, LANES), lambda o, j: (o, 0, 0, 0, j)),
        scratch_shapes=[pltpu.VMEM((n1f * X_PITCH, LANES), F32), pltpu.VMEM((2, DFT_N2 * S_PITCH, LANES), F32)],
        compiler_params=_cparams(("parallel", "parallel"), DFT_VMEM_LIMIT),
        name="filter_spectrum",
    )(f2s, w1, f2)


def _hyena(uc, filt, skip):
    hf = _filter_spectra(filt)
    z1 = _long_conv(uc, hf, 0, uc, 1, skip[0:1], F32)
    return _long_conv(z1, hf, 1, uc, 2, skip[1:2], BF16)


def _permute_wa_heads(w, start, axis):
    idx = np.arange(w.shape[axis])
    blocks = [np.arange(start + h * HEAD_DIM, start + (h + 1) * HEAD_DIM) for h in WA_HEAD_ORDER]
    idx[start:start + N_WA_HEADS * HEAD_DIM] = np.concatenate(blocks)
    return jnp.take(w, idx, axis=axis)


def _layer(x, ctx, mod, lyr, rope_tabs, update_ctx, final_g):
    b, s, d = x.shape
    lc = ctx.shape[1]
    mx = [mod[:b, None, i * d:(i + 1) * d] for i in range(6)]
    mc = [mod[b:b + 1, None, i * d:(i + 1) * d] for i in range(6)]
    g1, g2 = lyr["g1"][None, :], lyr["g2"][None, :]
    w_in = _permute_wa_heads(lyr["w_in"], OFF_WA_Q, axis=1).astype(BF16)
    w_out = _permute_wa_heads(lyr["w_out"], W_GROUP, axis=0).astype(BF16)
    rw_t = lyr["router_w"].T.astype(BF16)
    w_bd = _pool_weight(lyr["pool_w"])
    pscale = lyr["pool_scale"][None, :]
    hy_args = (lyr["hy_w1"], lyr["hy_b1"], lyr["hy_w2"], lyr["hy_b2"], lyr["hy_w3"], lyr["hy_freq"])

    conv = (lyr["hy_conv_w"], lyr["hy_conv_b"][None, :])
    u_hy, u_pool, q_wa, q_na, kv = _inproj(x, mx[0], mx[1], g1, w_in, conv, rope_tabs)
    if update_ctx:
        cu_hy, cu_pool, cq_wa, cq_na, ckv = _inproj(ctx, mc[0], mc[1], g1, w_in, conv)
    else:
        ckv = _inproj(ctx, mc[0], mc[1], g1, w_in[:, OFF_KV:], kv_only=True)

    ys = [_hyena(u_hy, _hyena_filters(s, *hy_args), lyr["hy_skip"]),
          _window_attn(q_wa, kv, ckv, lyr["wa_sink"]),
          _pool_mixer(u_pool, w_bd, pscale),
          _neighbourhood_attn(q_na, kv, ckv, lyr["na_rpb"])]
    x, h, aff = _outproj(ys, x, mx[2], w_out, g2, mx[3], mx[4], rw_t)
    moe = _expert_choice_ffn(h, aff, *lyr["experts"], lyr["index"])
    x = _combine(x, *moe, mx[5], final_g)

    if update_ctx:
        ycs = [_hyena(cu_hy, _hyena_filters(lc, *hy_args), lyr["hy_skip"]),
               _ctx_attn(cq_wa, ckv, 0, 1, N_WA_KV, lyr["wa_sink"], WA_HEAD_ORDER),
               _pool_mixer(cu_pool, w_bd, pscale),
               _ctx_attn(cq_na, ckv, 1, 2, N_NA_HEADS, None)]
        ctx, hc, affc = _outproj(ycs, ctx, mc[2], w_out, g2, mc[3], mc[4], rw_t)
        moe_c = _expert_choice_ffn(hc, affc, *lyr["experts"], lyr["index"])
        ctx = _combine(ctx, *moe_c, mc[5])
    return x, ctx


def kernel(x, c, ctx, c_ctx, ada_w, ada_b, norm1_g, norm2_g, w_in, hy_conv_w, hy_conv_b, hy_w1, hy_b1, hy_w2,
           hy_b2, hy_w3, hy_freq, hy_skip, wa_sink, pool_w, pool_scale, na_rpb, w_out, router_w, exp_w_gate,
           exp_w_up, exp_w_down, final_norm_g):
    b, s, d = x.shape
    cs = jnp.zeros((MOD_ROWS, d), F32).at[:b].set(c).at[b].set(c_ctx)
    mods = _ada_mod(cs, ada_w, ada_b[:, None, :])
    rope_tabs = _rope_tables(s)
    params = dict(g1=norm1_g, g2=norm2_g, w_in=w_in, hy_conv_w=hy_conv_w, hy_conv_b=hy_conv_b, hy_w1=hy_w1,
                  hy_b1=hy_b1, hy_w2=hy_w2, hy_b2=hy_b2, hy_w3=hy_w3, hy_freq=hy_freq, hy_skip=hy_skip,
                  wa_sink=wa_sink, pool_w=pool_w, pool_scale=pool_scale, na_rpb=na_rpb, w_out=w_out,
                  router_w=router_w)
    for l in range(DEPTH):
        lyr = {k: v[l] for k, v in params.items()}
        lyr.update(index=l, experts=(exp_w_gate, exp_w_up, exp_w_down))
        last = l == DEPTH - 1
        x, ctx = _layer(x, ctx, mods[l], lyr, rope_tabs, update_ctx=not last,
                        final_g=final_norm_g[None, :] if last else None)
    return x
```

```python
import functools
import math

import jax
import jax.numpy as jnp
import numpy as np
from jax import lax
from jax.experimental import pallas as pl
from jax.experimental.pallas import tpu as pltpu

F32 = jnp.float32
BF16 = jnp.bfloat16

D_MODEL = 1024
DEPTH = 2
GRID_W = 64
HEAD_DIM = 64
W_GROUP = 256
N_WA_HEADS = 4
N_WA_KV = 2
N_NA_HEADS = 4
KV_WA = 128
OFF_POOL = 768
OFF_WA_Q = 1024
OFF_NA_Q = 1280
OFF_KV = 1536
IN_WIDTH = 2304
KV_WIDTH = IN_WIDTH - OFF_KV
HY_ORDER = 2
HY_BANDS = 16
HY_DECAY_MIN = abs(math.log(1e-2) / 1.5)
HY_DECAY_MAX = abs(math.log(1e-2) / 0.3)
WA_BLOCK = 128
WA_STEP_BLOCKS = 4
WA_HEAD_ORDER = (0, 2, 1, 3)
POOL_WINDOWS = (2, 4, 8, 16)
POOL_GROUP = 64
HALO = 8
LANES = 128
SUBLANES = 8
DFT_N2 = 128
X_PITCH = DFT_N2 + 8
S_PITCH = 64 + 8
Y_PITCH = 32 + 8
NA_ROWS = 8
NA_COLS = 16
NA_ROW_BLOCK = 8
ROPE_BASE = 10000.0
N_EXPERTS = 16
EXPERT_HIDDEN = 2048
EC_CAPACITY = 2
NORM_EPS = 1e-6
NEG_INF = -1e30
Q_SCALE = HEAD_DIM ** -0.5

FFN_ROWS = 2048
FFN_HIDDEN_TILE = 256
GATHER_CHUNK = 2048
COMBINE_TILE = 512
COMBINE_SUB = 256
COMBINE_STRIP = 128
MOD_ROWS = 16
VMEM_LIMIT = 48 * 1024 * 1024
DFT_VMEM_LIMIT = 52 * 1024 * 1024
LANE_BITS = 7
F32_VALUE_BITS = 31


def _cparams(sem, vmem=VMEM_LIMIT):
    return pltpu.CompilerParams(dimension_semantics=sem, vmem_limit_bytes=vmem)


def _nt_dot(a, b):
    return lax.dot_general(a, b, (((1,), (1,)), ((), ())), preferred_element_type=F32)


def _rmsnorm_mod(x, g, shift, scale):
    y = x * lax.rsqrt(jnp.mean(x * x, axis=-1, keepdims=True) + NORM_EPS) * g
    return y * (1.0 + scale) + shift


def _ada_kernel(c_ref, w_ref, b_ref, o_ref):
    c = c_ref[...]
    s = (c * jax.nn.sigmoid(c)).astype(BF16)
    o_ref[0] = jnp.dot(s, w_ref[0].astype(BF16), preferred_element_type=F32) + b_ref[0]


def _ada_mod(cs, ada_w, ada_b):
    nl, d, n = ada_w.shape
    tn = 1024
    return pl.pallas_call(
        _ada_kernel,
        out_shape=jax.ShapeDtypeStruct((nl, MOD_ROWS, n), F32),
        grid=(nl, n // tn),
        in_specs=[pl.BlockSpec((MOD_ROWS, d), lambda l, j: (0, 0)),
                  pl.BlockSpec((1, d, tn), lambda l, j: (l, 0, j)),
                  pl.BlockSpec((1, 1, tn), lambda l, j: (l, 0, j))],
        out_specs=pl.BlockSpec((1, MOD_ROWS, tn), lambda l, j: (l, 0, j)),
        compiler_params=_cparams(("parallel", "parallel")),
        name="ada_mod",
    )(cs, ada_w, ada_b)


def _rope(t, cos, s_up, s_dn):
    w = t.shape[-1]
    if w > 128:
        cos, s_up, s_dn = (jnp.concatenate([a] * (w // 128), axis=-1) for a in (cos, s_up, s_dn))
    up = pltpu.roll(t, w - 16, 1)
    dn = pltpu.roll(t, 16, 1)
    return t * cos + up * s_up + dn * s_dn


def _inproj_kernel(*refs, rope, kv_only):
    if kv_only:
        x_ref, sh_ref, sc_ref, g_ref, w_ref, o_ref = refs
    elif rope:
        x_ref, xp_ref, xn_ref, sh_ref, sc_ref, g_ref, w_ref, cw_ref, cb_ref, cos_ref, sup_ref, sdn_ref, *outs = refs
    else:
        x_ref, xp_ref, xn_ref, sh_ref, sc_ref, g_ref, w_ref, cw_ref, cb_ref, *outs = refs
    h = _rmsnorm_mod(x_ref[0], g_ref[...], sh_ref[0], sc_ref[0])
    p = jnp.dot(h.astype(BF16), w_ref[...], preferred_element_type=F32)
    if kv_only:
        o_ref[0] = p.astype(BF16)
        return
    hy_ref, pool_ref, qwa_ref, qna_ref, kv_ref = outs
    j, nt, tm = pl.program_id(0), pl.num_programs(0), x_ref.shape[1]
    halo = _rmsnorm_mod(jnp.concatenate([xp_ref[0], xn_ref[0]], axis=0), g_ref[...], sh_ref[0], sc_ref[0])
    ph = jnp.dot(halo.astype(BF16), w_ref[:, :OFF_POOL], preferred_element_type=F32)
    e = jnp.concatenate([jnp.where(j == 0, 0.0, ph[:HALO]), p[:, :OFF_POOL],
                         jnp.where(j == nt - 1, 0.0, ph[HALO:])], axis=0)
    n = tm + 2 * HALO
    uc = pltpu.roll(e, 1, 0) * cw_ref[0:1, :] + e * cw_ref[1:2, :] + pltpu.roll(e, n - 1, 0) * cw_ref[2:3, :]
    hy_ref[0] = uc[HALO:HALO + tm] + cb_ref[...]
    pool_ref[0] = p[:, OFF_POOL:OFF_WA_Q]
    qwa = p[:, OFF_WA_Q:OFF_NA_Q]
    kwa = p[:, OFF_KV:OFF_KV + KV_WA]
    if rope:
        tabs = (cos_ref[...], sup_ref[...], sdn_ref[...])
        qwa = _rope(qwa, *tabs)
        kwa = _rope(kwa, *tabs)
    qwa_ref[0] = (qwa * Q_SCALE).astype(BF16)
    qna_ref[0] = (p[:, OFF_NA_Q:OFF_KV] * Q_SCALE).astype(BF16)
    kv_ref[0] = jnp.concatenate([kwa, p[:, OFF_KV + KV_WA:]], axis=-1).astype(BF16)


def _inproj(x, shift, scale, g, w_bf16, conv=None, rope_tabs=None, kv_only=False):
    b, l, d = x.shape
    n = w_bf16.shape[1]
    tm = min(l, 512)
    hb, nh = tm // HALO, l // HALO
    rope = rope_tabs is not None
    bm = shift.shape[0]
    mod_map = (lambda j, i: (i, 0, 0)) if bm > 1 else (lambda j, i: (0, 0, 0))
    in_specs = [pl.BlockSpec((1, tm, d), lambda j, i: (i, j, 0))]
    args = [x]
    if not kv_only:
        in_specs += [pl.BlockSpec((1, HALO, d), lambda j, i: (i, jnp.maximum(j * hb - 1, 0), 0)),
                     pl.BlockSpec((1, HALO, d), lambda j, i: (i, jnp.minimum((j + 1) * hb, nh - 1), 0))]
        args += [x, x]
    in_specs += [pl.BlockSpec((1, 1, d), mod_map),
                 pl.BlockSpec((1, 1, d), mod_map),
                 pl.BlockSpec((1, d), lambda j, i: (0, 0)),
                 pl.BlockSpec((d, n), lambda j, i: (0, 0))]
    args += [shift, scale, g, w_bf16]
    if not kv_only:
        in_specs += [pl.BlockSpec((3, OFF_POOL), lambda j, i: (0, 0)), pl.BlockSpec((1, OFF_POOL), lambda j, i: (0, 0))]
        args += list(conv)
    if rope:
        in_specs += [pl.BlockSpec((tm, 128), lambda j, i: (j, 0))] * 3
        args += list(rope_tabs)

    def tok(width, dtype):
        return (jax.ShapeDtypeStruct((b, l, width), dtype),
                pl.BlockSpec((1, tm, width), lambda j, i: (i, j, 0)))

    if kv_only:
        outs = [tok(n, BF16)]
    else:
        outs = [tok(OFF_POOL, F32), tok(W_GROUP, F32), tok(W_GROUP, BF16), tok(W_GROUP, BF16),
                tok(KV_WIDTH, BF16)]
    res = pl.pallas_call(
        functools.partial(_inproj_kernel, rope=rope, kv_only=kv_only),
        out_shape=[o[0] for o in outs],
        grid=(l // tm, b),
        in_specs=in_specs,
        out_specs=[o[1] for o in outs],
        compiler_params=_cparams(("parallel", "parallel")),
        name="inproj_kv" if kv_only else "inproj",
    )(*args)
    return res[0] if kv_only else res


def _rope_tables(s):
    pos = jnp.arange(s)
    p2 = jnp.stack([pos // GRID_W, pos % GRID_W], axis=-1).astype(F32)
    inv = ROPE_BASE ** (-jnp.arange(16, dtype=F32) / 16)
    lane = np.arange(HEAD_DIM)
    ang = p2[:, lane // 32] * inv[lane % 16][None, :]
    first = jnp.asarray((lane % 32) < 16)[None, :]
    cos, sin = jnp.cos(ang), jnp.sin(ang)
    s_up = jnp.where(first, -sin, 0.0)
    s_dn = jnp.where(first, 0.0, sin)
    return tuple(jnp.tile(t, (1, 2)) for t in (cos, s_up, s_dn))


def _softmax_parts(parts, extra=None):
    m = parts[0].max(axis=-1, keepdims=True)
    for s in parts[1:]:
        m = jnp.maximum(m, s.max(axis=-1, keepdims=True))
    if extra is not None:
        m = jnp.maximum(m, extra)
    ps = [jnp.exp(s - m) for s in parts]
    den = ps[0].sum(axis=-1, keepdims=True)
    for p in ps[1:]:
        den = den + p.sum(axis=-1, keepdims=True)
    if extra is not None:
        den = den + jnp.exp(extra - m)
    return ps, den


def _head_stack(t, masks):
    return jnp.concatenate([jnp.where(m, t, jnp.zeros_like(t)) for m in masks], axis=0)


def _wattn_kernel(sink_ref, q_ref, kp_ref, ko_ref, kn_ref, vp_ref, vo_ref, vn_ref, kc_ref, vc_ref, o_ref, *, nb):
    n = pl.program_id(1)
    blk = WA_BLOCK
    q = q_ref[0]
    kall = jnp.concatenate([kp_ref[0], ko_ref[0], kn_ref[0]], axis=0)
    vall = jnp.concatenate([vp_ref[0], vo_ref[0], vn_ref[0]], axis=0)
    kc, vc = kc_ref[0], vc_ref[0]
    rows = N_WA_HEADS * blk
    i = lax.broadcasted_iota(jnp.int32, (rows, 3 * blk), 0) & (blk - 1)
    j = lax.broadcasted_iota(jnp.int32, (rows, 3 * blk), 1)
    band = (j >= i) & (j <= i + 2 * blk)
    kv0 = lax.broadcasted_iota(jnp.int32, (blk, KV_WA), 1) < HEAD_DIM
    head = lax.broadcasted_iota(jnp.int32, (rows, 1), 0) // blk
    snk = jnp.where(head == 0, sink_ref[0],
                    jnp.where(head == 1, sink_ref[1], jnp.where(head == 2, sink_ref[2], sink_ref[3])))
    for sb in range(WA_STEP_BLOCKS):
        gb = n * WA_STEP_BLOCKS + sb
        jlo = jnp.where(gb == 0, blk, 0)
        jhi = jnp.where(gb == nb - 1, 2 * blk, 3 * blk)
        valid = band & (j >= jlo) & (j < jhi)
        keys = slice(sb * blk, (sb + 3) * blk)
        qa = q[sb * blk:(sb + 1) * blk, :KV_WA]
        qb = q[sb * blk:(sb + 1) * blk, KV_WA:]
        zero = jnp.zeros_like(qa)
        qs = jnp.concatenate([jnp.where(kv0, qa, zero), jnp.where(kv0, qb, zero),
                              jnp.where(kv0, zero, qa), jnp.where(kv0, zero, qb)], axis=0)
        s_loc = jnp.where(valid, _nt_dot(qs, kall[keys]), NEG_INF)
        s_ctx = _nt_dot(qs, kc)
        (p_loc, p_ctx), den = _softmax_parts([s_loc, s_ctx], snk)
        o = (jnp.dot(p_loc.astype(BF16), vall[keys], preferred_element_type=F32)
             + jnp.dot(p_ctx.astype(BF16), vc, preferred_element_type=F32)) / den
        o_ref[0, sb * blk:(sb + 1) * blk, :] = jnp.concatenate(
            [jnp.where(kv0, o[:blk], o[2 * blk:3 * blk]), jnp.where(kv0, o[blk:2 * blk], o[3 * blk:])],
            axis=-1).astype(BF16)


def _window_attn(q, kv, ckv, sink):
    b, s, _ = q.shape
    lc = ckv.shape[1]
    nb = s // WA_BLOCK
    sbk = WA_STEP_BLOCKS

    def halo_spec(col, off):
        return pl.BlockSpec((1, WA_BLOCK, KV_WA),
                            lambda i, n: (i, jnp.clip(n * sbk + off, 0, nb - 1), col))

    def own_spec(col):
        return pl.BlockSpec((1, sbk * WA_BLOCK, KV_WA), lambda i, n: (i, n, col))

    return pl.pallas_call(
        functools.partial(_wattn_kernel, nb=nb),
        out_shape=jax.ShapeDtypeStruct((b, s, W_GROUP), BF16),
        grid=(b, nb // sbk),
        in_specs=[pl.BlockSpec(memory_space=pltpu.SMEM),
                  pl.BlockSpec((1, sbk * WA_BLOCK, W_GROUP), lambda i, n: (i, n, 0)),
                  halo_spec(0, -1), own_spec(0), halo_spec(0, sbk),
                  halo_spec(1, -1), own_spec(1), halo_spec(1, sbk),
                  pl.BlockSpec((1, lc, KV_WA), lambda i, n: (i, 0, 0)),
                  pl.BlockSpec((1, lc, KV_WA), lambda i, n: (i, 0, 1))],
        out_specs=pl.BlockSpec((1, sbk * WA_BLOCK, W_GROUP), lambda i, n: (i, n, 0)),
        compiler_params=_cparams(("parallel", "parallel")),
        name="window_attn",
    )(sink, q, kv, kv, kv, kv, kv, kv, ckv, ckv)


def _nattn_kernel(q_ref, k_ref, v_ref, kc_ref, vc_ref, bias_ref, o_ref, *, rows):
    blk = pl.program_id(1)
    nk = NA_ROWS * GRID_W
    kc, vc = kc_ref[0], vc_ref[0]
    lane_head = lax.broadcasted_iota(jnp.int32, (GRID_W, W_GROUP), 1) // HEAD_DIM
    masks = [lane_head == h for h in range(N_NA_HEADS)]
    for rr in range(NA_ROW_BLOCK):
        r = blk * NA_ROW_BLOCK + rr
        r0 = jnp.clip(r - NA_ROWS // 2, 0, rows - NA_ROWS)
        var = r - r0
        start = pl.multiple_of(r0 * GRID_W, GRID_W)
        kt = k_ref[0, pl.ds(start, nk), :]
        vt = v_ref[0, pl.ds(start, nk), :]
        qrow = slice(rr * GRID_W, (rr + 1) * GRID_W)
        qs = _head_stack(q_ref[0, qrow, :], masks)
        bias = bias_ref[:, var].reshape(N_NA_HEADS * GRID_W, nk)
        (p_loc, p_ctx), den = _softmax_parts([_nt_dot(qs, kt) + bias, _nt_dot(qs, kc)])
        o = (jnp.dot(p_loc.astype(BF16), vt, preferred_element_type=F32)
             + jnp.dot(p_ctx.astype(BF16), vc, preferred_element_type=F32)) / den
        out = o[(N_NA_HEADS - 1) * GRID_W:]
        for h in range(N_NA_HEADS - 2, -1, -1):
            out = jnp.where(masks[h], o[h * GRID_W:(h + 1) * GRID_W], out)
        o_ref[0, qrow, :] = out.astype(BF16)


def _na_bias(rpb):
    var = np.arange(NA_ROWS)
    j = np.arange(NA_ROWS)
    qc = np.arange(GRID_W)
    kc = np.arange(GRID_W)
    dr = j[None, :] - var[:, None] + NA_ROWS - 1
    dc = np.clip(kc[None, :] - qc[:, None] + NA_COLS - 1, 0, 2 * NA_COLS - 2)
    ws = np.clip(qc - NA_COLS // 2, 0, GRID_W - NA_COLS)
    ok = (kc[None, :] >= ws[:, None]) & (kc[None, :] < ws[:, None] + NA_COLS)
    onehot = (dc[None] == np.arange(2 * NA_COLS - 1)[:, None, None]).astype(np.float32)
    bias = jnp.einsum("hvjd,dqk->hvqjk", rpb.astype(F32)[:, dr], jnp.asarray(onehot),
                      precision=lax.Precision.HIGHEST)
    bias = jnp.where(jnp.asarray(ok)[None, None, :, None, :], bias, NEG_INF)
    return bias.reshape(rpb.shape[0], NA_ROWS, GRID_W, NA_ROWS * GRID_W)


def _neighbourhood_attn(q, kv, ckv, rpb):
    b, s, _ = q.shape
    lc = ckv.shape[1]
    rows = s // GRID_W
    bias = _na_bias(rpb)
    return pl.pallas_call(
        functools.partial(_nattn_kernel, rows=rows),
        out_shape=jax.ShapeDtypeStruct((b, s, W_GROUP), BF16),
        grid=(b, rows // NA_ROW_BLOCK),
        in_specs=[pl.BlockSpec((1, NA_ROW_BLOCK * GRID_W, W_GROUP), lambda i, r: (i, r, 0)),
                  pl.BlockSpec((1, s, W_GROUP), lambda i, r: (i, 0, 1)),
                  pl.BlockSpec((1, s, W_GROUP), lambda i, r: (i, 0, 2)),
                  pl.BlockSpec((1, lc, W_GROUP), lambda i, r: (i, 0, 1)),
                  pl.BlockSpec((1, lc, W_GROUP), lambda i, r: (i, 0, 2)),
                  pl.BlockSpec(bias.shape, lambda i, r: (0, 0, 0, 0))],
        out_specs=pl.BlockSpec((1, NA_ROW_BLOCK * GRID_W, W_GROUP), lambda i, r: (i, r, 0)),
        compiler_params=_cparams(("parallel", "arbitrary")),
        name="neighbourhood_attn",
    )(q, kv, kv, ckv, ckv, bias)


def _cattn_kernel(*refs, n_kv, with_sink, head_order):
    if with_sink:
        sink_ref, q_ref, k_ref, v_ref, o_ref = refs
    else:
        q_ref, k_ref, v_ref, o_ref = refs
    q, k, v = q_ref[0], k_ref[0], v_ref[0]
    group = N_WA_HEADS // n_kv
    outs = []
    for pos, h in enumerate(head_order):
        sl = slice((h // group) * HEAD_DIM, (h // group + 1) * HEAD_DIM)
        s = _nt_dot(q[:, pos * HEAD_DIM:(pos + 1) * HEAD_DIM], k[:, sl])
        extra = jnp.full((s.shape[0], 1), sink_ref[h], F32) if with_sink else None
        (p,), den = _softmax_parts([s], extra)
        outs.append(jnp.dot(p.astype(BF16), v[:, sl], preferred_element_type=F32) / den)
    o_ref[0] = jnp.concatenate(outs, axis=-1).astype(BF16)


def _ctx_attn(q, ckv, k_col, v_col, n_kv, sink, head_order=(0, 1, 2, 3)):
    b, lc, _ = q.shape
    w = n_kv * HEAD_DIM
    with_sink = sink is not None
    in_specs = [pl.BlockSpec((1, lc, W_GROUP), lambda i: (i, 0, 0)),
                pl.BlockSpec((1, lc, w), lambda i: (i, 0, k_col)),
                pl.BlockSpec((1, lc, w), lambda i: (i, 0, v_col))]
    args = [q, ckv, ckv]
    if with_sink:
        in_specs = [pl.BlockSpec(memory_space=pltpu.SMEM)] + in_specs
        args = [sink] + args
    return pl.pallas_call(
        functools.partial(_cattn_kernel, n_kv=n_kv, with_sink=with_sink, head_order=head_order),
        out_shape=jax.ShapeDtypeStruct((b, lc, W_GROUP), BF16),
        grid=(b,),
        in_specs=in_specs,
        out_specs=pl.BlockSpec((1, lc, W_GROUP), lambda i: (i, 0, 0)),
        compiler_params=_cparams(("parallel",)),
        name="ctx_attn",
    )(*args)


def _pool_kernel(prev_ref, cur_ref, next_ref, w_ref, scale_ref, o_ref, *, seq, tl):
    j = pl.program_id(1)
    nt = seq // tl
    cur = cur_ref[0]
    prev = jnp.where(j == 0, 0.0, prev_ref[0])
    nxt = jnp.where(j == nt - 1, 0.0, next_ref[0])
    e = jnp.concatenate([prev, cur, nxt], axis=0)
    n = tl + 2 * HALO

    def sh(a, d):
        return pltpu.roll(a, d % n, 0)

    s2 = e + sh(e, 1)
    s4 = sh(s2, 1) + sh(s2, -1)
    s8 = sh(s4, 2) + sh(s4, -2)
    s16 = sh(s8, 4) + sh(s8, -4)
    lane = lax.broadcasted_iota(jnp.int32, (tl, W_GROUP), 1)
    t = lax.broadcasted_iota(jnp.int32, (tl, W_GROUP), 0) + j * tl
    g = lane // POOL_GROUP
    lo, hi = HALO, HALO + tl
    ssum = jnp.where(g == 0, s2[lo:hi], jnp.where(g == 1, s4[lo:hi], jnp.where(g == 2, s8[lo:hi], s16[lo:hi])))
    half = jnp.where(g == 0, 1, jnp.where(g == 1, 2, jnp.where(g == 2, 4, 8)))
    cnt = (jnp.minimum(t + half, seq) - jnp.maximum(t - half, 0)).astype(F32)
    d = (ssum / cnt - cur).astype(BF16)
    o_ref[0] = (jnp.dot(d, w_ref[...], preferred_element_type=F32) * scale_ref[...]).astype(BF16)


def _pool_mixer(u, w_bd, scale):
    b, l, c = u.shape
    tl = min(l, 512)
    hb = tl // HALO
    nh = l // HALO
    return pl.pallas_call(
        functools.partial(_pool_kernel, seq=l, tl=tl),
        out_shape=jax.ShapeDtypeStruct((b, l, c), BF16),
        grid=(b, l // tl),
        in_specs=[pl.BlockSpec((1, HALO, c), lambda i, j: (i, jnp.maximum(j * hb - 1, 0), 0)),
                  pl.BlockSpec((1, tl, c), lambda i, j: (i, j, 0)),
                  pl.BlockSpec((1, HALO, c), lambda i, j: (i, jnp.minimum((j + 1) * hb, nh - 1), 0)),
                  pl.BlockSpec((c, c), lambda i, j: (0, 0)),
                  pl.BlockSpec((1, c), lambda i, j: (0, 0))],
        out_specs=pl.BlockSpec((1, tl, c), lambda i, j: (i, j, 0)),
        compiler_params=_cparams(("parallel", "parallel")),
        name="pool_mixer",
    )(u, u, u, w_bd, scale)


def _pool_weight(pool_w):
    z = jnp.zeros((W_GROUP, W_GROUP), F32)
    for g in range(len(POOL_WINDOWS)):
        z = z.at[g * POOL_GROUP:(g + 1) * POOL_GROUP, g * POOL_GROUP:(g + 1) * POOL_GROUP].set(pool_w[g])
    return z.astype(BF16)


def _outproj_kernel(yh_ref, ya_ref, yp_ref, yn_ref, x_ref, gt_ref, w_ref, g2_ref, sh_ref, sc_ref, rw_ref,
                    xo_ref, h_ref, aff_ref):
    y = jnp.concatenate([yh_ref[0], ya_ref[0], yp_ref[0], yn_ref[0]], axis=-1)
    x = x_ref[0] + gt_ref[0] * jnp.dot(y, w_ref[...], preferred_element_type=F32)
    xo_ref[0] = x
    h = _rmsnorm_mod(x, g2_ref[...], sh_ref[0], sc_ref[0])
    nsl = h.shape[1] // LANES
    for s in range(nsl):
        h_ref[0, pl.ds(s, h.shape[0], stride=nsl), :] = h[:, s * LANES:(s + 1) * LANES]
    logits = _nt_dot(rw_ref[...], h.astype(BF16))
    m = logits.max(axis=0, keepdims=True)
    p = jnp.exp(logits - m)
    aff_ref[0] = p / p.sum(axis=0, keepdims=True)


def _outproj(ys, x, gate, w_bf16, g2, shift, scale, rw_t):
    b, l, d = x.shape
    tm = min(l, 512)
    bm = gate.shape[0]
    mod_map = (lambda i, j: (i, 0, 0)) if bm > 1 else (lambda i, j: (0, 0, 0))
    tok = lambda w: pl.BlockSpec((1, tm, w), lambda i, j: (i, j, 0))
    mod = pl.BlockSpec((1, 1, d), mod_map)
    return pl.pallas_call(
        _outproj_kernel,
        out_shape=[jax.ShapeDtypeStruct((b, l, d), F32), jax.ShapeDtypeStruct((b, l * d // LANES, LANES), F32),
                   jax.ShapeDtypeStruct((b, N_EXPERTS, l), F32)],
        grid=(b, l // tm),
        in_specs=[tok(W_GROUP)] * 4 + [tok(d), mod, pl.BlockSpec((d, d), lambda i, j: (0, 0)),
                                        pl.BlockSpec((1, d), lambda i, j: (0, 0)), mod, mod,
                                        pl.BlockSpec((N_EXPERTS, d), lambda i, j: (0, 0))],
        out_specs=[tok(d), pl.BlockSpec((1, tm * d // LANES, LANES), lambda i, j: (i, j, 0)),
                   pl.BlockSpec((1, N_EXPERTS, tm), lambda i, j: (i, 0, j))],
        compiler_params=_cparams(("parallel", "parallel")),
        name="outproj_router",
    )(*ys, x, gate, w_bf16, g2, shift, scale, rw_t)


def _row_gather_kernel(idx_ref, nxt_ref, table_ref, out_ref, stage, sems):
    i, n = pl.program_id(0), pl.num_programs(0)
    cur = i % 2
    chunk, d = out_ref.shape
    nsl = d // LANES

    def issue(ref, s):
        def one(r, c):
            pltpu.make_async_copy(table_ref.at[pl.ds(pl.multiple_of(ref[0, 0, r] * nsl, nsl), nsl)],
                                  stage.at[s, pl.ds(pl.multiple_of(r * nsl, nsl), nsl)], sems.at[s]).start()
            return c
        lax.fori_loop(0, chunk, one, 0, unroll=8)

    @pl.when(i == 0)
    def _():
        issue(idx_ref, cur)

    @pl.when(i + 1 < n)
    def _():
        issue(nxt_ref, 1 - cur)

    pltpu.make_async_copy(table_ref.at[pl.ds(0, chunk * nsl)], stage.at[cur], sems.at[cur]).wait()
    for s in range(nsl):
        out_ref[:, s * LANES:(s + 1) * LANES] = stage[cur, pl.ds(s, chunk, stride=nsl), :].astype(BF16)


def _row_gather(table, rows, d):
    n = rows.shape[0]
    chunk = min(GATHER_CHUNK, n)
    nch = n // chunk
    idx_spec = lambda off: pl.BlockSpec((1, 1, chunk), lambda i: (jnp.minimum(i + off, nch - 1), 0, 0),
                                        memory_space=pltpu.SMEM)
    rows3 = rows.reshape(nch, 1, chunk)
    return pl.pallas_call(
        _row_gather_kernel,
        out_shape=jax.ShapeDtypeStruct((n, d), BF16),
        grid=(nch,),
        in_specs=[idx_spec(0), idx_spec(1), pl.BlockSpec(memory_space=pl.ANY)],
        out_specs=pl.BlockSpec((chunk, d), lambda i: (i, 0)),
        scratch_shapes=[pltpu.VMEM((2, chunk * d // LANES, LANES), table.dtype), pltpu.SemaphoreType.DMA((2,))],
        compiler_params=pltpu.CompilerParams(dimension_semantics=("arbitrary",), vmem_limit_bytes=VMEM_LIMIT,
                                             disable_bounds_checks=True),
        name="moe_row_gather",
    )(rows3, rows3, table)


def _moe_kernel(x_ref, g_ref, wg_ref, wu_ref, wd_ref, o_ref):
    f = pl.program_id(2)

    @pl.when(f == 0)
    def _():
        o_ref[0] = jnp.zeros(o_ref.shape[1:], F32)

    x = x_ref[0]
    a = jnp.dot(x, wg_ref[0, 0].astype(BF16), preferred_element_type=F32)
    u = jnp.dot(x, wu_ref[0, 0].astype(BF16), preferred_element_type=F32)
    hid = (a * jax.nn.sigmoid(a) * u).astype(BF16)
    o_ref[0] += jnp.dot(hid, wd_ref[0, 0].astype(BF16), preferred_element_type=F32)

    @pl.when(f == pl.num_programs(2) - 1)
    def _():
        o_ref[0] = o_ref[0] * g_ref[0]


def _moe_ffn(xg, gate, w_gate, w_up, w_down, lyr):
    e, m, d = xg.shape
    f = w_gate.shape[3]
    tm = min(m, FFN_ROWS)
    tf = FFN_HIDDEN_TILE
    return pl.pallas_call(
        _moe_kernel,
        out_shape=jax.ShapeDtypeStruct((e, m, d), F32),
        grid=(e, m // tm, f // tf),
        in_specs=[pl.BlockSpec((1, tm, d), lambda i, j, k: (i, j, 0)),
                  pl.BlockSpec((1, tm, 1), lambda i, j, k: (i, j, 0)),
                  pl.BlockSpec((1, 1, d, tf), lambda i, j, k: (lyr, i, 0, k)),
                  pl.BlockSpec((1, 1, d, tf), lambda i, j, k: (lyr, i, 0, k)),
                  pl.BlockSpec((1, 1, tf, d), lambda i, j, k: (lyr, i, k, 0))],
        out_specs=pl.BlockSpec((1, tm, d), lambda i, j, k: (i, j, 0)),
        compiler_params=_cparams(("parallel", "parallel", "arbitrary")),
        name="moe_ffn",
    )(xg, gate, w_gate, w_up, w_down)


def _route_kernel(aff_ref, tri_ref, idx_ref, g_ref, *, k):
    x = aff_ref[...]
    r, l = x.shape
    nb = l // LANES
    nq = idx_ref.shape[1] // LANES
    def count(mask):
        return jnp.sum(jnp.where(mask, 1.0, 0.0), axis=-1, keepdims=True)

    def as_float(pattern):
        return pltpu.bitcast(jnp.broadcast_to(pattern, (r, LANES)), F32)[:, :1]

    def bisect(i, t):
        cand = t | jnp.left_shift(jnp.int32(1), F32_VALUE_BITS - 1 - i)
        return jnp.where(count(x >= as_float(cand)) >= k, cand, t)

    kth = lax.fori_loop(0, F32_VALUE_BITS, bisect, jnp.zeros((r, 1), jnp.int32))
    gt = x >= as_float(kth + 1)
    eq = (x >= as_float(kth)) & jnp.logical_not(gt)
    need = k - count(gt)
    tri = tri_ref[...]
    lane = lax.broadcasted_iota(jnp.int32, (r, LANES), 1)

    def block_prefix(mask, j):
        m = jnp.where(mask[:, j * LANES:(j + 1) * LANES], 1.0, 0.0)
        return jnp.dot(m.astype(BF16), tri, preferred_element_type=F32), m

    out_tok = [jnp.zeros((r, LANES), jnp.int32) for _ in range(nq)]
    out_g = [jnp.zeros((r, LANES), F32) for _ in range(nq)]
    eq_before = jnp.zeros((r, 1), F32)
    start = jnp.zeros((r, 1), jnp.int32)
    for j in range(nb):
        blk = slice(j * LANES, (j + 1) * LANES)
        eq_inc, eq_f = block_prefix(eq, j)
        sel = gt[:, blk] | (eq[:, blk] & (eq_inc - eq_f + eq_before < need))
        eq_before = eq_before + eq_inc[:, LANES - 1:]
        sel_f = jnp.where(sel, 1.0, 0.0)
        sel_inc = jnp.dot(sel_f.astype(BF16), tri, preferred_element_type=F32)
        cnt = sel_inc[:, LANES - 1:].astype(jnp.int32)
        d = jnp.where(sel, lane - (sel_inc - sel_f).astype(jnp.int32), -1)
        tok = lane + j * LANES
        gv = x[:, blk]
        for kb in range(LANE_BITS):
            sh = LANES - (1 << kb)
            rd, rt, rg = pltpu.roll(d, sh, 1), pltpu.roll(tok, sh, 1), pltpu.roll(gv, sh, 1)
            arrive = (rd >= 0) & (((rd >> kb) & 1) == 1)
            stay = (d >= 0) & (((d >> kb) & 1) == 0)
            tok = jnp.where(arrive, rt, tok)
            gv = jnp.where(arrive, rg, gv)
            d = jnp.where(arrive, rd, jnp.where(stay, d, -1))
        s, q = start & (LANES - 1), start >> LANE_BITS
        for kb in range(LANE_BITS):
            mv = ((s >> kb) & 1) == 1
            tok = jnp.where(mv, pltpu.roll(tok, 1 << kb, 1), tok)
            gv = jnp.where(mv, pltpu.roll(gv, 1 << kb, 1), gv)
        end = s + cnt
        here = (lane >= s) & (lane < end)
        wrapped = lane < end - LANES
        for qq in range(nq):
            hit = (here & (q == qq)) | (wrapped & (q + 1 == qq))
            out_tok[qq] = jnp.where(hit, tok, out_tok[qq])
            out_g[qq] = jnp.where(hit, gv, out_g[qq])
        start = start + cnt
    idx_ref[...] = jnp.concatenate(out_tok, axis=1)
    g_ref[...] = jnp.concatenate(out_g, axis=1)


def _route(aff_t, k):
    b, e, l = aff_t.shape
    r = b * e
    width = -(-k // LANES) * LANES
    tri = jnp.asarray(np.triu(np.ones((LANES, LANES), np.float32))).astype(BF16)
    idx, g = pl.pallas_call(
        functools.partial(_route_kernel, k=k),
        out_shape=[jax.ShapeDtypeStruct((r, width), jnp.int32), jax.ShapeDtypeStruct((r, width), F32)],
        compiler_params=pltpu.CompilerParams(vmem_limit_bytes=VMEM_LIMIT),
        name="expert_choice_route",
    )(aff_t.reshape(r, l), tri)
    return idx[:, :k].reshape(b, e, k), g[:, :k].reshape(b, e, k)


def _expert_choice_ffn(h, aff_t, w_gate, w_up, w_down, lyr):
    b, _, l = aff_t.shape
    d = h.shape[1] * LANES // l
    cap = EC_CAPACITY * l // N_EXPERTS
    m = b * cap
    idx, g = _route(aff_t, cap)
    idx_t = jnp.swapaxes(idx, 0, 1)
    rows = (idx_t + (jnp.arange(b, dtype=jnp.int32) * l)[None, :, None]).reshape(-1)
    xg = _row_gather(h.reshape(-1, LANES), rows, d).reshape(N_EXPERTS, m, d)
    gate = jnp.swapaxes(g, 0, 1).reshape(N_EXPERTS, m, 1)
    o = _moe_ffn(xg, gate, w_gate, w_up, w_down, lyr).reshape(N_EXPERTS * m, d)
    return o, idx


def _combine_kernel(*refs, final, tile, nt, cap, strip, m):
    if final:
        lo_ref, idx_ref, x_ref, gt_ref, g_ref, o_hbm, out_ref, stage, sems = refs
    else:
        lo_ref, idx_ref, x_ref, gt_ref, o_hbm, out_ref, stage, sems = refs
    bi, j = pl.program_id(0), pl.program_id(1)
    step = bi * nt + j
    cur = step % 2
    ne = N_EXPERTS
    gps = strip // SUBLANES
    per_sub = COMBINE_SUB // strip
    win = 2 * LANES
    sentinel = tile * nt

    def runs(jj):
        out = []
        for e in range(ne):
            lo = lo_ref[bi, e * (nt + 1) + jj]
            hi = lo_ref[bi, e * (nt + 1) + jj + 1]
            a8 = (lo // SUBLANES) * SUBLANES
            out.append((a8, jnp.where(hi > lo, hi - a8, 0)))
        return out

    def groups(run, r):
        return [(a8 + strip * r, jnp.clip((n - strip * r + SUBLANES - 1) // SUBLANES, 0, gps)) for a8, n in run]

    def issue(s, grp):
        for e, (s0, ng) in enumerate(grp):
            base = e * m + bi * cap + s0

            def one(k, c, base=base, e=e):
                pltpu.make_async_copy(
                    o_hbm.at[pl.ds(pl.multiple_of(base + k * SUBLANES, SUBLANES), SUBLANES)],
                    stage.at[s, pl.ds(pl.multiple_of(e * strip + k * SUBLANES, SUBLANES), SUBLANES)],
                    sems.at[s]).start()
                return c
            lax.fori_loop(0, ng, one, 0)

    def wait(s, grp):
        cnt = sum(ng for _, ng in grp) * SUBLANES

        @pl.when(cnt > 0)
        def _():
            rows = pl.ds(0, pl.multiple_of(cnt, SUBLANES))
            pltpu.make_async_copy(o_hbm.at[rows], stage.at[s, rows], sems.at[s]).wait()

    def reduce(s, grp):
        t_ids = lax.broadcasted_iota(jnp.int32, (tile, COMBINE_SUB), 0) + j * tile
        acc = jnp.zeros((tile, out_ref.shape[2]), F32)
        for sub in range(ne // per_sub):
            pieces = []
            for e in range(sub * per_sub, (sub + 1) * per_sub):
                s0, ng = grp[e]
                a = jnp.minimum((s0 // LANES) * LANES, idx_ref.shape[2] - win)
                w = idx_ref[0, e:e + 1, pl.ds(pl.multiple_of(a, LANES), win)]
                w = pltpu.roll(w, (win - (s0 - a)) % win, 1)
                pieces.append(jnp.where(ng > 0, w[:, :strip], sentinel))
            toks = jnp.concatenate(pieces, axis=1)
            rows = stage[s, sub * COMBINE_SUB:(sub + 1) * COMBINE_SUB, :]
            onehot = jnp.where(toks == t_ids, 1.0, 0.0).astype(BF16)
            hi = rows.astype(BF16)
            lo = (rows - hi.astype(F32)).astype(BF16)
            acc = (acc + jnp.dot(onehot, hi, preferred_element_type=F32)
                   + jnp.dot(onehot, lo, preferred_element_type=F32))
        return acc

    @pl.when(step == 0)
    def _():
        stage[...] = jnp.zeros(stage.shape, F32)

    run = runs(j)
    first = groups(run, 0)

    @pl.when(j == 0)
    def _():
        issue(cur, first)

    @pl.when(j + 1 < nt)
    def _():
        issue(1 - cur, groups(runs(jnp.minimum(j + 1, nt - 1)), 0))

    wait(cur, first)
    acc = reduce(cur, first)
    longest = run[0][1]
    for _, n in run[1:]:
        longest = jnp.maximum(longest, n)

    def extra_round(r, acc):
        grp = groups(run, r)
        issue(cur, grp)
        wait(cur, grp)
        return acc + reduce(cur, grp)

    acc = lax.fori_loop(1, (longest + strip - 1) // strip, extra_round, acc)
    x = x_ref[0] + gt_ref[0] * acc
    if final:
        x = x * lax.rsqrt(jnp.mean(x * x, axis=-1, keepdims=True) + NORM_EPS) * g_ref[...]
    out_ref[0] = x


def _combine(x, o, idx, gate, final_g=None):
    b, l, d = x.shape
    ne, cap = idx.shape[1], idx.shape[2]
    tile = min(l, COMBINE_TILE)
    nt = l // tile
    strip = min(COMBINE_STRIP, cap)
    bounds = jnp.arange(nt + 1, dtype=jnp.int32) * tile
    lo = jnp.sum(idx[:, :, None, :] < bounds[None, None, :, None], axis=-1, dtype=jnp.int32)
    cp = -(-cap // LANES) * LANES + LANES
    idx_p = jnp.pad(idx, ((0, 0), (0, 0), (0, cp - cap)), constant_values=l)
    bm = gate.shape[0]
    mod_map = (lambda i, j: (i, 0, 0)) if bm > 1 else (lambda i, j: (0, 0, 0))
    tok = pl.BlockSpec((1, tile, d), lambda i, j: (i, j, 0))
    final = final_g is not None
    in_specs = [pl.BlockSpec(memory_space=pltpu.SMEM),
                pl.BlockSpec((1, ne, cp), lambda i, j: (i, 0, 0)),
                tok, pl.BlockSpec((1, 1, d), mod_map)]
    args = [lo.reshape(b, ne * (nt + 1)), idx_p, x, gate]
    if final:
        in_specs.append(pl.BlockSpec((1, d), lambda i, j: (0, 0)))
        args.append(final_g)
    in_specs.append(pl.BlockSpec(memory_space=pl.ANY))
    args.append(o)
    return pl.pallas_call(
        functools.partial(_combine_kernel, final=final, tile=tile, nt=nt, cap=cap, strip=strip, m=b * cap),
        out_shape=jax.ShapeDtypeStruct((b, l, d), F32),
        grid=(b, nt),
        in_specs=in_specs,
        out_specs=tok,
        scratch_shapes=[pltpu.VMEM((2, ne * strip, d), F32), pltpu.SemaphoreType.DMA((2,))],
        compiler_params=pltpu.CompilerParams(dimension_semantics=("arbitrary", "arbitrary"),
                                             vmem_limit_bytes=VMEM_LIMIT, disable_bounds_checks=True),
        name="moe_combine_final" if final else "moe_combine",
    )(*args)


def _hyena_filters(l, w1, b1, w2, b2, w3, freq):
    hp = lax.Precision.HIGHEST
    bands = jnp.linspace(1e-4, HY_BANDS - 1, HY_BANDS, dtype=F32)
    deltas = jnp.linspace(HY_DECAY_MIN, HY_DECAY_MAX, W_GROUP, dtype=F32)

    def at(pos):
        t = pos / max(l - 1, 1)
        ang = (2 * math.pi / l) * pos[:, None] * bands[None, :]
        feats = jnp.concatenate([t[:, None], jnp.cos(ang), -jnp.sin(ang)], axis=-1)
        h = jnp.sin(freq * (jnp.dot(feats, w1, precision=hp) + b1))
        h = jnp.sin(freq * (jnp.dot(h, w2, precision=hp) + b2))
        h = jnp.dot(h, w3, precision=hp).reshape(l, HY_ORDER, 2, W_GROUP)
        return h * jnp.exp(-t[:, None] * deltas[None, :])[:, None, None, :]

    lag = jnp.arange(l)
    fwd = at(lag.astype(F32))
    rev = at(((l - lag) % l).astype(F32))
    first = (lag == 0)[:, None, None]
    half0 = fwd[:, :, 0] + jnp.where(first, fwd[:, :, 1], 0.0)
    half1 = jnp.where(first, 0.0, rev[:, :, 1])
    return jnp.transpose(jnp.stack([half0, half1], axis=0), (2, 0, 1, 3))


def _cplx_block(e):
    return np.block([[e.real, -e.imag], [e.imag, e.real]])


@functools.lru_cache(maxsize=None)
def _dft_consts(l):
    n = 2 * l
    n1f = n // DFT_N2
    n1h = n1f // 2
    k1, n1, n2 = np.arange(n1f), np.arange(n1h), np.arange(DFT_N2)
    ph = np.outer(k1, n1)[None] / n1f + (n2[:, None, None] * k1[None, :, None]) / n
    e1 = np.exp(-2j * np.pi * ph)
    w1 = np.stack([_cplx_block(e1[i]) for i in range(DFT_N2)])
    e2 = np.exp(-2j * np.pi * np.outer(n2, n2) / DFT_N2)
    f2 = _cplx_block(e2)
    g2 = _cplx_block(np.conj(e2).T)
    g1 = np.stack([_cplx_block(np.conj(e1[i]).T / n) for i in range(DFT_N2)])
    return tuple(jnp.asarray(a, F32).astype(BF16) for a in (w1, f2, g2, g1))


def _lconv_kernel(a_ref, m_ref, sk_ref, hf_ref, w1_ref, f2_ref, g2_ref, g1_ref, y_ref, xy_ref, s_ref, *, n1h):
    n1f = 2 * n1h

    def copy_in(i, c):
        for ri in range(2):
            xy_ref[ri, pl.ds(pl.multiple_of(i * X_PITCH, 8), DFT_N2), :] = \
                a_ref[0, ri, pl.ds(pl.multiple_of(i * DFT_N2, DFT_N2), DFT_N2), :]
        return c

    lax.fori_loop(0, n1h, copy_in, 0, unroll=4)

    def stage1(n2, c):
        slab = jnp.concatenate([xy_ref[0, pl.ds(n2, n1h, stride=X_PITCH), :],
                                xy_ref[1, pl.ds(n2, n1h, stride=X_PITCH), :]], axis=0).astype(BF16)
        a = jnp.dot(w1_ref[n2], slab, preferred_element_type=F32)
        base = pl.multiple_of(n2 * S_PITCH, 8)
        s_ref[0, pl.ds(base, n1f), :] = a[:n1f]
        s_ref[1, pl.ds(base, n1f), :] = a[n1f:]
        return c

    lax.fori_loop(0, DFT_N2, stage1, 0, unroll=32)

    def stage2(k1, c):
        slab = jnp.concatenate([s_ref[0, pl.ds(k1, DFT_N2, stride=S_PITCH), :],
                                s_ref[1, pl.ds(k1, DFT_N2, stride=S_PITCH), :]], axis=0).astype(BF16)
        x = jnp.dot(f2_ref[...], slab, preferred_element_type=F32)
        xr, xi = x[:DFT_N2], x[DFT_N2:]
        hr, hi = hf_ref[0, k1, 0], hf_ref[0, k1, 1]
        y = jnp.concatenate([xr * hr - xi * hi, xr * hi + xi * hr], axis=0).astype(BF16)
        cc = jnp.dot(g2_ref[...], y, preferred_element_type=F32)
        s_ref[0, pl.ds(k1, DFT_N2, stride=S_PITCH), :] = cc[:DFT_N2]
        s_ref[1, pl.ds(k1, DFT_N2, stride=S_PITCH), :] = cc[DFT_N2:]
        return c

    lax.fori_loop(0, n1f, stage2, 0, unroll=16)

    def stage3(n2, c):
        base = pl.multiple_of(n2 * S_PITCH, 8)
        d = jnp.concatenate([s_ref[0, pl.ds(base, n1f), :], s_ref[1, pl.ds(base, n1f), :]], axis=0).astype(BF16)
        yv = jnp.dot(g1_ref[n2], d, preferred_element_type=F32)
        yb = pl.multiple_of(n2 * Y_PITCH, 8)
        xy_ref[0, pl.ds(yb, n1h), :] = yv[:n1h]
        xy_ref[1, pl.ds(yb, n1h), :] = yv[n1h:]
        return c

    lax.fori_loop(0, DFT_N2, stage3, 0, unroll=32)

    def copy_out(i, c):
        for ri in range(2):
            rows = pl.ds(pl.multiple_of(i * DFT_N2, DFT_N2), DFT_N2)
            y = xy_ref[ri, pl.ds(i, DFT_N2, stride=Y_PITCH), :]
            y_ref[0, ri, rows, :] = (m_ref[0, ri, rows, :] * (y + a_ref[0, ri, rows, :] * sk_ref[...])).astype(y_ref.dtype)
        return c

    lax.fori_loop(0, n1h, copy_out, 0, unroll=4)


def _lconv_small_kernel(a_ref, m_ref, sk_ref, hf_ref, f_ref, g_ref, y_ref):
    l = a_ref.shape[2]
    slab = jnp.concatenate([a_ref[0, 0], a_ref[0, 1]], axis=0).astype(BF16)
    x = jnp.dot(f_ref[...], slab, preferred_element_type=F32)
    xr, xi = x[:2 * l], x[2 * l:]
    hr, hi = hf_ref[0, 0], hf_ref[0, 1]
    y = jnp.concatenate([xr * hr - xi * hi, xr * hi + xi * hr], axis=0).astype(BF16)
    out = jnp.dot(g_ref[...], y, preferred_element_type=F32)
    for ri in range(2):
        y_ref[0, ri] = (m_ref[0, ri] * (out[ri * l:(ri + 1) * l] + a_ref[0, ri] * sk_ref[...])).astype(y_ref.dtype)


@functools.lru_cache(maxsize=None)
def _dft_consts_small(l):
    n = 2 * l
    e = np.exp(-2j * np.pi * np.outer(np.arange(n), np.arange(l)) / n)
    f = _cplx_block(e)
    g = _cplx_block(np.conj(e).T / n)
    return jnp.asarray(f, F32).astype(BF16), jnp.asarray(g, F32).astype(BF16)


def _long_conv(a, hf, order, m, m_col, sk, out_dtype):
    b, l, ca = a.shape
    c = W_GROUP
    nl = c // LANES
    a4 = a.reshape(b // 2, 2, l, ca)
    m4 = m.reshape(b // 2, 2, l, m.shape[2])
    io_spec = pl.BlockSpec((1, 2, l, LANES), lambda j, p: (p, 0, 0, j))
    once = dict(pipeline_mode=pl.Buffered(1))
    m_spec = pl.BlockSpec((1, 2, l, LANES), lambda j, p: (p, 0, 0, m_col * nl + j), **once)
    sk_spec = pl.BlockSpec((1, LANES), lambda j, p: (0, j))
    const = lambda arr: pl.BlockSpec(arr.shape, lambda j, p: (0,) * arr.ndim, **once)
    if l <= DFT_N2 * 2:
        f, g = _dft_consts_small(l)
        y = pl.pallas_call(
            _lconv_small_kernel,
            out_shape=jax.ShapeDtypeStruct((b // 2, 2, l, c), out_dtype),
            grid=(nl, b // 2),
            in_specs=[io_spec, m_spec, sk_spec, pl.BlockSpec((1, 2, 2 * l, LANES), lambda j, p: (order, 0, 0, j)),
                      const(f), const(g)],
            out_specs=io_spec,
            compiler_params=_cparams(("parallel", "arbitrary")),
            name="long_conv_small",
        )(a4, m4, sk, hf, f, g)
        return y.reshape(b, l, c)
    n1f = 2 * l // DFT_N2
    n1h = n1f // 2
    assert (n1f + 8, n1h + 8) == (S_PITCH, Y_PITCH), "scratch pitches are sized for this sequence length"
    w1, f2, g2, g1 = _dft_consts(l)
    rows_xy = max(n1h * X_PITCH, DFT_N2 * Y_PITCH)
    y = pl.pallas_call(
        functools.partial(_lconv_kernel, n1h=n1h),
        out_shape=jax.ShapeDtypeStruct((b // 2, 2, l, c), out_dtype),
        grid=(nl, b // 2),
        in_specs=[io_spec, m_spec, sk_spec,
                  pl.BlockSpec((1, n1f, 2, DFT_N2, LANES), lambda j, p: (order, 0, 0, 0, j), **once),
                  const(w1), const(f2), const(g2), const(g1)],
        out_specs=io_spec,
        scratch_shapes=[pltpu.VMEM((2, rows_xy, LANES), F32), pltpu.VMEM((2, DFT_N2 * S_PITCH, LANES), F32)],
        compiler_params=_cparams(("parallel", "arbitrary"), DFT_VMEM_LIMIT),
        name="long_conv",
    )(a4, m4, sk, hf, w1, f2, g2, g1)
    return y.reshape(b, l, c)


@functools.lru_cache(maxsize=None)
def _fspec_consts(n):
    n1f = n // DFT_N2
    k1, n1, n2 = np.arange(n1f), np.arange(n1f), np.arange(DFT_N2)
    ph = np.outer(k1, n1)[None] / n1f + (n2[:, None, None] * k1[None, :, None]) / n
    e1 = np.exp(-2j * np.pi * ph)
    w1 = np.concatenate([e1.real, e1.imag], axis=1)
    f2 = _cplx_block(np.exp(-2j * np.pi * np.outer(n2, n2) / DFT_N2))
    return jnp.asarray(w1, F32).astype(BF16), jnp.asarray(f2, F32).astype(BF16)


def _fspec_kernel(f_ref, w1_ref, f2_ref, hf_ref, x_ref, s_ref, *, n1f):
    n1h = n1f // 2

    def copy_in(i, c):
        for half in range(2):
            x_ref[pl.ds(pl.multiple_of((half * n1h + i) * X_PITCH, 8), DFT_N2), :] = \
                f_ref[0, half, pl.ds(pl.multiple_of(i * DFT_N2, DFT_N2), DFT_N2), :]
        return c

    lax.fori_loop(0, n1h, copy_in, 0, unroll=4)

    def stage1(n2, c):
        slab = x_ref[pl.ds(n2, n1f, stride=X_PITCH), :].astype(BF16)
        a = jnp.dot(w1_ref[n2], slab, preferred_element_type=F32)
        base = pl.multiple_of(n2 * S_PITCH, 8)
        s_ref[0, pl.ds(base, n1f), :] = a[:n1f]
        s_ref[1, pl.ds(base, n1f), :] = a[n1f:]
        return c

    lax.fori_loop(0, DFT_N2, stage1, 0, unroll=32)

    def stage2(k1, c):
        slab = jnp.concatenate([s_ref[0, pl.ds(k1, DFT_N2, stride=S_PITCH), :],
                                s_ref[1, pl.ds(k1, DFT_N2, stride=S_PITCH), :]], axis=0).astype(BF16)
        x = jnp.dot(f2_ref[...], slab, preferred_element_type=F32)
        hf_ref[0, k1, 0] = x[:DFT_N2]
        hf_ref[0, k1, 1] = x[DFT_N2:]
        return c

    lax.fori_loop(0, n1f, stage2, 0, unroll=16)


def _fspec_small_kernel(f_ref, w_ref, hf_ref):
    n = 2 * f_ref.shape[2]
    filt = jnp.concatenate([f_ref[0, 0], f_ref[0, 1]], axis=0).astype(BF16)
    x = jnp.dot(w_ref[...], filt, preferred_element_type=F32)
    hf_ref[0, 0] = x[:n]
    hf_ref[0, 1] = x[n:]


@functools.lru_cache(maxsize=None)
def _fspec_consts_small(n):
    e = np.exp(-2j * np.pi * np.outer(np.arange(n), np.arange(n)) / n)
    return jnp.asarray(np.concatenate([e.real, e.imag], axis=0), F32).astype(BF16)


def _filter_spectra(filt):
    orders, _, l, c = filt.shape
    n = 2 * l
    f2s = filt
    once = dict(pipeline_mode=pl.Buffered(1))
    const = lambda arr: pl.BlockSpec(arr.shape, lambda o, j: (0,) * arr.ndim, **once)
    in_spec = pl.BlockSpec((1, 2, l, LANES), lambda o, j: (o, 0, 0, j))
    if l <= DFT_N2 * 2:
        w = _fspec_consts_small(n)
        return pl.pallas_call(
            _fspec_small_kernel,
            out_shape=jax.ShapeDtypeStruct((orders, 2, n, c), F32),
            grid=(orders, c // LANES),
            in_specs=[in_spec, const(w)],
            out_specs=pl.BlockSpec((1, 2, n, LANES), lambda o, j: (o, 0, 0, j)),
            compiler_params=_cparams(("parallel", "parallel")),
            name="filter_spectrum_small",
        )(f2s, w)
    n1f = n // DFT_N2
    assert n1f + 8 == S_PITCH, "scratch pitches are sized for this sequence length"
    w1, f2 = _fspec_consts(n)
    return pl.pallas_call(
        functools.partial(_fspec_kernel, n1f=n1f),
        out_shape=jax.ShapeDtypeStruct((orders, n1f, 2, DFT_N2, c), F32),
        grid=(orders, c // LANES),
        in_specs=[in_spec, const(w1), const(f2)],
        out_specs=pl.BlockSpec((1, n1f, 2, DFT_N2, LANES), lambda o, j: (o, 0, 0, 0, j)),
        scratch_shapes=[pltpu.VMEM((n1f * X_PITCH, LANES), F32), pltpu.VMEM((2, DFT_N2 * S_PITCH, LANES), F32)],
        compiler_params=_cparams(("parallel", "parallel"), DFT_VMEM_LIMIT),
        name="filter_spectrum",
    )(f2s, w1, f2)


def _hyena(uc, filt, skip):
    hf = _filter_spectra(filt)
    z1 = _long_conv(uc, hf, 0, uc, 1, skip[0:1], F32)
    return _long_conv(z1, hf, 1, uc, 2, skip[1:2], BF16)


def _permute_wa_heads(w, start, axis):
    idx = np.arange(w.shape[axis])
    blocks = [np.arange(start + h * HEAD_DIM, start + (h + 1) * HEAD_DIM) for h in WA_HEAD_ORDER]
    idx[start:start + N_WA_HEADS * HEAD_DIM] = np.concatenate(blocks)
    return jnp.take(w, idx, axis=axis)


def _layer(x, ctx, mod, lyr, rope_tabs, update_ctx, final_g):
    b, s, d = x.shape
    lc = ctx.shape[1]
    mx = [mod[:b, None, i * d:(i + 1) * d] for i in range(6)]
    mc = [mod[b:b + 1, None, i * d:(i + 1) * d] for i in range(6)]
    g1, g2 = lyr["g1"][None, :], lyr["g2"][None, :]
    w_in = _permute_wa_heads(lyr["w_in"], OFF_WA_Q, axis=1).astype(BF16)
    w_out = _permute_wa_heads(lyr["w_out"], W_GROUP, axis=0).astype(BF16)
    rw_t = lyr["router_w"].T.astype(BF16)
    w_bd = _pool_weight(lyr["pool_w"])
    pscale = lyr["pool_scale"][None, :]
    hy_args = (lyr["hy_w1"], lyr["hy_b1"], lyr["hy_w2"], lyr["hy_b2"], lyr["hy_w3"], lyr["hy_freq"])

    conv = (lyr["hy_conv_w"], lyr["hy_conv_b"][None, :])
    u_hy, u_pool, q_wa, q_na, kv = _inproj(x, mx[0], mx[1], g1, w_in, conv, rope_tabs)
    if update_ctx:
        cu_hy, cu_pool, cq_wa, cq_na, ckv = _inproj(ctx, mc[0], mc[1], g1, w_in, conv)
    else:
        ckv = _inproj(ctx, mc[0], mc[1], g1, w_in[:, OFF_KV:], kv_only=True)

    ys = [_hyena(u_hy, _hyena_filters(s, *hy_args), lyr["hy_skip"]),
          _window_attn(q_wa, kv, ckv, lyr["wa_sink"]),
          _pool_mixer(u_pool, w_bd, pscale),
          _neighbourhood_attn(q_na, kv, ckv, lyr["na_rpb"])]
    x, h, aff = _outproj(ys, x, mx[2], w_out, g2, mx[3], mx[4], rw_t)
    moe = _expert_choice_ffn(h, aff, *lyr["experts"], lyr["index"])
    x = _combine(x, *moe, mx[5], final_g)

    if update_ctx:
        ycs = [_hyena(cu_hy, _hyena_filters(lc, *hy_args), lyr["hy_skip"]),
               _ctx_attn(cq_wa, ckv, 0, 1, N_WA_KV, lyr["wa_sink"], WA_HEAD_ORDER),
               _pool_mixer(cu_pool, w_bd, pscale),
               _ctx_attn(cq_na, ckv, 1, 2, N_NA_HEADS, None)]
        ctx, hc, affc = _outproj(ycs, ctx, mc[2], w_out, g2, mc[3], mc[4], rw_t)
        moe_c = _expert_choice_ffn(hc, affc, *lyr["experts"], lyr["index"])
        ctx = _combine(ctx, *moe_c, mc[5])
    return x, ctx


def kernel(x, c, ctx, c_ctx, ada_w, ada_b, norm1_g, norm2_g, w_in, hy_conv_w, hy_conv_b, hy_w1, hy_b1, hy_w2,
           hy_b2, hy_w3, hy_freq, hy_skip, wa_sink, pool_w, pool_scale, na_rpb, w_out, router_w, exp_w_gate,
           exp_w_up, exp_w_down, final_norm_g):
    b, s, d = x.shape
    cs = jnp.zeros((MOD_ROWS, d), F32).at[:b].set(c).at[b].set(c_ctx)
    mods = _ada_mod(cs, ada_w, ada_b[:, None, :])
    rope_tabs = _rope_tables(s)
    params = dict(g1=norm1_g, g2=norm2_g, w_in=w_in, hy_conv_w=hy_conv_w, hy_conv_b=hy_conv_b, hy_w1=hy_w1,
                  hy_b1=hy_b1, hy_w2=hy_w2, hy_b2=hy_b2, hy_w3=hy_w3, hy_freq=hy_freq, hy_skip=hy_skip,
                  wa_sink=wa_sink, pool_w=pool_w, pool_scale=pool_scale, na_rpb=na_rpb, w_out=w_out,
                  router_w=router_w)
    for l in range(DEPTH):
        lyr = {k: v[l] for k, v in params.items()}
        lyr.update(index=l, experts=(exp_w_gate, exp_w_up, exp_w_down))
        last = l == DEPTH - 1
        x, ctx = _layer(x, ctx, mods[l], lyr, rope_tabs, update_ctx=not last,
                        final_g=final_norm_g[None, :] if last else None)
    return x
```

```python
import functools
import math

import jax
import jax.numpy as jnp
import numpy as np
from jax import lax
from jax.experimental import pallas as pl
from jax.experimental.pallas import tpu as pltpu

F32 = jnp.float32
BF16 = jnp.bfloat16

D_MODEL = 1024
DEPTH = 2
GRID_W = 64
HEAD_DIM = 64
W_GROUP = 256
N_WA_HEADS = 4
N_WA_KV = 2
N_NA_HEADS = 4
KV_WA = 128
OFF_POOL = 768
OFF_WA_Q = 1024
OFF_NA_Q = 1280
OFF_KV = 1536
IN_WIDTH = 2304
KV_WIDTH = IN_WIDTH - OFF_KV
HY_ORDER = 2
HY_BANDS = 16
HY_DECAY_MIN = abs(math.log(1e-2) / 1.5)
HY_DECAY_MAX = abs(math.log(1e-2) / 0.3)
WA_BLOCK = 128
WA_STEP_BLOCKS = 4
WA_HEAD_ORDER = (0, 2, 1, 3)
POOL_WINDOWS = (2, 4, 8, 16)
POOL_GROUP = 64
HALO = 8
LANES = 128
SUBLANES = 8
DFT_N2 = 128
X_PITCH = DFT_N2 + 8
S_PITCH = 64 + 8
Y_PITCH = 32 + 8
NA_ROWS = 8
NA_COLS = 16
NA_ROW_BLOCK = 8
ROPE_BASE = 10000.0
N_EXPERTS = 16
EXPERT_HIDDEN = 2048
EC_CAPACITY = 2
NORM_EPS = 1e-6
NEG_INF = -1e30
Q_SCALE = HEAD_DIM ** -0.5

FFN_ROWS = 2048
FFN_HIDDEN_TILE = 256
GATHER_CHUNK = 2048
COMBINE_TILE = 256
COMBINE_SUB = 256
COMBINE_STRIP = 64
MOD_ROWS = 16
VMEM_LIMIT = 48 * 1024 * 1024
DFT_VMEM_LIMIT = 52 * 1024 * 1024
LANE_BITS = 7
F32_VALUE_BITS = 31


def _cparams(sem, vmem=VMEM_LIMIT):
    return pltpu.CompilerParams(dimension_semantics=sem, vmem_limit_bytes=vmem)


def _nt_dot(a, b):
    return lax.dot_general(a, b, (((1,), (1,)), ((), ())), preferred_element_type=F32)


def _rmsnorm_mod(x, g, shift, scale):
    y = x * lax.rsqrt(jnp.mean(x * x, axis=-1, keepdims=True) + NORM_EPS) * g
    return y * (1.0 + scale) + shift


def _ada_kernel(c_ref, w_ref, b_ref, o_ref):
    c = c_ref[...]
    s = (c * jax.nn.sigmoid(c)).astype(BF16)
    o_ref[0] = jnp.dot(s, w_ref[0].astype(BF16), preferred_element_type=F32) + b_ref[0]


def _ada_mod(cs, ada_w, ada_b):
    nl, d, n = ada_w.shape
    tn = 1024
    return pl.pallas_call(
        _ada_kernel,
        out_shape=jax.ShapeDtypeStruct((nl, MOD_ROWS, n), F32),
        grid=(nl, n // tn),
        in_specs=[pl.BlockSpec((MOD_ROWS, d), lambda l, j: (0, 0)),
                  pl.BlockSpec((1, d, tn), lambda l, j: (l, 0, j)),
                  pl.BlockSpec((1, 1, tn), lambda l, j: (l, 0, j))],
        out_specs=pl.BlockSpec((1, MOD_ROWS, tn), lambda l, j: (l, 0, j)),
        compiler_params=_cparams(("parallel", "parallel")),
        name="ada_mod",
    )(cs, ada_w, ada_b)


def _rope(t, cos, s_up, s_dn):
    w = t.shape[-1]
    if w > 128:
        cos, s_up, s_dn = (jnp.concatenate([a] * (w // 128), axis=-1) for a in (cos, s_up, s_dn))
    up = pltpu.roll(t, w - 16, 1)
    dn = pltpu.roll(t, 16, 1)
    return t * cos + up * s_up + dn * s_dn


def _inproj_kernel(*refs, rope, kv_only):
    if kv_only:
        x_ref, sh_ref, sc_ref, g_ref, w_ref, o_ref = refs
    elif rope:
        x_ref, xp_ref, xn_ref, sh_ref, sc_ref, g_ref, w_ref, cw_ref, cb_ref, cos_ref, sup_ref, sdn_ref, *outs = refs
    else:
        x_ref, xp_ref, xn_ref, sh_ref, sc_ref, g_ref, w_ref, cw_ref, cb_ref, *outs = refs
    h = _rmsnorm_mod(x_ref[0], g_ref[...], sh_ref[0], sc_ref[0])
    p = jnp.dot(h.astype(BF16), w_ref[...], preferred_element_type=F32)
    if kv_only:
        o_ref[0] = p.astype(BF16)
        return
    hy_ref, pool_ref, qwa_ref, qna_ref, kv_ref = outs
    j, nt, tm = pl.program_id(0), pl.num_programs(0), x_ref.shape[1]
    halo = _rmsnorm_mod(jnp.concatenate([xp_ref[0], xn_ref[0]], axis=0), g_ref[...], sh_ref[0], sc_ref[0])
    ph = jnp.dot(halo.astype(BF16), w_ref[:, :OFF_POOL], preferred_element_type=F32)
    e = jnp.concatenate([jnp.where(j == 0, 0.0, ph[:HALO]), p[:, :OFF_POOL],
                         jnp.where(j == nt - 1, 0.0, ph[HALO:])], axis=0)
    n = tm + 2 * HALO
    uc = pltpu.roll(e, 1, 0) * cw_ref[0:1, :] + e * cw_ref[1:2, :] + pltpu.roll(e, n - 1, 0) * cw_ref[2:3, :]
    hy_ref[0] = uc[HALO:HALO + tm] + cb_ref[...]
    pool_ref[0] = p[:, OFF_POOL:OFF_WA_Q]
    qwa = p[:, OFF_WA_Q:OFF_NA_Q]
    kwa = p[:, OFF_KV:OFF_KV + KV_WA]
    if rope:
        tabs = (cos_ref[...], sup_ref[...], sdn_ref[...])
        qwa = _rope(qwa, *tabs)
        kwa = _rope(kwa, *tabs)
    qwa_ref[0] = (qwa * Q_SCALE).astype(BF16)
    qna_ref[0] = (p[:, OFF_NA_Q:OFF_KV] * Q_SCALE).astype(BF16)
    kv_ref[0] = jnp.concatenate([kwa, p[:, OFF_KV + KV_WA:]], axis=-1).astype(BF16)


def _inproj(x, shift, scale, g, w_bf16, conv=None, rope_tabs=None, kv_only=False):
    b, l, d = x.shape
    n = w_bf16.shape[1]
    tm = min(l, 512)
    hb, nh = tm // HALO, l // HALO
    rope = rope_tabs is not None
    bm = shift.shape[0]
    mod_map = (lambda j, i: (i, 0, 0)) if bm > 1 else (lambda j, i: (0, 0, 0))
    in_specs = [pl.BlockSpec((1, tm, d), lambda j, i: (i, j, 0))]
    args = [x]
    if not kv_only:
        in_specs += [pl.BlockSpec((1, HALO, d), lambda j, i: (i, jnp.maximum(j * hb - 1, 0), 0)),
                     pl.BlockSpec((1, HALO, d), lambda j, i: (i, jnp.minimum((j + 1) * hb, nh - 1), 0))]
        args += [x, x]
    in_specs += [pl.BlockSpec((1, 1, d), mod_map),
                 pl.BlockSpec((1, 1, d), mod_map),
                 pl.BlockSpec((1, d), lambda j, i: (0, 0)),
                 pl.BlockSpec((d, n), lambda j, i: (0, 0))]
    args += [shift, scale, g, w_bf16]
    if not kv_only:
        in_specs += [pl.BlockSpec((3, OFF_POOL), lambda j, i: (0, 0)), pl.BlockSpec((1, OFF_POOL), lambda j, i: (0, 0))]
        args += list(conv)
    if rope:
        in_specs += [pl.BlockSpec((tm, 128), lambda j, i: (j, 0))] * 3
        args += list(rope_tabs)

    def tok(width, dtype):
        return (jax.ShapeDtypeStruct((b, l, width), dtype),
                pl.BlockSpec((1, tm, width), lambda j, i: (i, j, 0)))

    if kv_only:
        outs = [tok(n, BF16)]
    else:
        outs = [tok(OFF_POOL, F32), tok(W_GROUP, F32), tok(W_GROUP, BF16), tok(W_GROUP, BF16),
                tok(KV_WIDTH, BF16)]
    res = pl.pallas_call(
        functools.partial(_inproj_kernel, rope=rope, kv_only=kv_only),
        out_shape=[o[0] for o in outs],
        grid=(l // tm, b),
        in_specs=in_specs,
        out_specs=[o[1] for o in outs],
        compiler_params=_cparams(("parallel", "parallel")),
        name="inproj_kv" if kv_only else "inproj",
    )(*args)
    return res[0] if kv_only else res


def _rope_tables(s):
    pos = jnp.arange(s)
    p2 = jnp.stack([pos // GRID_W, pos % GRID_W], axis=-1).astype(F32)
    inv = ROPE_BASE ** (-jnp.arange(16, dtype=F32) / 16)
    lane = np.arange(HEAD_DIM)
    ang = p2[:, lane // 32] * inv[lane % 16][None, :]
    first = jnp.asarray((lane % 32) < 16)[None, :]
    cos, sin = jnp.cos(ang), jnp.sin(ang)
    s_up = jnp.where(first, -sin, 0.0)
    s_dn = jnp.where(first, 0.0, sin)
    return tuple(jnp.tile(t, (1, 2)) for t in (cos, s_up, s_dn))


def _softmax_parts(parts, extra=None):
    m = parts[0].max(axis=-1, keepdims=True)
    for s in parts[1:]:
        m = jnp.maximum(m, s.max(axis=-1, keepdims=True))
    if extra is not None:
        m = jnp.maximum(m, extra)
    ps = [jnp.exp(s - m) for s in parts]
    den = ps[0].sum(axis=-1, keepdims=True)
    for p in ps[1:]:
        den = den + p.sum(axis=-1, keepdims=True)
    if extra is not None:
        den = den + jnp.exp(extra - m)
    return ps, den


def _head_stack(t, masks):
    return jnp.concatenate([jnp.where(m, t, jnp.zeros_like(t)) for m in masks], axis=0)


def _wattn_kernel(sink_ref, q_ref, kp_ref, ko_ref, kn_ref, vp_ref, vo_ref, vn_ref, kc_ref, vc_ref, o_ref, *, nb):
    n = pl.program_id(1)
    blk = WA_BLOCK
    q = q_ref[0]
    kall = jnp.concatenate([kp_ref[0], ko_ref[0], kn_ref[0]], axis=0)
    vall = jnp.concatenate([vp_ref[0], vo_ref[0], vn_ref[0]], axis=0)
    kc, vc = kc_ref[0], vc_ref[0]
    rows = N_WA_HEADS * blk
    i = lax.broadcasted_iota(jnp.int32, (rows, 3 * blk), 0) & (blk - 1)
    j = lax.broadcasted_iota(jnp.int32, (rows, 3 * blk), 1)
    band = (j >= i) & (j <= i + 2 * blk)
    kv0 = lax.broadcasted_iota(jnp.int32, (blk, KV_WA), 1) < HEAD_DIM
    head = lax.broadcasted_iota(jnp.int32, (rows, 1), 0) // blk
    snk = jnp.where(head == 0, sink_ref[0],
                    jnp.where(head == 1, sink_ref[1], jnp.where(head == 2, sink_ref[2], sink_ref[3])))
    for sb in range(WA_STEP_BLOCKS):
        gb = n * WA_STEP_BLOCKS + sb
        jlo = jnp.where(gb == 0, blk, 0)
        jhi = jnp.where(gb == nb - 1, 2 * blk, 3 * blk)
        valid = band & (j >= jlo) & (j < jhi)
        keys = slice(sb * blk, (sb + 3) * blk)
        qa = q[sb * blk:(sb + 1) * blk, :KV_WA]
        qb = q[sb * blk:(sb + 1) * blk, KV_WA:]
        zero = jnp.zeros_like(qa)
        qs = jnp.concatenate([jnp.where(kv0, qa, zero), jnp.where(kv0, qb, zero),
                              jnp.where(kv0, zero, qa), jnp.where(kv0, zero, qb)], axis=0)
        s_loc = jnp.where(valid, _nt_dot(qs, kall[keys]), NEG_INF)
        s_ctx = _nt_dot(qs, kc)
        (p_loc, p_ctx), den = _softmax_parts([s_loc, s_ctx], snk)
        o = (jnp.dot(p_loc.astype(BF16), vall[keys], preferred_element_type=F32)
             + jnp.dot(p_ctx.astype(BF16), vc, preferred_element_type=F32)) / den
        o_ref[0, sb * blk:(sb + 1) * blk, :] = jnp.concatenate(
            [jnp.where(kv0, o[:blk], o[2 * blk:3 * blk]), jnp.where(kv0, o[blk:2 * blk], o[3 * blk:])],
            axis=-1).astype(BF16)


def _window_attn(q, kv, ckv, sink):
    b, s, _ = q.shape
    lc = ckv.shape[1]
    nb = s // WA_BLOCK
    sbk = WA_STEP_BLOCKS

    def halo_spec(col, off):
        return pl.BlockSpec((1, WA_BLOCK, KV_WA),
                            lambda i, n: (i, jnp.clip(n * sbk + off, 0, nb - 1), col))

    def own_spec(col):
        return pl.BlockSpec((1, sbk * WA_BLOCK, KV_WA), lambda i, n: (i, n, col))

    return pl.pallas_call(
        functools.partial(_wattn_kernel, nb=nb),
        out_shape=jax.ShapeDtypeStruct((b, s, W_GROUP), BF16),
        grid=(b, nb // sbk),
        in_specs=[pl.BlockSpec(memory_space=pltpu.SMEM),
                  pl.BlockSpec((1, sbk * WA_BLOCK, W_GROUP), lambda i, n: (i, n, 0)),
                  halo_spec(0, -1), own_spec(0), halo_spec(0, sbk),
                  halo_spec(1, -1), own_spec(1), halo_spec(1, sbk),
                  pl.BlockSpec((1, lc, KV_WA), lambda i, n: (i, 0, 0)),
                  pl.BlockSpec((1, lc, KV_WA), lambda i, n: (i, 0, 1))],
        out_specs=pl.BlockSpec((1, sbk * WA_BLOCK, W_GROUP), lambda i, n: (i, n, 0)),
        compiler_params=_cparams(("parallel", "parallel")),
        name="window_attn",
    )(sink, q, kv, kv, kv, kv, kv, kv, ckv, ckv)


def _nattn_kernel(q_ref, k_ref, v_ref, kc_ref, vc_ref, bias_ref, o_ref, *, rows):
    blk = pl.program_id(1)
    nk = NA_ROWS * GRID_W
    kc, vc = kc_ref[0], vc_ref[0]
    lane_head = lax.broadcasted_iota(jnp.int32, (GRID_W, W_GROUP), 1) // HEAD_DIM
    masks = [lane_head == h for h in range(N_NA_HEADS)]
    for rr in range(NA_ROW_BLOCK):
        r = blk * NA_ROW_BLOCK + rr
        r0 = jnp.clip(r - NA_ROWS // 2, 0, rows - NA_ROWS)
        var = r - r0
        start = pl.multiple_of(r0 * GRID_W, GRID_W)
        kt = k_ref[0, pl.ds(start, nk), :]
        vt = v_ref[0, pl.ds(start, nk), :]
        qrow = slice(rr * GRID_W, (rr + 1) * GRID_W)
        qs = _head_stack(q_ref[0, qrow, :], masks)
        bias = bias_ref[:, var].reshape(N_NA_HEADS * GRID_W, nk)
        (p_loc, p_ctx), den = _softmax_parts([_nt_dot(qs, kt) + bias, _nt_dot(qs, kc)])
        o = (jnp.dot(p_loc.astype(BF16), vt, preferred_element_type=F32)
             + jnp.dot(p_ctx.astype(BF16), vc, preferred_element_type=F32)) / den
        out = o[(N_NA_HEADS - 1) * GRID_W:]
        for h in range(N_NA_HEADS - 2, -1, -1):
            out = jnp.where(masks[h], o[h * GRID_W:(h + 1) * GRID_W], out)
        o_ref[0, qrow, :] = out.astype(BF16)


def _na_bias(rpb):
    var = np.arange(NA_ROWS)
    j = np.arange(NA_ROWS)
    qc = np.arange(GRID_W)
    kc = np.arange(GRID_W)
    dr = j[None, :] - var[:, None] + NA_ROWS - 1
    dc = np.clip(kc[None, :] - qc[:, None] + NA_COLS - 1, 0, 2 * NA_COLS - 2)
    ws = np.clip(qc - NA_COLS // 2, 0, GRID_W - NA_COLS)
    ok = (kc[None, :] >= ws[:, None]) & (kc[None, :] < ws[:, None] + NA_COLS)
    onehot = (dc[None] == np.arange(2 * NA_COLS - 1)[:, None, None]).astype(np.float32)
    bias = jnp.einsum("hvjd,dqk->hvqjk", rpb.astype(F32)[:, dr], jnp.asarray(onehot),
                      precision=lax.Precision.HIGHEST)
    bias = jnp.where(jnp.asarray(ok)[None, None, :, None, :], bias, NEG_INF)
    return bias.reshape(rpb.shape[0], NA_ROWS, GRID_W, NA_ROWS * GRID_W)


def _neighbourhood_attn(q, kv, ckv, rpb):
    b, s, _ = q.shape
    lc = ckv.shape[1]
    rows = s // GRID_W
    bias = _na_bias(rpb)
    return pl.pallas_call(
        functools.partial(_nattn_kernel, rows=rows),
        out_shape=jax.ShapeDtypeStruct((b, s, W_GROUP), BF16),
        grid=(b, rows // NA_ROW_BLOCK),
        in_specs=[pl.BlockSpec((1, NA_ROW_BLOCK * GRID_W, W_GROUP), lambda i, r: (i, r, 0)),
                  pl.BlockSpec((1, s, W_GROUP), lambda i, r: (i, 0, 1)),
                  pl.BlockSpec((1, s, W_GROUP), lambda i, r: (i, 0, 2)),
                  pl.BlockSpec((1, lc, W_GROUP), lambda i, r: (i, 0, 1)),
                  pl.BlockSpec((1, lc, W_GROUP), lambda i, r: (i, 0, 2)),
                  pl.BlockSpec(bias.shape, lambda i, r: (0, 0, 0, 0))],
        out_specs=pl.BlockSpec((1, NA_ROW_BLOCK * GRID_W, W_GROUP), lambda i, r: (i, r, 0)),
        compiler_params=_cparams(("parallel", "arbitrary")),
        name="neighbourhood_attn",
    )(q, kv, kv, ckv, ckv, bias)


def _cattn_kernel(*refs, n_kv, with_sink, head_order):
    if with_sink:
        sink_ref, q_ref, k_ref, v_ref, o_ref = refs
    else:
        q_ref, k_ref, v_ref, o_ref = refs
    q, k, v = q_ref[0], k_ref[0], v_ref[0]
    group = N_WA_HEADS // n_kv
    outs = []
    for pos, h in enumerate(head_order):
        sl = slice((h // group) * HEAD_DIM, (h // group + 1) * HEAD_DIM)
        s = _nt_dot(q[:, pos * HEAD_DIM:(pos + 1) * HEAD_DIM], k[:, sl])
        extra = jnp.full((s.shape[0], 1), sink_ref[h], F32) if with_sink else None
        (p,), den = _softmax_parts([s], extra)
        outs.append(jnp.dot(p.astype(BF16), v[:, sl], preferred_element_type=F32) / den)
    o_ref[0] = jnp.concatenate(outs, axis=-1).astype(BF16)


def _ctx_attn(q, ckv, k_col, v_col, n_kv, sink, head_order=(0, 1, 2, 3)):
    b, lc, _ = q.shape
    w = n_kv * HEAD_DIM
    with_sink = sink is not None
    in_specs = [pl.BlockSpec((1, lc, W_GROUP), lambda i: (i, 0, 0)),
                pl.BlockSpec((1, lc, w), lambda i: (i, 0, k_col)),
                pl.BlockSpec((1, lc, w), lambda i: (i, 0, v_col))]
    args = [q, ckv, ckv]
    if with_sink:
        in_specs = [pl.BlockSpec(memory_space=pltpu.SMEM)] + in_specs
        args = [sink] + args
    return pl.pallas_call(
        functools.partial(_cattn_kernel, n_kv=n_kv, with_sink=with_sink, head_order=head_order),
        out_shape=jax.ShapeDtypeStruct((b, lc, W_GROUP), BF16),
        grid=(b,),
        in_specs=in_specs,
        out_specs=pl.BlockSpec((1, lc, W_GROUP), lambda i: (i, 0, 0)),
        compiler_params=_cparams(("parallel",)),
        name="ctx_attn",
    )(*args)


def _pool_kernel(prev_ref, cur_ref, next_ref, w_ref, scale_ref, o_ref, *, seq, tl):
    j = pl.program_id(1)
    nt = seq // tl
    cur = cur_ref[0]
    prev = jnp.where(j == 0, 0.0, prev_ref[0])
    nxt = jnp.where(j == nt - 1, 0.0, next_ref[0])
    e = jnp.concatenate([prev, cur, nxt], axis=0)
    n = tl + 2 * HALO

    def sh(a, d):
        return pltpu.roll(a, d % n, 0)

    s2 = e + sh(e, 1)
    s4 = sh(s2, 1) + sh(s2, -1)
    s8 = sh(s4, 2) + sh(s4, -2)
    s16 = sh(s8, 4) + sh(s8, -4)
    lane = lax.broadcasted_iota(jnp.int32, (tl, W_GROUP), 1)
    t = lax.broadcasted_iota(jnp.int32, (tl, W_GROUP), 0) + j * tl
    g = lane // POOL_GROUP
    lo, hi = HALO, HALO + tl
    ssum = jnp.where(g == 0, s2[lo:hi], jnp.where(g == 1, s4[lo:hi], jnp.where(g == 2, s8[lo:hi], s16[lo:hi])))
    half = jnp.where(g == 0, 1, jnp.where(g == 1, 2, jnp.where(g == 2, 4, 8)))
    cnt = (jnp.minimum(t + half, seq) - jnp.maximum(t - half, 0)).astype(F32)
    d = (ssum / cnt - cur).astype(BF16)
    o_ref[0] = (jnp.dot(d, w_ref[...], preferred_element_type=F32) * scale_ref[...]).astype(BF16)


def _pool_mixer(u, w_bd, scale):
    b, l, c = u.shape
    tl = min(l, 512)
    hb = tl // HALO
    nh = l // HALO
    return pl.pallas_call(
        functools.partial(_pool_kernel, seq=l, tl=tl),
        out_shape=jax.ShapeDtypeStruct((b, l, c), BF16),
        grid=(b, l // tl),
        in_specs=[pl.BlockSpec((1, HALO, c), lambda i, j: (i, jnp.maximum(j * hb - 1, 0), 0)),
                  pl.BlockSpec((1, tl, c), lambda i, j: (i, j, 0)),
                  pl.BlockSpec((1, HALO, c), lambda i, j: (i, jnp.minimum((j + 1) * hb, nh - 1), 0)),
                  pl.BlockSpec((c, c), lambda i, j: (0, 0)),
                  pl.BlockSpec((1, c), lambda i, j: (0, 0))],
        out_specs=pl.BlockSpec((1, tl, c), lambda i, j: (i, j, 0)),
        compiler_params=_cparams(("parallel", "parallel")),
        name="pool_mixer",
    )(u, u, u, w_bd, scale)


def _pool_weight(pool_w):
    z = jnp.zeros((W_GROUP, W_GROUP), F32)
    for g in range(len(POOL_WINDOWS)):
        z = z.at[g * POOL_GROUP:(g + 1) * POOL_GROUP, g * POOL_GROUP:(g + 1) * POOL_GROUP].set(pool_w[g])
    return z.astype(BF16)


def _outproj_kernel(yh_ref, ya_ref, yp_ref, yn_ref, x_ref, gt_ref, w_ref, g2_ref, sh_ref, sc_ref, rw_ref,
                    xo_ref, h_ref, aff_ref):
    y = jnp.concatenate([yh_ref[0], ya_ref[0], yp_ref[0], yn_ref[0]], axis=-1)
    x = x_ref[0] + gt_ref[0] * jnp.dot(y, w_ref[...], preferred_element_type=F32)
    xo_ref[0] = x
    h = _rmsnorm_mod(x, g2_ref[...], sh_ref[0], sc_ref[0])
    nsl = h.shape[1] // LANES
    for s in range(nsl):
        h_ref[0, pl.ds(s, h.shape[0], stride=nsl), :] = h[:, s * LANES:(s + 1) * LANES]
    logits = _nt_dot(rw_ref[...], h.astype(BF16))
    m = logits.max(axis=0, keepdims=True)
    p = jnp.exp(logits - m)
    aff_ref[0] = p / p.sum(axis=0, keepdims=True)


def _outproj(ys, x, gate, w_bf16, g2, shift, scale, rw_t):
    b, l, d = x.shape
    tm = min(l, 512)
    bm = gate.shape[0]
    mod_map = (lambda i, j: (i, 0, 0)) if bm > 1 else (lambda i, j: (0, 0, 0))
    tok = lambda w: pl.BlockSpec((1, tm, w), lambda i, j: (i, j, 0))
    mod = pl.BlockSpec((1, 1, d), mod_map)
    return pl.pallas_call(
        _outproj_kernel,
        out_shape=[jax.ShapeDtypeStruct((b, l, d), F32), jax.ShapeDtypeStruct((b, l * d // LANES, LANES), F32),
                   jax.ShapeDtypeStruct((b, N_EXPERTS, l), F32)],
        grid=(b, l // tm),
        in_specs=[tok(W_GROUP)] * 4 + [tok(d), mod, pl.BlockSpec((d, d), lambda i, j: (0, 0)),
                                        pl.BlockSpec((1, d), lambda i, j: (0, 0)), mod, mod,
                                        pl.BlockSpec((N_EXPERTS, d), lambda i, j: (0, 0))],
        out_specs=[tok(d), pl.BlockSpec((1, tm * d // LANES, LANES), lambda i, j: (i, j, 0)),
                   pl.BlockSpec((1, N_EXPERTS, tm), lambda i, j: (i, 0, j))],
        compiler_params=_cparams(("parallel", "parallel")),
        name="outproj_router",
    )(*ys, x, gate, w_bf16, g2, shift, scale, rw_t)


def _row_gather_kernel(idx_ref, nxt_ref, table_ref, out_ref, stage, sems):
    i, n = pl.program_id(0), pl.num_programs(0)
    cur = i % 2
    chunk, d = out_ref.shape
    nsl = d // LANES

    def issue(ref, s):
        def group(q, c):
            for k in range(SUBLANES):
                r = q * SUBLANES + k
                pltpu.make_async_copy(table_ref.at[pl.ds(pl.multiple_of(ref[0, 0, r] * nsl, nsl), nsl)],
                                      stage.at[s, pl.ds(pl.multiple_of(r * nsl, nsl), nsl)],
                                      sems.at[s]).start(priority=k % 2)
            return c
        lax.fori_loop(0, chunk // SUBLANES, group, 0)

    @pl.when(i == 0)
    def _():
        issue(idx_ref, cur)

    @pl.when(i + 1 < n)
    def _():
        issue(nxt_ref, 1 - cur)

    pltpu.make_async_copy(table_ref.at[pl.ds(0, chunk * nsl)], stage.at[cur], sems.at[cur]).wait()
    for s in range(nsl):
        out_ref[:, s * LANES:(s + 1) * LANES] = stage[cur, pl.ds(s, chunk, stride=nsl), :].astype(BF16)


def _row_gather(table, rows, d):
    n = rows.shape[0]
    chunk = min(GATHER_CHUNK, n)
    nch = n // chunk
    idx_spec = lambda off: pl.BlockSpec((1, 1, chunk), lambda i: (jnp.minimum(i + off, nch - 1), 0, 0),
                                        memory_space=pltpu.SMEM)
    rows3 = rows.reshape(nch, 1, chunk)
    return pl.pallas_call(
        _row_gather_kernel,
        out_shape=jax.ShapeDtypeStruct((n, d), BF16),
        grid=(nch,),
        in_specs=[idx_spec(0), idx_spec(1), pl.BlockSpec(memory_space=pl.ANY)],
        out_specs=pl.BlockSpec((chunk, d), lambda i: (i, 0)),
        scratch_shapes=[pltpu.VMEM((2, chunk * d // LANES, LANES), table.dtype), pltpu.SemaphoreType.DMA((2,))],
        compiler_params=pltpu.CompilerParams(dimension_semantics=("arbitrary",), vmem_limit_bytes=VMEM_LIMIT,
                                             disable_bounds_checks=True),
        name="moe_row_gather",
    )(rows3, rows3, table)


def _moe_kernel(x_ref, g_ref, wg_ref, wu_ref, wd_ref, o_ref):
    f = pl.program_id(2)

    @pl.when(f == 0)
    def _():
        o_ref[0] = jnp.zeros(o_ref.shape[1:], F32)

    x = x_ref[0]
    a = jnp.dot(x, wg_ref[0, 0].astype(BF16), preferred_element_type=F32)
    u = jnp.dot(x, wu_ref[0, 0].astype(BF16), preferred_element_type=F32)
    hid = (a * jax.nn.sigmoid(a) * u).astype(BF16)
    o_ref[0] += jnp.dot(hid, wd_ref[0, 0].astype(BF16), preferred_element_type=F32)

    @pl.when(f == pl.num_programs(2) - 1)
    def _():
        o_ref[0] = o_ref[0] * g_ref[0]


def _moe_ffn(xg, gate, w_gate, w_up, w_down, lyr):
    e, m, d = xg.shape
    f = w_gate.shape[3]
    tm = min(m, FFN_ROWS)
    tf = FFN_HIDDEN_TILE
    return pl.pallas_call(
        _moe_kernel,
        out_shape=jax.ShapeDtypeStruct((e, m, d), F32),
        grid=(e, m // tm, f // tf),
        in_specs=[pl.BlockSpec((1, tm, d), lambda i, j, k: (i, j, 0)),
                  pl.BlockSpec((1, tm, 1), lambda i, j, k: (i, j, 0)),
                  pl.BlockSpec((1, 1, d, tf), lambda i, j, k: (lyr, i, 0, k)),
                  pl.BlockSpec((1, 1, d, tf), lambda i, j, k: (lyr, i, 0, k)),
                  pl.BlockSpec((1, 1, tf, d), lambda i, j, k: (lyr, i, k, 0))],
        out_specs=pl.BlockSpec((1, tm, d), lambda i, j, k: (i, j, 0)),
        compiler_params=_cparams(("parallel", "parallel", "arbitrary")),
        name="moe_ffn",
    )(xg, gate, w_gate, w_up, w_down)


def _route_kernel(aff_ref, tri_ref, idx_ref, g_ref, *, k):
    x = aff_ref[...]
    r, l = x.shape
    nb = l // LANES
    nq = idx_ref.shape[1] // LANES
    def count(mask):
        return jnp.sum(jnp.where(mask, 1.0, 0.0), axis=-1, keepdims=True)

    def as_float(pattern):
        return pltpu.bitcast(jnp.broadcast_to(pattern, (r, LANES)), F32)[:, :1]

    def bisect(i, t):
        cand = t | jnp.left_shift(jnp.int32(1), F32_VALUE_BITS - 1 - i)
        return jnp.where(count(x >= as_float(cand)) >= k, cand, t)

    kth = lax.fori_loop(0, F32_VALUE_BITS, bisect, jnp.zeros((r, 1), jnp.int32))
    gt = x >= as_float(kth + 1)
    eq = (x >= as_float(kth)) & jnp.logical_not(gt)
    need = k - count(gt)
    tri = tri_ref[...]
    lane = lax.broadcasted_iota(jnp.int32, (r, LANES), 1)

    def block_prefix(mask, j):
        m = jnp.where(mask[:, j * LANES:(j + 1) * LANES], 1.0, 0.0)
        return jnp.dot(m.astype(BF16), tri, preferred_element_type=F32), m

    out_tok = [jnp.zeros((r, LANES), jnp.int32) for _ in range(nq)]
    out_g = [jnp.zeros((r, LANES), F32) for _ in range(nq)]
    eq_before = jnp.zeros((r, 1), F32)
    start = jnp.zeros((r, 1), jnp.int32)
    for j in range(nb):
        blk = slice(j * LANES, (j + 1) * LANES)
        eq_inc, eq_f = block_prefix(eq, j)
        sel = gt[:, blk] | (eq[:, blk] & (eq_inc - eq_f + eq_before < need))
        eq_before = eq_before + eq_inc[:, LANES - 1:]
        sel_f = jnp.where(sel, 1.0, 0.0)
        sel_inc = jnp.dot(sel_f.astype(BF16), tri, preferred_element_type=F32)
        cnt = sel_inc[:, LANES - 1:].astype(jnp.int32)
        d = jnp.where(sel, lane - (sel_inc - sel_f).astype(jnp.int32), -1)
        tok = lane + j * LANES
        gv = x[:, blk]
        for kb in range(LANE_BITS):
            sh = LANES - (1 << kb)
            rd, rt, rg = pltpu.roll(d, sh, 1), pltpu.roll(tok, sh, 1), pltpu.roll(gv, sh, 1)
            arrive = (rd >= 0) & (((rd >> kb) & 1) == 1)
            stay = (d >= 0) & (((d >> kb) & 1) == 0)
            tok = jnp.where(arrive, rt, tok)
            gv = jnp.where(arrive, rg, gv)
            d = jnp.where(arrive, rd, jnp.where(stay, d, -1))
        s, q = start & (LANES - 1), start >> LANE_BITS
        for kb in range(LANE_BITS):
            mv = ((s >> kb) & 1) == 1
            tok = jnp.where(mv, pltpu.roll(tok, 1 << kb, 1), tok)
            gv = jnp.where(mv, pltpu.roll(gv, 1 << kb, 1), gv)
        end = s + cnt
        here = (lane >= s) & (lane < end)
        wrapped = lane < end - LANES
        for qq in range(nq):
            hit = (here & (q == qq)) | (wrapped & (q + 1 == qq))
            out_tok[qq] = jnp.where(hit, tok, out_tok[qq])
            out_g[qq] = jnp.where(hit, gv, out_g[qq])
        start = start + cnt
    idx_ref[...] = jnp.concatenate(out_tok, axis=1)
    g_ref[...] = jnp.concatenate(out_g, axis=1)


def _route(aff_t, k):
    b, e, l = aff_t.shape
    r = b * e
    width = -(-k // LANES) * LANES
    tri = jnp.asarray(np.triu(np.ones((LANES, LANES), np.float32))).astype(BF16)
    idx, g = pl.pallas_call(
        functools.partial(_route_kernel, k=k),
        out_shape=[jax.ShapeDtypeStruct((r, width), jnp.int32), jax.ShapeDtypeStruct((r, width), F32)],
        compiler_params=pltpu.CompilerParams(vmem_limit_bytes=VMEM_LIMIT),
        name="expert_choice_route",
    )(aff_t.reshape(r, l), tri)
    return idx[:, :k].reshape(b, e, k), g[:, :k].reshape(b, e, k)


def _expert_choice_ffn(h, aff_t, w_gate, w_up, w_down, lyr):
    b, _, l = aff_t.shape
    d = h.shape[1] * LANES // l
    cap = EC_CAPACITY * l // N_EXPERTS
    m = b * cap
    idx, g = _route(aff_t, cap)
    idx_t = jnp.swapaxes(idx, 0, 1)
    rows = (idx_t + (jnp.arange(b, dtype=jnp.int32) * l)[None, :, None]).reshape(-1)
    xg = _row_gather(h.reshape(-1, LANES), rows, d).reshape(N_EXPERTS, m, d)
    gate = jnp.swapaxes(g, 0, 1).reshape(N_EXPERTS, m, 1)
    o = _moe_ffn(xg, gate, w_gate, w_up, w_down, lyr).reshape(N_EXPERTS * m, d)
    return o, idx


def _combine_kernel(*refs, final, tile, nt, cap, strip, m):
    if final:
        lo_ref, idx_ref, x_ref, gt_ref, g_ref, o_hbm, out_ref, stage, sems = refs
    else:
        lo_ref, idx_ref, x_ref, gt_ref, o_hbm, out_ref, stage, sems = refs
    bi, j = pl.program_id(0), pl.program_id(1)
    step = bi * nt + j
    cur = step % 2
    ne = N_EXPERTS
    gps = strip // SUBLANES
    per_sub = COMBINE_SUB // strip
    win = 2 * LANES
    sentinel = tile * nt

    def runs(jj):
        out = []
        for e in range(ne):
            lo = lo_ref[bi, e * (nt + 1) + jj]
            hi = lo_ref[bi, e * (nt + 1) + jj + 1]
            a8 = (lo // SUBLANES) * SUBLANES
            out.append((a8, jnp.where(hi > lo, hi - a8, 0)))
        return out

    def groups(run, r):
        return [(a8 + strip * r, jnp.clip((n - strip * r + SUBLANES - 1) // SUBLANES, 0, gps)) for a8, n in run]

    def issue(s, grp):
        for e, (s0, ng) in enumerate(grp):
            base = e * m + bi * cap + s0

            def one(k, c, base=base, e=e):
                pltpu.make_async_copy(
                    o_hbm.at[pl.ds(pl.multiple_of(base + k * SUBLANES, SUBLANES), SUBLANES)],
                    stage.at[s, pl.ds(pl.multiple_of(e * strip + k * SUBLANES, SUBLANES), SUBLANES)],
                    sems.at[s]).start()
                return c
            lax.fori_loop(0, ng, one, 0)

    def wait(s, grp):
        cnt = sum(ng for _, ng in grp) * SUBLANES

        @pl.when(cnt > 0)
        def _():
            rows = pl.ds(0, pl.multiple_of(cnt, SUBLANES))
            pltpu.make_async_copy(o_hbm.at[rows], stage.at[s, rows], sems.at[s]).wait()

    def reduce(s, grp):
        t_ids = lax.broadcasted_iota(jnp.int32, (tile, COMBINE_SUB), 0) + j * tile
        acc = jnp.zeros((tile, out_ref.shape[2]), F32)
        for sub in range(ne // per_sub):
            pieces = []
            for e in range(sub * per_sub, (sub + 1) * per_sub):
                s0, ng = grp[e]
                a = jnp.minimum((s0 // LANES) * LANES, idx_ref.shape[2] - win)
                w = idx_ref[0, e:e + 1, pl.ds(pl.multiple_of(a, LANES), win)]
                w = pltpu.roll(w, (win - (s0 - a)) % win, 1)
                pieces.append(jnp.where(ng > 0, w[:, :strip], sentinel))
            toks = jnp.concatenate(pieces, axis=1)
            rows = stage[s, sub * COMBINE_SUB:(sub + 1) * COMBINE_SUB, :]
            onehot = jnp.where(toks == t_ids, 1.0, 0.0).astype(BF16)
            hi = rows.astype(BF16)
            lo = (rows - hi.astype(F32)).astype(BF16)
            acc = (acc + jnp.dot(onehot, hi, preferred_element_type=F32)
                   + jnp.dot(onehot, lo, preferred_element_type=F32))
        return acc

    @pl.when(step == 0)
    def _():
        stage[...] = jnp.zeros(stage.shape, F32)

    run = runs(j)
    first = groups(run, 0)

    @pl.when(j == 0)
    def _():
        issue(cur, first)

    @pl.when(j + 1 < nt)
    def _():
        issue(1 - cur, groups(runs(jnp.minimum(j + 1, nt - 1)), 0))

    wait(cur, first)
    acc = reduce(cur, first)
    longest = run[0][1]
    for _, n in run[1:]:
        longest = jnp.maximum(longest, n)

    def extra_round(r, acc):
        grp = groups(run, r)
        issue(cur, grp)
        wait(cur, grp)
        return acc + reduce(cur, grp)

    acc = lax.fori_loop(1, (longest + strip - 1) // strip, extra_round, acc)
    x = x_ref[0] + gt_ref[0] * acc
    if final:
        x = x * lax.rsqrt(jnp.mean(x * x, axis=-1, keepdims=True) + NORM_EPS) * g_ref[...]
    out_ref[0] = x


def _combine(x, o, idx, gate, final_g=None):
    b, l, d = x.shape
    ne, cap = idx.shape[1], idx.shape[2]
    tile = min(l, COMBINE_TILE)
    nt = l // tile
    strip = min(COMBINE_STRIP, cap)
    bounds = jnp.arange(nt + 1, dtype=jnp.int32) * tile
    lo = jnp.sum(idx[:, :, None, :] < bounds[None, None, :, None], axis=-1, dtype=jnp.int32)
    cp = -(-cap // LANES) * LANES + LANES
    idx_p = jnp.pad(idx, ((0, 0), (0, 0), (0, cp - cap)), constant_values=l)
    bm = gate.shape[0]
    mod_map = (lambda i, j: (i, 0, 0)) if bm > 1 else (lambda i, j: (0, 0, 0))
    tok = pl.BlockSpec((1, tile, d), lambda i, j: (i, j, 0))
    final = final_g is not None
    in_specs = [pl.BlockSpec(memory_space=pltpu.SMEM),
                pl.BlockSpec((1, ne, cp), lambda i, j: (i, 0, 0)),
                tok, pl.BlockSpec((1, 1, d), mod_map)]
    args = [lo.reshape(b, ne * (nt + 1)), idx_p, x, gate]
    if final:
        in_specs.append(pl.BlockSpec((1, d), lambda i, j: (0, 0)))
        args.append(final_g)
    in_specs.append(pl.BlockSpec(memory_space=pl.ANY))
    args.append(o)
    return pl.pallas_call(
        functools.partial(_combine_kernel, final=final, tile=tile, nt=nt, cap=cap, strip=strip, m=b * cap),
        out_shape=jax.ShapeDtypeStruct((b, l, d), F32),
        grid=(b, nt),
        in_specs=in_specs,
        out_specs=tok,
        scratch_shapes=[pltpu.VMEM((2, ne * strip, d), F32), pltpu.SemaphoreType.DMA((2,))],
        compiler_params=pltpu.CompilerParams(dimension_semantics=("arbitrary", "arbitrary"),
                                             vmem_limit_bytes=VMEM_LIMIT, disable_bounds_checks=True),
        name="moe_combine_final" if final else "moe_combine",
    )(*args)


def _hyena_filters(l, w1, b1, w2, b2, w3, freq):
    hp = lax.Precision.HIGHEST
    bands = jnp.linspace(1e-4, HY_BANDS - 1, HY_BANDS, dtype=F32)
    deltas = jnp.linspace(HY_DECAY_MIN, HY_DECAY_MAX, W_GROUP, dtype=F32)

    def at(pos):
        t = pos / max(l - 1, 1)
        ang = (2 * math.pi / l) * pos[:, None] * bands[None, :]
        feats = jnp.concatenate([t[:, None], jnp.cos(ang), -jnp.sin(ang)], axis=-1)
        h = jnp.sin(freq * (jnp.dot(feats, w1, precision=hp) + b1))
        h = jnp.sin(freq * (jnp.dot(h, w2, precision=hp) + b2))
        h = jnp.dot(h, w3, precision=hp).reshape(l, HY_ORDER, 2, W_GROUP)
        return h * jnp.exp(-t[:, None] * deltas[None, :])[:, None, None, :]

    lag = jnp.arange(l)
    fwd = at(lag.astype(F32))
    rev = at(((l - lag) % l).astype(F32))
    first = (lag == 0)[:, None, None]
    half0 = fwd[:, :, 0] + jnp.where(first, fwd[:, :, 1], 0.0)
    half1 = jnp.where(first, 0.0, rev[:, :, 1])
    return jnp.transpose(jnp.stack([half0, half1], axis=0), (2, 0, 1, 3))


def _cplx_block(e):
    return np.block([[e.real, -e.imag], [e.imag, e.real]])


@functools.lru_cache(maxsize=None)
def _dft_consts(l):
    n = 2 * l
    n1f = n // DFT_N2
    n1h = n1f // 2
    k1, n1, n2 = np.arange(n1f), np.arange(n1h), np.arange(DFT_N2)
    ph = np.outer(k1, n1)[None] / n1f + (n2[:, None, None] * k1[None, :, None]) / n
    e1 = np.exp(-2j * np.pi * ph)
    w1 = np.stack([_cplx_block(e1[i]) for i in range(DFT_N2)])
    e2 = np.exp(-2j * np.pi * np.outer(n2, n2) / DFT_N2)
    f2 = _cplx_block(e2)
    g2 = _cplx_block(np.conj(e2).T)
    g1 = np.stack([_cplx_block(np.conj(e1[i]).T / n) for i in range(DFT_N2)])
    return tuple(jnp.asarray(a, F32).astype(BF16) for a in (w1, f2, g2, g1))


def _lconv_kernel(a_ref, m_ref, sk_ref, hf_ref, w1_ref, f2_ref, g2_ref, g1_ref, y_ref, xy_ref, s_ref, *, n1h):
    n1f = 2 * n1h

    def copy_in(i, c):
        for ri in range(2):
            xy_ref[ri, pl.ds(pl.multiple_of(i * X_PITCH, 8), DFT_N2), :] = \
                a_ref[0, ri, pl.ds(pl.multiple_of(i * DFT_N2, DFT_N2), DFT_N2), :]
        return c

    lax.fori_loop(0, n1h, copy_in, 0, unroll=4)

    def stage1(n2, c):
        slab = jnp.concatenate([xy_ref[0, pl.ds(n2, n1h, stride=X_PITCH), :],
                                xy_ref[1, pl.ds(n2, n1h, stride=X_PITCH), :]], axis=0).astype(BF16)
        a = jnp.dot(w1_ref[n2], slab, preferred_element_type=F32)
        base = pl.multiple_of(n2 * S_PITCH, 8)
        s_ref[0, pl.ds(base, n1f), :] = a[:n1f]
        s_ref[1, pl.ds(base, n1f), :] = a[n1f:]
        return c

    lax.fori_loop(0, DFT_N2, stage1, 0, unroll=32)

    def stage2(k1, c):
        slab = jnp.concatenate([s_ref[0, pl.ds(k1, DFT_N2, stride=S_PITCH), :],
                                s_ref[1, pl.ds(k1, DFT_N2, stride=S_PITCH), :]], axis=0).astype(BF16)
        x = jnp.dot(f2_ref[...], slab, preferred_element_type=F32)
        xr, xi = x[:DFT_N2], x[DFT_N2:]
        hr, hi = hf_ref[0, k1, 0], hf_ref[0, k1, 1]
        y = jnp.concatenate([xr * hr - xi * hi, xr * hi + xi * hr], axis=0).astype(BF16)
        cc = jnp.dot(g2_ref[...], y, preferred_element_type=F32)
        s_ref[0, pl.ds(k1, DFT_N2, stride=S_PITCH), :] = cc[:DFT_N2]
        s_ref[1, pl.ds(k1, DFT_N2, stride=S_PITCH), :] = cc[DFT_N2:]
        return c

    lax.fori_loop(0, n1f, stage2, 0, unroll=16)

    def stage3(n2, c):
        base = pl.multiple_of(n2 * S_PITCH, 8)
        d = jnp.concatenate([s_ref[0, pl.ds(base, n1f), :], s_ref[1, pl.ds(base, n1f), :]], axis=0).astype(BF16)
        yv = jnp.dot(g1_ref[n2], d, preferred_element_type=F32)
        yb = pl.multiple_of(n2 * Y_PITCH, 8)
        xy_ref[0, pl.ds(yb, n1h), :] = yv[:n1h]
        xy_ref[1, pl.ds(yb, n1h), :] = yv[n1h:]
        return c

    lax.fori_loop(0, DFT_N2, stage3, 0, unroll=32)

    def copy_out(i, c):
        for ri in range(2):
            rows = pl.ds(pl.multiple_of(i * DFT_N2, DFT_N2), DFT_N2)
            y = xy_ref[ri, pl.ds(i, DFT_N2, stride=Y_PITCH), :]
            y_ref[0, ri, rows, :] = (m_ref[0, ri, rows, :] * (y + a_ref[0, ri, rows, :] * sk_ref[...])).astype(y_ref.dtype)
        return c

    lax.fori_loop(0, n1h, copy_out, 0, unroll=4)


def _lconv_small_kernel(a_ref, m_ref, sk_ref, hf_ref, f_ref, g_ref, y_ref):
    l = a_ref.shape[2]
    slab = jnp.concatenate([a_ref[0, 0], a_ref[0, 1]], axis=0).astype(BF16)
    x = jnp.dot(f_ref[...], slab, preferred_element_type=F32)
    xr, xi = x[:2 * l], x[2 * l:]
    hr, hi = hf_ref[0, 0], hf_ref[0, 1]
    y = jnp.concatenate([xr * hr - xi * hi, xr * hi + xi * hr], axis=0).astype(BF16)
    out = jnp.dot(g_ref[...], y, preferred_element_type=F32)
    for ri in range(2):
        y_ref[0, ri] = (m_ref[0, ri] * (out[ri * l:(ri + 1) * l] + a_ref[0, ri] * sk_ref[...])).astype(y_ref.dtype)


@functools.lru_cache(maxsize=None)
def _dft_consts_small(l):
    n = 2 * l
    e = np.exp(-2j * np.pi * np.outer(np.arange(n), np.arange(l)) / n)
    f = _cplx_block(e)
    g = _cplx_block(np.conj(e).T / n)
    return jnp.asarray(f, F32).astype(BF16), jnp.asarray(g, F32).astype(BF16)


def _long_conv(a, hf, order, m, m_col, sk, out_dtype):
    b, l, ca = a.shape
    c = W_GROUP
    nl = c // LANES
    a4 = a.reshape(b // 2, 2, l, ca)
    m4 = m.reshape(b // 2, 2, l, m.shape[2])
    io_spec = pl.BlockSpec((1, 2, l, LANES), lambda j, p: (p, 0, 0, j))
    once = dict(pipeline_mode=pl.Buffered(1))
    m_spec = pl.BlockSpec((1, 2, l, LANES), lambda j, p: (p, 0, 0, m_col * nl + j), **once)
    sk_spec = pl.BlockSpec((1, LANES), lambda j, p: (0, j))
    const = lambda arr: pl.BlockSpec(arr.shape, lambda j, p: (0,) * arr.ndim, **once)
    if l <= DFT_N2 * 2:
        f, g = _dft_consts_small(l)
        y = pl.pallas_call(
            _lconv_small_kernel,
            out_shape=jax.ShapeDtypeStruct((b // 2, 2, l, c), out_dtype),
            grid=(nl, b // 2),
            in_specs=[io_spec, m_spec, sk_spec, pl.BlockSpec((1, 2, 2 * l, LANES), lambda j, p: (order, 0, 0, j)),
                      const(f), const(g)],
            out_specs=io_spec,
            compiler_params=_cparams(("parallel", "arbitrary")),
            name="long_conv_small",
        )(a4, m4, sk, hf, f, g)
        return y.reshape(b, l, c)
    n1f = 2 * l // DFT_N2
    n1h = n1f // 2
    assert (n1f + 8, n1h + 8) == (S_PITCH, Y_PITCH), "scratch pitches are sized for this sequence length"
    w1, f2, g2, g1 = _dft_consts(l)
    rows_xy = max(n1h * X_PITCH, DFT_N2 * Y_PITCH)
    y = pl.pallas_call(
        functools.partial(_lconv_kernel, n1h=n1h),
        out_shape=jax.ShapeDtypeStruct((b // 2, 2, l, c), out_dtype),
        grid=(nl, b // 2),
        in_specs=[io_spec, m_spec, sk_spec,
                  pl.BlockSpec((1, n1f, 2, DFT_N2, LANES), lambda j, p: (order, 0, 0, 0, j), **once),
                  const(w1), const(f2), const(g2), const(g1)],
        out_specs=io_spec,
        scratch_shapes=[pltpu.VMEM((2, rows_xy, LANES), F32), pltpu.VMEM((2, DFT_N2 * S_PITCH, LANES), F32)],
        compiler_params=_cparams(("parallel", "arbitrary"), DFT_VMEM_LIMIT),
        name="long_conv",
    )(a4, m4, sk, hf, w1, f2, g2, g1)
    return y.reshape(b, l, c)


@functools.lru_cache(maxsize=None)
def _fspec_consts(n):
    n1f = n // DFT_N2
    k1, n1, n2 = np.arange(n1f), np.arange(n1f), np.arange(DFT_N2)
    ph = np.outer(k1, n1)[None] / n1f + (n2[:, None, None] * k1[None, :, None]) / n
    e1 = np.exp(-2j * np.pi * ph)
    w1 = np.concatenate([e1.real, e1.imag], axis=1)
    f2 = _cplx_block(np.exp(-2j * np.pi * np.outer(n2, n2) / DFT_N2))
    return jnp.asarray(w1, F32).astype(BF16), jnp.asarray(f2, F32).astype(BF16)


def _fspec_kernel(f_ref, w1_ref, f2_ref, hf_ref, x_ref, s_ref, *, n1f):
    n1h = n1f // 2

    def copy_in(i, c):
        for half in range(2):
            x_ref[pl.ds(pl.multiple_of((half * n1h + i) * X_PITCH, 8), DFT_N2), :] = \
                f_ref[0, half, pl.ds(pl.multiple_of(i * DFT_N2, DFT_N2), DFT_N2), :]
        return c

    lax.fori_loop(0, n1h, copy_in, 0, unroll=4)

    def stage1(n2, c):
        slab = x_ref[pl.ds(n2, n1f, stride=X_PITCH), :].astype(BF16)
        a = jnp.dot(w1_ref[n2], slab, preferred_element_type=F32)
        base = pl.multiple_of(n2 * S_PITCH, 8)
        s_ref[0, pl.ds(base, n1f), :] = a[:n1f]
        s_ref[1, pl.ds(base, n1f), :] = a[n1f:]
        return c

    lax.fori_loop(0, DFT_N2, stage1, 0, unroll=32)

    def stage2(k1, c):
        slab = jnp.concatenate([s_ref[0, pl.ds(k1, DFT_N2, stride=S_PITCH), :],
                                s_ref[1, pl.ds(k1, DFT_N2, stride=S_PITCH), :]], axis=0).astype(BF16)
        x = jnp.dot(f2_ref[...], slab, preferred_element_type=F32)
        hf_ref[0, k1, 0] = x[:DFT_N2]
        hf_ref[0, k1, 1] = x[DFT_N2:]
        return c

    lax.fori_loop(0, n1f, stage2, 0, unroll=16)


def _fspec_small_kernel(f_ref, w_ref, hf_ref):
    n = 2 * f_ref.shape[2]
    filt = jnp.concatenate([f_ref[0, 0], f_ref[0, 1]], axis=0).astype(BF16)
    x = jnp.dot(w_ref[...], filt, preferred_element_type=F32)
    hf_ref[0, 0] = x[:n]
    hf_ref[0, 1] = x[n:]


@functools.lru_cache(maxsize=None)
def _fspec_consts_small(n):
    e = np.exp(-2j * np.pi * np.outer(np.arange(n), np.arange(n)) / n)
    return jnp.asarray(np.concatenate([e.real, e.imag], axis=0), F32).astype(BF16)


def _filter_spectra(filt):
    orders, _, l, c = filt.shape
    n = 2 * l
    f2s = filt
    once = dict(pipeline_mode=pl.Buffered(1))
    const = lambda arr: pl.BlockSpec(arr.shape, lambda o, j: (0,) * arr.ndim, **once)
    in_spec = pl.BlockSpec((1, 2, l, LANES), lambda o, j: (o, 0, 0, j))
    if l <= DFT_N2 * 2:
        w = _fspec_consts_small(n)
        return pl.pallas_call(
            _fspec_small_kernel,
            out_shape=jax.ShapeDtypeStruct((orders, 2, n, c), F32),
            grid=(orders, c // LANES),
            in_specs=[in_spec, const(w)],
            out_specs=pl.BlockSpec((1, 2, n, LANES), lambda o, j: (o, 0, 0, j)),
            compiler_params=_cparams(("parallel", "parallel")),
            name="filter_spectrum_small",
        )(f2s, w)
    n1f = n // DFT_N2
    assert n1f + 8 == S_PITCH, "scratch pitches are sized for this sequence length"
    w1, f2 = _fspec_consts(n)
    return pl.pallas_call(
        functools.partial(_fspec_kernel, n1f=n1f),
        out_shape=jax.ShapeDtypeStruct((orders, n1f, 2, DFT_N2, c), F32),
        grid=(orders, c // LANES),
        in_specs=[in_spec, const(w1), const(f2)],
        out_specs=pl.BlockSpec((1, n1f, 2, DFT_N2, LANES), lambda o, j: (o, 0, 0, 0, j)),
        scratch_shapes=[pltpu.VMEM((n1f * X_PITCH, LANES), F32), pltpu.VMEM((2, DFT_N2 * S_PITCH, LANES), F32)],
        compiler_params=_cparams(("parallel", "parallel"), DFT_VMEM_LIMIT),
        name="filter_spectrum",
    )(f2s, w1, f2)


def _hyena(uc, filt, skip):
    hf = _filter_spectra(filt)
    z1 = _long_conv(uc, hf, 0, uc, 1, skip[0:1], F32)
    return _long_conv(z1, hf, 1, uc, 2, skip[1:2], BF16)


def _permute_wa_heads(w, start, axis):
    idx = np.arange(w.shape[axis])
    blocks = [np.arange(start + h * HEAD_DIM, start + (h + 1) * HEAD_DIM) for h in WA_HEAD_ORDER]
    idx[start:start + N_WA_HEADS * HEAD_DIM] = np.concatenate(blocks)
    return jnp.take(w, idx, axis=axis)


def _layer(x, ctx, mod, lyr, rope_tabs, update_ctx, final_g):
    b, s, d = x.shape
    lc = ctx.shape[1]
    mx = [mod[:b, None, i * d:(i + 1) * d] for i in range(6)]
    mc = [mod[b:b + 1, None, i * d:(i + 1) * d] for i in range(6)]
    g1, g2 = lyr["g1"][None, :], lyr["g2"][None, :]
    w_in = _permute_wa_heads(lyr["w_in"], OFF_WA_Q, axis=1).astype(BF16)
    w_out = _permute_wa_heads(lyr["w_out"], W_GROUP, axis=0).astype(BF16)
    rw_t = lyr["router_w"].T.astype(BF16)
    w_bd = _pool_weight(lyr["pool_w"])
    pscale = lyr["pool_scale"][None, :]
    hy_args = (lyr["hy_w1"], lyr["hy_b1"], lyr["hy_w2"], lyr["hy_b2"], lyr["hy_w3"], lyr["hy_freq"])

    conv = (lyr["hy_conv_w"], lyr["hy_conv_b"][None, :])
    u_hy, u_pool, q_wa, q_na, kv = _inproj(x, mx[0], mx[1], g1, w_in, conv, rope_tabs)
    if update_ctx:
        cu_hy, cu_pool, cq_wa, cq_na, ckv = _inproj(ctx, mc[0], mc[1], g1, w_in, conv)
    else:
        ckv = _inproj(ctx, mc[0], mc[1], g1, w_in[:, OFF_KV:], kv_only=True)

    ys = [_hyena(u_hy, _hyena_filters(s, *hy_args), lyr["hy_skip"]),
          _window_attn(q_wa, kv, ckv, lyr["wa_sink"]),
          _pool_mixer(u_pool, w_bd, pscale),
          _neighbourhood_attn(q_na, kv, ckv, lyr["na_rpb"])]
    x, h, aff = _outproj(ys, x, mx[2], w_out, g2, mx[3], mx[4], rw_t)
    moe = _expert_choice_ffn(h, aff, *lyr["experts"], lyr["index"])
    x = _combine(x, *moe, mx[5], final_g)

    if update_ctx:
        ycs = [_hyena(cu_hy, _hyena_filters(lc, *hy_args), lyr["hy_skip"]),
               _ctx_attn(cq_wa, ckv, 0, 1, N_WA_KV, lyr["wa_sink"], WA_HEAD_ORDER),
               _pool_mixer(cu_pool, w_bd, pscale),
               _ctx_attn(cq_na, ckv, 1, 2, N_NA_HEADS, None)]
        ctx, hc, affc = _outproj(ycs, ctx, mc[2], w_out, g2, mc[3], mc[4], rw_t)
        moe_c = _expert_choice_ffn(hc, affc, *lyr["experts"], lyr["index"])
        ctx = _combine(ctx, *moe_c, mc[5])
    return x, ctx


def kernel(x, c, ctx, c_ctx, ada_w, ada_b, norm1_g, norm2_g, w_in, hy_conv_w, hy_conv_b, hy_w1, hy_b1, hy_w2,
           hy_b2, hy_w3, hy_freq, hy_skip, wa_sink, pool_w, pool_scale, na_rpb, w_out, router_w, exp_w_gate,
           exp_w_up, exp_w_down, final_norm_g):
    b, s, d = x.shape
    cs = jnp.zeros((MOD_ROWS, d), F32).at[:b].set(c).at[b].set(c_ctx)
    mods = _ada_mod(cs, ada_w, ada_b[:, None, :])
    rope_tabs = _rope_tables(s)
    params = dict(g1=norm1_g, g2=norm2_g, w_in=w_in, hy_conv_w=hy_conv_w, hy_conv_b=hy_conv_b, hy_w1=hy_w1,
                  hy_b1=hy_b1, hy_w2=hy_w2, hy_b2=hy_b2, hy_w3=hy_w3, hy_freq=hy_freq, hy_skip=hy_skip,
                  wa_sink=wa_sink, pool_w=pool_w, pool_scale=pool_scale, na_rpb=na_rpb, w_out=w_out,
                  router_w=router_w)
    for l in range(DEPTH):
        lyr = {k: v[l] for k, v in params.items()}
        lyr.update(index=l, experts=(exp_w_gate, exp_w_up, exp_w_down))
        last = l == DEPTH - 1
        x, ctx = _layer(x, ctx, mods[l], lyr, rope_tabs, update_ctx=not last,
                        final_g=final_norm_g[None, :] if last else None)
    return x
```
